```python
import jax, jax.numpy as jnp
from jax import lax
import numpy as np

D_MODEL = 1024
BATCH = 4
SEQ = 4096
DEPTH = 1
DEC_BATCH = 128
DEC_SEQ = 8
PAST_LEN = 16384
PAGE_SIZE = 128

N_HEADS = 8
N_KV_HEADS = 2
HEAD_DIM = 64
GQA_GROUP = N_HEADS // N_KV_HEADS
WINDOW = 128
ATTN_WIDTH = N_HEADS * HEAD_DIM
KV_WIDTH = N_KV_HEADS * HEAD_DIM
SCALE = HEAD_DIM ** -0.5
HG_HEADS = 4
HG_DK = 128
HG_DV = 128
HG_KEY_WIDTH = HG_HEADS * HG_DK
HG_WIDTH = HG_HEADS * HG_DV
HG_CHUNK = 64
MIX_WIDTH = ATTN_WIDTH + HG_WIDTH
IN_SPLITS = [ATTN_WIDTH, KV_WIDTH, KV_WIDTH, HG_KEY_WIDTH, HG_KEY_WIDTH, HG_WIDTH, HG_WIDTH]
IN_WIDTH = sum(IN_SPLITS)
D_FF = 2816
EPS = 1e-6

kernel_name = "hymba_swa_sink_hgrn2_macaron_step"


def rms_norm(x, g):
    xf = x.astype(jnp.float32)
    y = xf * lax.rsqrt(jnp.mean(xf * xf, axis=-1, keepdims=True) + EPS)
    return (y * g.astype(jnp.float32)).astype(x.dtype)


def swiglu(x, w_gu, w_down):
    g, u = jnp.split(x @ w_gu, 2, axis=-1)
    return (jax.nn.silu(g) * u) @ w_down


def sink_probs(s, mask, sink):
    s = jnp.where(mask, s, -jnp.inf)
    sk = sink.astype(jnp.float32)[..., None, None]
    m = jnp.maximum(jnp.max(s, axis=-1, keepdims=True), sk)
    p = jnp.exp(s - m)
    den = jnp.sum(p, axis=-1, keepdims=True) + jnp.exp(sk - m)
    return p / den


def banded_window_attention(q, k, v, sinks):
    B, L = q.shape[:2]
    W = WINDOW
    nb = L // W
    qb = q.reshape(B, nb, W, N_KV_HEADS, GQA_GROUP, HEAD_DIM).astype(jnp.float32)

    def with_prev(a):
        a = a.reshape(B, nb, W, N_KV_HEADS, HEAD_DIM).astype(jnp.float32)
        prev = jnp.pad(a, ((0, 0), (1, 0), (0, 0), (0, 0), (0, 0)))[:, :-1]
        return jnp.concatenate([prev, a], axis=2)

    kk, vv = with_prev(k), with_prev(v)
    s = jnp.einsum('bnqhgd,bnkhd->bnhgqk', qb, kk) * SCALE
    qi = jnp.arange(W)[:, None]
    kj = jnp.arange(2 * W)[None, :]
    dist = qi + W - kj
    band = (dist >= 0) & (dist <= WINDOW)
    valid = (jnp.arange(nb)[:, None, None] > 0) | (kj >= W)[None]
    mask = band[None] & valid
    p = sink_probs(s, mask[:, None, None], sinks.reshape(N_KV_HEADS, GQA_GROUP))
    out = jnp.einsum('bnhgqk,bnkhd->bnqhgd', p, vv)
    keep = min(WINDOW, L)
    return out.reshape(B, L, ATTN_WIDTH).astype(q.dtype), k[:, L - keep:], v[:, L - keep:]


def cached_window_attention(q, k, v, sinks, k_buf, v_buf):
    Bd, Ld = q.shape[:2]
    Wb = k_buf.shape[1]
    kk = jnp.concatenate([k_buf.astype(k.dtype), k], axis=1)
    vv = jnp.concatenate([v_buf.astype(v.dtype), v], axis=1)
    s = jnp.einsum('bqhgd,bkhd->bhgqk', q.astype(jnp.float32), kk.astype(jnp.float32)) * SCALE
    dist = (Wb + jnp.arange(Ld))[:, None] - jnp.arange(Wb + Ld)[None, :]
    mask = (dist >= 0) & (dist <= WINDOW)
    p = sink_probs(s, mask, sinks.reshape(N_KV_HEADS, GQA_GROUP))
    out = jnp.einsum('bhgqk,bkhd->bqhgd', p, vv.astype(jnp.float32))
    return out.reshape(Bd, Ld, ATTN_WIDTH).astype(q.dtype), kk[:, Ld:], vv[:, Ld:]


def hgrn2_chunked(q, k, logf, v, s0):
    B, L = q.shape[:2]
    C = min(HG_CHUNK, L)
    n = -(-L // C)
    pad = n * C - L

    def prep(a):
        a = jnp.pad(a, ((0, 0), (0, pad), (0, 0), (0, 0)))
        return a.reshape(B, n, C, a.shape[2], a.shape[3]).transpose(1, 0, 3, 2, 4)

    qc, kc, gc, vc = prep(q), prep(k), prep(logf), prep(v)
    causal = jnp.tril(jnp.ones((C, C), dtype=bool))[:, :, None]

    def step(S, inp):
        qb, kb, gb, vb = inp
        G = jnp.cumsum(gb, axis=2)
        diff = G[:, :, :, None, :] - G[:, :, None, :, :]
        decay = jnp.exp(jnp.where(causal, diff, -jnp.inf))
        A = jnp.einsum('bhtk,bhsk,bhtsk->bhts', qb, kb, decay)
        o = jnp.einsum('bhts,bhsv->bhtv', A, vb) + jnp.einsum('bhtk,bhkv->bhtv', qb * jnp.exp(G), S)
        G_last = G[:, :, -1:, :]
        S = jnp.exp(G_last[:, :, 0, :])[..., None] * S + jnp.einsum(
            'bhsk,bhsv->bhkv', kb * jnp.exp(G_last - G), vb)
        return S, o

    S, o = lax.scan(step, s0, (qc, kc, gc, vc))
    o = o.transpose(1, 0, 3, 2, 4).reshape(B, n * C, q.shape[2], v.shape[3])[:, :L]
    return o, S


def setup_inputs(seed: int = 0) -> dict:
    key = jax.random.key(seed)
    ks = jax.random.split(key, 24)
    f32 = jnp.float32
    win_buf = min(WINDOW, PAST_LEN)

    def nrm(k, shape, scale):
        return jax.random.normal(k, shape, f32) * scale

    def gain(k, shape):
        return 1.0 + 0.02 * jax.random.normal(k, shape, f32)

    return {
        "x_prompt": nrm(ks[0], (BATCH, SEQ, D_MODEL), 1.0),
        "x_sample": nrm(ks[1], (DEC_BATCH, DEC_SEQ, D_MODEL), 1.0),
        "cache_k_win": nrm(ks[2], (DEPTH, DEC_BATCH, win_buf, N_KV_HEADS, HEAD_DIM), 1.0),
        "cache_v_win": nrm(ks[3], (DEPTH, DEC_BATCH, win_buf, N_KV_HEADS, HEAD_DIM), 1.0),
        "state_hgrn": nrm(ks[4], (DEPTH, DEC_BATCH, HG_HEADS, HG_DK, HG_DV), 0.5),
        "w_in": nrm(ks[5], (DEPTH, D_MODEL, IN_WIDTH), D_MODEL ** -0.5),
        "b_in": nrm(ks[6], (DEPTH, IN_WIDTH), 0.02),
        "attn_sinks": nrm(ks[7], (DEPTH, N_HEADS), 0.5),
        "attn_out_norm": gain(ks[8], (DEPTH, ATTN_WIDTH)),
        "hg_lb_logits": nrm(ks[9], (DEPTH + 1, HG_KEY_WIDTH), 0.1),
        "hg_out_norm": gain(ks[10], (DEPTH, HG_DV)),
        "w_out": nrm(ks[11], (DEPTH, MIX_WIDTH, D_MODEL), MIX_WIDTH ** -0.5),
        "ffn1_w_gu": nrm(ks[12], (DEPTH, D_MODEL, 2 * D_FF), D_MODEL ** -0.5),
        "ffn1_w_down": nrm(ks[13], (DEPTH, D_FF, D_MODEL), D_FF ** -0.5),
        "ffn2_w_gu": nrm(ks[14], (DEPTH, D_MODEL, 2 * D_FF), D_MODEL ** -0.5),
        "ffn2_w_down": nrm(ks[15], (DEPTH, D_FF, D_MODEL), D_FF ** -0.5),
        "norm_ffn1_pre": gain(ks[16], (DEPTH, D_MODEL)),
        "norm_ffn1_post": gain(ks[17], (DEPTH, D_MODEL)),
        "norm_mix_pre": gain(ks[18], (DEPTH, D_MODEL)),
        "norm_mix_post": gain(ks[19], (DEPTH, D_MODEL)),
        "norm_ffn2_pre": gain(ks[20], (DEPTH, D_MODEL)),
        "norm_ffn2_post": gain(ks[21], (DEPTH, D_MODEL)),
    }


def reference(x_prompt, x_sample, cache_k_win, cache_v_win, state_hgrn,
              w_in, b_in, attn_sinks, attn_out_norm, hg_lb_logits, hg_out_norm, w_out,
              ffn1_w_gu, ffn1_w_down, ffn2_w_gu, ffn2_w_down,
              norm_ffn1_pre, norm_ffn1_post, norm_mix_pre, norm_mix_post,
              norm_ffn2_pre, norm_ffn2_post):
    f32 = jnp.float32
    lb_all = jnp.cumsum(jax.nn.softmax(hg_lb_logits.astype(f32), axis=0), axis=0)

    def run_layer(x, l, attend, s0):
        B, L = x.shape[:2]
        x = x + 0.5 * rms_norm(swiglu(rms_norm(x, norm_ffn1_pre[l]), ffn1_w_gu[l], ffn1_w_down[l]),
                               norm_ffn1_post[l])
        h = rms_norm(x, norm_mix_pre[l])
        z = h @ w_in[l] + b_in[l]
        q, k, v, hq, hf, hi, hg = jnp.split(z, [int(c) for c in np.cumsum(IN_SPLITS)[:-1]], axis=-1)
        q = q.reshape(B, L, N_KV_HEADS, GQA_GROUP, HEAD_DIM)
        k = k.reshape(B, L, N_KV_HEADS, HEAD_DIM)
        v = v.reshape(B, L, N_KV_HEADS, HEAD_DIM)
        a, k_new, v_new = attend(q, k, v, attn_sinks[l])
        a = rms_norm(a, attn_out_norm[l])
        lb = lb_all[l].reshape(HG_HEADS, HG_DK)
        hq = jax.nn.silu(hq.reshape(B, L, HG_HEADS, HG_DK).astype(f32))
        f = lb + (1.0 - lb) * jax.nn.sigmoid(hf.reshape(B, L, HG_HEADS, HG_DK).astype(f32))
        o, S = hgrn2_chunked(hq, 1.0 - f, jnp.log(f),
                             hi.reshape(B, L, HG_HEADS, HG_DV).astype(f32), s0)
        o = rms_norm(o, hg_out_norm[l]) * jax.nn.silu(hg.reshape(B, L, HG_HEADS, HG_DV).astype(f32))
        o = o.reshape(B, L, HG_WIDTH).astype(x.dtype)
        mix = jnp.concatenate([a, o], axis=-1) @ w_out[l]
        x = x + rms_norm(mix, norm_mix_post[l])
        x = x + 0.5 * rms_norm(swiglu(rms_norm(x, norm_ffn2_pre[l]), ffn2_w_gu[l], ffn2_w_down[l]),
                               norm_ffn2_post[l])
        return x, k_new, v_new, S

    yp, ys = x_prompt, x_sample
    kp, vp, sp, kd, vd, sd = [], [], [], [], [], []
    for l in range(DEPTH):
        s0p = jnp.zeros((x_prompt.shape[0], HG_HEADS, HG_DK, HG_DV), f32)
        yp, k1, v1, s1 = run_layer(yp, l, banded_window_attention, s0p)
        kp.append(k1); vp.append(v1); sp.append(s1)

        def attend_cached(q, k, v, sinks, l=l):
            return cached_window_attention(q, k, v, sinks, cache_k_win[l], cache_v_win[l])

        ys, k2, v2, s2 = run_layer(ys, l, attend_cached, state_hgrn[l].astype(f32))
        kd.append(k2); vd.append(v2); sd.append(s2)

    return (yp, ys, jnp.stack(kp), jnp.stack(vp), jnp.stack(sp), jnp.stack(kd), jnp.stack(vd), jnp.stack(sd))
```

```python
import functools

import jax
import jax.numpy as jnp
from jax import lax
from jax.experimental import pallas as pl
from jax.experimental.pallas import tpu as pltpu

F32 = jnp.float32
BF16 = jnp.bfloat16

D_MODEL = 1024
N_HEADS = 8
N_KV_HEADS = 2
HEAD_DIM = 64
GQA_GROUP = N_HEADS // N_KV_HEADS
WINDOW = 128
ATTN_WIDTH = N_HEADS * HEAD_DIM
KV_WIDTH = N_KV_HEADS * HEAD_DIM
SCALE = HEAD_DIM ** -0.5
HG_HEADS = 4
HG_DK = 128
HG_DV = 128
HG_WIDTH = HG_HEADS * HG_DV
D_FF = 2816
EPS = 1e-6

Q_COLS = (0, ATTN_WIDTH)
KV_COLS = (ATTN_WIDTH, ATTN_WIDTH + 2 * KV_WIDTH)
HP_COLS = (KV_COLS[1], KV_COLS[1] + 3 * HG_WIDTH)
HG_COLS = (HP_COLS[1], HP_COLS[1] + HG_WIDTH)

VMEM_LIMIT_BIG = 58 * 1024 * 1024
VMEM_LIMIT_SMALL = 40 * 1024 * 1024
FF_CHUNKS = ((0, 768), (768, 1536), (1536, 2304), (2304, 2816))
TOKEN_TILE = 512
HG_TILE = 64
ATTN_SEQ_GROUP = 4
NEG = -1e30


def _rms(x, g):
    return x * lax.rsqrt(jnp.mean(x * x, axis=-1, keepdims=True) + EPS) * g


def _silu(x):
    return x * jax.nn.sigmoid(x)


def _dot(a, b):
    return jnp.dot(a, b, preferred_element_type=F32)


def _dot_nt(a, b):
    return lax.dot_general(a, b, (((1,), (1,)), ((), ())), preferred_element_type=F32)


def _dot_tn(a, b):
    return lax.dot_general(a, b, (((0,), (0,)), ((), ())), preferred_element_type=F32)


def _split3(x):
    a = x.astype(BF16)
    r = x - a.astype(F32)
    b = r.astype(BF16)
    c = (r - b.astype(F32)).astype(BF16)
    return a, b, c


def _swiglu_ffn(x, g_pre, wgu_ref, wd_ref, g_post):
    h = _rms(x, g_pre).astype(BF16)
    acc = None
    for lo, hi in FF_CHUNKS:
        g = _dot(h, wgu_ref[:, lo:hi])
        u = _dot(h, wgu_ref[:, D_FF + lo:D_FF + hi])
        part = _dot((_silu(g) * u).astype(BF16), wd_ref[lo:hi, :])
        acc = part if acc is None else acc + part
    return x + 0.5 * _rms(acc, g_post)


def _ffn_in_kernel(x_ref, g_pre, g_post, g_mix, wgu_ref, wd_ref, win_ref, bin_ref,
                   x1_ref, q_ref, kv_ref, hp_ref, hg_ref):
    x1 = _swiglu_ffn(x_ref[...], g_pre[...], wgu_ref, wd_ref, g_post[...])
    x1_ref[...] = x1
    h = _rms(x1, g_mix[...]).astype(BF16)
    for ref, (lo, hi) in ((q_ref, Q_COLS), (kv_ref, KV_COLS), (hp_ref, HP_COLS), (hg_ref, HG_COLS)):
        ref[...] = _dot(h, win_ref[:, lo:hi]) + bin_ref[:, lo:hi]


def _const_spec(shape):
    zeros = (0,) * len(shape)
    return pl.BlockSpec(shape, lambda *_: zeros, pipeline_mode=pl.Buffered(1))


def _row_spec(tile, width):
    return pl.BlockSpec((tile, width), lambda i: (i, 0))


def _ffn_in(x, g_pre, g_post, g_mix, wgu, wd, win, b_in):
    t = x.shape[0]
    tile = min(TOKEN_TILE, t)
    widths = [D_MODEL] + [hi - lo for lo, hi in (Q_COLS, KV_COLS, HP_COLS, HG_COLS)]
    return pl.pallas_call(
        _ffn_in_kernel,
        grid=(t // tile,),
        in_specs=[_row_spec(tile, D_MODEL), _const_spec(g_pre.shape), _const_spec(g_post.shape),
                  _const_spec(g_mix.shape), _const_spec(wgu.shape), _const_spec(wd.shape),
                  _const_spec(win.shape), _const_spec(b_in.shape)],
        out_specs=[_row_spec(tile, w) for w in widths],
        out_shape=[jax.ShapeDtypeStruct((t, w), F32) for w in widths],
        compiler_params=pltpu.CompilerParams(dimension_semantics=("arbitrary",),
                                             vmem_limit_bytes=VMEM_LIMIT_BIG),
        name="ffn_in",
    )(x, g_pre, g_post, g_mix, wgu, wd, win, b_in)


def _out_ffn_kernel(x1_ref, a_ref, o_ref, g_mixpost, g_pre, g_post, wout_ref, wgu_ref, wd_ref, y_ref):
    mix = (_dot(a_ref[...].astype(BF16), wout_ref[0:ATTN_WIDTH, :])
           + _dot(o_ref[...].astype(BF16), wout_ref[ATTN_WIDTH:ATTN_WIDTH + HG_WIDTH, :]))
    x2 = x1_ref[...] + _rms(mix, g_mixpost[...])
    y_ref[...] = _swiglu_ffn(x2, g_pre[...], wgu_ref, wd_ref, g_post[...])


def _out_ffn(x1, a, o, g_mixpost, g_pre, g_post, wout, wgu, wd):
    t = x1.shape[0]
    tile = min(TOKEN_TILE, t)
    return pl.pallas_call(
        _out_ffn_kernel,
        grid=(t // tile,),
        in_specs=[_row_spec(tile, D_MODEL), _row_spec(tile, ATTN_WIDTH), _row_spec(tile, HG_WIDTH),
                  _const_spec(g_mixpost.shape), _const_spec(g_pre.shape), _const_spec(g_post.shape),
                  _const_spec(wout.shape), _const_spec(wgu.shape), _const_spec(wd.shape)],
        out_specs=_row_spec(tile, D_MODEL),
        out_shape=jax.ShapeDtypeStruct((t, D_MODEL), F32),
        compiler_params=pltpu.CompilerParams(dimension_semantics=("arbitrary",),
                                             vmem_limit_bytes=VMEM_LIMIT_BIG),
        name="out_ffn",
    )(x1, a, o, g_mixpost, g_pre, g_post, wout, wgu, wd)


def _softmax_sink_pv(scores, masks, values, sink):
    scores = [jnp.where(mk, s * SCALE, NEG) for s, mk in zip(scores, masks)]
    m = sink
    for s in scores:
        m = jnp.maximum(m, jnp.max(s, axis=-1, keepdims=True))
    den = jnp.exp(sink - m)
    out = None
    for s, v in zip(scores, values):
        p = jnp.exp(s - m)
        den = den + jnp.sum(p, axis=-1, keepdims=True)
        pv = _dot(p.astype(BF16), v)
        out = pv if out is None else out + pv
    return out / den


def _attn_prompt_kernel(sink_ref, q_ref, kvc_ref, kvp_ref, gn_ref, a_ref):
    i = pl.program_id(1)
    w = WINDOW
    qi = lax.broadcasted_iota(jnp.int32, (w, w), 0)
    kj = lax.broadcasted_iota(jnp.int32, (w, w), 1)
    mask_cur = kj <= qi
    mask_prev = (kj >= qi) & (i > 0)
    q = q_ref[...]
    kvc = kvc_ref[...].astype(BF16)
    kvp = kvp_ref[...].astype(BF16)
    for h in range(N_HEADS):
        hk = h // GQA_GROUP
        kcols = slice(hk * HEAD_DIM, (hk + 1) * HEAD_DIM)
        vcols = slice(KV_WIDTH + hk * HEAD_DIM, KV_WIDTH + (hk + 1) * HEAD_DIM)
        qh = q[:, h * HEAD_DIM:(h + 1) * HEAD_DIM].astype(BF16)
        scores = [_dot_nt(qh, kvp[:, kcols]), _dot_nt(qh, kvc[:, kcols])]
        a_ref[:, h * HEAD_DIM:(h + 1) * HEAD_DIM] = _softmax_sink_pv(
            scores, [mask_prev, mask_cur], [kvp[:, vcols], kvc[:, vcols]], sink_ref[h])
    a_ref[...] = _rms(a_ref[...], gn_ref[...])


def _attn_prompt(q, kv, sinks, gn, batch, seq):
    nb = seq // WINDOW
    return pl.pallas_call(
        _attn_prompt_kernel,
        grid=(batch, nb),
        in_specs=[pl.BlockSpec(memory_space=pltpu.SMEM),
                  pl.BlockSpec((WINDOW, ATTN_WIDTH), lambda b, i: (b * nb + i, 0)),
                  pl.BlockSpec((WINDOW, 2 * KV_WIDTH), lambda b, i: (b * nb + i, 0)),
                  pl.BlockSpec((WINDOW, 2 * KV_WIDTH), lambda b, i: (b * nb + jnp.maximum(i - 1, 0), 0)),
                  pl.BlockSpec(gn.shape, lambda b, i: (0, 0))],
        out_specs=pl.BlockSpec((WINDOW, ATTN_WIDTH), lambda b, i: (b * nb + i, 0)),
        out_shape=jax.ShapeDtypeStruct((batch * seq, ATTN_WIDTH), F32),
        compiler_params=pltpu.CompilerParams(dimension_semantics=("arbitrary", "arbitrary"),
                                             vmem_limit_bytes=VMEM_LIMIT_SMALL),
        name="attn_prompt",
    )(sinks, q, kv, kv, gn)


def _attn_sample_kernel(sink_ref, q_ref, kvn_ref, ck_ref, cv_ref, gn_ref, a_ref, ko_ref, vo_ref, *, dec_len):
    ns = ATTN_SEQ_GROUP
    rows = ns * dec_len
    wb = ck_ref.shape[1]
    m_rows = GQA_GROUP * rows
    r = lax.broadcasted_iota(jnp.int32, (m_rows, ns * wb), 0)
    c = lax.broadcasted_iota(jnp.int32, (m_rows, ns * wb), 1)
    r_seq, r_pos = (r % rows) // dec_len, r % dec_len
    mask_cache = (r_seq == c // wb) & (c % wb >= r_pos)
    r = lax.broadcasted_iota(jnp.int32, (m_rows, rows), 0)
    c = lax.broadcasted_iota(jnp.int32, (m_rows, rows), 1)
    mask_new = ((r % rows) // dec_len == c // dec_len) & (c % dec_len <= r % dec_len)
    row_head = lax.broadcasted_iota(jnp.int32, (m_rows, 1), 0) // rows

    for grp in range(q_ref.shape[0] // rows):
        rs = slice(grp * rows, (grp + 1) * rows)
        ss = slice(grp * ns, (grp + 1) * ns)
        q = q_ref[rs, :]
        kvn = kvn_ref[rs, :]
        ck = ck_ref[ss].reshape(ns * wb, KV_WIDTH)
        cv = cv_ref[ss].reshape(ns * wb, KV_WIDTH)
        kvn_b, ck_b, cv_b = kvn.astype(BF16), ck.astype(BF16), cv.astype(BF16)
        for hk in range(N_KV_HEADS):
            kcols = slice(hk * HEAD_DIM, (hk + 1) * HEAD_DIM)
            vcols = slice(KV_WIDTH + hk * HEAD_DIM, KV_WIDTH + (hk + 1) * HEAD_DIM)
            heads = [hk * GQA_GROUP + g for g in range(GQA_GROUP)]
            qs = jnp.concatenate([q[:, h * HEAD_DIM:(h + 1) * HEAD_DIM] for h in heads], axis=0).astype(BF16)
            sink = jnp.zeros((m_rows, 1), F32)
            for g, h in enumerate(heads):
                sink = jnp.where(row_head == g, sink_ref[h], sink)
            scores = [_dot_nt(qs, ck_b[:, kcols]), _dot_nt(qs, kvn_b[:, kcols])]
            out = _softmax_sink_pv(scores, [mask_cache, mask_new], [cv_b[:, kcols], kvn_b[:, vcols]], sink)
            for g, h in enumerate(heads):
                a_ref[rs, h * HEAD_DIM:(h + 1) * HEAD_DIM] = out[g * rows:(g + 1) * rows, :]
        a_ref[rs, :] = _rms(a_ref[rs, :], gn_ref[...])
        for n in range(ns):
            s = grp * ns + n
            ts = slice(grp * rows + n * dec_len, grp * rows + (n + 1) * dec_len)
            ko_ref[s, 0:wb - dec_len, :] = ck_ref[s, dec_len:wb, :]
            vo_ref[s, 0:wb - dec_len, :] = cv_ref[s, dec_len:wb, :]
            ko_ref[s, wb - dec_len:wb, :] = kvn_ref[ts, 0:KV_WIDTH]
            vo_ref[s, wb - dec_len:wb, :] = kvn_ref[ts, KV_WIDTH:2 * KV_WIDTH]


def _attn_sample(q, kv, cache_k, cache_v, sinks, gn, dec_len, groups_per_step=2):
    nseq, wb, _ = cache_k.shape
    sps = ATTN_SEQ_GROUP * groups_per_step
    rows = sps * dec_len
    cache_spec = pl.BlockSpec((sps, wb, KV_WIDTH), lambda i: (i, 0, 0))
    return pl.pallas_call(
        functools.partial(_attn_sample_kernel, dec_len=dec_len),
        grid=(nseq // sps,),
        in_specs=[pl.BlockSpec(memory_space=pltpu.SMEM),
                  _row_spec(rows, ATTN_WIDTH), _row_spec(rows, 2 * KV_WIDTH),
                  cache_spec, cache_spec, pl.BlockSpec(gn.shape, lambda i: (0, 0))],
        out_specs=[_row_spec(rows, ATTN_WIDTH), cache_spec, cache_spec],
        out_shape=[jax.ShapeDtypeStruct((nseq * dec_len, ATTN_WIDTH), F32),
                   jax.ShapeDtypeStruct(cache_k.shape, F32), jax.ShapeDtypeStruct(cache_v.shape, F32)],
        compiler_params=pltpu.CompilerParams(dimension_semantics=("arbitrary",),
                                             vmem_limit_bytes=VMEM_LIMIT_SMALL),
        name="attn_sample",
    )(sinks, q, kv, cache_k, cache_v, gn)


def _hgrn_kernel(*refs, nseq, carry):
    if carry:
        hp_ref, hg_ref, lbl_ref, gn_ref, o_ref, s_out = refs
        s_in = s_out

        @pl.when(pl.program_id(0) == 0)
        def _():
            s_out[...] = jnp.zeros(s_out.shape, F32)
    else:
        hp_ref, hg_ref, lbl_ref, gn_ref, s_in, o_ref, s_out = refs

    ch = HG_TILE
    ls = ch // nseq
    r = lax.broadcasted_iota(jnp.int32, (ch, ch), 0)
    c = lax.broadcasted_iota(jnp.int32, (ch, ch), 1)
    same = (r // ls) == (c // ls)
    causal = same & (c <= r)
    cum_lhs = jnp.concatenate([causal, same & (c % ls < ls // 2), same], axis=0).astype(F32).astype(BF16)
    sr = lax.broadcasted_iota(jnp.int32, (ch, nseq * HG_DV), 0)
    sc = lax.broadcasted_iota(jnp.int32, (ch, nseq * HG_DV), 1)
    seg_sel = ((sr // ls) == (sc // HG_DV)).astype(F32).astype(BF16)

    lbl = lbl_ref[...]
    lmax = jnp.max(lbl, axis=0, keepdims=True)
    e = jnp.exp(lbl - lmax)
    lb = e[0:1, :] / jnp.sum(e, axis=0, keepdims=True)
    gn = gn_ref[...]

    n_outer, n_inner = hp_ref.shape[0], hp_ref.shape[1]
    for to in range(n_outer):
        for ti in range(n_inner):
            hp = hp_ref[to, ti]
            qa = _silu(hp[:, 0:HG_WIDTH])
            f = lb + (1.0 - lb) * jax.nn.sigmoid(hp[:, HG_WIDTH:2 * HG_WIDTH])
            v = hp[:, 2 * HG_WIDTH:3 * HG_WIDTH].astype(BF16)
            kx = 1.0 - f
            parts = _split3(jnp.log(f))
            gs = sum(_dot(cum_lhs, p) for p in parts)
            g, g_mid, g_last = gs[0:ch], gs[ch:2 * ch], gs[2 * ch:3 * ch]
            dcol = sum(_dot_tn(p, seg_sel) for p in parts)
            qt = (qa * jnp.exp(g - g_mid)).astype(BF16)
            kt = (kx * jnp.exp(g_mid - g)).astype(BF16)
            qg = (qa * jnp.exp(g)).astype(BF16)
            kd = (kx * jnp.exp(g_last - g)).astype(BF16)
            gate = _silu(hg_ref[to, ti])
            for h in range(HG_HEADS):
                cs = slice(h * HG_DK, (h + 1) * HG_DK)
                a = jnp.where(causal, _dot_nt(qt[:, cs], kt[:, cs]), 0.0)
                o = _dot(a.astype(BF16), v[:, cs])
                inter = []
                for n in range(nseq):
                    rs = slice(n * ls, (n + 1) * ls)
                    sidx = (to * n_inner + ti) * nseq + n
                    s = s_in[sidx, h]
                    inter.append(_dot(qg[rs, cs], s.astype(BF16)))
                    decay = jnp.exp(dcol[cs, n * HG_DV:(n + 1) * HG_DV])
                    s_out[sidx, h] = s * decay + _dot_tn(kd[rs, cs], v[rs, cs])
                o = o + (inter[0] if nseq == 1 else jnp.concatenate(inter, axis=0))
                o_ref[to, ti, :, cs] = _rms(o, gn) * gate[:, cs]


def _hgrn(hp, hg, lb_logits, gn, state, nseq):
    n_outer, n_inner = hp.shape[0], hp.shape[1]
    carry = state is None
    tile_block = (n_outer, 1, HG_TILE, None)
    hp_spec = pl.BlockSpec((n_outer, 1, HG_TILE, hp.shape[3]), lambda i: (0, i, 0, 0))
    hg_spec = pl.BlockSpec((n_outer, 1, HG_TILE, HG_WIDTH), lambda i: (0, i, 0, 0))
    del tile_block
    small = [pl.BlockSpec(lb_logits.shape, lambda i: (0, 0)), pl.BlockSpec(gn.shape, lambda i: (0, 0))]
    nstate = n_outer * nseq
    if carry:
        s_shape = (nstate, HG_HEADS, HG_DK, HG_DV)
        s_spec = pl.BlockSpec(s_shape, lambda i: (0, 0, 0, 0))
        in_specs, args = [hp_spec, hg_spec] + small, (hp, hg, lb_logits, gn)
    else:
        s_shape = state.shape
        s_spec = pl.BlockSpec((nstate, HG_HEADS, HG_DK, HG_DV), lambda i: (i, 0, 0, 0))
        in_specs, args = [hp_spec, hg_spec] + small + [s_spec], (hp, hg, lb_logits, gn, state)
    return pl.pallas_call(
        functools.partial(_hgrn_kernel, nseq=nseq, carry=carry),
        grid=(n_inner,),
        in_specs=in_specs,
        out_specs=[hg_spec, s_spec],
        out_shape=[jax.ShapeDtypeStruct(hg.shape, F32), jax.ShapeDtypeStruct(s_shape, F32)],
        compiler_params=pltpu.CompilerParams(dimension_semantics=("arbitrary",),
                                             vmem_limit_bytes=VMEM_LIMIT_SMALL),
        name="hgrn_prompt" if carry else "hgrn_sample",
    )(*args)


def kernel(x_prompt, x_sample, cache_k_win, cache_v_win, state_hgrn, w_in, b_in, attn_sinks, attn_out_norm,
           hg_lb_logits, hg_out_norm, w_out, ffn1_w_gu, ffn1_w_down, ffn2_w_gu, ffn2_w_down,
           norm_ffn1_pre, norm_ffn1_post, norm_mix_pre, norm_mix_post, norm_ffn2_pre, norm_ffn2_post):
    depth = w_in.shape[0]
    assert depth == 1, "single-layer trunk"
    batch, seq, _ = x_prompt.shape
    dec_batch, dec_len, _ = x_sample.shape
    wb = cache_k_win.shape[2]
    assert seq % WINDOW == 0 and wb == WINDOW and HG_TILE % dec_len == 0
    layer = 0
    row = lambda p: p[layer].reshape(1, -1).astype(F32)
    wgu1, wd1 = ffn1_w_gu[layer].astype(BF16), ffn1_w_down[layer].astype(BF16)
    wgu2, wd2 = ffn2_w_gu[layer].astype(BF16), ffn2_w_down[layer].astype(BF16)
    win, wout = w_in[layer].astype(BF16), w_out[layer].astype(BF16)
    sinks = attn_sinks[layer].astype(F32)
    lb_logits = hg_lb_logits.astype(F32)
    g_attn, g_hg = row(attn_out_norm), row(hg_out_norm)

    def front(x2d):
        return _ffn_in(x2d, row(norm_ffn1_pre), row(norm_ffn1_post), row(norm_mix_pre), wgu1, wd1, win, row(b_in))

    def back(x1, a, o):
        return _out_ffn(x1, a, o, row(norm_mix_post), row(norm_ffn2_pre), row(norm_ffn2_post), wout, wgu2, wd2)

    t_p = batch * seq
    x1, q, kv, hp, hg = front(x_prompt.reshape(t_p, D_MODEL))
    a = _attn_prompt(q, kv, sinks, g_attn, batch, seq)
    nch = seq // HG_TILE
    o, s_prompt = _hgrn(hp.reshape(batch, nch, HG_TILE, 3 * HG_WIDTH), hg.reshape(batch, nch, HG_TILE, HG_WIDTH),
                        lb_logits, g_hg, None, nseq=1)
    y_prompt = back(x1, a, o.reshape(t_p, HG_WIDTH)).reshape(batch, seq, D_MODEL)
    kv_last = kv.reshape(batch, seq, 2 * KV_WIDTH)[:, seq - WINDOW:]
    k_prompt = kv_last[..., :KV_WIDTH].reshape(1, batch, WINDOW, N_KV_HEADS, HEAD_DIM)
    v_prompt = kv_last[..., KV_WIDTH:].reshape(1, batch, WINDOW, N_KV_HEADS, HEAD_DIM)

    t_s = dec_batch * dec_len
    x1, q, kv, hp, hg = front(x_sample.reshape(t_s, D_MODEL))
    a, k_s, v_s = _attn_sample(q, kv, cache_k_win[layer].reshape(dec_batch, wb, KV_WIDTH),
                               cache_v_win[layer].reshape(dec_batch, wb, KV_WIDTH), sinks, g_attn, dec_len)
    ntile = t_s // HG_TILE
    o, s_sample = _hgrn(hp.reshape(1, ntile, HG_TILE, 3 * HG_WIDTH), hg.reshape(1, ntile, HG_TILE, HG_WIDTH),
                        lb_logits, g_hg, state_hgrn[layer].astype(F32), nseq=HG_TILE // dec_len)
    y_sample = back(x1, a, o.reshape(t_s, HG_WIDTH)).reshape(dec_batch, dec_len, D_MODEL)
    k_sample = k_s.reshape(1, dec_batch, wb, N_KV_HEADS, HEAD_DIM)
    v_sample = v_s.reshape(1, dec_batch, wb, N_KV_HEADS, HEAD_DIM)

    return (y_prompt, y_sample, k_prompt, v_prompt, s_prompt[None], k_sample, v_sample, s_sample[None])
```

```python
import functools

import jax
import jax.numpy as jnp
from jax import lax
from jax.experimental import pallas as pl
from jax.experimental.pallas import tpu as pltpu

F32 = jnp.float32
BF16 = jnp.bfloat16

D_MODEL = 1024
N_HEADS = 8
N_KV_HEADS = 2
HEAD_DIM = 64
GQA_GROUP = N_HEADS // N_KV_HEADS
WINDOW = 128
ATTN_WIDTH = N_HEADS * HEAD_DIM
KV_WIDTH = N_KV_HEADS * HEAD_DIM
SCALE = HEAD_DIM ** -0.5
HG_HEADS = 4
HG_DK = 128
HG_DV = 128
HG_WIDTH = HG_HEADS * HG_DV
D_FF = 2816
EPS = 1e-6

Q_COLS = (0, ATTN_WIDTH)
KV_COLS = (ATTN_WIDTH, ATTN_WIDTH + 2 * KV_WIDTH)
HP_COLS = (KV_COLS[1], KV_COLS[1] + 3 * HG_WIDTH)
HG_COLS = (HP_COLS[1], HP_COLS[1] + HG_WIDTH)

VMEM_LIMIT_BIG = 58 * 1024 * 1024
VMEM_LIMIT_SMALL = 40 * 1024 * 1024
FF_CHUNKS = ((0, 768), (768, 1536), (1536, 2304), (2304, 2816))
TOKEN_TILE = 512
HG_TILE = 64
ATTN_SEQ_GROUP = 4
ATTN_BLOCKS_PER_STEP = 2
NEG = -1e30
LOG2E = 1.4426950408889634


def _rms(x, g):
    return x * lax.rsqrt(jnp.mean(x * x, axis=-1, keepdims=True) + EPS) * g


def _silu(x):
    return x * jax.nn.sigmoid(x)


def _dot(a, b):
    return jnp.dot(a, b, preferred_element_type=F32)


def _dot_nt(a, b):
    return lax.dot_general(a, b, (((1,), (1,)), ((), ())), preferred_element_type=F32)


def _dot_tn(a, b):
    return lax.dot_general(a, b, (((0,), (0,)), ((), ())), preferred_element_type=F32)


def _split3(x):
    a = x.astype(BF16)
    r = x - a.astype(F32)
    b = r.astype(BF16)
    c = (r - b.astype(F32)).astype(BF16)
    return a, b, c


def _swiglu_ffn(x, g_pre, wgu_ref, wd_ref, g_post):
    h = _rms(x, g_pre).astype(BF16)
    acc = None
    for lo, hi in FF_CHUNKS:
        g = _dot(h, wgu_ref[:, lo:hi])
        u = _dot(h, wgu_ref[:, D_FF + lo:D_FF + hi])
        part = _dot((_silu(g) * u).astype(BF16), wd_ref[lo:hi, :])
        acc = part if acc is None else acc + part
    return x + 0.5 * _rms(acc, g_post)


def _ffn_in_kernel(x_ref, g_pre, g_post, g_mix, wgu_ref, wd_ref, win_ref, bin_ref,
                   x1_ref, q_ref, kv_ref, hp_ref, hg_ref):
    x1 = _swiglu_ffn(x_ref[...], g_pre[...], wgu_ref, wd_ref, g_post[...])
    x1_ref[...] = x1
    h = _rms(x1, g_mix[...]).astype(BF16)
    for ref, (lo, hi) in ((q_ref, Q_COLS), (kv_ref, KV_COLS), (hp_ref, HP_COLS), (hg_ref, HG_COLS)):
        ref[...] = _dot(h, win_ref[:, lo:hi]) + bin_ref[:, lo:hi]


def _const_spec(shape):
    zeros = (0,) * len(shape)
    return pl.BlockSpec(shape, lambda *_: zeros, pipeline_mode=pl.Buffered(1))


def _row_spec(tile, width):
    return pl.BlockSpec((tile, width), lambda i: (i, 0))


def _ffn_in(x, g_pre, g_post, g_mix, wgu, wd, win, b_in):
    t = x.shape[0]
    tile = min(TOKEN_TILE, t)
    widths = [D_MODEL] + [hi - lo for lo, hi in (Q_COLS, KV_COLS, HP_COLS, HG_COLS)]
    return pl.pallas_call(
        _ffn_in_kernel,
        grid=(t // tile,),
        in_specs=[_row_spec(tile, D_MODEL), _const_spec(g_pre.shape), _const_spec(g_post.shape),
                  _const_spec(g_mix.shape), _const_spec(wgu.shape), _const_spec(wd.shape),
                  _const_spec(win.shape), _const_spec(b_in.shape)],
        out_specs=[_row_spec(tile, w) for w in widths],
        out_shape=[jax.ShapeDtypeStruct((t, w), F32) for w in widths],
        compiler_params=pltpu.CompilerParams(dimension_semantics=("arbitrary",),
                                             vmem_limit_bytes=VMEM_LIMIT_BIG),
        name="ffn_in",
    )(x, g_pre, g_post, g_mix, wgu, wd, win, b_in)


def _out_ffn_kernel(x1_ref, a_ref, o_ref, g_mixpost, g_pre, g_post, wout_ref, wgu_ref, wd_ref, y_ref,
                    *, a_transposed):
    a_dot = _dot_tn if a_transposed else _dot
    mix = (a_dot(a_ref[...].astype(BF16), wout_ref[0:ATTN_WIDTH, :])
           + _dot(o_ref[...].astype(BF16), wout_ref[ATTN_WIDTH:ATTN_WIDTH + HG_WIDTH, :]))
    x2 = x1_ref[...] + _rms(mix, g_mixpost[...])
    y_ref[...] = _swiglu_ffn(x2, g_pre[...], wgu_ref, wd_ref, g_post[...])


def _out_ffn(x1, a, o, g_mixpost, g_pre, g_post, wout, wgu, wd, a_transposed):
    t = x1.shape[0]
    tile = min(TOKEN_TILE, t)
    a_spec = pl.BlockSpec((ATTN_WIDTH, tile), lambda i: (0, i)) if a_transposed else _row_spec(tile, ATTN_WIDTH)
    return pl.pallas_call(
        functools.partial(_out_ffn_kernel, a_transposed=a_transposed),
        grid=(t // tile,),
        in_specs=[_row_spec(tile, D_MODEL), a_spec, _row_spec(tile, HG_WIDTH),
                  _const_spec(g_mixpost.shape), _const_spec(g_pre.shape), _const_spec(g_post.shape),
                  _const_spec(wout.shape), _const_spec(wgu.shape), _const_spec(wd.shape)],
        out_specs=_row_spec(tile, D_MODEL),
        out_shape=jax.ShapeDtypeStruct((t, D_MODEL), F32),
        compiler_params=pltpu.CompilerParams(dimension_semantics=("arbitrary",),
                                             vmem_limit_bytes=VMEM_LIMIT_BIG),
        name="out_ffn",
    )(x1, a, o, g_mixpost, g_pre, g_post, wout, wgu, wd)


def _softmax_sink_pv(scores, masks, values, sink):
    scores = [jnp.where(mk, s * SCALE, NEG) for s, mk in zip(scores, masks)]
    m = sink
    for s in scores:
        m = jnp.maximum(m, jnp.max(s, axis=-1, keepdims=True))
    den = jnp.exp(sink - m)
    out = None
    for s, v in zip(scores, values):
        p = jnp.exp(s - m)
        den = den + jnp.sum(p, axis=-1, keepdims=True)
        pv = _dot(p.astype(BF16), v)
        out = pv if out is None else out + pv
    return out / den


def _attn_prompt_kernel(sink_ref, q_ref, kv_ref, gnt_ref, at_ref):
    i = pl.program_id(1)
    w = WINDOW
    nkeys = 2 * w
    r = lax.broadcasted_iota(jnp.int32, (nkeys, w), 0)
    c = lax.broadcasted_iota(jnp.int32, (nkeys, w), 1)
    first_half = lax.broadcasted_iota(jnp.int32, (w, 2 * HEAD_DIM), 1) < HEAD_DIM
    for j in range(ATTN_BLOCKS_PER_STEP):
        q0 = (i * ATTN_BLOCKS_PER_STEP + j) * w
        k0 = jnp.maximum(q0 - w, 0)
        dist = c + (q0 - k0) - r
        bias = jnp.where((dist >= 0) & (dist <= WINDOW), 0.0, NEG)
        kvb = kv_ref[pl.ds(pl.multiple_of(k0, w), nkeys), :]
        kk = kvb[:, 0:KV_WIDTH]
        k_nat = kk.astype(BF16)
        k_swp = pltpu.roll(kk, HEAD_DIM, axis=1).astype(BF16)
        vt = kvb[:, KV_WIDTH:2 * KV_WIDTH].T.astype(BF16)
        rows = []
        for h in range(N_HEADS):
            hk, odd = h // GQA_GROUP, h % 2
            qp = q_ref[j * w:(j + 1) * w, (h // 2) * 2 * HEAD_DIM:(h // 2 + 1) * 2 * HEAD_DIM]
            qp = jnp.where(first_half != bool(odd), qp, 0.0).astype(BF16)
            keys = k_nat if hk == odd else k_swp
            st = _dot_nt(keys, qp) * (SCALE * LOG2E) + bias
            sink2 = sink_ref[h] * LOG2E
            m = jnp.maximum(jnp.max(st, axis=0, keepdims=True), sink2)
            p = jnp.exp2(st - m)
            den = jnp.sum(p, axis=0, keepdims=True) + jnp.exp2(sink2 - m)
            ot = _dot(vt, p.astype(BF16))
            rows.append(ot[hk * HEAD_DIM:(hk + 1) * HEAD_DIM, :] / den)
        at = jnp.concatenate(rows, axis=0)
        inv = lax.rsqrt(jnp.sum(at * at, axis=0, keepdims=True) * (1.0 / ATTN_WIDTH) + EPS)
        at_ref[:, j * w:(j + 1) * w] = at * inv * gnt_ref[...]


def _attn_prompt(q, kv, sinks, gn, batch, seq):
    qrows = ATTN_BLOCKS_PER_STEP * WINDOW
    nsteps = seq // qrows
    gnt = jnp.broadcast_to(gn.reshape(ATTN_WIDTH, 1), (ATTN_WIDTH, WINDOW))
    return pl.pallas_call(
        _attn_prompt_kernel,
        grid=(batch, nsteps),
        in_specs=[pl.BlockSpec(memory_space=pltpu.SMEM),
                  pl.BlockSpec((qrows, ATTN_WIDTH), lambda b, i: (b * nsteps + i, 0)),
                  pl.BlockSpec((seq, 2 * KV_WIDTH), lambda b, i: (b, 0)),
                  pl.BlockSpec(gnt.shape, lambda b, i: (0, 0))],
        out_specs=pl.BlockSpec((ATTN_WIDTH, qrows), lambda b, i: (0, b * nsteps + i)),
        out_shape=jax.ShapeDtypeStruct((ATTN_WIDTH, batch * seq), F32),
        compiler_params=pltpu.CompilerParams(dimension_semantics=("arbitrary", "arbitrary"),
                                             vmem_limit_bytes=VMEM_LIMIT_SMALL),
        name="attn_prompt",
    )(sinks, q, kv, gnt)


def _attn_sample_kernel(sink_ref, q_ref, kvn_ref, ck_ref, cv_ref, gn_ref, a_ref, ko_ref, vo_ref, *, dec_len):
    ns = ATTN_SEQ_GROUP
    rows = ns * dec_len
    wb = ck_ref.shape[1]
    m_rows = GQA_GROUP * rows
    r = lax.broadcasted_iota(jnp.int32, (m_rows, ns * wb), 0)
    c = lax.broadcasted_iota(jnp.int32, (m_rows, ns * wb), 1)
    r_seq, r_pos = (r % rows) // dec_len, r % dec_len
    mask_cache = (r_seq == c // wb) & (c % wb >= r_pos)
    r = lax.broadcasted_iota(jnp.int32, (m_rows, rows), 0)
    c = lax.broadcasted_iota(jnp.int32, (m_rows, rows), 1)
    mask_new = ((r % rows) // dec_len == c // dec_len) & (c % dec_len <= r % dec_len)
    row_head = lax.broadcasted_iota(jnp.int32, (m_rows, 1), 0) // rows

    for grp in range(q_ref.shape[0] // rows):
        rs = slice(grp * rows, (grp + 1) * rows)
        ss = slice(grp * ns, (grp + 1) * ns)
        q = q_ref[rs, :]
        kvn = kvn_ref[rs, :]
        ck = ck_ref[ss].reshape(ns * wb, KV_WIDTH)
        cv = cv_ref[ss].reshape(ns * wb, KV_WIDTH)
        kvn_b, ck_b, cv_b = kvn.astype(BF16), ck.astype(BF16), cv.astype(BF16)
        for hk in range(N_KV_HEADS):
            kcols = slice(hk * HEAD_DIM, (hk + 1) * HEAD_DIM)
            vcols = slice(KV_WIDTH + hk * HEAD_DIM, KV_WIDTH + (hk + 1) * HEAD_DIM)
            heads = [hk * GQA_GROUP + g for g in range(GQA_GROUP)]
            qs = jnp.concatenate([q[:, h * HEAD_DIM:(h + 1) * HEAD_DIM] for h in heads], axis=0).astype(BF16)
            sink = jnp.zeros((m_rows, 1), F32)
            for g, h in enumerate(heads):
                sink = jnp.where(row_head == g, sink_ref[h], sink)
            scores = [_dot_nt(qs, ck_b[:, kcols]), _dot_nt(qs, kvn_b[:, kcols])]
            out = _softmax_sink_pv(scores, [mask_cache, mask_new], [cv_b[:, kcols], kvn_b[:, vcols]], sink)
            for g, h in enumerate(heads):
                a_ref[rs, h * HEAD_DIM:(h + 1) * HEAD_DIM] = out[g * rows:(g + 1) * rows, :]
        a_ref[rs, :] = _rms(a_ref[rs, :], gn_ref[...])
        for n in range(ns):
            s = grp * ns + n
            ts = slice(grp * rows + n * dec_len, grp * rows + (n + 1) * dec_len)
            ko_ref[s, 0:wb - dec_len, :] = ck_ref[s, dec_len:wb, :]
            vo_ref[s, 0:wb - dec_len, :] = cv_ref[s, dec_len:wb, :]
            ko_ref[s, wb - dec_len:wb, :] = kvn_ref[ts, 0:KV_WIDTH]
            vo_ref[s, wb - dec_len:wb, :] = kvn_ref[ts, KV_WIDTH:2 * KV_WIDTH]


def _attn_sample(q, kv, cache_k, cache_v, sinks, gn, dec_len, groups_per_step=2):
    nseq, wb, _ = cache_k.shape
    sps = ATTN_SEQ_GROUP * groups_per_step
    rows = sps * dec_len
    cache_spec = pl.BlockSpec((sps, wb, KV_WIDTH), lambda i: (i, 0, 0))
    return pl.pallas_call(
        functools.partial(_attn_sample_kernel, dec_len=dec_len),
        grid=(nseq // sps,),
        in_specs=[pl.BlockSpec(memory_space=pltpu.SMEM),
                  _row_spec(rows, ATTN_WIDTH), _row_spec(rows, 2 * KV_WIDTH),
                  cache_spec, cache_spec, pl.BlockSpec(gn.shape, lambda i: (0, 0))],
        out_specs=[_row_spec(rows, ATTN_WIDTH), cache_spec, cache_spec],
        out_shape=[jax.ShapeDtypeStruct((nseq * dec_len, ATTN_WIDTH), F32),
                   jax.ShapeDtypeStruct(cache_k.shape, F32), jax.ShapeDtypeStruct(cache_v.shape, F32)],
        compiler_params=pltpu.CompilerParams(dimension_semantics=("arbitrary",),
                                             vmem_limit_bytes=VMEM_LIMIT_SMALL),
        name="attn_sample",
    )(sinks, q, kv, cache_k, cache_v, gn)


def _hgrn_kernel(*refs, nseq, carry):
    if carry:
        hp_ref, hg_ref, lbl_ref, gn_ref, o_ref, s_out = refs
        s_in = s_out

        @pl.when(pl.program_id(0) == 0)
        def _():
            s_out[...] = jnp.zeros(s_out.shape, F32)
    else:
        hp_ref, hg_ref, lbl_ref, gn_ref, s_in, o_ref, s_out = refs

    ch = HG_TILE
    ls = ch // nseq
    r = lax.broadcasted_iota(jnp.int32, (ch, ch), 0)
    c = lax.broadcasted_iota(jnp.int32, (ch, ch), 1)
    same = (r // ls) == (c // ls)
    causal = same & (c <= r)
    cum_lhs = jnp.concatenate([causal, same & (c % ls < ls // 2), same], axis=0).astype(F32).astype(BF16)
    sr = lax.broadcasted_iota(jnp.int32, (ch, nseq * HG_DV), 0)
    sc = lax.broadcasted_iota(jnp.int32, (ch, nseq * HG_DV), 1)
    seg_sel = ((sr // ls) == (sc // HG_DV)).astype(F32).astype(BF16)

    lbl = lbl_ref[...]
    lmax = jnp.max(lbl, axis=0, keepdims=True)
    e = jnp.exp(lbl - lmax)
    lb = e[0:1, :] / jnp.sum(e, axis=0, keepdims=True)
    gn = gn_ref[...]

    n_outer, n_inner = hp_ref.shape[0], hp_ref.shape[1]
    for to in range(n_outer):
        for ti in range(n_inner):
            hp = hp_ref[to, ti]
            qa = _silu(hp[:, 0:HG_WIDTH])
            f = lb + (1.0 - lb) * jax.nn.sigmoid(hp[:, HG_WIDTH:2 * HG_WIDTH])
            v = hp[:, 2 * HG_WIDTH:3 * HG_WIDTH].astype(BF16)
            kx = 1.0 - f
            parts = _split3(jnp.log(f))
            gs = sum(_dot(cum_lhs, p) for p in parts)
            g, g_mid, g_last = gs[0:ch], gs[ch:2 * ch], gs[2 * ch:3 * ch]
            dcol = sum(_dot_tn(p, seg_sel) for p in parts)
            qt = (qa * jnp.exp(g - g_mid)).astype(BF16)
            kt = (kx * jnp.exp(g_mid - g)).astype(BF16)
            qg = (qa * jnp.exp(g)).astype(BF16)
            kd = (kx * jnp.exp(g_last - g)).astype(BF16)
            gate = _silu(hg_ref[to, ti])
            for h in range(HG_HEADS):
                cs = slice(h * HG_DK, (h + 1) * HG_DK)
                a = jnp.where(causal, _dot_nt(qt[:, cs], kt[:, cs]), 0.0)
                o = _dot(a.astype(BF16), v[:, cs])
                inter = []
                for n in range(nseq):
                    rs = slice(n * ls, (n + 1) * ls)
                    sidx = (to * n_inner + ti) * nseq + n
                    s = s_in[sidx, h]
                    inter.append(_dot(qg[rs, cs], s.astype(BF16)))
                    decay = jnp.exp(dcol[cs, n * HG_DV:(n + 1) * HG_DV])
                    s_out[sidx, h] = s * decay + _dot_tn(kd[rs, cs], v[rs, cs])
                o = o + (inter[0] if nseq == 1 else jnp.concatenate(inter, axis=0))
                o_ref[to, ti, :, cs] = _rms(o, gn) * gate[:, cs]


def _hgrn(hp, hg, lb_logits, gn, state, nseq):
    n_outer, n_inner = hp.shape[0], hp.shape[1]
    carry = state is None
    tile_block = (n_outer, 1, HG_TILE, None)
    hp_spec = pl.BlockSpec((n_outer, 1, HG_TILE, hp.shape[3]), lambda i: (0, i, 0, 0))
    hg_spec = pl.BlockSpec((n_outer, 1, HG_TILE, HG_WIDTH), lambda i: (0, i, 0, 0))
    del tile_block
    small = [pl.BlockSpec(lb_logits.shape, lambda i: (0, 0)), pl.BlockSpec(gn.shape, lambda i: (0, 0))]
    nstate = n_outer * nseq
    if carry:
        s_shape = (nstate, HG_HEADS, HG_DK, HG_DV)
        s_spec = pl.BlockSpec(s_shape, lambda i: (0, 0, 0, 0))
        in_specs, args = [hp_spec, hg_spec] + small, (hp, hg, lb_logits, gn)
    else:
        s_shape = state.shape
        s_spec = pl.BlockSpec((nstate, HG_HEADS, HG_DK, HG_DV), lambda i: (i, 0, 0, 0))
        in_specs, args = [hp_spec, hg_spec] + small + [s_spec], (hp, hg, lb_logits, gn, state)
    return pl.pallas_call(
        functools.partial(_hgrn_kernel, nseq=nseq, carry=carry),
        grid=(n_inner,),
        in_specs=in_specs,
        out_specs=[hg_spec, s_spec],
        out_shape=[jax.ShapeDtypeStruct(hg.shape, F32), jax.ShapeDtypeStruct(s_shape, F32)],
        compiler_params=pltpu.CompilerParams(dimension_semantics=("arbitrary",),
                                             vmem_limit_bytes=VMEM_LIMIT_SMALL),
        name="hgrn_prompt" if carry else "hgrn_sample",
    )(*args)


def kernel(x_prompt, x_sample, cache_k_win, cache_v_win, state_hgrn, w_in, b_in, attn_sinks, attn_out_norm,
           hg_lb_logits, hg_out_norm, w_out, ffn1_w_gu, ffn1_w_down, ffn2_w_gu, ffn2_w_down,
           norm_ffn1_pre, norm_ffn1_post, norm_mix_pre, norm_mix_post, norm_ffn2_pre, norm_ffn2_post):
    depth = w_in.shape[0]
    assert depth == 1, "single-layer trunk"
    batch, seq, _ = x_prompt.shape
    dec_batch, dec_len, _ = x_sample.shape
    wb = cache_k_win.shape[2]
    assert seq % WINDOW == 0 and wb == WINDOW and HG_TILE % dec_len == 0
    layer = 0
    row = lambda p: p[layer].reshape(1, -1).astype(F32)
    wgu1, wd1 = ffn1_w_gu[layer].astype(BF16), ffn1_w_down[layer].astype(BF16)
    wgu2, wd2 = ffn2_w_gu[layer].astype(BF16), ffn2_w_down[layer].astype(BF16)
    win, wout = w_in[layer].astype(BF16), w_out[layer].astype(BF16)
    sinks = attn_sinks[layer].astype(F32)
    lb_logits = hg_lb_logits.astype(F32)
    g_attn, g_hg = row(attn_out_norm), row(hg_out_norm)

    def front(x2d):
        return _ffn_in(x2d, row(norm_ffn1_pre), row(norm_ffn1_post), row(norm_mix_pre), wgu1, wd1, win, row(b_in))

    def back(x1, a, o, a_transposed):
        return _out_ffn(x1, a, o, row(norm_mix_post), row(norm_ffn2_pre), row(norm_ffn2_post), wout, wgu2, wd2,
                        a_transposed)

    t_p = batch * seq
    x1, q, kv, hp, hg = front(x_prompt.reshape(t_p, D_MODEL))
    a = _attn_prompt(q, kv, sinks, g_attn, batch, seq)
    nch = seq // HG_TILE
    o, s_prompt = _hgrn(hp.reshape(batch, nch, HG_TILE, 3 * HG_WIDTH), hg.reshape(batch, nch, HG_TILE, HG_WIDTH),
                        lb_logits, g_hg, None, nseq=1)
    y_prompt = back(x1, a, o.reshape(t_p, HG_WIDTH), True).reshape(batch, seq, D_MODEL)
    kv_last = kv.reshape(batch, seq, 2 * KV_WIDTH)[:, seq - WINDOW:]
    k_prompt = kv_last[..., :KV_WIDTH].reshape(1, batch, WINDOW, N_KV_HEADS, HEAD_DIM)
    v_prompt = kv_last[..., KV_WIDTH:].reshape(1, batch, WINDOW, N_KV_HEADS, HEAD_DIM)

    t_s = dec_batch * dec_len
    x1, q, kv, hp, hg = front(x_sample.reshape(t_s, D_MODEL))
    a, k_s, v_s = _attn_sample(q, kv, cache_k_win[layer].reshape(dec_batch, wb, KV_WIDTH),
                               cache_v_win[layer].reshape(dec_batch, wb, KV_WIDTH), sinks, g_attn, dec_len)
    ntile = t_s // HG_TILE
    o, s_sample = _hgrn(hp.reshape(1, ntile, HG_TILE, 3 * HG_WIDTH), hg.reshape(1, ntile, HG_TILE, HG_WIDTH),
                        lb_logits, g_hg, state_hgrn[layer].astype(F32), nseq=HG_TILE // dec_len)
    y_sample = back(x1, a, o.reshape(t_s, HG_WIDTH), False).reshape(dec_batch, dec_len, D_MODEL)
    k_sample = k_s.reshape(1, dec_batch, wb, N_KV_HEADS, HEAD_DIM)
    v_sample = v_s.reshape(1, dec_batch, wb, N_KV_HEADS, HEAD_DIM)

    return (y_prompt, y_sample, k_prompt, v_prompt, s_prompt[None], k_sample, v_sample, s_sample[None])
```

```python
import functools

import jax
import jax.numpy as jnp
from jax import lax
from jax.experimental import pallas as pl
from jax.experimental.pallas import tpu as pltpu

F32 = jnp.float32
BF16 = jnp.bfloat16

D_MODEL = 1024
N_HEADS = 8
N_KV_HEADS = 2
HEAD_DIM = 64
GQA_GROUP = N_HEADS // N_KV_HEADS
WINDOW = 128
ATTN_WIDTH = N_HEADS * HEAD_DIM
KV_WIDTH = N_KV_HEADS * HEAD_DIM
SCALE = HEAD_DIM ** -0.5
HG_HEADS = 4
HG_DK = 128
HG_DV = 128
HG_WIDTH = HG_HEADS * HG_DV
D_FF = 2816
EPS = 1e-6

Q_COLS = (0, ATTN_WIDTH)
KV_COLS = (ATTN_WIDTH, ATTN_WIDTH + 2 * KV_WIDTH)
HP_COLS = (KV_COLS[1], KV_COLS[1] + 3 * HG_WIDTH)
HG_COLS = (HP_COLS[1], HP_COLS[1] + HG_WIDTH)

VMEM_LIMIT_BIG = 58 * 1024 * 1024
VMEM_LIMIT_SMALL = 40 * 1024 * 1024
FF_CHUNKS = ((0, 768), (768, 1536), (1536, 2304), (2304, 2816))
TOKEN_TILE = 512
HG_TILE = 64
HG_CHUNKS_PER_STEP = 2
ATTN_SEQ_GROUP = 4
ATTN_BLOCKS_PER_STEP = 2
NEG = -1e30
LOG2E = 1.4426950408889634


def _rms(x, g):
    return x * lax.rsqrt(jnp.mean(x * x, axis=-1, keepdims=True) + EPS) * g


def _silu(x):
    return x * jax.nn.sigmoid(x)


def _dot(a, b):
    return jnp.dot(a, b, preferred_element_type=F32)


def _dot_nt(a, b):
    return lax.dot_general(a, b, (((1,), (1,)), ((), ())), preferred_element_type=F32)


def _dot_tn(a, b):
    return lax.dot_general(a, b, (((0,), (0,)), ((), ())), preferred_element_type=F32)


def _sigmoid_t(x):
    return 0.5 * jnp.tanh(0.5 * x) + 0.5


def _silu_t(x):
    return x * _sigmoid_t(x)


def _split2(x):
    a = x.astype(BF16)
    b = (x - a.astype(F32)).astype(BF16)
    return a, b


def _swiglu_ffn(x, g_pre, wgu_ref, wd_ref, g_post):
    h = _rms(x, g_pre).astype(BF16)
    acc = None
    for lo, hi in FF_CHUNKS:
        g = _dot(h, wgu_ref[:, lo:hi])
        u = _dot(h, wgu_ref[:, D_FF + lo:D_FF + hi])
        part = _dot((_silu(g) * u).astype(BF16), wd_ref[lo:hi, :])
        acc = part if acc is None else acc + part
    return x + 0.5 * _rms(acc, g_post)


def _ffn_in_kernel(x_ref, g_pre, g_post, g_mix, wgu_ref, wd_ref, win_ref, bin_ref,
                   x1_ref, q_ref, kv_ref, hp_ref, hg_ref):
    x1 = _swiglu_ffn(x_ref[...], g_pre[...], wgu_ref, wd_ref, g_post[...])
    x1_ref[...] = x1
    h = _rms(x1, g_mix[...]).astype(BF16)
    for ref, (lo, hi) in ((q_ref, Q_COLS), (kv_ref, KV_COLS), (hp_ref, HP_COLS), (hg_ref, HG_COLS)):
        ref[...] = _dot(h, win_ref[:, lo:hi]) + bin_ref[:, lo:hi]


def _const_spec(shape):
    zeros = (0,) * len(shape)
    return pl.BlockSpec(shape, lambda *_: zeros, pipeline_mode=pl.Buffered(1))


def _row_spec(tile, width):
    return pl.BlockSpec((tile, width), lambda i: (i, 0))


def _ffn_in(x, g_pre, g_post, g_mix, wgu, wd, win, b_in):
    t = x.shape[0]
    tile = min(TOKEN_TILE, t)
    widths = [D_MODEL] + [hi - lo for lo, hi in (Q_COLS, KV_COLS, HP_COLS, HG_COLS)]
    return pl.pallas_call(
        _ffn_in_kernel,
        grid=(t // tile,),
        in_specs=[_row_spec(tile, D_MODEL), _const_spec(g_pre.shape), _const_spec(g_post.shape),
                  _const_spec(g_mix.shape), _const_spec(wgu.shape), _const_spec(wd.shape),
                  _const_spec(win.shape), _const_spec(b_in.shape)],
        out_specs=[_row_spec(tile, w) for w in widths],
        out_shape=[jax.ShapeDtypeStruct((t, w), F32) for w in widths],
        compiler_params=pltpu.CompilerParams(dimension_semantics=("arbitrary",),
                                             vmem_limit_bytes=VMEM_LIMIT_BIG),
        name="ffn_in",
    )(x, g_pre, g_post, g_mix, wgu, wd, win, b_in)


def _out_ffn_kernel(x1_ref, a_ref, o_ref, g_mixpost, g_pre, g_post, wout_ref, wgu_ref, wd_ref, y_ref,
                    *, a_transposed):
    a_dot = _dot_tn if a_transposed else _dot
    mix = (a_dot(a_ref[...].astype(BF16), wout_ref[0:ATTN_WIDTH, :])
           + _dot(o_ref[...].astype(BF16), wout_ref[ATTN_WIDTH:ATTN_WIDTH + HG_WIDTH, :]))
    x2 = x1_ref[...] + _rms(mix, g_mixpost[...])
    y_ref[...] = _swiglu_ffn(x2, g_pre[...], wgu_ref, wd_ref, g_post[...])


def _out_ffn(x1, a, o, g_mixpost, g_pre, g_post, wout, wgu, wd, a_transposed):
    t = x1.shape[0]
    tile = min(TOKEN_TILE, t)
    a_spec = pl.BlockSpec((ATTN_WIDTH, tile), lambda i: (0, i)) if a_transposed else _row_spec(tile, ATTN_WIDTH)
    return pl.pallas_call(
        functools.partial(_out_ffn_kernel, a_transposed=a_transposed),
        grid=(t // tile,),
        in_specs=[_row_spec(tile, D_MODEL), a_spec, _row_spec(tile, HG_WIDTH),
                  _const_spec(g_mixpost.shape), _const_spec(g_pre.shape), _const_spec(g_post.shape),
                  _const_spec(wout.shape), _const_spec(wgu.shape), _const_spec(wd.shape)],
        out_specs=_row_spec(tile, D_MODEL),
        out_shape=jax.ShapeDtypeStruct((t, D_MODEL), F32),
        compiler_params=pltpu.CompilerParams(dimension_semantics=("arbitrary",),
                                             vmem_limit_bytes=VMEM_LIMIT_BIG),
        name="out_ffn",
    )(x1, a, o, g_mixpost, g_pre, g_post, wout, wgu, wd)


def _softmax_sink_pv(scores, masks, values, sink):
    scores = [jnp.where(mk, s * SCALE, NEG) for s, mk in zip(scores, masks)]
    m = sink
    for s in scores:
        m = jnp.maximum(m, jnp.max(s, axis=-1, keepdims=True))
    den = jnp.exp(sink - m)
    out = None
    for s, v in zip(scores, values):
        p = jnp.exp(s - m)
        den = den + jnp.sum(p, axis=-1, keepdims=True)
        pv = _dot(p.astype(BF16), v)
        out = pv if out is None else out + pv
    return out / den


def _attn_prompt_kernel(sink_ref, q_ref, kv_ref, gnt_ref, at_ref):
    i = pl.program_id(1)
    w = WINDOW
    nkeys = 2 * w
    r = lax.broadcasted_iota(jnp.int32, (nkeys, w), 0)
    c = lax.broadcasted_iota(jnp.int32, (nkeys, w), 1)
    first_half = lax.broadcasted_iota(jnp.int32, (w, 2 * HEAD_DIM), 1) < HEAD_DIM
    for j in range(ATTN_BLOCKS_PER_STEP):
        q0 = (i * ATTN_BLOCKS_PER_STEP + j) * w
        k0 = jnp.maximum(q0 - w, 0)
        dist = c + (q0 - k0) - r
        bias = jnp.where((dist >= 0) & (dist <= WINDOW), 0.0, NEG)
        kvb = kv_ref[pl.ds(pl.multiple_of(k0, w), nkeys), :]
        kk = kvb[:, 0:KV_WIDTH]
        k_nat = kk.astype(BF16)
        k_swp = pltpu.roll(kk, HEAD_DIM, axis=1).astype(BF16)
        vt = kvb[:, KV_WIDTH:2 * KV_WIDTH].T.astype(BF16)
        rows = []
        for h in range(N_HEADS):
            hk, odd = h // GQA_GROUP, h % 2
            qp = q_ref[j * w:(j + 1) * w, (h // 2) * 2 * HEAD_DIM:(h // 2 + 1) * 2 * HEAD_DIM]
            qp = jnp.where(first_half != bool(odd), qp, 0.0).astype(BF16)
            keys = k_nat if hk == odd else k_swp
            st = _dot_nt(keys, qp) * (SCALE * LOG2E) + bias
            sink2 = sink_ref[h] * LOG2E
            m = jnp.maximum(jnp.max(st, axis=0, keepdims=True), sink2)
            p = jnp.exp2(st - m)
            den = jnp.sum(p, axis=0, keepdims=True) + jnp.exp2(sink2 - m)
            ot = _dot(vt, p.astype(BF16))
            rows.append(ot[hk * HEAD_DIM:(hk + 1) * HEAD_DIM, :] / den)
        at = jnp.concatenate(rows, axis=0)
        inv = lax.rsqrt(jnp.sum(at * at, axis=0, keepdims=True) * (1.0 / ATTN_WIDTH) + EPS)
        at_ref[:, j * w:(j + 1) * w] = at * inv * gnt_ref[...]


def _attn_prompt(q, kv, sinks, gn, batch, seq):
    qrows = ATTN_BLOCKS_PER_STEP * WINDOW
    nsteps = seq // qrows
    gnt = jnp.broadcast_to(gn.reshape(ATTN_WIDTH, 1), (ATTN_WIDTH, WINDOW))
    return pl.pallas_call(
        _attn_prompt_kernel,
        grid=(batch, nsteps),
        in_specs=[pl.BlockSpec(memory_space=pltpu.SMEM),
                  pl.BlockSpec((qrows, ATTN_WIDTH), lambda b, i: (b * nsteps + i, 0)),
                  pl.BlockSpec((seq, 2 * KV_WIDTH), lambda b, i: (b, 0)),
                  pl.BlockSpec(gnt.shape, lambda b, i: (0, 0))],
        out_specs=pl.BlockSpec((ATTN_WIDTH, qrows), lambda b, i: (0, b * nsteps + i)),
        out_shape=jax.ShapeDtypeStruct((ATTN_WIDTH, batch * seq), F32),
        compiler_params=pltpu.CompilerParams(dimension_semantics=("arbitrary", "arbitrary"),
                                             vmem_limit_bytes=VMEM_LIMIT_SMALL),
        name="attn_prompt",
    )(sinks, q, kv, gnt)


def _attn_sample_kernel(sink_ref, q_ref, kvn_ref, ck_ref, cv_ref, gn_ref, a_ref, ko_ref, vo_ref, *, dec_len):
    ns = ATTN_SEQ_GROUP
    rows = ns * dec_len
    wb = ck_ref.shape[1]
    m_rows = GQA_GROUP * rows
    r = lax.broadcasted_iota(jnp.int32, (m_rows, ns * wb), 0)
    c = lax.broadcasted_iota(jnp.int32, (m_rows, ns * wb), 1)
    r_seq, r_pos = (r % rows) // dec_len, r % dec_len
    mask_cache = (r_seq == c // wb) & (c % wb >= r_pos)
    r = lax.broadcasted_iota(jnp.int32, (m_rows, rows), 0)
    c = lax.broadcasted_iota(jnp.int32, (m_rows, rows), 1)
    mask_new = ((r % rows) // dec_len == c // dec_len) & (c % dec_len <= r % dec_len)
    row_head = lax.broadcasted_iota(jnp.int32, (m_rows, 1), 0) // rows

    for grp in range(q_ref.shape[0] // rows):
        rs = slice(grp * rows, (grp + 1) * rows)
        ss = slice(grp * ns, (grp + 1) * ns)
        q = q_ref[rs, :]
        kvn = kvn_ref[rs, :]
        ck = ck_ref[ss].reshape(ns * wb, KV_WIDTH)
        cv = cv_ref[ss].reshape(ns * wb, KV_WIDTH)
        kvn_b, ck_b, cv_b = kvn.astype(BF16), ck.astype(BF16), cv.astype(BF16)
        for hk in range(N_KV_HEADS):
            kcols = slice(hk * HEAD_DIM, (hk + 1) * HEAD_DIM)
            vcols = slice(KV_WIDTH + hk * HEAD_DIM, KV_WIDTH + (hk + 1) * HEAD_DIM)
            heads = [hk * GQA_GROUP + g for g in range(GQA_GROUP)]
            qs = jnp.concatenate([q[:, h * HEAD_DIM:(h + 1) * HEAD_DIM] for h in heads], axis=0).astype(BF16)
            sink = jnp.zeros((m_rows, 1), F32)
            for g, h in enumerate(heads):
                sink = jnp.where(row_head == g, sink_ref[h], sink)
            scores = [_dot_nt(qs, ck_b[:, kcols]), _dot_nt(qs, kvn_b[:, kcols])]
            out = _softmax_sink_pv(scores, [mask_cache, mask_new], [cv_b[:, kcols], kvn_b[:, vcols]], sink)
            for g, h in enumerate(heads):
                a_ref[rs, h * HEAD_DIM:(h + 1) * HEAD_DIM] = out[g * rows:(g + 1) * rows, :]
        a_ref[rs, :] = _rms(a_ref[rs, :], gn_ref[...])
        for n in range(ns):
            s = grp * ns + n
            ts = slice(grp * rows + n * dec_len, grp * rows + (n + 1) * dec_len)
            ko_ref[s, 0:wb - dec_len, :] = ck_ref[s, dec_len:wb, :]
            vo_ref[s, 0:wb - dec_len, :] = cv_ref[s, dec_len:wb, :]
            ko_ref[s, wb - dec_len:wb, :] = kvn_ref[ts, 0:KV_WIDTH]
            vo_ref[s, wb - dec_len:wb, :] = kvn_ref[ts, KV_WIDTH:2 * KV_WIDTH]


def _attn_sample(q, kv, cache_k, cache_v, sinks, gn, dec_len, groups_per_step=2):
    nseq, wb, _ = cache_k.shape
    sps = ATTN_SEQ_GROUP * groups_per_step
    rows = sps * dec_len
    cache_spec = pl.BlockSpec((sps, wb, KV_WIDTH), lambda i: (i, 0, 0))
    return pl.pallas_call(
        functools.partial(_attn_sample_kernel, dec_len=dec_len),
        grid=(nseq // sps,),
        in_specs=[pl.BlockSpec(memory_space=pltpu.SMEM),
                  _row_spec(rows, ATTN_WIDTH), _row_spec(rows, 2 * KV_WIDTH),
                  cache_spec, cache_spec, pl.BlockSpec(gn.shape, lambda i: (0, 0))],
        out_specs=[_row_spec(rows, ATTN_WIDTH), cache_spec, cache_spec],
        out_shape=[jax.ShapeDtypeStruct((nseq * dec_len, ATTN_WIDTH), F32),
                   jax.ShapeDtypeStruct(cache_k.shape, F32), jax.ShapeDtypeStruct(cache_v.shape, F32)],
        compiler_params=pltpu.CompilerParams(dimension_semantics=("arbitrary",),
                                             vmem_limit_bytes=VMEM_LIMIT_SMALL),
        name="attn_sample",
    )(sinks, q, kv, cache_k, cache_v, gn)


def _hgrn_kernel(*refs, nseq, carry):
    if carry:
        assert nseq == 1
        hp_ref, hg_ref, lbl_ref, gn_ref, o_ref, s_out = refs
        s_in = s_out

        @pl.when(pl.program_id(0) == 0)
        def _():
            s_out[...] = jnp.zeros(s_out.shape, F32)
    else:
        hp_ref, hg_ref, lbl_ref, gn_ref, s_in, o_ref, s_out = refs

    ch = HG_TILE
    ls = ch // nseq
    r = lax.broadcasted_iota(jnp.int32, (ch, ch), 0)
    c = lax.broadcasted_iota(jnp.int32, (ch, ch), 1)
    same = (r // ls) == (c // ls)
    causal = same & (c <= r)
    if carry:
        cum_lhs = causal.astype(F32).astype(BF16)
    else:
        cum_lhs = jnp.concatenate([causal, same & (c % ls < ls // 2), same], axis=0).astype(F32).astype(BF16)
        sr = lax.broadcasted_iota(jnp.int32, (ch, nseq * HG_DV), 0)
        sc = lax.broadcasted_iota(jnp.int32, (ch, nseq * HG_DV), 1)
        seg_sel = ((sr // ls) == (sc // HG_DV)).astype(F32).astype(BF16)

    lbl = lbl_ref[...]
    lmax = jnp.max(lbl, axis=0, keepdims=True)
    e = jnp.exp(lbl - lmax)
    lb = e[0:1, :] / jnp.sum(e, axis=0, keepdims=True)
    gn = gn_ref[...]

    n_outer, n_inner = hp_ref.shape[0], hp_ref.shape[1]
    for to in range(n_outer):
        for ti in range(n_inner):
            hp = hp_ref[to, ti]
            qa = _silu_t(hp[:, 0:HG_WIDTH])
            f = lb + (1.0 - lb) * _sigmoid_t(hp[:, HG_WIDTH:2 * HG_WIDTH])
            v = hp[:, 2 * HG_WIDTH:3 * HG_WIDTH].astype(BF16)
            kx = 1.0 - f
            parts = _split2(jnp.log(f))
            gs = sum(_dot(cum_lhs, p) for p in parts)
            if carry:
                g, g_mid, g_last = gs, gs[ls // 2 - 1:ls // 2, :], gs[ls - 1:ls, :]
            else:
                g, g_mid, g_last = gs[0:ch], gs[ch:2 * ch], gs[2 * ch:3 * ch]
                dcol = sum(_dot_tn(p, seg_sel) for p in parts)
            e_q = jnp.exp(g - g_mid)
            e_k = jnp.exp(g_mid - g)
            qt, kt = qa * e_q, kx * e_k
            qg = (qt * jnp.exp(g_mid)).astype(BF16)
            kd = (kt * jnp.exp(g_last - g_mid)).astype(BF16)
            qt, kt = qt.astype(BF16), kt.astype(BF16)
            gate = _silu_t(hg_ref[to, ti])
            for h in range(HG_HEADS):
                cs = slice(h * HG_DK, (h + 1) * HG_DK)
                a = jnp.where(causal, _dot_nt(qt[:, cs], kt[:, cs]), 0.0)
                o = _dot(a.astype(BF16), v[:, cs])
                if carry:
                    st = s_in[to, h]
                    o = o + _dot_nt(qg[:, cs], st.astype(BF16))
                    s_out[to, h] = st * jnp.exp(g_last[:, cs]) + _dot_tn(v[:, cs], kd[:, cs])
                else:
                    inter = []
                    for n in range(nseq):
                        rs = slice(n * ls, (n + 1) * ls)
                        sidx = (to * n_inner + ti) * nseq + n
                        s = s_in[sidx, h]
                        inter.append(_dot(qg[rs, cs], s.astype(BF16)))
                        decay = jnp.exp(dcol[cs, n * HG_DV:(n + 1) * HG_DV])
                        s_out[sidx, h] = s * decay + _dot_tn(kd[rs, cs], v[rs, cs])
                    o = o + jnp.concatenate(inter, axis=0)
                o_ref[to, ti, :, cs] = _rms(o, gn) * gate[:, cs]

    if carry:
        @pl.when(pl.program_id(0) == pl.num_programs(0) - 1)
        def _():
            for to in range(n_outer):
                for h in range(HG_HEADS):
                    s_out[to, h] = s_out[to, h].T


def _hgrn(hp, hg, lb_logits, gn, state, nseq):
    n_outer, n_inner = hp.shape[0], hp.shape[1]
    carry = state is None
    tps = HG_CHUNKS_PER_STEP if carry else 1
    hp_spec = pl.BlockSpec((n_outer, tps, HG_TILE, hp.shape[3]), lambda i: (0, i, 0, 0))
    hg_spec = pl.BlockSpec((n_outer, tps, HG_TILE, HG_WIDTH), lambda i: (0, i, 0, 0))
    small = [pl.BlockSpec(lb_logits.shape, lambda i: (0, 0)), pl.BlockSpec(gn.shape, lambda i: (0, 0))]
    nstate = n_outer * nseq
    if carry:
        s_shape = (nstate, HG_HEADS, HG_DK, HG_DV)
        s_spec = pl.BlockSpec(s_shape, lambda i: (0, 0, 0, 0))
        in_specs, args = [hp_spec, hg_spec] + small, (hp, hg, lb_logits, gn)
    else:
        s_shape = state.shape
        s_spec = pl.BlockSpec((nstate, HG_HEADS, HG_DK, HG_DV), lambda i: (i, 0, 0, 0))
        in_specs, args = [hp_spec, hg_spec] + small + [s_spec], (hp, hg, lb_logits, gn, state)
    return pl.pallas_call(
        functools.partial(_hgrn_kernel, nseq=nseq, carry=carry),
        grid=(n_inner // tps,),
        in_specs=in_specs,
        out_specs=[hg_spec, s_spec],
        out_shape=[jax.ShapeDtypeStruct(hg.shape, F32), jax.ShapeDtypeStruct(s_shape, F32)],
        compiler_params=pltpu.CompilerParams(dimension_semantics=("arbitrary",),
                                             vmem_limit_bytes=VMEM_LIMIT_SMALL),
        name="hgrn_prompt" if carry else "hgrn_sample",
    )(*args)


def kernel(x_prompt, x_sample, cache_k_win, cache_v_win, state_hgrn, w_in, b_in, attn_sinks, attn_out_norm,
           hg_lb_logits, hg_out_norm, w_out, ffn1_w_gu, ffn1_w_down, ffn2_w_gu, ffn2_w_down,
           norm_ffn1_pre, norm_ffn1_post, norm_mix_pre, norm_mix_post, norm_ffn2_pre, norm_ffn2_post):
    depth = w_in.shape[0]
    assert depth == 1, "single-layer trunk"
    batch, seq, _ = x_prompt.shape
    dec_batch, dec_len, _ = x_sample.shape
    wb = cache_k_win.shape[2]
    assert seq % WINDOW == 0 and wb == WINDOW and HG_TILE % dec_len == 0
    layer = 0
    row = lambda p: p[layer].reshape(1, -1).astype(F32)
    wgu1, wd1 = ffn1_w_gu[layer].astype(BF16), ffn1_w_down[layer].astype(BF16)
    wgu2, wd2 = ffn2_w_gu[layer].astype(BF16), ffn2_w_down[layer].astype(BF16)
    win, wout = w_in[layer].astype(BF16), w_out[layer].astype(BF16)
    sinks = attn_sinks[layer].astype(F32)
    lb_logits = hg_lb_logits.astype(F32)
    g_attn, g_hg = row(attn_out_norm), row(hg_out_norm)

    def front(x2d):
        return _ffn_in(x2d, row(norm_ffn1_pre), row(norm_ffn1_post), row(norm_mix_pre), wgu1, wd1, win, row(b_in))

    def back(x1, a, o, a_transposed):
        return _out_ffn(x1, a, o, row(norm_mix_post), row(norm_ffn2_pre), row(norm_ffn2_post), wout, wgu2, wd2,
                        a_transposed)

    t_p = batch * seq
    x1, q, kv, hp, hg = front(x_prompt.reshape(t_p, D_MODEL))
    a = _attn_prompt(q, kv, sinks, g_attn, batch, seq)
    nch = seq // HG_TILE
    o, s_prompt = _hgrn(hp.reshape(batch, nch, HG_TILE, 3 * HG_WIDTH), hg.reshape(batch, nch, HG_TILE, HG_WIDTH),
                        lb_logits, g_hg, None, nseq=1)
    y_prompt = back(x1, a, o.reshape(t_p, HG_WIDTH), True).reshape(batch, seq, D_MODEL)
    kv_last = kv.reshape(batch, seq, 2 * KV_WIDTH)[:, seq - WINDOW:]
    k_prompt = kv_last[..., :KV_WIDTH].reshape(1, batch, WINDOW, N_KV_HEADS, HEAD_DIM)
    v_prompt = kv_last[..., KV_WIDTH:].reshape(1, batch, WINDOW, N_KV_HEADS, HEAD_DIM)

    t_s = dec_batch * dec_len
    x1, q, kv, hp, hg = front(x_sample.reshape(t_s, D_MODEL))
    a, k_s, v_s = _attn_sample(q, kv, cache_k_win[layer].reshape(dec_batch, wb, KV_WIDTH),
                               cache_v_win[layer].reshape(dec_batch, wb, KV_WIDTH), sinks, g_attn, dec_len)
    ntile = t_s // HG_TILE
    o, s_sample = _hgrn(hp.reshape(1, ntile, HG_TILE, 3 * HG_WIDTH), hg.reshape(1, ntile, HG_TILE, HG_WIDTH),
                        lb_logits, g_hg, state_hgrn[layer].astype(F32), nseq=HG_TILE // dec_len)
    y_sample = back(x1, a, o.reshape(t_s, HG_WIDTH), False).reshape(dec_batch, dec_len, D_MODEL)
    k_sample = k_s.reshape(1, dec_batch, wb, N_KV_HEADS, HEAD_DIM)
    v_sample = v_s.reshape(1, dec_batch, wb, N_KV_HEADS, HEAD_DIM)

    return (y_prompt, y_sample, k_prompt, v_prompt, s_prompt[None], k_sample, v_sample, s_sample[None])
```

```python
import functools

import jax
import jax.numpy as jnp
from jax import lax
from jax.experimental import pallas as pl
from jax.experimental.pallas import tpu as pltpu

F32 = jnp.float32
BF16 = jnp.bfloat16

D_MODEL = 1024
N_HEADS = 8
N_KV_HEADS = 2
HEAD_DIM = 64
GQA_GROUP = N_HEADS // N_KV_HEADS
WINDOW = 128
ATTN_WIDTH = N_HEADS * HEAD_DIM
KV_WIDTH = N_KV_HEADS * HEAD_DIM
SCALE = HEAD_DIM ** -0.5
HG_HEADS = 4
HG_DK = 128
HG_DV = 128
HG_WIDTH = HG_HEADS * HG_DV
D_FF = 2816
EPS = 1e-6

Q_COLS = (0, ATTN_WIDTH)
KV_COLS = (ATTN_WIDTH, ATTN_WIDTH + 2 * KV_WIDTH)
HP_COLS = (KV_COLS[1], KV_COLS[1] + 3 * HG_WIDTH)
HG_COLS = (HP_COLS[1], HP_COLS[1] + HG_WIDTH)

VMEM_LIMIT_BIG = 58 * 1024 * 1024
VMEM_LIMIT_SMALL = 40 * 1024 * 1024
FF_CHUNKS = ((0, 512), (512, 1024), (1024, 1536), (1536, 2048), (2048, 2560), (2560, 2816))
TOKEN_TILE = 512
HG_TILE = 64
ATTN_SEQ_GROUP = 4
NEG = -1e30
LOG2E = 1.4426950408889634


def _rms(x, g):
    return x * lax.rsqrt(jnp.mean(x * x, axis=-1, keepdims=True) + EPS) * g


def _silu(x):
    return x * jax.nn.sigmoid(x)


def _silu_t(x):
    h = 0.5 * x
    return h * jnp.tanh(h) + h


def _dot(a, b):
    return jnp.dot(a, b, preferred_element_type=F32)


def _dot_nt(a, b):
    return lax.dot_general(a, b, (((1,), (1,)), ((), ())), preferred_element_type=F32)


def _dot_tn(a, b):
    return lax.dot_general(a, b, (((0,), (0,)), ((), ())), preferred_element_type=F32)


def _split2(x):
    a = x.astype(BF16)
    b = (x - a.astype(F32)).astype(BF16)
    return a, b


def _no_side_work():
    pass


def _swiglu_ffn(x, g_pre, wgu_ref, wd_ref, g_post, side=_no_side_work):
    h = _rms(x, g_pre).astype(BF16)
    acc = None
    for lo, hi in FF_CHUNKS:
        g = _dot(h, wgu_ref[:, lo:hi])
        side()
        u = _dot(h, wgu_ref[:, D_FF + lo:D_FF + hi])
        side()
        part = _dot((_silu(g) * u).astype(BF16), wd_ref[lo:hi, :])
        side()
        acc = part if acc is None else acc + part
    return x + 0.5 * _rms(acc, g_post)


def _const_spec(shape):
    zeros = (0,) * len(shape)
    return pl.BlockSpec(shape, lambda *_: zeros, pipeline_mode=pl.Buffered(1))


def _row_spec(tile, width):
    return pl.BlockSpec((tile, width), lambda i: (i, 0))


def _ffn_in_kernel(x_ref, g_pre, g_post, g_mix, wgu_ref, wd_ref, win_ref, bin_ref,
                   x1_ref, q_ref, kv_ref, hp_ref, hg_ref):
    x1 = _swiglu_ffn(x_ref[...], g_pre[...], wgu_ref, wd_ref, g_post[...])
    x1_ref[...] = x1
    h = _rms(x1, g_mix[...]).astype(BF16)
    for ref, (lo, hi) in ((q_ref, Q_COLS), (kv_ref, KV_COLS), (hp_ref, HP_COLS), (hg_ref, HG_COLS)):
        ref[...] = _dot(h, win_ref[:, lo:hi]) + bin_ref[:, lo:hi]


def _ffn_in(x, g_pre, g_post, g_mix, wgu, wd, win, b_in):
    t = x.shape[0]
    tile = min(TOKEN_TILE, t)
    widths = [D_MODEL] + [hi - lo for lo, hi in (Q_COLS, KV_COLS, HP_COLS, HG_COLS)]
    return pl.pallas_call(
        _ffn_in_kernel,
        grid=(t // tile,),
        in_specs=[_row_spec(tile, D_MODEL), _const_spec(g_pre.shape), _const_spec(g_post.shape),
                  _const_spec(g_mix.shape), _const_spec(wgu.shape), _const_spec(wd.shape),
                  _const_spec(win.shape), _const_spec(b_in.shape)],
        out_specs=[_row_spec(tile, w) for w in widths],
        out_shape=[jax.ShapeDtypeStruct((t, w), F32) for w in widths],
        compiler_params=pltpu.CompilerParams(dimension_semantics=("arbitrary",),
                                             vmem_limit_bytes=VMEM_LIMIT_BIG),
        name="ffn_in",
    )(x, g_pre, g_post, g_mix, wgu, wd, win, b_in)


def _mix_ffn(x1, mix, g_mixpost, g_pre, g_post, wgu_ref, wd_ref, side=_no_side_work):
    x2 = x1 + _rms(mix, g_mixpost)
    return _swiglu_ffn(x2, g_pre, wgu_ref, wd_ref, g_post, side)


def _out_ffn_kernel(x1_ref, a_ref, o_ref, g_mixpost, g_pre, g_post, wout_ref, wgu_ref, wd_ref, y_ref):
    mix = (_dot(a_ref[...].astype(BF16), wout_ref[0:ATTN_WIDTH, :])
           + _dot(o_ref[...].astype(BF16), wout_ref[ATTN_WIDTH:ATTN_WIDTH + HG_WIDTH, :]))
    y_ref[...] = _mix_ffn(x1_ref[...], mix, g_mixpost[...], g_pre[...], g_post[...], wgu_ref, wd_ref)


def _out_ffn(x1, a, o, g_mixpost, g_pre, g_post, wout, wgu, wd):
    t = x1.shape[0]
    tile = min(TOKEN_TILE, t)
    return pl.pallas_call(
        _out_ffn_kernel,
        grid=(t // tile,),
        in_specs=[_row_spec(tile, D_MODEL), _row_spec(tile, ATTN_WIDTH), _row_spec(tile, HG_WIDTH),
                  _const_spec(g_mixpost.shape), _const_spec(g_pre.shape), _const_spec(g_post.shape),
                  _const_spec(wout.shape), _const_spec(wgu.shape), _const_spec(wd.shape)],
        out_specs=_row_spec(tile, D_MODEL),
        out_shape=jax.ShapeDtypeStruct((t, D_MODEL), F32),
        compiler_params=pltpu.CompilerParams(dimension_semantics=("arbitrary",),
                                             vmem_limit_bytes=VMEM_LIMIT_BIG),
        name="out_ffn",
    )(x1, a, o, g_mixpost, g_pre, g_post, wout, wgu, wd)


def _attn_block_t(sink_ref, q, kv_ref, q0, gnt, store):
    w = WINDOW
    nkeys = 2 * w
    r = lax.broadcasted_iota(jnp.int32, (nkeys, w), 0)
    c = lax.broadcasted_iota(jnp.int32, (nkeys, w), 1)
    first_half = lax.broadcasted_iota(jnp.int32, (w, 2 * HEAD_DIM), 1) < HEAD_DIM
    k0 = jnp.maximum(q0 - w, 0)
    dist = c + (q0 - k0) - r
    bias = jnp.where((dist >= 0) & (dist <= WINDOW), 0.0, NEG)
    kvb = kv_ref[pl.ds(pl.multiple_of(k0, w), nkeys), :]
    kk = kvb[:, 0:KV_WIDTH]
    k_nat = kk.astype(BF16)
    k_swp = pltpu.roll(kk, HEAD_DIM, axis=1).astype(BF16)
    vt = kvb[:, KV_WIDTH:2 * KV_WIDTH].T.astype(BF16)
    def masked_q(h):
        qp = q[:, (h // 2) * 2 * HEAD_DIM:(h // 2 + 1) * 2 * HEAD_DIM]
        return jnp.where(first_half != bool(h % 2), qp, 0.0).astype(BF16)

    scores = [None] * N_HEADS
    for keys, use_swapped in ((k_nat, False), (k_swp, True)):
        heads = [h for h in range(N_HEADS) if (h // GQA_GROUP != h % 2) == use_swapped]
        st = _dot_nt(keys, jnp.concatenate([masked_q(h) for h in heads], axis=0))
        for i, h in enumerate(heads):
            scores[h] = st[:, i * w:(i + 1) * w]
    yield
    probs, dens = [], []
    for h in range(N_HEADS):
        st = scores[h] * (SCALE * LOG2E) + bias
        sink2 = sink_ref[h] * LOG2E
        m = jnp.maximum(jnp.max(st, axis=0, keepdims=True), sink2)
        p = jnp.exp2(st - m)
        dens.append(jnp.sum(p, axis=0, keepdims=True) + jnp.exp2(sink2 - m))
        probs.append(p.astype(BF16))
    ot = _dot(vt, jnp.concatenate(probs, axis=1))
    yield
    rows = []
    for h in range(N_HEADS):
        hk = h // GQA_GROUP
        rows.append(ot[hk * HEAD_DIM:(hk + 1) * HEAD_DIM, h * w:(h + 1) * w] / dens[h])
    at = jnp.concatenate(rows, axis=0)
    inv = lax.rsqrt(jnp.sum(at * at, axis=0, keepdims=True) * (1.0 / ATTN_WIDTH) + EPS)
    store(at * inv * gnt)


def _softmax_sink_pv(scores, masks, values, sink):
    scores = [jnp.where(mk, s * SCALE, NEG) for s, mk in zip(scores, masks)]
    m = sink
    for s in scores:
        m = jnp.maximum(m, jnp.max(s, axis=-1, keepdims=True))
    den = jnp.exp(sink - m)
    out = None
    for s, v in zip(scores, values):
        p = jnp.exp(s - m)
        den = den + jnp.sum(p, axis=-1, keepdims=True)
        pv = _dot(p.astype(BF16), v)
        out = pv if out is None else out + pv
    return out / den


def _attn_sample_kernel(sink_ref, q_ref, kvn_ref, ck_ref, cv_ref, gn_ref, a_ref, ko_ref, vo_ref, *, dec_len):
    ns = ATTN_SEQ_GROUP
    rows = ns * dec_len
    wb = ck_ref.shape[1]
    m_rows = GQA_GROUP * rows
    r = lax.broadcasted_iota(jnp.int32, (m_rows, ns * wb), 0)
    c = lax.broadcasted_iota(jnp.int32, (m_rows, ns * wb), 1)
    r_seq, r_pos = (r % rows) // dec_len, r % dec_len
    mask_cache = (r_seq == c // wb) & (c % wb >= r_pos)
    r = lax.broadcasted_iota(jnp.int32, (m_rows, rows), 0)
    c = lax.broadcasted_iota(jnp.int32, (m_rows, rows), 1)
    mask_new = ((r % rows) // dec_len == c // dec_len) & (c % dec_len <= r % dec_len)
    row_head = lax.broadcasted_iota(jnp.int32, (m_rows, 1), 0) // rows

    for grp in range(q_ref.shape[0] // rows):
        rs = slice(grp * rows, (grp + 1) * rows)
        ss = slice(grp * ns, (grp + 1) * ns)
        q = q_ref[rs, :]
        kvn = kvn_ref[rs, :]
        ck = ck_ref[ss].reshape(ns * wb, KV_WIDTH)
        cv = cv_ref[ss].reshape(ns * wb, KV_WIDTH)
        kvn_b, ck_b, cv_b = kvn.astype(BF16), ck.astype(BF16), cv.astype(BF16)
        for hk in range(N_KV_HEADS):
            kcols = slice(hk * HEAD_DIM, (hk + 1) * HEAD_DIM)
            vcols = slice(KV_WIDTH + hk * HEAD_DIM, KV_WIDTH + (hk + 1) * HEAD_DIM)
            heads = [hk * GQA_GROUP + g for g in range(GQA_GROUP)]
            qs = jnp.concatenate([q[:, h * HEAD_DIM:(h + 1) * HEAD_DIM] for h in heads], axis=0).astype(BF16)
            sink = jnp.zeros((m_rows, 1), F32)
            for g, h in enumerate(heads):
                sink = jnp.where(row_head == g, sink_ref[h], sink)
            scores = [_dot_nt(qs, ck_b[:, kcols]), _dot_nt(qs, kvn_b[:, kcols])]
            out = _softmax_sink_pv(scores, [mask_cache, mask_new], [cv_b[:, kcols], kvn_b[:, vcols]], sink)
            for g, h in enumerate(heads):
                a_ref[rs, h * HEAD_DIM:(h + 1) * HEAD_DIM] = out[g * rows:(g + 1) * rows, :]
        a_ref[rs, :] = _rms(a_ref[rs, :], gn_ref[...])
        for n in range(ns):
            s = grp * ns + n
            ts = slice(grp * rows + n * dec_len, grp * rows + (n + 1) * dec_len)
            ko_ref[s, 0:wb - dec_len, :] = ck_ref[s, dec_len:wb, :]
            vo_ref[s, 0:wb - dec_len, :] = cv_ref[s, dec_len:wb, :]
            ko_ref[s, wb - dec_len:wb, :] = kvn_ref[ts, 0:KV_WIDTH]
            vo_ref[s, wb - dec_len:wb, :] = kvn_ref[ts, KV_WIDTH:2 * KV_WIDTH]


def _attn_sample(q, kv, cache_k, cache_v, sinks, gn, dec_len, groups_per_step=2):
    nseq, wb, _ = cache_k.shape
    sps = ATTN_SEQ_GROUP * groups_per_step
    rows = sps * dec_len
    cache_spec = pl.BlockSpec((sps, wb, KV_WIDTH), lambda i: (i, 0, 0))
    return pl.pallas_call(
        functools.partial(_attn_sample_kernel, dec_len=dec_len),
        grid=(nseq // sps,),
        in_specs=[pl.BlockSpec(memory_space=pltpu.SMEM),
                  _row_spec(rows, ATTN_WIDTH), _row_spec(rows, 2 * KV_WIDTH),
                  cache_spec, cache_spec, pl.BlockSpec(gn.shape, lambda i: (0, 0))],
        out_specs=[_row_spec(rows, ATTN_WIDTH), cache_spec, cache_spec],
        out_shape=[jax.ShapeDtypeStruct((nseq * dec_len, ATTN_WIDTH), F32),
                   jax.ShapeDtypeStruct(cache_k.shape, F32), jax.ShapeDtypeStruct(cache_v.shape, F32)],
        compiler_params=pltpu.CompilerParams(dimension_semantics=("arbitrary",),
                                             vmem_limit_bytes=VMEM_LIMIT_SMALL),
        name="attn_sample",
    )(sinks, q, kv, cache_k, cache_v, gn)


def _hgrn_bound_consts(lbl_ref):
    lbl = lbl_ref[...]
    e = jnp.exp(lbl - jnp.max(lbl, axis=0, keepdims=True))
    lb = e[0:1, :] / jnp.sum(e, axis=0, keepdims=True)
    return 0.5 + 0.5 * lb, 0.5 - 0.5 * lb


def _hgrn_gates(hp, c0, c1):
    qa = _silu_t(hp[:, 0:HG_WIDTH])
    t = c1 * jnp.tanh(0.5 * hp[:, HG_WIDTH:2 * HG_WIDTH])
    f, kx = c0 + t, c1 - t
    v = hp[:, 2 * HG_WIDTH:3 * HG_WIDTH].astype(BF16)
    return qa, kx, _split2(jnp.log2(f)), v


def _hgrn_factors(qa, kx, g, g_mid, g_last):
    qt = qa * jnp.exp2(g - g_mid)
    kt = kx * jnp.exp2(g_mid - g)
    qg = (qt * jnp.exp2(g_mid)).astype(BF16)
    kd = (kt * jnp.exp2(g_last - g_mid)).astype(BF16)
    return qt.astype(BF16), kt.astype(BF16), qg, kd


def _segment_masks(nseq):
    ls = HG_TILE // nseq
    r = lax.broadcasted_iota(jnp.int32, (HG_TILE, HG_TILE), 0)
    c = lax.broadcasted_iota(jnp.int32, (HG_TILE, HG_TILE), 1)
    same = (r // ls) == (c // ls)
    return same, same & (c <= r), same & (c % ls < ls // 2)


def _as_bf16(mask):
    return mask.astype(F32).astype(BF16)


def _hgrn_sample_kernel(hp_ref, hg_ref, lbl_ref, gn_ref, s_in, o_ref, s_out, *, nseq):
    ch = HG_TILE
    ls = ch // nseq
    same, causal, first_half = _segment_masks(nseq)
    cum_lhs = _as_bf16(jnp.concatenate([causal, first_half, same], axis=0))
    sr = lax.broadcasted_iota(jnp.int32, (ch, nseq * HG_DV), 0)
    sc = lax.broadcasted_iota(jnp.int32, (ch, nseq * HG_DV), 1)
    seg_sel = _as_bf16((sr // ls) == (sc // HG_DV))
    c0, c1 = _hgrn_bound_consts(lbl_ref)
    gn = gn_ref[...]

    qa, kx, parts, v = _hgrn_gates(hp_ref[...], c0, c1)
    gs = sum(_dot(cum_lhs, p) for p in parts)
    dcol = sum(_dot_tn(p, seg_sel) for p in parts)
    qt, kt, qg, kd = _hgrn_factors(qa, kx, gs[0:ch], gs[ch:2 * ch], gs[2 * ch:3 * ch])
    gate = _silu_t(hg_ref[...])
    for h in range(HG_HEADS):
        cs = slice(h * HG_DK, (h + 1) * HG_DK)
        a = jnp.where(causal, _dot_nt(qt[:, cs], kt[:, cs]), 0.0)
        o = _dot(a.astype(BF16), v[:, cs])
        inter = []
        for n in range(nseq):
            rs = slice(n * ls, (n + 1) * ls)
            s = s_in[n, h]
            inter.append(_dot(qg[rs, cs], s.astype(BF16)))
            decay = jnp.exp2(dcol[cs, n * HG_DV:(n + 1) * HG_DV])
            s_out[n, h] = s * decay + _dot_tn(kd[rs, cs], v[rs, cs])
        o = o + jnp.concatenate(inter, axis=0)
        o_ref[:, cs] = _rms(o, gn) * gate[:, cs]


def _hgrn_sample(hp, hg, lb_logits, gn, state, seq_len):
    nseq = HG_TILE // seq_len
    s_spec = pl.BlockSpec((nseq, HG_HEADS, HG_DK, HG_DV), lambda i: (i, 0, 0, 0))
    return pl.pallas_call(
        functools.partial(_hgrn_sample_kernel, nseq=nseq),
        grid=(hp.shape[0] // HG_TILE,),
        in_specs=[_row_spec(HG_TILE, hp.shape[1]), _row_spec(HG_TILE, HG_WIDTH),
                  pl.BlockSpec(lb_logits.shape, lambda i: (0, 0)), pl.BlockSpec(gn.shape, lambda i: (0, 0)),
                  s_spec],
        out_specs=[_row_spec(HG_TILE, HG_WIDTH), s_spec],
        out_shape=[jax.ShapeDtypeStruct(hg.shape, F32), jax.ShapeDtypeStruct(state.shape, F32)],
        compiler_params=pltpu.CompilerParams(dimension_semantics=("arbitrary",),
                                             vmem_limit_bytes=VMEM_LIMIT_SMALL),
        name="hgrn_sample",
    )(hp, hg, lb_logits, gn, state)


def _mixer_ffn_kernel(sink_ref, x1_ref, q_ref, kv_ref, hp_ref, hg_ref, lbl_ref, gnt_ref, ghg_ref,
                      g_mixpost, g_pre, g_post, wout_ref, wgu_ref, wd_ref,
                      y_ref, s_out_ref, at_scr, o_scr, st_scr, *, n_tiles, tiles_per_seq):
    j = pl.program_id(0)
    tile = x1_ref.shape[0]
    pos = jnp.minimum(j, n_tiles - 1) % tiles_per_seq

    @pl.when(j == 0)
    def _():
        at_scr[...] = jnp.zeros(at_scr.shape, F32)
        o_scr[...] = jnp.zeros(o_scr.shape, F32)
        st_scr[...] = jnp.zeros(st_scr.shape, F32)

    def attn_block(blk):
        rs = slice(blk * WINDOW, (blk + 1) * WINDOW)

        def store(at):
            at_scr[:, rs] = at

        return _attn_block_t(sink_ref, q_ref[rs, :], kv_ref, pos * tile + blk * WINDOW, gnt_ref[...], store)

    ch = HG_TILE
    _, causal, _ = _segment_masks(1)
    cum_lhs = _as_bf16(causal)
    c0, c1 = _hgrn_bound_consts(lbl_ref)
    ghg = ghg_ref[...]

    def hgrn_chunk(ci):
        rs = slice(ci * ch, (ci + 1) * ch)
        qa, kx, parts, v = _hgrn_gates(hp_ref[rs, :], c0, c1)
        g = sum(_dot(cum_lhs, p) for p in parts)
        yield
        g_last = g[ch - 1:ch, :]
        qt, kt, qg, kd = _hgrn_factors(qa, kx, g, g[ch // 2 - 1:ch // 2, :], g_last)
        decay = jnp.exp2(g_last)
        cols = [slice(h * HG_DK, (h + 1) * HG_DK) for h in range(HG_HEADS)]
        scores = [_dot_nt(qt[:, cs], kt[:, cs]) for cs in cols]
        yield
        outs = []
        for h, cs in enumerate(cols):
            a = jnp.where(causal, scores[h], 0.0)
            st = st_scr[h]
            if ci == 0:
                st = jnp.where(pos == 0, 0.0, st)
            outs.append(_dot(a.astype(BF16), v[:, cs]) + _dot_nt(qg[:, cs], st.astype(BF16)))
            st_scr[h] = st * decay[:, cs] + _dot_tn(v[:, cs], kd[:, cs])
        yield
        gate = _silu_t(hg_ref[rs, :])
        for h, cs in enumerate(cols):
            o_scr[rs, cs] = _rms(outs[h], ghg) * gate[:, cs]

    n_blk, n_chunk = tile // WINDOW, tile // ch
    waiting = []
    for blk in range(n_blk):
        waiting += [hgrn_chunk(ci) for ci in range(blk * n_chunk // n_blk, (blk + 1) * n_chunk // n_blk)]
        waiting.insert(len(waiting) - 1, attn_block(blk))
    running = []
    starts_per_slot = -(-len(waiting) // (2 * len(FF_CHUNKS)))

    def side():
        for _ in range(min(starts_per_slot, len(waiting))):
            running.append(waiting.pop(0))
        for gen in list(running):
            if next(gen, "done") == "done":
                running.remove(gen)

    mix = _dot_tn(at_scr[...].astype(BF16), wout_ref[0:ATTN_WIDTH, :])
    side()
    mix = mix + _dot(o_scr[...].astype(BF16), wout_ref[ATTN_WIDTH:ATTN_WIDTH + HG_WIDTH, :])
    side()
    y_ref[...] = _mix_ffn(x1_ref[...], mix, g_mixpost[...], g_pre[...], g_post[...], wgu_ref, wd_ref, side)
    while waiting or running:
        side()

    @pl.when((pos == tiles_per_seq - 1) & (j < n_tiles))
    def _():
        b = j // tiles_per_seq
        for h in range(HG_HEADS):
            s_out_ref[pl.ds(b, 1), h] = st_scr[h].T[None]


def _mixer_ffn(x1, q, kv, hp, hg, sinks, g_attn, lb_logits, g_hg, g_mixpost, g_pre, g_post, wout, wgu, wd,
               batch, seq):
    t = batch * seq
    tile = TOKEN_TILE
    assert seq % tile == 0
    n_tiles, tps = t // tile, seq // tile
    gnt = jnp.broadcast_to(g_attn.reshape(ATTN_WIDTH, 1), (ATTN_WIDTH, WINDOW))
    cur = lambda j: jnp.minimum(j, n_tiles - 1)
    prev = lambda j: jnp.maximum(j - 1, 0)
    cur_spec = lambda width: pl.BlockSpec((tile, width), lambda j: (cur(j), 0))
    prev_spec = pl.BlockSpec((tile, D_MODEL), lambda j: (prev(j), 0))
    s_shape = (batch, HG_HEADS, HG_DK, HG_DV)
    return pl.pallas_call(
        functools.partial(_mixer_ffn_kernel, n_tiles=n_tiles, tiles_per_seq=tps),
        grid=(n_tiles + 1,),
        in_specs=[pl.BlockSpec(memory_space=pltpu.SMEM),
                  prev_spec, cur_spec(ATTN_WIDTH),
                  pl.BlockSpec((seq, 2 * KV_WIDTH), lambda j: (cur(j) // tps, 0)),
                  cur_spec(3 * HG_WIDTH), cur_spec(HG_WIDTH),
                  _const_spec(lb_logits.shape), _const_spec(gnt.shape), _const_spec(g_hg.shape),
                  _const_spec(g_mixpost.shape), _const_spec(g_pre.shape), _const_spec(g_post.shape),
                  _const_spec(wout.shape), _const_spec(wgu.shape), _const_spec(wd.shape)],
        out_specs=[prev_spec, pl.BlockSpec(s_shape, lambda j: (0, 0, 0, 0))],
        out_shape=[jax.ShapeDtypeStruct((t, D_MODEL), F32), jax.ShapeDtypeStruct(s_shape, F32)],
        scratch_shapes=[pltpu.VMEM((ATTN_WIDTH, tile), F32), pltpu.VMEM((tile, HG_WIDTH), F32),
                        pltpu.VMEM((HG_HEADS, HG_DV, HG_DK), F32)],
        compiler_params=pltpu.CompilerParams(dimension_semantics=("arbitrary",),
                                             vmem_limit_bytes=VMEM_LIMIT_BIG),
        name="mixer_ffn",
    )(sinks, x1, q, kv, hp, hg, lb_logits, gnt, g_hg, g_mixpost, g_pre, g_post, wout, wgu, wd)


def kernel(x_prompt, x_sample, cache_k_win, cache_v_win, state_hgrn, w_in, b_in, attn_sinks, attn_out_norm,
           hg_lb_logits, hg_out_norm, w_out, ffn1_w_gu, ffn1_w_down, ffn2_w_gu, ffn2_w_down,
           norm_ffn1_pre, norm_ffn1_post, norm_mix_pre, norm_mix_post, norm_ffn2_pre, norm_ffn2_post):
    depth = w_in.shape[0]
    assert depth == 1, "single-layer trunk"
    batch, seq, _ = x_prompt.shape
    dec_batch, dec_len, _ = x_sample.shape
    wb = cache_k_win.shape[2]
    assert seq % WINDOW == 0 and wb == WINDOW and HG_TILE % dec_len == 0
    layer = 0
    row = lambda p: p[layer].reshape(1, -1).astype(F32)
    wgu1, wd1 = ffn1_w_gu[layer].astype(BF16), ffn1_w_down[layer].astype(BF16)
    wgu2, wd2 = ffn2_w_gu[layer].astype(BF16), ffn2_w_down[layer].astype(BF16)
    win, wout = w_in[layer].astype(BF16), w_out[layer].astype(BF16)
    sinks = attn_sinks[layer].astype(F32)
    lb_logits = hg_lb_logits.astype(F32)
    g_attn, g_hg = row(attn_out_norm), row(hg_out_norm)
    back_params = (row(norm_mix_post), row(norm_ffn2_pre), row(norm_ffn2_post), wout, wgu2, wd2)

    def front(x2d):
        return _ffn_in(x2d, row(norm_ffn1_pre), row(norm_ffn1_post), row(norm_mix_pre), wgu1, wd1, win, row(b_in))

    t_p = batch * seq
    x1, q, kv, hp, hg = front(x_prompt.reshape(t_p, D_MODEL))
    y_p, s_prompt = _mixer_ffn(x1, q, kv, hp, hg, sinks, g_attn, lb_logits, g_hg, *back_params, batch, seq)
    y_prompt = y_p.reshape(batch, seq, D_MODEL)
    kv_last = kv.reshape(batch, seq, 2 * KV_WIDTH)[:, seq - WINDOW:]
    k_prompt = kv_last[..., :KV_WIDTH].reshape(1, batch, WINDOW, N_KV_HEADS, HEAD_DIM)
    v_prompt = kv_last[..., KV_WIDTH:].reshape(1, batch, WINDOW, N_KV_HEADS, HEAD_DIM)

    t_s = dec_batch * dec_len
    x1, q, kv, hp, hg = front(x_sample.reshape(t_s, D_MODEL))
    a, k_s, v_s = _attn_sample(q, kv, cache_k_win[layer].reshape(dec_batch, wb, KV_WIDTH),
                               cache_v_win[layer].reshape(dec_batch, wb, KV_WIDTH), sinks, g_attn, dec_len)
    o, s_sample = _hgrn_sample(hp, hg, lb_logits, g_hg, state_hgrn[layer].astype(F32), dec_len)
    y_sample = _out_ffn(x1, a, o, *back_params).reshape(dec_batch, dec_len, D_MODEL)
    k_sample = k_s.reshape(1, dec_batch, wb, N_KV_HEADS, HEAD_DIM)
    v_sample = v_s.reshape(1, dec_batch, wb, N_KV_HEADS, HEAD_DIM)

    return (y_prompt, y_sample, k_prompt, v_prompt, s_prompt[None], k_sample, v_sample, s_sample[None])
```

```python
import functools

import jax
import jax.numpy as jnp
from jax import lax
from jax.experimental import pallas as pl
from jax.experimental.pallas import tpu as pltpu

F32 = jnp.float32
BF16 = jnp.bfloat16

D_MODEL = 1024
N_HEADS = 8
N_KV_HEADS = 2
HEAD_DIM = 64
GQA_GROUP = N_HEADS // N_KV_HEADS
WINDOW = 128
ATTN_WIDTH = N_HEADS * HEAD_DIM
KV_WIDTH = N_KV_HEADS * HEAD_DIM
SCALE = HEAD_DIM ** -0.5
HG_HEADS = 4
HG_DK = 128
HG_DV = 128
HG_WIDTH = HG_HEADS * HG_DV
D_FF = 2816
EPS = 1e-6

Q_COLS = (0, ATTN_WIDTH)
KV_COLS = (ATTN_WIDTH, ATTN_WIDTH + 2 * KV_WIDTH)
HP_COLS = (KV_COLS[1], KV_COLS[1] + 3 * HG_WIDTH)
HG_COLS = (HP_COLS[1], HP_COLS[1] + HG_WIDTH)

VMEM_LIMIT_BIG = 58 * 1024 * 1024
FF_CHUNKS = ((0, 512), (512, 1024), (1024, 1536), (1536, 2048), (2048, 2560), (2560, 2816))
TOKEN_TILE = 512
SAMPLE_TILE = 128
HG_TILE = 64
ATTN_SEQ_GROUP = 4
NEG = -1e30
LOG2E = 1.4426950408889634


def _rms(x, g):
    return x * lax.rsqrt(jnp.mean(x * x, axis=-1, keepdims=True) + EPS) * g


def _silu(x):
    return x * jax.nn.sigmoid(x)


def _silu_t(x):
    h = 0.5 * x
    return h * jnp.tanh(h) + h


def _dot(a, b):
    return jnp.dot(a, b, preferred_element_type=F32)


def _dot_nt(a, b):
    return lax.dot_general(a, b, (((1,), (1,)), ((), ())), preferred_element_type=F32)


def _dot_tn(a, b):
    return lax.dot_general(a, b, (((0,), (0,)), ((), ())), preferred_element_type=F32)


def _split2(x):
    a = x.astype(BF16)
    b = (x - a.astype(F32)).astype(BF16)
    return a, b


def _no_side_work():
    pass


def _swiglu_ffn(x, g_pre, wgu_ref, wd_ref, g_post, side=_no_side_work):
    h = _rms(x, g_pre).astype(BF16)
    acc = None
    for lo, hi in FF_CHUNKS:
        g = _dot(h, wgu_ref[:, lo:hi])
        side()
        u = _dot(h, wgu_ref[:, D_FF + lo:D_FF + hi])
        side()
        part = _dot((_silu(g) * u).astype(BF16), wd_ref[lo:hi, :])
        side()
        acc = part if acc is None else acc + part
    return x + 0.5 * _rms(acc, g_post)


def _const_spec(shape):
    zeros = (0,) * len(shape)
    return pl.BlockSpec(shape, lambda *_: zeros, pipeline_mode=pl.Buffered(1))


def _row_spec(tile, width):
    return pl.BlockSpec((tile, width), lambda i: (i, 0))


def _ffn_in_kernel(xa_ref, xb_ref, g_pre, g_post, g_mix, wgu_ref, wd_ref, win_ref, bin_ref,
                   x1_ref, q_ref, kv_ref, hp_ref, hg_ref, *, tiles_a):
    x = jnp.where(pl.program_id(0) < tiles_a, xa_ref[...], xb_ref[...])
    x1 = _swiglu_ffn(x, g_pre[...], wgu_ref, wd_ref, g_post[...])
    x1_ref[...] = x1
    h = _rms(x1, g_mix[...]).astype(BF16)
    for ref, (lo, hi) in ((q_ref, Q_COLS), (kv_ref, KV_COLS), (hp_ref, HP_COLS), (hg_ref, HG_COLS)):
        ref[...] = _dot(h, win_ref[:, lo:hi]) + bin_ref[:, lo:hi]


def _ffn_in(xa, xb, g_pre, g_post, g_mix, wgu, wd, win, b_in):
    tile = TOKEN_TILE
    assert xa.shape[0] % tile == 0 and xb.shape[0] % tile == 0
    tiles_a, tiles_b = xa.shape[0] // tile, xb.shape[0] // tile
    n_tiles = tiles_a + tiles_b
    widths = [D_MODEL] + [hi - lo for lo, hi in (Q_COLS, KV_COLS, HP_COLS, HG_COLS)]
    return pl.pallas_call(
        functools.partial(_ffn_in_kernel, tiles_a=tiles_a),
        grid=(n_tiles,),
        in_specs=[pl.BlockSpec((tile, D_MODEL), lambda j: (jnp.minimum(j, tiles_a - 1), 0)),
                  pl.BlockSpec((tile, D_MODEL), lambda j: (jnp.maximum(j - tiles_a, 0), 0)),
                  _const_spec(g_pre.shape), _const_spec(g_post.shape),
                  _const_spec(g_mix.shape), _const_spec(wgu.shape), _const_spec(wd.shape),
                  _const_spec(win.shape), _const_spec(b_in.shape)],
        out_specs=[_row_spec(tile, w) for w in widths],
        out_shape=[jax.ShapeDtypeStruct((n_tiles * tile, w), F32) for w in widths],
        compiler_params=pltpu.CompilerParams(dimension_semantics=("arbitrary",),
                                             vmem_limit_bytes=VMEM_LIMIT_BIG),
        name="ffn_in",
    )(xa, xb, g_pre, g_post, g_mix, wgu, wd, win, b_in)


def _mix_ffn(x1, mix, g_mixpost, g_pre, g_post, wgu_ref, wd_ref, side=_no_side_work):
    x2 = x1 + _rms(mix, g_mixpost)
    return _swiglu_ffn(x2, g_pre, wgu_ref, wd_ref, g_post, side)


def _staged(mixers):
    waiting, running = list(mixers), []
    n_slots = 3 * len(FF_CHUNKS) + 2
    starts_per_slot = -(-len(waiting) // (2 * n_slots // 3))

    def side():
        for _ in range(min(starts_per_slot, len(waiting))):
            running.append(waiting.pop(0))
        for gen in list(running):
            if next(gen, "done") == "done":
                running.remove(gen)

    def drain():
        while waiting or running:
            side()

    return side, drain


def _attn_block_t(sink_ref, q, kv_ref, q0, gnt, store):
    w = WINDOW
    nkeys = 2 * w
    r = lax.broadcasted_iota(jnp.int32, (nkeys, w), 0)
    c = lax.broadcasted_iota(jnp.int32, (nkeys, w), 1)
    first_half = lax.broadcasted_iota(jnp.int32, (w, 2 * HEAD_DIM), 1) < HEAD_DIM
    k0 = jnp.maximum(q0 - w, 0)
    dist = c + (q0 - k0) - r
    bias = jnp.where((dist >= 0) & (dist <= WINDOW), 0.0, NEG)
    kvb = kv_ref[pl.ds(pl.multiple_of(k0, w), nkeys), :]
    kk = kvb[:, 0:KV_WIDTH]
    k_nat = kk.astype(BF16)
    k_swp = pltpu.roll(kk, HEAD_DIM, axis=1).astype(BF16)
    vt = kvb[:, KV_WIDTH:2 * KV_WIDTH].T.astype(BF16)

    def masked_q(h):
        qp = q[:, (h // 2) * 2 * HEAD_DIM:(h // 2 + 1) * 2 * HEAD_DIM]
        return jnp.where(first_half != bool(h % 2), qp, 0.0).astype(BF16)

    scores = [None] * N_HEADS
    for keys, use_swapped in ((k_nat, False), (k_swp, True)):
        heads = [h for h in range(N_HEADS) if (h // GQA_GROUP != h % 2) == use_swapped]
        st = _dot_nt(keys, jnp.concatenate([masked_q(h) for h in heads], axis=0))
        for i, h in enumerate(heads):
            scores[h] = st[:, i * w:(i + 1) * w]
    yield
    probs, dens = [], []
    for h in range(N_HEADS):
        st = scores[h] * (SCALE * LOG2E) + bias
        sink2 = sink_ref[h] * LOG2E
        m = jnp.maximum(jnp.max(st, axis=0, keepdims=True), sink2)
        p = jnp.exp2(st - m)
        dens.append(jnp.sum(p, axis=0, keepdims=True) + jnp.exp2(sink2 - m))
        probs.append(p.astype(BF16))
    ot = _dot(vt, jnp.concatenate(probs, axis=1))
    yield
    rows = []
    for h in range(N_HEADS):
        hk = h // GQA_GROUP
        rows.append(ot[hk * HEAD_DIM:(hk + 1) * HEAD_DIM, h * w:(h + 1) * w] / dens[h])
    at = jnp.concatenate(rows, axis=0)
    inv = lax.rsqrt(jnp.sum(at * at, axis=0, keepdims=True) * (1.0 / ATTN_WIDTH) + EPS)
    store(at * inv * gnt)


def _softmax_sink_pv(scores, masks, values, sink):
    scores = [jnp.where(mk, s * SCALE, NEG) for s, mk in zip(scores, masks)]
    m = sink
    for s in scores:
        m = jnp.maximum(m, jnp.max(s, axis=-1, keepdims=True))
    den = jnp.exp(sink - m)
    out = None
    for s, v in zip(scores, values):
        p = jnp.exp(s - m)
        den = den + jnp.sum(p, axis=-1, keepdims=True)
        pv = _dot(p.astype(BF16), v)
        out = pv if out is None else out + pv
    return out / den


def _attn_sample_group(sink_ref, q_ref, kvn_ref, ck_ref, cv_ref, gn_ref, a_ref, ko_ref, vo_ref, grp, dec_len):
    ns = ATTN_SEQ_GROUP
    rows = ns * dec_len
    wb = ck_ref.shape[1]
    m_rows = GQA_GROUP * rows
    r = lax.broadcasted_iota(jnp.int32, (m_rows, ns * wb), 0)
    c = lax.broadcasted_iota(jnp.int32, (m_rows, ns * wb), 1)
    r_seq, r_pos = (r % rows) // dec_len, r % dec_len
    mask_cache = (r_seq == c // wb) & (c % wb >= r_pos)
    r = lax.broadcasted_iota(jnp.int32, (m_rows, rows), 0)
    c = lax.broadcasted_iota(jnp.int32, (m_rows, rows), 1)
    mask_new = ((r % rows) // dec_len == c // dec_len) & (c % dec_len <= r % dec_len)
    row_head = lax.broadcasted_iota(jnp.int32, (m_rows, 1), 0) // rows

    rs = slice(grp * rows, (grp + 1) * rows)
    ss = slice(grp * ns, (grp + 1) * ns)
    q = q_ref[rs, :]
    kvn_b = kvn_ref[rs, :].astype(BF16)
    ck_b = ck_ref[ss].reshape(ns * wb, KV_WIDTH).astype(BF16)
    cv_b = cv_ref[ss].reshape(ns * wb, KV_WIDTH).astype(BF16)
    scores = []
    for hk in range(N_KV_HEADS):
        kcols = slice(hk * HEAD_DIM, (hk + 1) * HEAD_DIM)
        heads = [hk * GQA_GROUP + g for g in range(GQA_GROUP)]
        qs = jnp.concatenate([q[:, h * HEAD_DIM:(h + 1) * HEAD_DIM] for h in heads], axis=0).astype(BF16)
        scores.append([_dot_nt(qs, ck_b[:, kcols]), _dot_nt(qs, kvn_b[:, kcols])])
    yield
    outs = []
    for hk in range(N_KV_HEADS):
        kcols = slice(hk * HEAD_DIM, (hk + 1) * HEAD_DIM)
        vcols = slice(KV_WIDTH + hk * HEAD_DIM, KV_WIDTH + (hk + 1) * HEAD_DIM)
        sink = jnp.zeros((m_rows, 1), F32)
        for g in range(GQA_GROUP):
            sink = jnp.where(row_head == g, sink_ref[hk * GQA_GROUP + g], sink)
        outs.append(_softmax_sink_pv(scores[hk], [mask_cache, mask_new], [cv_b[:, kcols], kvn_b[:, vcols]], sink))
    yield
    for hk in range(N_KV_HEADS):
        for g in range(GQA_GROUP):
            h = hk * GQA_GROUP + g
            a_ref[rs, h * HEAD_DIM:(h + 1) * HEAD_DIM] = outs[hk][g * rows:(g + 1) * rows, :]
    a_ref[rs, :] = _rms(a_ref[rs, :], gn_ref[...])
    for n in range(ns):
        s = grp * ns + n
        ts = slice(grp * rows + n * dec_len, grp * rows + (n + 1) * dec_len)
        ko_ref[s, 0:wb - dec_len, :] = ck_ref[s, dec_len:wb, :]
        vo_ref[s, 0:wb - dec_len, :] = cv_ref[s, dec_len:wb, :]
        ko_ref[s, wb - dec_len:wb, :] = kvn_ref[ts, 0:KV_WIDTH]
        vo_ref[s, wb - dec_len:wb, :] = kvn_ref[ts, KV_WIDTH:2 * KV_WIDTH]


def _hgrn_bound_consts(lbl_ref):
    lbl = lbl_ref[...]
    e = jnp.exp(lbl - jnp.max(lbl, axis=0, keepdims=True))
    lb = e[0:1, :] / jnp.sum(e, axis=0, keepdims=True)
    return 0.5 + 0.5 * lb, 0.5 - 0.5 * lb


def _hgrn_gates(hp, c0, c1):
    qa = _silu_t(hp[:, 0:HG_WIDTH])
    t = c1 * jnp.tanh(0.5 * hp[:, HG_WIDTH:2 * HG_WIDTH])
    f, kx = c0 + t, c1 - t
    v = hp[:, 2 * HG_WIDTH:3 * HG_WIDTH].astype(BF16)
    return qa, kx, _split2(jnp.log2(f)), v


def _hgrn_factors(qa, kx, g, g_mid, g_last):
    qt = qa * jnp.exp2(g - g_mid)
    kt = kx * jnp.exp2(g_mid - g)
    qg = (qt * jnp.exp2(g_mid)).astype(BF16)
    kd = (kt * jnp.exp2(g_last - g_mid)).astype(BF16)
    return qt.astype(BF16), kt.astype(BF16), qg, kd


def _segment_masks(nseq):
    ls = HG_TILE // nseq
    r = lax.broadcasted_iota(jnp.int32, (HG_TILE, HG_TILE), 0)
    c = lax.broadcasted_iota(jnp.int32, (HG_TILE, HG_TILE), 1)
    same = (r // ls) == (c // ls)
    return same, same & (c <= r), same & (c % ls < ls // 2)


def _as_bf16(mask):
    return mask.astype(F32).astype(BF16)


def _hgrn_short_tile(hp_ref, hg_ref, c0, c1, gn, s_in, s_out, o_ref, ti, nseq):
    ch = HG_TILE
    ls = ch // nseq
    same, causal, first_half = _segment_masks(nseq)
    cum_lhs = _as_bf16(jnp.concatenate([causal, first_half, same], axis=0))
    sr = lax.broadcasted_iota(jnp.int32, (ch, nseq * HG_DV), 0)
    sc = lax.broadcasted_iota(jnp.int32, (ch, nseq * HG_DV), 1)
    seg_sel = _as_bf16((sr // ls) == (sc // HG_DV))
    rows = slice(ti * ch, (ti + 1) * ch)

    qa, kx, parts, v = _hgrn_gates(hp_ref[rows, :], c0, c1)
    gs = sum(_dot(cum_lhs, p) for p in parts)
    dcol = sum(_dot_tn(p, seg_sel) for p in parts)
    yield
    qt, kt, qg, kd = _hgrn_factors(qa, kx, gs[0:ch], gs[ch:2 * ch], gs[2 * ch:3 * ch])
    cols = [slice(h * HG_DK, (h + 1) * HG_DK) for h in range(HG_HEADS)]
    scores = [_dot_nt(qt[:, cs], kt[:, cs]) for cs in cols]
    yield
    outs = []
    for h, cs in enumerate(cols):
        a = jnp.where(causal, scores[h], 0.0)
        o = _dot(a.astype(BF16), v[:, cs])
        inter = []
        for n in range(nseq):
            rs = slice(n * ls, (n + 1) * ls)
            s = s_in[ti * nseq + n, h]
            inter.append(_dot(qg[rs, cs], s.astype(BF16)))
            decay = jnp.exp2(dcol[cs, n * HG_DV:(n + 1) * HG_DV])
            s_out[ti * nseq + n, h] = s * decay + _dot_tn(kd[rs, cs], v[rs, cs])
        outs.append(o + jnp.concatenate(inter, axis=0))
    yield
    gate = _silu_t(hg_ref[rows, :])
    for h, cs in enumerate(cols):
        o_ref[rows, cs] = _rms(outs[h], gn) * gate[:, cs]


def _sample_mixer_ffn_kernel(sink_ref, x1_ref, q_ref, kvn_ref, ck_ref, cv_ref, hp_ref, hg_ref, s_in_ref,
                             lbl_ref, gn_ref, ghg_ref, g_mixpost, g_pre, g_post, wout_ref, wgu_ref, wd_ref,
                             y_ref, ko_ref, vo_ref, s_out_ref, a_scr, o_scr, *, dec_len):
    j = pl.program_id(0)
    tile = x1_ref.shape[0]

    @pl.when(j == 0)
    def _():
        a_scr[...] = jnp.zeros(a_scr.shape, F32)
        o_scr[...] = jnp.zeros(o_scr.shape, F32)

    c0, c1 = _hgrn_bound_consts(lbl_ref)
    nseq = HG_TILE // dec_len
    mixers = [_attn_sample_group(sink_ref, q_ref, kvn_ref, ck_ref, cv_ref, gn_ref, a_scr, ko_ref, vo_ref, grp, dec_len)
              for grp in range(tile // (ATTN_SEQ_GROUP * dec_len))]
    mixers += [_hgrn_short_tile(hp_ref, hg_ref, c0, c1, ghg_ref[...], s_in_ref, s_out_ref, o_scr, ti, nseq)
               for ti in range(tile // HG_TILE)]
    side, drain = _staged(mixers)

    mix = _dot(a_scr[...].astype(BF16), wout_ref[0:ATTN_WIDTH, :])
    side()
    mix = mix + _dot(o_scr[...].astype(BF16), wout_ref[ATTN_WIDTH:ATTN_WIDTH + HG_WIDTH, :])
    side()
    y_ref[...] = _mix_ffn(x1_ref[...], mix, g_mixpost[...], g_pre[...], g_post[...], wgu_ref, wd_ref, side)
    drain()


def _sample_mixer_ffn(x1, q, kv, hp, hg, cache_k, cache_v, state, sinks, g_attn, lb_logits, g_hg,
                      g_mixpost, g_pre, g_post, wout, wgu, wd, dec_len, row0):
    tile = SAMPLE_TILE
    t = cache_k.shape[0] * dec_len
    n_tiles = t // tile
    spt = tile // dec_len
    assert t % tile == 0 and row0 % tile == 0 and tile % HG_TILE == 0 and tile % (ATTN_SEQ_GROUP * dec_len) == 0
    cur = lambda j: jnp.minimum(j, n_tiles - 1)
    prev = lambda j: jnp.maximum(j - 1, 0)
    cur_spec = lambda width: pl.BlockSpec((tile, width), lambda j: (row0 // tile + cur(j), 0))
    cache_spec = pl.BlockSpec((spt,) + cache_k.shape[1:], lambda j: (cur(j), 0, 0))
    state_spec = pl.BlockSpec((spt,) + state.shape[1:], lambda j: (cur(j), 0, 0, 0))
    return pl.pallas_call(
        functools.partial(_sample_mixer_ffn_kernel, dec_len=dec_len),
        grid=(n_tiles + 1,),
        in_specs=[pl.BlockSpec(memory_space=pltpu.SMEM),
                  pl.BlockSpec((tile, D_MODEL), lambda j: (row0 // tile + prev(j), 0)),
                  cur_spec(ATTN_WIDTH), cur_spec(2 * KV_WIDTH), cache_spec, cache_spec,
                  cur_spec(3 * HG_WIDTH), cur_spec(HG_WIDTH), state_spec,
                  _const_spec(lb_logits.shape), _const_spec(g_attn.shape), _const_spec(g_hg.shape),
                  _const_spec(g_mixpost.shape), _const_spec(g_pre.shape), _const_spec(g_post.shape),
                  _const_spec(wout.shape), _const_spec(wgu.shape), _const_spec(wd.shape)],
        out_specs=[pl.BlockSpec((tile, D_MODEL), lambda j: (prev(j), 0)), cache_spec, cache_spec, state_spec],
        out_shape=[jax.ShapeDtypeStruct((t, D_MODEL), F32), jax.ShapeDtypeStruct(cache_k.shape, F32),
                   jax.ShapeDtypeStruct(cache_v.shape, F32), jax.ShapeDtypeStruct(state.shape, F32)],
        scratch_shapes=[pltpu.VMEM((tile, ATTN_WIDTH), F32), pltpu.VMEM((tile, HG_WIDTH), F32)],
        compiler_params=pltpu.CompilerParams(dimension_semantics=("arbitrary",),
                                             vmem_limit_bytes=VMEM_LIMIT_BIG),
        name="sample_mixer_ffn",
    )(sinks, x1, q, kv, cache_k, cache_v, hp, hg, state, lb_logits, g_attn, g_hg,
      g_mixpost, g_pre, g_post, wout, wgu, wd)


def _mixer_ffn_kernel(sink_ref, x1_ref, q_ref, kv_ref, hp_ref, hg_ref, lbl_ref, gnt_ref, ghg_ref,
                      g_mixpost, g_pre, g_post, wout_ref, wgu_ref, wd_ref,
                      y_ref, s_out_ref, at_scr, o_scr, st_scr, *, n_tiles, tiles_per_seq):
    j = pl.program_id(0)
    tile = x1_ref.shape[0]
    pos = jnp.minimum(j, n_tiles - 1) % tiles_per_seq

    @pl.when(j == 0)
    def _():
        at_scr[...] = jnp.zeros(at_scr.shape, F32)
        o_scr[...] = jnp.zeros(o_scr.shape, F32)
        st_scr[...] = jnp.zeros(st_scr.shape, F32)

    def attn_block(blk):
        rs = slice(blk * WINDOW, (blk + 1) * WINDOW)

        def store(at):
            at_scr[:, rs] = at

        return _attn_block_t(sink_ref, q_ref[rs, :], kv_ref, pos * tile + blk * WINDOW, gnt_ref[...], store)

    ch = HG_TILE
    _, causal, _ = _segment_masks(1)
    cum_lhs = _as_bf16(causal)
    c0, c1 = _hgrn_bound_consts(lbl_ref)
    ghg = ghg_ref[...]

    def hgrn_chunk(ci):
        rs = slice(ci * ch, (ci + 1) * ch)
        qa, kx, parts, v = _hgrn_gates(hp_ref[rs, :], c0, c1)
        g = sum(_dot(cum_lhs, p) for p in parts)
        yield
        g_last = g[ch - 1:ch, :]
        qt, kt, qg, kd = _hgrn_factors(qa, kx, g, g[ch // 2 - 1:ch // 2, :], g_last)
        decay = jnp.exp2(g_last)
        cols = [slice(h * HG_DK, (h + 1) * HG_DK) for h in range(HG_HEADS)]
        scores = [_dot_nt(qt[:, cs], kt[:, cs]) for cs in cols]
        yield
        outs = []
        for h, cs in enumerate(cols):
            a = jnp.where(causal, scores[h], 0.0)
            st = st_scr[h]
            if ci == 0:
                st = jnp.where(pos == 0, 0.0, st)
            outs.append(_dot(a.astype(BF16), v[:, cs]) + _dot_nt(qg[:, cs], st.astype(BF16)))
            st_scr[h] = st * decay[:, cs] + _dot_tn(v[:, cs], kd[:, cs])
        yield
        gate = _silu_t(hg_ref[rs, :])
        for h, cs in enumerate(cols):
            o_scr[rs, cs] = _rms(outs[h], ghg) * gate[:, cs]

    n_blk, n_chunk = tile // WINDOW, tile // ch
    mixers = []
    for blk in range(n_blk):
        mixers += [hgrn_chunk(ci) for ci in range(blk * n_chunk // n_blk, (blk + 1) * n_chunk // n_blk)]
        mixers.insert(len(mixers) - 1, attn_block(blk))
    side, drain = _staged(mixers)

    mix = _dot_tn(at_scr[...].astype(BF16), wout_ref[0:ATTN_WIDTH, :])
    side()
    mix = mix + _dot(o_scr[...].astype(BF16), wout_ref[ATTN_WIDTH:ATTN_WIDTH + HG_WIDTH, :])
    side()
    y_ref[...] = _mix_ffn(x1_ref[...], mix, g_mixpost[...], g_pre[...], g_post[...], wgu_ref, wd_ref, side)
    drain()

    @pl.when((pos == tiles_per_seq - 1) & (j < n_tiles))
    def _():
        b = j // tiles_per_seq
        for h in range(HG_HEADS):
            s_out_ref[pl.ds(b, 1), h] = st_scr[h].T[None]


def _mixer_ffn(x1, q, kv, hp, hg, sinks, g_attn, lb_logits, g_hg, g_mixpost, g_pre, g_post, wout, wgu, wd,
               batch, seq):
    t = batch * seq
    tile = TOKEN_TILE
    assert seq % tile == 0
    n_tiles, tps = t // tile, seq // tile
    gnt = jnp.broadcast_to(g_attn.reshape(ATTN_WIDTH, 1), (ATTN_WIDTH, WINDOW))
    cur = lambda j: jnp.minimum(j, n_tiles - 1)
    prev = lambda j: jnp.maximum(j - 1, 0)
    cur_spec = lambda width: pl.BlockSpec((tile, width), lambda j: (cur(j), 0))
    prev_spec = pl.BlockSpec((tile, D_MODEL), lambda j: (prev(j), 0))
    s_shape = (batch, HG_HEADS, HG_DK, HG_DV)
    return pl.pallas_call(
        functools.partial(_mixer_ffn_kernel, n_tiles=n_tiles, tiles_per_seq=tps),
        grid=(n_tiles + 1,),
        in_specs=[pl.BlockSpec(memory_space=pltpu.SMEM),
                  prev_spec, cur_spec(ATTN_WIDTH),
                  pl.BlockSpec((seq, 2 * KV_WIDTH), lambda j: (cur(j) // tps, 0)),
                  cur_spec(3 * HG_WIDTH), cur_spec(HG_WIDTH),
                  _const_spec(lb_logits.shape), _const_spec(gnt.shape), _const_spec(g_hg.shape),
                  _const_spec(g_mixpost.shape), _const_spec(g_pre.shape), _const_spec(g_post.shape),
                  _const_spec(wout.shape), _const_spec(wgu.shape), _const_spec(wd.shape)],
        out_specs=[prev_spec, pl.BlockSpec(s_shape, lambda j: (0, 0, 0, 0))],
        out_shape=[jax.ShapeDtypeStruct((t, D_MODEL), F32), jax.ShapeDtypeStruct(s_shape, F32)],
        scratch_shapes=[pltpu.VMEM((ATTN_WIDTH, tile), F32), pltpu.VMEM((tile, HG_WIDTH), F32),
                        pltpu.VMEM((HG_HEADS, HG_DV, HG_DK), F32)],
        compiler_params=pltpu.CompilerParams(dimension_semantics=("arbitrary",),
                                             vmem_limit_bytes=VMEM_LIMIT_BIG),
        name="mixer_ffn",
    )(sinks, x1, q, kv, hp, hg, lb_logits, gnt, g_hg, g_mixpost, g_pre, g_post, wout, wgu, wd)


def kernel(x_prompt, x_sample, cache_k_win, cache_v_win, state_hgrn, w_in, b_in, attn_sinks, attn_out_norm,
           hg_lb_logits, hg_out_norm, w_out, ffn1_w_gu, ffn1_w_down, ffn2_w_gu, ffn2_w_down,
           norm_ffn1_pre, norm_ffn1_post, norm_mix_pre, norm_mix_post, norm_ffn2_pre, norm_ffn2_post):
    depth = w_in.shape[0]
    assert depth == 1, "single-layer trunk"
    batch, seq, _ = x_prompt.shape
    dec_batch, dec_len, _ = x_sample.shape
    wb = cache_k_win.shape[2]
    assert seq % WINDOW == 0 and wb == WINDOW and HG_TILE % dec_len == 0
    layer = 0
    row = lambda p: p[layer].reshape(1, -1).astype(F32)
    wgu1, wd1 = ffn1_w_gu[layer].astype(BF16), ffn1_w_down[layer].astype(BF16)
    wgu2, wd2 = ffn2_w_gu[layer].astype(BF16), ffn2_w_down[layer].astype(BF16)
    win, wout = w_in[layer].astype(BF16), w_out[layer].astype(BF16)
    sinks = attn_sinks[layer].astype(F32)
    lb_logits = hg_lb_logits.astype(F32)
    g_attn, g_hg = row(attn_out_norm), row(hg_out_norm)
    back_params = (row(norm_mix_post), row(norm_ffn2_pre), row(norm_ffn2_post), wout, wgu2, wd2)

    t_p, t_s = batch * seq, dec_batch * dec_len
    x1, q, kv, hp, hg = _ffn_in(x_prompt.reshape(t_p, D_MODEL), x_sample.reshape(t_s, D_MODEL),
                                row(norm_ffn1_pre), row(norm_ffn1_post), row(norm_mix_pre), wgu1, wd1, win, row(b_in))

    y_p, s_prompt = _mixer_ffn(x1, q, kv, hp, hg, sinks, g_attn, lb_logits, g_hg, *back_params, batch, seq)
    y_prompt = y_p.reshape(batch, seq, D_MODEL)
    kv_last = kv[:t_p].reshape(batch, seq, 2 * KV_WIDTH)[:, seq - WINDOW:]
    k_prompt = kv_last[..., :KV_WIDTH].reshape(1, batch, WINDOW, N_KV_HEADS, HEAD_DIM)
    v_prompt = kv_last[..., KV_WIDTH:].reshape(1, batch, WINDOW, N_KV_HEADS, HEAD_DIM)

    y_s, k_s, v_s, s_sample = _sample_mixer_ffn(
        x1, q, kv, hp, hg, cache_k_win[layer].reshape(dec_batch, wb, KV_WIDTH),
        cache_v_win[layer].reshape(dec_batch, wb, KV_WIDTH), state_hgrn[layer].astype(F32),
        sinks, g_attn, lb_logits, g_hg, *back_params, dec_len, t_p)
    y_sample = y_s.reshape(dec_batch, dec_len, D_MODEL)
    k_sample = k_s.reshape(1, dec_batch, wb, N_KV_HEADS, HEAD_DIM)
    v_sample = v_s.reshape(1, dec_batch, wb, N_KV_HEADS, HEAD_DIM)

    return (y_prompt, y_sample, k_prompt, v_prompt, s_prompt[None], k_sample, v_sample, s_sample[None])
```

```python
import functools

import jax
import jax.numpy as jnp
from jax import lax
from jax.experimental import pallas as pl
from jax.experimental.pallas import tpu as pltpu

F32 = jnp.float32
BF16 = jnp.bfloat16

D_MODEL = 1024
N_HEADS = 8
N_KV_HEADS = 2
HEAD_DIM = 64
GQA_GROUP = N_HEADS // N_KV_HEADS
WINDOW = 128
ATTN_WIDTH = N_HEADS * HEAD_DIM
KV_WIDTH = N_KV_HEADS * HEAD_DIM
SCALE = HEAD_DIM ** -0.5
HG_HEADS = 4
HG_DK = 128
HG_DV = 128
HG_WIDTH = HG_HEADS * HG_DV
D_FF = 2816
EPS = 1e-6

Q_COLS = (0, ATTN_WIDTH)
KV_COLS = (ATTN_WIDTH, ATTN_WIDTH + 2 * KV_WIDTH)
HP_COLS = (KV_COLS[1], KV_COLS[1] + 3 * HG_WIDTH)
HG_COLS = (HP_COLS[1], HP_COLS[1] + HG_WIDTH)
Z_Q = slice(0, ATTN_WIDTH)
Z_HP = slice(ATTN_WIDTH, ATTN_WIDTH + 3 * HG_WIDTH)
Z_HG = slice(ATTN_WIDTH + 3 * HG_WIDTH, ATTN_WIDTH + 4 * HG_WIDTH)
Z_WIDTH = ATTN_WIDTH + 4 * HG_WIDTH

VMEM_LIMIT_BIG = 58 * 1024 * 1024
FF_CHUNKS = ((0, 512), (512, 1024), (1024, 1536), (1536, 2048), (2048, 2560), (2560, 2816))
TOKEN_TILE = 512
SAMPLE_TILE = 128
HG_TILE = 64
ATTN_SEQ_GROUP = 4
NEG = -1e30
LOG2E = 1.4426950408889634


def _rms(x, g):
    return x * lax.rsqrt(jnp.mean(x * x, axis=-1, keepdims=True) + EPS) * g


def _silu(x):
    return x * jax.nn.sigmoid(x)


def _silu_t(x):
    h = 0.5 * x
    return h * jnp.tanh(h) + h


def _dot(a, b):
    return jnp.dot(a, b, preferred_element_type=F32)


def _dot_nt(a, b):
    return lax.dot_general(a, b, (((1,), (1,)), ((), ())), preferred_element_type=F32)


def _dot_tn(a, b):
    return lax.dot_general(a, b, (((0,), (0,)), ((), ())), preferred_element_type=F32)


def _split2(x):
    a = x.astype(BF16)
    b = (x - a.astype(F32)).astype(BF16)
    return a, b


def _no_side_work():
    pass


def _swiglu_ffn(x, g_pre, wgu_ref, wd_ref, g_post, side=_no_side_work):
    h = _rms(x, g_pre).astype(BF16)
    acc = None
    for lo, hi in FF_CHUNKS:
        g = _dot(h, wgu_ref[:, lo:hi])
        side()
        u = _dot(h, wgu_ref[:, D_FF + lo:D_FF + hi])
        side()
        part = _dot((_silu(g) * u).astype(BF16), wd_ref[lo:hi, :])
        side()
        acc = part if acc is None else acc + part
    return x + 0.5 * _rms(acc, g_post)


def _const_spec(shape):
    zeros = (0,) * len(shape)
    return pl.BlockSpec(shape, lambda *_: zeros, pipeline_mode=pl.Buffered(1))


def _row_spec(tile, width):
    return pl.BlockSpec((tile, width), lambda i: (i, 0))


def _ffn_in_kernel(xa_ref, xb_ref, g_pre, g_post, g_mix, wgu_ref, wd_ref, win_ref, bin_ref,
                   x1_ref, kv_ref, z_ref, *, tiles_a):
    x = jnp.where(pl.program_id(0) < tiles_a, xa_ref[...], xb_ref[...])
    x1 = _swiglu_ffn(x, g_pre[...], wgu_ref, wd_ref, g_post[...])
    x1_ref[...] = x1
    h = _rms(x1, g_mix[...]).astype(BF16)

    def project(lo, hi):
        return _dot(h, win_ref[:, lo:hi]) + bin_ref[:, lo:hi]

    kv_ref[...] = project(*KV_COLS)
    z_ref[:, Z_Q] = project(*Q_COLS)
    z_ref[:, Z_HP.start:Z_HG.stop] = project(HP_COLS[0], HG_COLS[1])


def _ffn_in(xa, xb, g_pre, g_post, g_mix, wgu, wd, win, b_in):
    tile = TOKEN_TILE
    assert xa.shape[0] % tile == 0 and xb.shape[0] % tile == 0
    tiles_a, tiles_b = xa.shape[0] // tile, xb.shape[0] // tile
    n_tiles = tiles_a + tiles_b
    widths = [D_MODEL, 2 * KV_WIDTH, Z_WIDTH]
    return pl.pallas_call(
        functools.partial(_ffn_in_kernel, tiles_a=tiles_a),
        grid=(n_tiles,),
        in_specs=[pl.BlockSpec((tile, D_MODEL), lambda j: (jnp.minimum(j, tiles_a - 1), 0)),
                  pl.BlockSpec((tile, D_MODEL), lambda j: (jnp.maximum(j - tiles_a, 0), 0)),
                  _const_spec(g_pre.shape), _const_spec(g_post.shape),
                  _const_spec(g_mix.shape), _const_spec(wgu.shape), _const_spec(wd.shape),
                  _const_spec(win.shape), _const_spec(b_in.shape)],
        out_specs=[_row_spec(tile, w) for w in widths],
        out_shape=[jax.ShapeDtypeStruct((n_tiles * tile, w), F32) for w in widths],
        compiler_params=pltpu.CompilerParams(dimension_semantics=("arbitrary",),
                                             vmem_limit_bytes=VMEM_LIMIT_BIG),
        name="ffn_in",
    )(xa, xb, g_pre, g_post, g_mix, wgu, wd, win, b_in)


def _mix_ffn(x1, mix, g_mixpost, g_pre, g_post, wgu_ref, wd_ref, side=_no_side_work):
    x2 = x1 + _rms(mix, g_mixpost)
    return _swiglu_ffn(x2, g_pre, wgu_ref, wd_ref, g_post, side)


def _staged(mixers):
    waiting, running = list(mixers), []
    n_slots = 3 * len(FF_CHUNKS) + 2
    starts_per_slot = -(-len(waiting) // (2 * n_slots // 3))

    def side():
        for _ in range(min(starts_per_slot, len(waiting))):
            running.append(waiting.pop(0))
        for gen in list(running):
            if next(gen, "done") == "done":
                running.remove(gen)

    def drain():
        while waiting or running:
            side()

    return side, drain


def _attn_block_t(sink_ref, q, kv_ref, q0, gnt, store):
    w = WINDOW
    nkeys = 2 * w
    r = lax.broadcasted_iota(jnp.int32, (nkeys, w), 0)
    c = lax.broadcasted_iota(jnp.int32, (nkeys, w), 1)
    first_half = lax.broadcasted_iota(jnp.int32, (w, 2 * HEAD_DIM), 1) < HEAD_DIM
    k0 = jnp.maximum(q0 - w, 0)
    dist = c + (q0 - k0) - r
    bias = jnp.where((dist >= 0) & (dist <= WINDOW), 0.0, NEG)
    kvb = kv_ref[pl.ds(pl.multiple_of(k0, w), nkeys), :]
    kk = kvb[:, 0:KV_WIDTH]
    k_nat = kk.astype(BF16)
    k_swp = pltpu.roll(kk, HEAD_DIM, axis=1).astype(BF16)
    vt = kvb[:, KV_WIDTH:2 * KV_WIDTH].T.astype(BF16)

    def masked_q(h):
        qp = q[:, (h // 2) * 2 * HEAD_DIM:(h // 2 + 1) * 2 * HEAD_DIM]
        return jnp.where(first_half != bool(h % 2), qp, 0.0).astype(BF16)

    groups = []
    for keys, use_swapped in ((k_nat, False), (k_swp, True)):
        heads = [h for h in range(N_HEADS) if (h // GQA_GROUP != h % 2) == use_swapped]
        groups.append((keys, heads, jnp.concatenate([masked_q(h) for h in heads], axis=0)))
    yield
    scores = [None] * N_HEADS
    for keys, heads, qs in groups:
        st = _dot_nt(keys, qs)
        for i, h in enumerate(heads):
            scores[h] = st[:, i * w:(i + 1) * w]
    yield
    probs, dens = [], []
    for h in range(N_HEADS):
        st = scores[h] * (SCALE * LOG2E) + bias
        sink2 = sink_ref[h] * LOG2E
        m = jnp.maximum(jnp.max(st, axis=0, keepdims=True), sink2)
        p = jnp.exp2(st - m)
        dens.append(jnp.sum(p, axis=0, keepdims=True) + jnp.exp2(sink2 - m))
        probs.append(p.astype(BF16))
    yield
    ot = _dot(vt, jnp.concatenate(probs, axis=1))
    yield
    rows = []
    for h in range(N_HEADS):
        hk = h // GQA_GROUP
        rows.append(ot[hk * HEAD_DIM:(hk + 1) * HEAD_DIM, h * w:(h + 1) * w] / dens[h])
    at = jnp.concatenate(rows, axis=0)
    inv = lax.rsqrt(jnp.sum(at * at, axis=0, keepdims=True) * (1.0 / ATTN_WIDTH) + EPS)
    store(at * inv * gnt)


def _softmax_sink_pv(scores, masks, values, sink):
    scores = [jnp.where(mk, s * SCALE, NEG) for s, mk in zip(scores, masks)]
    m = sink
    for s in scores:
        m = jnp.maximum(m, jnp.max(s, axis=-1, keepdims=True))
    den = jnp.exp(sink - m)
    out = None
    for s, v in zip(scores, values):
        p = jnp.exp(s - m)
        den = den + jnp.sum(p, axis=-1, keepdims=True)
        pv = _dot(p.astype(BF16), v)
        out = pv if out is None else out + pv
    return out / den


def _attn_sample_group(sink_ref, z_ref, kvn_ref, ck_ref, cv_ref, gn_ref, a_ref, ko_ref, vo_ref, grp, dec_len):
    ns = ATTN_SEQ_GROUP
    rows = ns * dec_len
    wb = ck_ref.shape[1]
    m_rows = GQA_GROUP * rows
    r = lax.broadcasted_iota(jnp.int32, (m_rows, ns * wb), 0)
    c = lax.broadcasted_iota(jnp.int32, (m_rows, ns * wb), 1)
    r_seq, r_pos = (r % rows) // dec_len, r % dec_len
    mask_cache = (r_seq == c // wb) & (c % wb >= r_pos)
    r = lax.broadcasted_iota(jnp.int32, (m_rows, rows), 0)
    c = lax.broadcasted_iota(jnp.int32, (m_rows, rows), 1)
    mask_new = ((r % rows) // dec_len == c // dec_len) & (c % dec_len <= r % dec_len)
    row_head = lax.broadcasted_iota(jnp.int32, (m_rows, 1), 0) // rows

    rs = slice(grp * rows, (grp + 1) * rows)
    ss = slice(grp * ns, (grp + 1) * ns)
    q = z_ref[rs, Z_Q]
    kvn_b = kvn_ref[rs, :].astype(BF16)
    ck_b = ck_ref[ss].reshape(ns * wb, KV_WIDTH).astype(BF16)
    cv_b = cv_ref[ss].reshape(ns * wb, KV_WIDTH).astype(BF16)
    scores = []
    for hk in range(N_KV_HEADS):
        kcols = slice(hk * HEAD_DIM, (hk + 1) * HEAD_DIM)
        heads = [hk * GQA_GROUP + g for g in range(GQA_GROUP)]
        qs = jnp.concatenate([q[:, h * HEAD_DIM:(h + 1) * HEAD_DIM] for h in heads], axis=0).astype(BF16)
        scores.append([_dot_nt(qs, ck_b[:, kcols]), _dot_nt(qs, kvn_b[:, kcols])])
    yield
    outs = []
    for hk in range(N_KV_HEADS):
        kcols = slice(hk * HEAD_DIM, (hk + 1) * HEAD_DIM)
        vcols = slice(KV_WIDTH + hk * HEAD_DIM, KV_WIDTH + (hk + 1) * HEAD_DIM)
        sink = jnp.zeros((m_rows, 1), F32)
        for g in range(GQA_GROUP):
            sink = jnp.where(row_head == g, sink_ref[hk * GQA_GROUP + g], sink)
        outs.append(_softmax_sink_pv(scores[hk], [mask_cache, mask_new], [cv_b[:, kcols], kvn_b[:, vcols]], sink))
    yield
    for hk in range(N_KV_HEADS):
        for g in range(GQA_GROUP):
            h = hk * GQA_GROUP + g
            a_ref[rs, h * HEAD_DIM:(h + 1) * HEAD_DIM] = outs[hk][g * rows:(g + 1) * rows, :]
    a_ref[rs, :] = _rms(a_ref[rs, :], gn_ref[...])
    for n in range(ns):
        s = grp * ns + n
        ts = slice(grp * rows + n * dec_len, grp * rows + (n + 1) * dec_len)
        ko_ref[s, 0:wb - dec_len, :] = ck_ref[s, dec_len:wb, :]
        vo_ref[s, 0:wb - dec_len, :] = cv_ref[s, dec_len:wb, :]
        ko_ref[s, wb - dec_len:wb, :] = kvn_ref[ts, 0:KV_WIDTH]
        vo_ref[s, wb - dec_len:wb, :] = kvn_ref[ts, KV_WIDTH:2 * KV_WIDTH]


def _hgrn_bound_consts(lbl_ref):
    lbl = lbl_ref[...]
    e = jnp.exp(lbl - jnp.max(lbl, axis=0, keepdims=True))
    lb = e[0:1, :] / jnp.sum(e, axis=0, keepdims=True)
    return 0.5 + 0.5 * lb, 0.5 - 0.5 * lb


def _hgrn_gates(hp, c0, c1):
    qa = _silu_t(hp[:, 0:HG_WIDTH])
    t = c1 * jnp.tanh(0.5 * hp[:, HG_WIDTH:2 * HG_WIDTH])
    f, kx = c0 + t, c1 - t
    v = hp[:, 2 * HG_WIDTH:3 * HG_WIDTH].astype(BF16)
    return qa, kx, _split2(jnp.log2(f)), v


def _hgrn_factors(qa, kx, g, g_mid, g_last):
    qt = qa * jnp.exp2(g - g_mid)
    kt = kx * jnp.exp2(g_mid - g)
    qg = (qt * jnp.exp2(g_mid)).astype(BF16)
    kd = (kt * jnp.exp2(g_last - g_mid)).astype(BF16)
    return qt.astype(BF16), kt.astype(BF16), qg, kd


def _segment_masks(nseq):
    ls = HG_TILE // nseq
    r = lax.broadcasted_iota(jnp.int32, (HG_TILE, HG_TILE), 0)
    c = lax.broadcasted_iota(jnp.int32, (HG_TILE, HG_TILE), 1)
    same = (r // ls) == (c // ls)
    return same, same & (c <= r), same & (c % ls < ls // 2)


def _as_bf16(mask):
    return mask.astype(F32).astype(BF16)


def _hgrn_short_tile(z_ref, c0, c1, gn, s_in, s_out, o_ref, ti, nseq):
    ch = HG_TILE
    ls = ch // nseq
    same, causal, first_half = _segment_masks(nseq)
    cum_lhs = _as_bf16(jnp.concatenate([causal, first_half, same], axis=0))
    sr = lax.broadcasted_iota(jnp.int32, (ch, nseq * HG_DV), 0)
    sc = lax.broadcasted_iota(jnp.int32, (ch, nseq * HG_DV), 1)
    seg_sel = _as_bf16((sr // ls) == (sc // HG_DV))
    rows = slice(ti * ch, (ti + 1) * ch)

    qa, kx, parts, v = _hgrn_gates(z_ref[rows, Z_HP], c0, c1)
    gs = sum(_dot(cum_lhs, p) for p in parts)
    dcol = sum(_dot_tn(p, seg_sel) for p in parts)
    yield
    qt, kt, qg, kd = _hgrn_factors(qa, kx, gs[0:ch], gs[ch:2 * ch], gs[2 * ch:3 * ch])
    cols = [slice(h * HG_DK, (h + 1) * HG_DK) for h in range(HG_HEADS)]
    scores = [_dot_nt(qt[:, cs], kt[:, cs]) for cs in cols]
    yield
    outs = []
    for h, cs in enumerate(cols):
        a = jnp.where(causal, scores[h], 0.0)
        o = _dot(a.astype(BF16), v[:, cs])
        inter = []
        for n in range(nseq):
            rs = slice(n * ls, (n + 1) * ls)
            s = s_in[ti * nseq + n, h]
            inter.append(_dot(qg[rs, cs], s.astype(BF16)))
            decay = jnp.exp2(dcol[cs, n * HG_DV:(n + 1) * HG_DV])
            s_out[ti * nseq + n, h] = s * decay + _dot_tn(kd[rs, cs], v[rs, cs])
        outs.append(o + jnp.concatenate(inter, axis=0))
    yield
    gate = _silu_t(z_ref[rows, Z_HG])
    for h, cs in enumerate(cols):
        o_ref[rows, cs] = _rms(outs[h], gn) * gate[:, cs]


def _sample_mixer_ffn_kernel(sink_ref, x1_ref, z_ref, kvn_ref, ck_ref, cv_ref, s_in_ref,
                             lbl_ref, gn_ref, ghg_ref, g_mixpost, g_pre, g_post, wout_ref, wgu_ref, wd_ref,
                             y_ref, ko_ref, vo_ref, s_out_ref, a_scr, o_scr, *, dec_len):
    j = pl.program_id(0)
    tile = x1_ref.shape[0]

    @pl.when(j == 0)
    def _():
        a_scr[...] = jnp.zeros(a_scr.shape, F32)
        o_scr[...] = jnp.zeros(o_scr.shape, F32)

    c0, c1 = _hgrn_bound_consts(lbl_ref)
    nseq = HG_TILE // dec_len
    mixers = [_attn_sample_group(sink_ref, z_ref, kvn_ref, ck_ref, cv_ref, gn_ref, a_scr, ko_ref, vo_ref, grp, dec_len)
              for grp in range(tile // (ATTN_SEQ_GROUP * dec_len))]
    mixers += [_hgrn_short_tile(z_ref, c0, c1, ghg_ref[...], s_in_ref, s_out_ref, o_scr, ti, nseq)
               for ti in range(tile // HG_TILE)]
    side, drain = _staged(mixers)

    mix = _dot(a_scr[...].astype(BF16), wout_ref[0:ATTN_WIDTH, :])
    side()
    mix = mix + _dot(o_scr[...].astype(BF16), wout_ref[ATTN_WIDTH:ATTN_WIDTH + HG_WIDTH, :])
    side()
    y_ref[...] = _mix_ffn(x1_ref[...], mix, g_mixpost[...], g_pre[...], g_post[...], wgu_ref, wd_ref, side)
    drain()


def _sample_mixer_ffn(x1, kv, z, cache_k, cache_v, state, sinks, g_attn, lb_logits, g_hg,
                      g_mixpost, g_pre, g_post, wout, wgu, wd, dec_len, row0):
    tile = SAMPLE_TILE
    t = cache_k.shape[0] * dec_len
    n_tiles = t // tile
    spt = tile // dec_len
    assert t % tile == 0 and row0 % tile == 0 and tile % HG_TILE == 0 and tile % (ATTN_SEQ_GROUP * dec_len) == 0
    cur = lambda j: jnp.minimum(j, n_tiles - 1)
    prev = lambda j: jnp.maximum(j - 1, 0)
    cur_spec = lambda width: pl.BlockSpec((tile, width), lambda j: (row0 // tile + cur(j), 0))
    cache_spec = pl.BlockSpec((spt,) + cache_k.shape[1:], lambda j: (cur(j), 0, 0))
    state_spec = pl.BlockSpec((spt,) + state.shape[1:], lambda j: (cur(j), 0, 0, 0))
    return pl.pallas_call(
        functools.partial(_sample_mixer_ffn_kernel, dec_len=dec_len),
        grid=(n_tiles + 1,),
        in_specs=[pl.BlockSpec(memory_space=pltpu.SMEM),
                  pl.BlockSpec((tile, D_MODEL), lambda j: (row0 // tile + prev(j), 0)),
                  cur_spec(Z_WIDTH), cur_spec(2 * KV_WIDTH), cache_spec, cache_spec, state_spec,
                  _const_spec(lb_logits.shape), _const_spec(g_attn.shape), _const_spec(g_hg.shape),
                  _const_spec(g_mixpost.shape), _const_spec(g_pre.shape), _const_spec(g_post.shape),
                  _const_spec(wout.shape), _const_spec(wgu.shape), _const_spec(wd.shape)],
        out_specs=[pl.BlockSpec((tile, D_MODEL), lambda j: (prev(j), 0)), cache_spec, cache_spec, state_spec],
        out_shape=[jax.ShapeDtypeStruct((t, D_MODEL), F32), jax.ShapeDtypeStruct(cache_k.shape, F32),
                   jax.ShapeDtypeStruct(cache_v.shape, F32), jax.ShapeDtypeStruct(state.shape, F32)],
        scratch_shapes=[pltpu.VMEM((tile, ATTN_WIDTH), F32), pltpu.VMEM((tile, HG_WIDTH), F32)],
        compiler_params=pltpu.CompilerParams(dimension_semantics=("arbitrary",),
                                             vmem_limit_bytes=VMEM_LIMIT_BIG),
        name="sample_mixer_ffn",
    )(sinks, x1, z, kv, cache_k, cache_v, state, lb_logits, g_attn, g_hg,
      g_mixpost, g_pre, g_post, wout, wgu, wd)


def _mixer_ffn_kernel(sink_ref, x1_ref, z_ref, kv_ref, lbl_ref, gnt_ref, ghg_ref,
                      g_mixpost, g_pre, g_post, wout_ref, wgu_ref, wd_ref,
                      y_ref, s_out_ref, at_scr, o_scr, st_scr, *, n_tiles, tiles_per_seq):
    j = pl.program_id(0)
    tile = x1_ref.shape[0]
    pos = jnp.minimum(j, n_tiles - 1) % tiles_per_seq

    @pl.when(j == 0)
    def _():
        at_scr[...] = jnp.zeros(at_scr.shape, F32)
        o_scr[...] = jnp.zeros(o_scr.shape, F32)
        st_scr[...] = jnp.zeros(st_scr.shape, F32)

    def attn_block(blk):
        rs = slice(blk * WINDOW, (blk + 1) * WINDOW)

        def store(at):
            at_scr[:, rs] = at

        return _attn_block_t(sink_ref, z_ref[rs, Z_Q], kv_ref, pos * tile + blk * WINDOW, gnt_ref[...], store)

    ch = HG_TILE
    _, causal, _ = _segment_masks(1)
    cum_lhs = _as_bf16(causal)
    c0, c1 = _hgrn_bound_consts(lbl_ref)
    ghg = ghg_ref[...]

    def hgrn_chunk(ci):
        rs = slice(ci * ch, (ci + 1) * ch)
        qa, kx, parts, v = _hgrn_gates(z_ref[rs, Z_HP], c0, c1)
        yield
        g = sum(_dot(cum_lhs, p) for p in parts)
        yield
        g_last = g[ch - 1:ch, :]
        qt, kt, qg, kd = _hgrn_factors(qa, kx, g, g[ch // 2 - 1:ch // 2, :], g_last)
        decay = jnp.exp2(g_last)
        cols = [slice(h * HG_DK, (h + 1) * HG_DK) for h in range(HG_HEADS)]
        yield
        scores = [_dot_nt(qt[:, cs], kt[:, cs]) for cs in cols]
        yield
        probs = [jnp.where(causal, sc, 0.0).astype(BF16) for sc in scores]
        yield
        outs = []
        for h, cs in enumerate(cols):
            st = st_scr[h]
            if ci == 0:
                st = jnp.where(pos == 0, 0.0, st)
            outs.append(_dot(probs[h], v[:, cs]) + _dot_nt(qg[:, cs], st.astype(BF16)))
            st_scr[h] = st * decay[:, cs] + _dot_tn(v[:, cs], kd[:, cs])
        yield
        gate = _silu_t(z_ref[rs, Z_HG])
        for h, cs in enumerate(cols):
            o_scr[rs, cs] = _rms(outs[h], ghg) * gate[:, cs]

    n_blk, n_chunk = tile // WINDOW, tile // ch
    mixers = []
    for blk in range(n_blk):
        mixers += [hgrn_chunk(ci) for ci in range(blk * n_chunk // n_blk, (blk + 1) * n_chunk // n_blk)]
        mixers.insert(len(mixers) - 1, attn_block(blk))
    side, drain = _staged(mixers)

    mix = _dot_tn(at_scr[...].astype(BF16), wout_ref[0:ATTN_WIDTH, :])
    side()
    mix = mix + _dot(o_scr[...].astype(BF16), wout_ref[ATTN_WIDTH:ATTN_WIDTH + HG_WIDTH, :])
    side()
    y_ref[...] = _mix_ffn(x1_ref[...], mix, g_mixpost[...], g_pre[...], g_post[...], wgu_ref, wd_ref, side)
    drain()

    @pl.when((pos == tiles_per_seq - 1) & (j < n_tiles))
    def _():
        b = j // tiles_per_seq
        for h in range(HG_HEADS):
            s_out_ref[pl.ds(b, 1), h] = st_scr[h].T[None]


def _mixer_ffn(x1, kv, z, sinks, g_attn, lb_logits, g_hg, g_mixpost, g_pre, g_post, wout, wgu, wd,
               batch, seq):
    t = batch * seq
    tile = TOKEN_TILE
    assert seq % tile == 0
    n_tiles, tps = t // tile, seq // tile
    gnt = jnp.broadcast_to(g_attn.reshape(ATTN_WIDTH, 1), (ATTN_WIDTH, WINDOW))
    cur = lambda j: jnp.minimum(j, n_tiles - 1)
    prev = lambda j: jnp.maximum(j - 1, 0)
    cur_spec = lambda width: pl.BlockSpec((tile, width), lambda j: (cur(j), 0))
    prev_spec = pl.BlockSpec((tile, D_MODEL), lambda j: (prev(j), 0))
    s_shape = (batch, HG_HEADS, HG_DK, HG_DV)
    return pl.pallas_call(
        functools.partial(_mixer_ffn_kernel, n_tiles=n_tiles, tiles_per_seq=tps),
        grid=(n_tiles + 1,),
        in_specs=[pl.BlockSpec(memory_space=pltpu.SMEM),
                  prev_spec, cur_spec(Z_WIDTH),
                  pl.BlockSpec((seq, 2 * KV_WIDTH), lambda j: (cur(j) // tps, 0)),
                  _const_spec(lb_logits.shape), _const_spec(gnt.shape), _const_spec(g_hg.shape),
                  _const_spec(g_mixpost.shape), _const_spec(g_pre.shape), _const_spec(g_post.shape),
                  _const_spec(wout.shape), _const_spec(wgu.shape), _const_spec(wd.shape)],
        out_specs=[prev_spec, pl.BlockSpec(s_shape, lambda j: (0, 0, 0, 0))],
        out_shape=[jax.ShapeDtypeStruct((t, D_MODEL), F32), jax.ShapeDtypeStruct(s_shape, F32)],
        scratch_shapes=[pltpu.VMEM((ATTN_WIDTH, tile), F32), pltpu.VMEM((tile, HG_WIDTH), F32),
                        pltpu.VMEM((HG_HEADS, HG_DV, HG_DK), F32)],
        compiler_params=pltpu.CompilerParams(dimension_semantics=("arbitrary",),
                                             vmem_limit_bytes=VMEM_LIMIT_BIG),
        name="mixer_ffn",
    )(sinks, x1, z, kv, lb_logits, gnt, g_hg, g_mixpost, g_pre, g_post, wout, wgu, wd)


def kernel(x_prompt, x_sample, cache_k_win, cache_v_win, state_hgrn, w_in, b_in, attn_sinks, attn_out_norm,
           hg_lb_logits, hg_out_norm, w_out, ffn1_w_gu, ffn1_w_down, ffn2_w_gu, ffn2_w_down,
           norm_ffn1_pre, norm_ffn1_post, norm_mix_pre, norm_mix_post, norm_ffn2_pre, norm_ffn2_post):
    depth = w_in.shape[0]
    assert depth == 1, "single-layer trunk"
    batch, seq, _ = x_prompt.shape
    dec_batch, dec_len, _ = x_sample.shape
    wb = cache_k_win.shape[2]
    assert seq % WINDOW == 0 and wb == WINDOW and HG_TILE % dec_len == 0
    layer = 0
    row = lambda p: p[layer].reshape(1, -1).astype(F32)
    wgu1, wd1 = ffn1_w_gu[layer].astype(BF16), ffn1_w_down[layer].astype(BF16)
    wgu2, wd2 = ffn2_w_gu[layer].astype(BF16), ffn2_w_down[layer].astype(BF16)
    win, wout = w_in[layer].astype(BF16), w_out[layer].astype(BF16)
    sinks = attn_sinks[layer].astype(F32)
    lb_logits = hg_lb_logits.astype(F32)
    g_attn, g_hg = row(attn_out_norm), row(hg_out_norm)
    back_params = (row(norm_mix_post), row(norm_ffn2_pre), row(norm_ffn2_post), wout, wgu2, wd2)

    t_p, t_s = batch * seq, dec_batch * dec_len
    x1, kv, z = _ffn_in(x_prompt.reshape(t_p, D_MODEL), x_sample.reshape(t_s, D_MODEL),
                                row(norm_ffn1_pre), row(norm_ffn1_post), row(norm_mix_pre), wgu1, wd1, win, row(b_in))

    y_p, s_prompt = _mixer_ffn(x1, kv, z, sinks, g_attn, lb_logits, g_hg, *back_params, batch, seq)
    y_prompt = y_p.reshape(batch, seq, D_MODEL)
    kv_last = jnp.stack([kv[(b + 1) * seq - WINDOW:(b + 1) * seq] for b in range(batch)])
    k_prompt = kv_last[..., :KV_WIDTH].reshape(1, batch, WINDOW, N_KV_HEADS, HEAD_DIM)
    v_prompt = kv_last[..., KV_WIDTH:].reshape(1, batch, WINDOW, N_KV_HEADS, HEAD_DIM)

    y_s, k_s, v_s, s_sample = _sample_mixer_ffn(
        x1, kv, z, cache_k_win[layer].reshape(dec_batch, wb, KV_WIDTH),
        cache_v_win[layer].reshape(dec_batch, wb, KV_WIDTH), state_hgrn[layer].astype(F32),
        sinks, g_attn, lb_logits, g_hg, *back_params, dec_len, t_p)
    y_sample = y_s.reshape(dec_batch, dec_len, D_MODEL)
    k_sample = k_s.reshape(1, dec_batch, wb, N_KV_HEADS, HEAD_DIM)
    v_sample = v_s.reshape(1, dec_batch, wb, N_KV_HEADS, HEAD_DIM)

    return (y_prompt, y_sample, k_prompt, v_prompt, s_prompt[None], k_sample, v_sample, s_sample[None])
```

```python
import functools

import jax
import jax.numpy as jnp
from jax import lax
from jax.experimental import pallas as pl
from jax.experimental.pallas import tpu as pltpu

F32 = jnp.float32
BF16 = jnp.bfloat16

D_MODEL = 1024
N_HEADS = 8
N_KV_HEADS = 2
HEAD_DIM = 64
GQA_GROUP = N_HEADS // N_KV_HEADS
WINDOW = 128
ATTN_WIDTH = N_HEADS * HEAD_DIM
KV_WIDTH = N_KV_HEADS * HEAD_DIM
SCALE = HEAD_DIM ** -0.5
HG_HEADS = 4
HG_DK = 128
HG_DV = 128
HG_WIDTH = HG_HEADS * HG_DV
D_FF = 2816
EPS = 1e-6

Q_COLS = (0, ATTN_WIDTH)
KV_COLS = (ATTN_WIDTH, ATTN_WIDTH + 2 * KV_WIDTH)
HP_COLS = (KV_COLS[1], KV_COLS[1] + 3 * HG_WIDTH)
HG_COLS = (HP_COLS[1], HP_COLS[1] + HG_WIDTH)
Z_Q = slice(0, ATTN_WIDTH)
Z_HP = slice(ATTN_WIDTH, ATTN_WIDTH + 3 * HG_WIDTH)
Z_HG = slice(ATTN_WIDTH + 3 * HG_WIDTH, ATTN_WIDTH + 4 * HG_WIDTH)
Z_WIDTH = ATTN_WIDTH + 4 * HG_WIDTH

VMEM_LIMIT_BIG = 58 * 1024 * 1024
FF_CHUNKS = ((0, 512), (512, 1024), (1024, 1536), (1536, 2048), (2048, 2560), (2560, 2816))
TOKEN_TILE = 512
SAMPLE_TILE = 128
HG_TILE = 64
ATTN_SEQ_GROUP = 4
NEG = -1e30
LOG2E = 1.4426950408889634


def _rms(x, g):
    return x * lax.rsqrt(jnp.mean(x * x, axis=-1, keepdims=True) + EPS) * g


def _silu(x):
    return x * jax.nn.sigmoid(x)


def _silu_t(x):
    h = 0.5 * x
    return h * jnp.tanh(h) + h


def _dot(a, b):
    return jnp.dot(a, b, preferred_element_type=F32)


def _dot_nt(a, b):
    return lax.dot_general(a, b, (((1,), (1,)), ((), ())), preferred_element_type=F32)


def _dot_tn(a, b):
    return lax.dot_general(a, b, (((0,), (0,)), ((), ())), preferred_element_type=F32)


def _split2(x):
    a = x.astype(BF16)
    b = (x - a.astype(F32)).astype(BF16)
    return a, b


def _no_side_work():
    pass


def _swiglu_ffn(x, g_pre, wgu_ref, wd_ref, g_post, side=_no_side_work):
    h = _rms(x, g_pre).astype(BF16)
    acc = None
    for lo, hi in FF_CHUNKS:
        g = _dot(h, wgu_ref[:, lo:hi])
        side()
        u = _dot(h, wgu_ref[:, D_FF + lo:D_FF + hi])
        side()
        part = _dot((_silu(g) * u).astype(BF16), wd_ref[lo:hi, :])
        side()
        acc = part if acc is None else acc + part
    return x + 0.5 * _rms(acc, g_post)


def _const_spec(shape):
    zeros = (0,) * len(shape)
    return pl.BlockSpec(shape, lambda *_: zeros, pipeline_mode=pl.Buffered(1))


def _row_spec(tile, width):
    return pl.BlockSpec((tile, width), lambda i: (i, 0))


def _ffn_in_kernel(xa_ref, xb_ref, g_pre, g_post, g_mix, wgu_ref, wd_ref, win_ref, bin_ref,
                   x1_ref, kv_ref, z_ref, *, tiles_a):
    x = jnp.where(pl.program_id(0) < tiles_a, xa_ref[...], xb_ref[...])
    x1 = _swiglu_ffn(x, g_pre[...], wgu_ref, wd_ref, g_post[...])
    x1_ref[...] = x1
    h = _rms(x1, g_mix[...]).astype(BF16)

    def project(lo, hi):
        return _dot(h, win_ref[:, lo:hi]) + bin_ref[:, lo:hi]

    kv_ref[...] = project(*KV_COLS)
    z_ref[:, Z_Q] = project(*Q_COLS)
    z_ref[:, Z_HP.start:Z_HG.stop] = project(HP_COLS[0], HG_COLS[1])


def _ffn_in(xa, xb, g_pre, g_post, g_mix, wgu, wd, win, b_in):
    tile = TOKEN_TILE
    assert xa.shape[0] % tile == 0 and xb.shape[0] % tile == 0
    tiles_a, tiles_b = xa.shape[0] // tile, xb.shape[0] // tile
    n_tiles = tiles_a + tiles_b
    widths = [D_MODEL, 2 * KV_WIDTH, Z_WIDTH]
    return pl.pallas_call(
        functools.partial(_ffn_in_kernel, tiles_a=tiles_a),
        grid=(n_tiles,),
        in_specs=[pl.BlockSpec((tile, D_MODEL), lambda j: (jnp.minimum(j, tiles_a - 1), 0)),
                  pl.BlockSpec((tile, D_MODEL), lambda j: (jnp.maximum(j - tiles_a, 0), 0)),
                  _const_spec(g_pre.shape), _const_spec(g_post.shape),
                  _const_spec(g_mix.shape), _const_spec(wgu.shape), _const_spec(wd.shape),
                  _const_spec(win.shape), _const_spec(b_in.shape)],
        out_specs=[_row_spec(tile, w) for w in widths],
        out_shape=[jax.ShapeDtypeStruct((n_tiles * tile, w), F32) for w in widths],
        compiler_params=pltpu.CompilerParams(dimension_semantics=("arbitrary",),
                                             vmem_limit_bytes=VMEM_LIMIT_BIG),
        name="ffn_in",
    )(xa, xb, g_pre, g_post, g_mix, wgu, wd, win, b_in)


def _mix_ffn(x1, mix, g_mixpost, g_pre, g_post, wgu_ref, wd_ref, side=_no_side_work):
    x2 = x1 + _rms(mix, g_mixpost)
    return _swiglu_ffn(x2, g_pre, wgu_ref, wd_ref, g_post, side)


def _staged(mixers):
    waiting, running = list(mixers), []
    n_slots = 3 * len(FF_CHUNKS) + 2
    starts_per_slot = -(-len(waiting) // (2 * n_slots // 3))

    def side():
        for _ in range(min(starts_per_slot, len(waiting))):
            running.append(waiting.pop(0))
        for gen in list(running):
            if next(gen, "done") == "done":
                running.remove(gen)

    def drain():
        while waiting or running:
            side()

    return side, drain


def _attn_block_t(sink_ref, q, kv_ref, q0, gnt, store):
    w = WINDOW
    nkeys = 2 * w
    r = lax.broadcasted_iota(jnp.int32, (nkeys, w), 0)
    c = lax.broadcasted_iota(jnp.int32, (nkeys, w), 1)
    first_half = lax.broadcasted_iota(jnp.int32, (w, 2 * HEAD_DIM), 1) < HEAD_DIM
    k0 = jnp.maximum(q0 - w, 0)
    dist = c + (q0 - k0) - r
    bias = jnp.where((dist >= 0) & (dist <= WINDOW), 0.0, NEG)
    kvb = kv_ref[pl.ds(pl.multiple_of(k0, w), nkeys), :]
    kk = kvb[:, 0:KV_WIDTH]
    k_nat = kk.astype(BF16)
    k_swp = pltpu.roll(kk, HEAD_DIM, axis=1).astype(BF16)
    vt = kvb[:, KV_WIDTH:2 * KV_WIDTH].T.astype(BF16)

    def masked_q(h):
        qp = q[:, (h // 2) * 2 * HEAD_DIM:(h // 2 + 1) * 2 * HEAD_DIM]
        return jnp.where(first_half != bool(h % 2), qp, 0.0).astype(BF16)

    groups = []
    for keys, use_swapped in ((k_nat, False), (k_swp, True)):
        heads = [h for h in range(N_HEADS) if (h // GQA_GROUP != h % 2) == use_swapped]
        groups.append((keys, heads, jnp.concatenate([masked_q(h) for h in heads], axis=0)))
    yield
    scores = [None] * N_HEADS
    for keys, heads, qs in groups:
        st = _dot_nt(keys, qs)
        for i, h in enumerate(heads):
            scores[h] = st[:, i * w:(i + 1) * w]
    yield
    probs, dens = [], []
    for h in range(N_HEADS):
        st = scores[h] * (SCALE * LOG2E) + bias
        sink2 = sink_ref[h] * LOG2E
        m = jnp.maximum(jnp.max(st, axis=0, keepdims=True), sink2)
        p = jnp.exp2(st - m)
        dens.append(jnp.sum(p, axis=0, keepdims=True) + jnp.exp2(sink2 - m))
        probs.append(p.astype(BF16))
    yield
    ot = _dot(vt, jnp.concatenate(probs, axis=1))
    yield
    rows = []
    for h in range(N_HEADS):
        hk = h // GQA_GROUP
        rows.append(ot[hk * HEAD_DIM:(hk + 1) * HEAD_DIM, h * w:(h + 1) * w] / dens[h])
    at = jnp.concatenate(rows, axis=0)
    inv = lax.rsqrt(jnp.sum(at * at, axis=0, keepdims=True) * (1.0 / ATTN_WIDTH) + EPS)
    store(at * inv * gnt)


def _softmax_sink_pv(scores, masks, values, sink):
    scores = [jnp.where(mk, s * SCALE, NEG) for s, mk in zip(scores, masks)]
    m = sink
    for s in scores:
        m = jnp.maximum(m, jnp.max(s, axis=-1, keepdims=True))
    den = jnp.exp(sink - m)
    out = None
    for s, (v, transposed) in zip(scores, values):
        p = jnp.exp(s - m)
        den = den + jnp.sum(p, axis=-1, keepdims=True)
        pv = (_dot_nt if transposed else _dot)(p.astype(BF16), v)
        out = pv if out is None else out + pv
    return out / den


def _attn_sample_group(sink_ref, z_ref, kvn_ref, ck_ref, cv_ref, gn_ref, a_ref, ko_ref, vo_ref, grp, dec_len):
    ns = ATTN_SEQ_GROUP
    rows = ns * dec_len
    wb = ck_ref.shape[2]
    m_rows = GQA_GROUP * rows
    r = lax.broadcasted_iota(jnp.int32, (m_rows, ns * wb), 0)
    c = lax.broadcasted_iota(jnp.int32, (m_rows, ns * wb), 1)
    r_seq, r_pos = (r % rows) // dec_len, r % dec_len
    mask_cache = (r_seq == c // wb) & (c % wb >= r_pos)
    r = lax.broadcasted_iota(jnp.int32, (m_rows, rows), 0)
    c = lax.broadcasted_iota(jnp.int32, (m_rows, rows), 1)
    mask_new = ((r % rows) // dec_len == c // dec_len) & (c % dec_len <= r % dec_len)
    row_head = lax.broadcasted_iota(jnp.int32, (m_rows, 1), 0) // rows

    rs = slice(grp * rows, (grp + 1) * rows)
    q = z_ref[rs, Z_Q]
    kvn_b = kvn_ref[rs, :].astype(BF16)

    def cached(ref, hk):
        frows = slice(hk * HEAD_DIM, (hk + 1) * HEAD_DIM)
        return jnp.concatenate([ref[grp * ns + n, frows, :] for n in range(ns)], axis=1).astype(BF16)

    scores = []
    for hk in range(N_KV_HEADS):
        kcols = slice(hk * HEAD_DIM, (hk + 1) * HEAD_DIM)
        heads = [hk * GQA_GROUP + g for g in range(GQA_GROUP)]
        qs = jnp.concatenate([q[:, h * HEAD_DIM:(h + 1) * HEAD_DIM] for h in heads], axis=0).astype(BF16)
        scores.append([_dot(qs, cached(ck_ref, hk)), _dot_nt(qs, kvn_b[:, kcols])])
    yield
    outs = []
    for hk in range(N_KV_HEADS):
        vcols = slice(KV_WIDTH + hk * HEAD_DIM, KV_WIDTH + (hk + 1) * HEAD_DIM)
        sink = jnp.zeros((m_rows, 1), F32)
        for g in range(GQA_GROUP):
            sink = jnp.where(row_head == g, sink_ref[hk * GQA_GROUP + g], sink)
        values = [(cached(cv_ref, hk), True), (kvn_b[:, vcols], False)]
        outs.append(_softmax_sink_pv(scores[hk], [mask_cache, mask_new], values, sink))
    yield
    for hk in range(N_KV_HEADS):
        for g in range(GQA_GROUP):
            h = hk * GQA_GROUP + g
            a_ref[rs, h * HEAD_DIM:(h + 1) * HEAD_DIM] = outs[hk][g * rows:(g + 1) * rows, :]
    a_ref[rs, :] = _rms(a_ref[rs, :], gn_ref[...])
    keep = wb - dec_len
    is_new = lax.broadcasted_iota(jnp.int32, (KV_WIDTH, wb), 1) >= keep
    for n in range(ns):
        s = grp * ns + n
        ts = slice(grp * rows + n * dec_len, grp * rows + (n + 1) * dec_len)
        for src, dst, cols in ((ck_ref, ko_ref, slice(0, KV_WIDTH)), (cv_ref, vo_ref, slice(KV_WIDTH, 2 * KV_WIDTH))):
            new_t = jnp.concatenate([jnp.zeros((keep, KV_WIDTH), F32), kvn_ref[ts, cols]], axis=0).T
            dst[s] = jnp.where(is_new, new_t, pltpu.roll(src[s], keep, axis=1))


def _hgrn_bound_consts(lbl_ref):
    lbl = lbl_ref[...]
    e = jnp.exp(lbl - jnp.max(lbl, axis=0, keepdims=True))
    lb = e[0:1, :] / jnp.sum(e, axis=0, keepdims=True)
    return 0.5 + 0.5 * lb, 0.5 - 0.5 * lb


def _hgrn_gates(hp, c0, c1):
    qa = _silu_t(hp[:, 0:HG_WIDTH])
    t = c1 * jnp.tanh(0.5 * hp[:, HG_WIDTH:2 * HG_WIDTH])
    f, kx = c0 + t, c1 - t
    v = hp[:, 2 * HG_WIDTH:3 * HG_WIDTH].astype(BF16)
    return qa, kx, _split2(jnp.log2(f)), v


def _hgrn_factors(qa, kx, g, g_mid, g_last):
    qt = qa * jnp.exp2(g - g_mid)
    kt = kx * jnp.exp2(g_mid - g)
    qg = (qt * jnp.exp2(g_mid)).astype(BF16)
    kd = (kt * jnp.exp2(g_last - g_mid)).astype(BF16)
    return qt.astype(BF16), kt.astype(BF16), qg, kd


def _segment_masks(nseq):
    ls = HG_TILE // nseq
    r = lax.broadcasted_iota(jnp.int32, (HG_TILE, HG_TILE), 0)
    c = lax.broadcasted_iota(jnp.int32, (HG_TILE, HG_TILE), 1)
    same = (r // ls) == (c // ls)
    return same, same & (c <= r), same & (c % ls < ls // 2)


def _as_bf16(mask):
    return mask.astype(F32).astype(BF16)


def _hgrn_short_tile(z_ref, c0, c1, gn, s_in, s_out, o_ref, ti, nseq):
    ch = HG_TILE
    ls = ch // nseq
    same, causal, first_half = _segment_masks(nseq)
    cum_lhs = _as_bf16(jnp.concatenate([causal, first_half, same], axis=0))
    sr = lax.broadcasted_iota(jnp.int32, (ch, nseq * HG_DV), 0)
    sc = lax.broadcasted_iota(jnp.int32, (ch, nseq * HG_DV), 1)
    seg_sel = _as_bf16((sr // ls) == (sc // HG_DV))
    rows = slice(ti * ch, (ti + 1) * ch)

    qa, kx, parts, v = _hgrn_gates(z_ref[rows, Z_HP], c0, c1)
    gs = sum(_dot(cum_lhs, p) for p in parts)
    dcol = sum(_dot_tn(p, seg_sel) for p in parts)
    yield
    qt, kt, qg, kd = _hgrn_factors(qa, kx, gs[0:ch], gs[ch:2 * ch], gs[2 * ch:3 * ch])
    cols = [slice(h * HG_DK, (h + 1) * HG_DK) for h in range(HG_HEADS)]
    scores = [_dot_nt(qt[:, cs], kt[:, cs]) for cs in cols]
    yield
    outs = []
    for h, cs in enumerate(cols):
        a = jnp.where(causal, scores[h], 0.0)
        o = _dot(a.astype(BF16), v[:, cs])
        inter = []
        for n in range(nseq):
            rs = slice(n * ls, (n + 1) * ls)
            s = s_in[ti * nseq + n, h]
            inter.append(_dot(qg[rs, cs], s.astype(BF16)))
            decay = jnp.exp2(dcol[cs, n * HG_DV:(n + 1) * HG_DV])
            s_out[ti * nseq + n, h] = s * decay + _dot_tn(kd[rs, cs], v[rs, cs])
        outs.append(o + jnp.concatenate(inter, axis=0))
    yield
    gate = _silu_t(z_ref[rows, Z_HG])
    for h, cs in enumerate(cols):
        o_ref[rows, cs] = _rms(outs[h], gn) * gate[:, cs]


def _sample_mixer_ffn_kernel(sink_ref, x1_ref, z_ref, kvn_ref, ck_ref, cv_ref, s_in_ref,
                             lbl_ref, gn_ref, ghg_ref, g_mixpost, g_pre, g_post, wout_ref, wgu_ref, wd_ref,
                             y_ref, ko_ref, vo_ref, s_out_ref, a_scr, o_scr, *, dec_len):
    j = pl.program_id(0)
    tile = x1_ref.shape[0]

    @pl.when(j == 0)
    def _():
        a_scr[...] = jnp.zeros(a_scr.shape, F32)
        o_scr[...] = jnp.zeros(o_scr.shape, F32)

    c0, c1 = _hgrn_bound_consts(lbl_ref)
    nseq = HG_TILE // dec_len
    mixers = [_attn_sample_group(sink_ref, z_ref, kvn_ref, ck_ref, cv_ref, gn_ref, a_scr, ko_ref, vo_ref, grp, dec_len)
              for grp in range(tile // (ATTN_SEQ_GROUP * dec_len))]
    mixers += [_hgrn_short_tile(z_ref, c0, c1, ghg_ref[...], s_in_ref, s_out_ref, o_scr, ti, nseq)
               for ti in range(tile // HG_TILE)]
    side, drain = _staged(mixers)

    mix = _dot(a_scr[...].astype(BF16), wout_ref[0:ATTN_WIDTH, :])
    side()
    mix = mix + _dot(o_scr[...].astype(BF16), wout_ref[ATTN_WIDTH:ATTN_WIDTH + HG_WIDTH, :])
    side()
    y_ref[...] = _mix_ffn(x1_ref[...], mix, g_mixpost[...], g_pre[...], g_post[...], wgu_ref, wd_ref, side)
    drain()


def _sample_mixer_ffn(x1, kv, z, cache_k, cache_v, state, sinks, g_attn, lb_logits, g_hg,
                      g_mixpost, g_pre, g_post, wout, wgu, wd, dec_len, row0):
    tile = SAMPLE_TILE
    t = cache_k.shape[0] * dec_len
    n_tiles = t // tile
    spt = tile // dec_len
    assert t % tile == 0 and row0 % tile == 0 and tile % HG_TILE == 0 and tile % (ATTN_SEQ_GROUP * dec_len) == 0
    cur = lambda j: jnp.minimum(j, n_tiles - 1)
    prev = lambda j: jnp.maximum(j - 1, 0)
    cur_spec = lambda width: pl.BlockSpec((tile, width), lambda j: (row0 // tile + cur(j), 0))
    cache_spec = pl.BlockSpec((spt,) + cache_k.shape[1:], lambda j: (cur(j), 0, 0))
    state_spec = pl.BlockSpec((spt,) + state.shape[1:], lambda j: (cur(j), 0, 0, 0))
    return pl.pallas_call(
        functools.partial(_sample_mixer_ffn_kernel, dec_len=dec_len),
        grid=(n_tiles + 1,),
        in_specs=[pl.BlockSpec(memory_space=pltpu.SMEM),
                  pl.BlockSpec((tile, D_MODEL), lambda j: (row0 // tile + prev(j), 0)),
                  cur_spec(Z_WIDTH), cur_spec(2 * KV_WIDTH), cache_spec, cache_spec, state_spec,
                  _const_spec(lb_logits.shape), _const_spec(g_attn.shape), _const_spec(g_hg.shape),
                  _const_spec(g_mixpost.shape), _const_spec(g_pre.shape), _const_spec(g_post.shape),
                  _const_spec(wout.shape), _const_spec(wgu.shape), _const_spec(wd.shape)],
        out_specs=[pl.BlockSpec((tile, D_MODEL), lambda j: (prev(j), 0)), cache_spec, cache_spec, state_spec],
        out_shape=[jax.ShapeDtypeStruct((t, D_MODEL), F32), jax.ShapeDtypeStruct(cache_k.shape, F32),
                   jax.ShapeDtypeStruct(cache_v.shape, F32), jax.ShapeDtypeStruct(state.shape, F32)],
        scratch_shapes=[pltpu.VMEM((tile, ATTN_WIDTH), F32), pltpu.VMEM((tile, HG_WIDTH), F32)],
        compiler_params=pltpu.CompilerParams(dimension_semantics=("arbitrary",),
                                             vmem_limit_bytes=VMEM_LIMIT_BIG),
        name="sample_mixer_ffn",
    )(sinks, x1, z, kv, cache_k, cache_v, state, lb_logits, g_attn, g_hg,
      g_mixpost, g_pre, g_post, wout, wgu, wd)


def _mixer_ffn_kernel(sink_ref, x1_ref, z_ref, kv_ref, lbl_ref, gnt_ref, ghg_ref,
                      g_mixpost, g_pre, g_post, wout_ref, wgu_ref, wd_ref,
                      y_ref, s_out_ref, at_scr, o_scr, st_scr, *, n_tiles, tiles_per_seq):
    j = pl.program_id(0)
    tile = x1_ref.shape[0]
    pos = jnp.minimum(j, n_tiles - 1) % tiles_per_seq

    @pl.when(j == 0)
    def _():
        at_scr[...] = jnp.zeros(at_scr.shape, F32)
        o_scr[...] = jnp.zeros(o_scr.shape, F32)
        st_scr[...] = jnp.zeros(st_scr.shape, F32)

    def attn_block(blk):
        rs = slice(blk * WINDOW, (blk + 1) * WINDOW)

        def store(at):
            at_scr[:, rs] = at

        return _attn_block_t(sink_ref, z_ref[rs, Z_Q], kv_ref, pos * tile + blk * WINDOW, gnt_ref[...], store)

    ch = HG_TILE
    _, causal, _ = _segment_masks(1)
    cum_lhs = _as_bf16(causal)
    c0, c1 = _hgrn_bound_consts(lbl_ref)
    ghg = ghg_ref[...]

    def hgrn_chunk(ci):
        rs = slice(ci * ch, (ci + 1) * ch)
        qa, kx, parts, v = _hgrn_gates(z_ref[rs, Z_HP], c0, c1)
        yield
        g = sum(_dot(cum_lhs, p) for p in parts)
        yield
        g_last = g[ch - 1:ch, :]
        qt, kt, qg, kd = _hgrn_factors(qa, kx, g, g[ch // 2 - 1:ch // 2, :], g_last)
        decay = jnp.exp2(g_last)
        cols = [slice(h * HG_DK, (h + 1) * HG_DK) for h in range(HG_HEADS)]
        yield
        scores = [_dot_nt(qt[:, cs], kt[:, cs]) for cs in cols]
        yield
        probs = [jnp.where(causal, sc, 0.0).astype(BF16) for sc in scores]
        yield
        outs = []
        for h, cs in enumerate(cols):
            st = st_scr[h]
            if ci == 0:
                st = jnp.where(pos == 0, 0.0, st)
            outs.append(_dot(probs[h], v[:, cs]) + _dot_nt(qg[:, cs], st.astype(BF16)))
            st_scr[h] = st * decay[:, cs] + _dot_tn(v[:, cs], kd[:, cs])
        yield
        gate = _silu_t(z_ref[rs, Z_HG])
        for h, cs in enumerate(cols):
            o_scr[rs, cs] = _rms(outs[h], ghg) * gate[:, cs]

    n_blk, n_chunk = tile // WINDOW, tile // ch
    mixers = []
    for blk in range(n_blk):
        mixers += [hgrn_chunk(ci) for ci in range(blk * n_chunk // n_blk, (blk + 1) * n_chunk // n_blk)]
        mixers.insert(len(mixers) - 1, attn_block(blk))
    side, drain = _staged(mixers)

    mix = _dot_tn(at_scr[...].astype(BF16), wout_ref[0:ATTN_WIDTH, :])
    side()
    mix = mix + _dot(o_scr[...].astype(BF16), wout_ref[ATTN_WIDTH:ATTN_WIDTH + HG_WIDTH, :])
    side()
    y_ref[...] = _mix_ffn(x1_ref[...], mix, g_mixpost[...], g_pre[...], g_post[...], wgu_ref, wd_ref, side)
    drain()

    @pl.when((pos == tiles_per_seq - 1) & (j < n_tiles))
    def _():
        b = j // tiles_per_seq
        for h in range(HG_HEADS):
            s_out_ref[pl.ds(b, 1), h] = st_scr[h].T[None]


def _mixer_ffn(x1, kv, z, sinks, g_attn, lb_logits, g_hg, g_mixpost, g_pre, g_post, wout, wgu, wd,
               batch, seq):
    t = batch * seq
    tile = TOKEN_TILE
    assert seq % tile == 0
    n_tiles, tps = t // tile, seq // tile
    gnt = jnp.broadcast_to(g_attn.reshape(ATTN_WIDTH, 1), (ATTN_WIDTH, WINDOW))
    cur = lambda j: jnp.minimum(j, n_tiles - 1)
    prev = lambda j: jnp.maximum(j - 1, 0)
    cur_spec = lambda width: pl.BlockSpec((tile, width), lambda j: (cur(j), 0))
    prev_spec = pl.BlockSpec((tile, D_MODEL), lambda j: (prev(j), 0))
    s_shape = (batch, HG_HEADS, HG_DK, HG_DV)
    return pl.pallas_call(
        functools.partial(_mixer_ffn_kernel, n_tiles=n_tiles, tiles_per_seq=tps),
        grid=(n_tiles + 1,),
        in_specs=[pl.BlockSpec(memory_space=pltpu.SMEM),
                  prev_spec, cur_spec(Z_WIDTH),
                  pl.BlockSpec((seq, 2 * KV_WIDTH), lambda j: (cur(j) // tps, 0)),
                  _const_spec(lb_logits.shape), _const_spec(gnt.shape), _const_spec(g_hg.shape),
                  _const_spec(g_mixpost.shape), _const_spec(g_pre.shape), _const_spec(g_post.shape),
                  _const_spec(wout.shape), _const_spec(wgu.shape), _const_spec(wd.shape)],
        out_specs=[prev_spec, pl.BlockSpec(s_shape, lambda j: (0, 0, 0, 0))],
        out_shape=[jax.ShapeDtypeStruct((t, D_MODEL), F32), jax.ShapeDtypeStruct(s_shape, F32)],
        scratch_shapes=[pltpu.VMEM((ATTN_WIDTH, tile), F32), pltpu.VMEM((tile, HG_WIDTH), F32),
                        pltpu.VMEM((HG_HEADS, HG_DV, HG_DK), F32)],
        compiler_params=pltpu.CompilerParams(dimension_semantics=("arbitrary",),
                                             vmem_limit_bytes=VMEM_LIMIT_BIG),
        name="mixer_ffn",
    )(sinks, x1, z, kv, lb_logits, gnt, g_hg, g_mixpost, g_pre, g_post, wout, wgu, wd)


def kernel(x_prompt, x_sample, cache_k_win, cache_v_win, state_hgrn, w_in, b_in, attn_sinks, attn_out_norm,
           hg_lb_logits, hg_out_norm, w_out, ffn1_w_gu, ffn1_w_down, ffn2_w_gu, ffn2_w_down,
           norm_ffn1_pre, norm_ffn1_post, norm_mix_pre, norm_mix_post, norm_ffn2_pre, norm_ffn2_post):
    depth = w_in.shape[0]
    assert depth == 1, "single-layer trunk"
    batch, seq, _ = x_prompt.shape
    dec_batch, dec_len, _ = x_sample.shape
    wb = cache_k_win.shape[2]
    assert seq % WINDOW == 0 and wb == WINDOW and HG_TILE % dec_len == 0
    layer = 0
    row = lambda p: p[layer].reshape(1, -1).astype(F32)
    wgu1, wd1 = ffn1_w_gu[layer].astype(BF16), ffn1_w_down[layer].astype(BF16)
    wgu2, wd2 = ffn2_w_gu[layer].astype(BF16), ffn2_w_down[layer].astype(BF16)
    win, wout = w_in[layer].astype(BF16), w_out[layer].astype(BF16)
    sinks = attn_sinks[layer].astype(F32)
    lb_logits = hg_lb_logits.astype(F32)
    g_attn, g_hg = row(attn_out_norm), row(hg_out_norm)
    back_params = (row(norm_mix_post), row(norm_ffn2_pre), row(norm_ffn2_post), wout, wgu2, wd2)

    t_p, t_s = batch * seq, dec_batch * dec_len
    x1, kv, z = _ffn_in(x_prompt.reshape(t_p, D_MODEL), x_sample.reshape(t_s, D_MODEL),
                                row(norm_ffn1_pre), row(norm_ffn1_post), row(norm_mix_pre), wgu1, wd1, win, row(b_in))

    y_p, s_prompt = _mixer_ffn(x1, kv, z, sinks, g_attn, lb_logits, g_hg, *back_params, batch, seq)
    y_prompt = y_p.reshape(batch, seq, D_MODEL)
    kv_last = jnp.stack([kv[(b + 1) * seq - WINDOW:(b + 1) * seq] for b in range(batch)])
    k_prompt = kv_last[..., :KV_WIDTH].reshape(1, batch, WINDOW, N_KV_HEADS, HEAD_DIM)
    v_prompt = kv_last[..., KV_WIDTH:].reshape(1, batch, WINDOW, N_KV_HEADS, HEAD_DIM)

    def feature_major(buf):
        return jnp.transpose(buf, (0, 2, 3, 1)).reshape(dec_batch, KV_WIDTH, wb)

    def window_major(buf_t):
        return jnp.transpose(buf_t.reshape(dec_batch, N_KV_HEADS, HEAD_DIM, wb), (0, 3, 1, 2))[None]

    y_s, k_s, v_s, s_sample = _sample_mixer_ffn(
        x1, kv, z, feature_major(cache_k_win[layer]), feature_major(cache_v_win[layer]),
        state_hgrn[layer].astype(F32), sinks, g_attn, lb_logits, g_hg, *back_params, dec_len, t_p)
    y_sample = y_s.reshape(dec_batch, dec_len, D_MODEL)
    k_sample, v_sample = window_major(k_s), window_major(v_s)

    return (y_prompt, y_sample, k_prompt, v_prompt, s_prompt[None], k_sample, v_sample, s_sample[None])
```

```python
import functools

import jax
import jax.numpy as jnp
from jax import lax
from jax.experimental import pallas as pl
from jax.experimental.pallas import tpu as pltpu

F32 = jnp.float32
BF16 = jnp.bfloat16

D_MODEL = 1024
N_HEADS = 8
N_KV_HEADS = 2
HEAD_DIM = 64
GQA_GROUP = N_HEADS // N_KV_HEADS
WINDOW = 128
ATTN_WIDTH = N_HEADS * HEAD_DIM
KV_WIDTH = N_KV_HEADS * HEAD_DIM
SCALE = HEAD_DIM ** -0.5
HG_HEADS = 4
HG_DK = 128
HG_DV = 128
HG_WIDTH = HG_HEADS * HG_DV
D_FF = 2816
EPS = 1e-6

Q_COLS = (0, ATTN_WIDTH)
KV_COLS = (ATTN_WIDTH, ATTN_WIDTH + 2 * KV_WIDTH)
HP_COLS = (KV_COLS[1], KV_COLS[1] + 3 * HG_WIDTH)
HG_COLS = (HP_COLS[1], HP_COLS[1] + HG_WIDTH)
Z_Q = slice(0, ATTN_WIDTH)
Z_HP = slice(ATTN_WIDTH, ATTN_WIDTH + 3 * HG_WIDTH)
Z_HG = slice(ATTN_WIDTH + 3 * HG_WIDTH, ATTN_WIDTH + 4 * HG_WIDTH)
Z_WIDTH = ATTN_WIDTH + 4 * HG_WIDTH

VMEM_LIMIT_BIG = 58 * 1024 * 1024
FF_CHUNKS = ((0, 512), (512, 1024), (1024, 1536), (1536, 2048), (2048, 2560), (2560, 2816))
TOKEN_TILE = 512
SAMPLE_TILE = 128
HG_TILE = 64
ATTN_SEQ_GROUP = 4
NEG = -1e30
LOG2E = 1.4426950408889634


def _rms(x, g):
    return x * lax.rsqrt(jnp.mean(x * x, axis=-1, keepdims=True) + EPS) * g


def _silu(x):
    return x * jax.nn.sigmoid(x)


def _silu_t(x):
    h = 0.5 * x
    return h * jnp.tanh(h) + h


def _dot(a, b):
    return jnp.dot(a, b, preferred_element_type=F32)


def _dot_nt(a, b):
    return lax.dot_general(a, b, (((1,), (1,)), ((), ())), preferred_element_type=F32)


def _dot_tn(a, b):
    return lax.dot_general(a, b, (((0,), (0,)), ((), ())), preferred_element_type=F32)


def _split2(x):
    a = x.astype(BF16)
    b = (x - a.astype(F32)).astype(BF16)
    return a, b


def _no_side_work():
    pass


def _swiglu_ffn(x, g_pre, wgu_ref, wd_ref, g_post, side=_no_side_work):
    h = _rms(x, g_pre).astype(BF16)
    acc = None
    for lo, hi in FF_CHUNKS:
        g = _dot(h, wgu_ref[:, lo:hi])
        side()
        u = _dot(h, wgu_ref[:, D_FF + lo:D_FF + hi])
        side()
        part = _dot((_silu(g) * u).astype(BF16), wd_ref[lo:hi, :])
        side()
        acc = part if acc is None else acc + part
    return x + 0.5 * _rms(acc, g_post)


def _const_spec(shape):
    zeros = (0,) * len(shape)
    return pl.BlockSpec(shape, lambda *_: zeros, pipeline_mode=pl.Buffered(1))


def _row_spec(tile, width):
    return pl.BlockSpec((tile, width), lambda i: (i, 0))


def _ffn_in_kernel(xa_ref, xb_ref, g_pre, g_post, g_mix, wgu_ref, wd_ref, win_ref, bin_ref,
                   x1_ref, kv_ref, z_ref, *, tiles_a):
    x = jnp.where(pl.program_id(0) < tiles_a, xa_ref[...], xb_ref[...])
    x1 = _swiglu_ffn(x, g_pre[...], wgu_ref, wd_ref, g_post[...])
    x1_ref[...] = x1
    h = _rms(x1, g_mix[...]).astype(BF16)

    def project(lo, hi):
        return _dot(h, win_ref[:, lo:hi]) + bin_ref[:, lo:hi]

    kv_ref[...] = project(*KV_COLS)
    z_ref[:, Z_Q] = project(*Q_COLS)
    z_ref[:, Z_HP.start:Z_HG.stop] = project(HP_COLS[0], HG_COLS[1])


def _ffn_in(xa, xb, g_pre, g_post, g_mix, wgu, wd, win, b_in):
    tile = TOKEN_TILE
    assert xa.shape[0] % tile == 0 and xb.shape[0] % tile == 0
    tiles_a, tiles_b = xa.shape[0] // tile, xb.shape[0] // tile
    n_tiles = tiles_a + tiles_b
    widths = [D_MODEL, 2 * KV_WIDTH, Z_WIDTH]
    return pl.pallas_call(
        functools.partial(_ffn_in_kernel, tiles_a=tiles_a),
        grid=(n_tiles,),
        in_specs=[pl.BlockSpec((tile, D_MODEL), lambda j: (jnp.minimum(j, tiles_a - 1), 0)),
                  pl.BlockSpec((tile, D_MODEL), lambda j: (jnp.maximum(j - tiles_a, 0), 0)),
                  _const_spec(g_pre.shape), _const_spec(g_post.shape),
                  _const_spec(g_mix.shape), _const_spec(wgu.shape), _const_spec(wd.shape),
                  _const_spec(win.shape), _const_spec(b_in.shape)],
        out_specs=[_row_spec(tile, w) for w in widths],
        out_shape=[jax.ShapeDtypeStruct((n_tiles * tile, w), F32) for w in widths],
        compiler_params=pltpu.CompilerParams(dimension_semantics=("arbitrary",),
                                             vmem_limit_bytes=VMEM_LIMIT_BIG),
        name="ffn_in",
    )(xa, xb, g_pre, g_post, g_mix, wgu, wd, win, b_in)


def _mix_ffn(x1, mix, g_mixpost, g_pre, g_post, wgu_ref, wd_ref, side=_no_side_work):
    x2 = x1 + _rms(mix, g_mixpost)
    return _swiglu_ffn(x2, g_pre, wgu_ref, wd_ref, g_post, side)


def _staged(mixers):
    waiting, running = list(mixers), []
    n_slots = 3 * len(FF_CHUNKS) + 2
    starts_per_slot = -(-len(waiting) // (2 * n_slots // 3))

    def side():
        for _ in range(min(starts_per_slot, len(waiting))):
            running.append(waiting.pop(0))
        for gen in list(running):
            if next(gen, "done") == "done":
                running.remove(gen)

    def drain():
        while waiting or running:
            side()

    return side, drain


def _skewed_step(j, last, make_mixers, ffn_part):
    @pl.when(j == 0)
    def _():
        _staged(make_mixers())[1]()

    @pl.when((j > 0) & (j < last))
    def _():
        side, drain = _staged(make_mixers())
        ffn_part(side)
        drain()

    @pl.when(j == last)
    def _():
        ffn_part(_no_side_work)


def _attn_block_t(sink_ref, q, kv_ref, q0, gnt, store):
    w = WINDOW
    nkeys = 2 * w
    r = lax.broadcasted_iota(jnp.int32, (nkeys, w), 0)
    c = lax.broadcasted_iota(jnp.int32, (nkeys, w), 1)
    first_half = lax.broadcasted_iota(jnp.int32, (w, 2 * HEAD_DIM), 1) < HEAD_DIM
    k0 = jnp.maximum(q0 - w, 0)
    dist = c + (q0 - k0) - r
    bias = jnp.where((dist >= 0) & (dist <= WINDOW), 0.0, NEG)
    kvb = kv_ref[pl.ds(pl.multiple_of(k0, w), nkeys), :]
    kk = kvb[:, 0:KV_WIDTH]
    k_nat = kk.astype(BF16)
    k_swp = pltpu.roll(kk, HEAD_DIM, axis=1).astype(BF16)
    vt = kvb[:, KV_WIDTH:2 * KV_WIDTH].T.astype(BF16)

    def masked_q(h):
        qp = q[:, (h // 2) * 2 * HEAD_DIM:(h // 2 + 1) * 2 * HEAD_DIM]
        return jnp.where(first_half != bool(h % 2), qp, 0.0).astype(BF16)

    groups = []
    for keys, use_swapped in ((k_nat, False), (k_swp, True)):
        heads = [h for h in range(N_HEADS) if (h // GQA_GROUP != h % 2) == use_swapped]
        groups.append((keys, heads, jnp.concatenate([masked_q(h) for h in heads], axis=0)))
    yield
    scores = [None] * N_HEADS
    for keys, heads, qs in groups:
        st = _dot_nt(keys, qs)
        for i, h in enumerate(heads):
            scores[h] = st[:, i * w:(i + 1) * w]
    yield
    probs, dens = [], []
    for h in range(N_HEADS):
        st = scores[h] * (SCALE * LOG2E) + bias
        sink2 = sink_ref[h] * LOG2E
        m = jnp.maximum(jnp.max(st, axis=0, keepdims=True), sink2)
        p = jnp.exp2(st - m)
        dens.append(jnp.sum(p, axis=0, keepdims=True) + jnp.exp2(sink2 - m))
        probs.append(p.astype(BF16))
    yield
    ot = _dot(vt, jnp.concatenate(probs, axis=1))
    yield
    rows = []
    for h in range(N_HEADS):
        hk = h // GQA_GROUP
        rows.append(ot[hk * HEAD_DIM:(hk + 1) * HEAD_DIM, h * w:(h + 1) * w] / dens[h])
    at = jnp.concatenate(rows, axis=0)
    inv = lax.rsqrt(jnp.sum(at * at, axis=0, keepdims=True) * (1.0 / ATTN_WIDTH) + EPS)
    store(at * inv * gnt)


def _softmax_sink_pv(scores, masks, values, sink):
    scores = [jnp.where(mk, s * SCALE, NEG) for s, mk in zip(scores, masks)]
    m = sink
    for s in scores:
        m = jnp.maximum(m, jnp.max(s, axis=-1, keepdims=True))
    den = jnp.exp(sink - m)
    out = None
    for s, (v, transposed) in zip(scores, values):
        p = jnp.exp(s - m)
        den = den + jnp.sum(p, axis=-1, keepdims=True)
        pv = (_dot_nt if transposed else _dot)(p.astype(BF16), v)
        out = pv if out is None else out + pv
    return out / den


def _attn_sample_group(sink_ref, z_ref, kvn_ref, ck_ref, cv_ref, gn_ref, a_ref, ko_ref, vo_ref, grp, dec_len):
    ns = ATTN_SEQ_GROUP
    rows = ns * dec_len
    wb = ck_ref.shape[2]
    m_rows = GQA_GROUP * rows
    r = lax.broadcasted_iota(jnp.int32, (m_rows, ns * wb), 0)
    c = lax.broadcasted_iota(jnp.int32, (m_rows, ns * wb), 1)
    r_seq, r_pos = (r % rows) // dec_len, r % dec_len
    mask_cache = (r_seq == c // wb) & (c % wb >= r_pos)
    r = lax.broadcasted_iota(jnp.int32, (m_rows, rows), 0)
    c = lax.broadcasted_iota(jnp.int32, (m_rows, rows), 1)
    mask_new = ((r % rows) // dec_len == c // dec_len) & (c % dec_len <= r % dec_len)
    row_head = lax.broadcasted_iota(jnp.int32, (m_rows, 1), 0) // rows

    rs = slice(grp * rows, (grp + 1) * rows)
    q = z_ref[rs, Z_Q]
    kvn_b = kvn_ref[rs, :].astype(BF16)

    def cached(ref, hk):
        frows = slice(hk * HEAD_DIM, (hk + 1) * HEAD_DIM)
        return jnp.concatenate([ref[grp * ns + n, frows, :] for n in range(ns)], axis=1).astype(BF16)

    scores = []
    for hk in range(N_KV_HEADS):
        kcols = slice(hk * HEAD_DIM, (hk + 1) * HEAD_DIM)
        heads = [hk * GQA_GROUP + g for g in range(GQA_GROUP)]
        qs = jnp.concatenate([q[:, h * HEAD_DIM:(h + 1) * HEAD_DIM] for h in heads], axis=0).astype(BF16)
        scores.append([_dot(qs, cached(ck_ref, hk)), _dot_nt(qs, kvn_b[:, kcols])])
    yield
    outs = []
    for hk in range(N_KV_HEADS):
        vcols = slice(KV_WIDTH + hk * HEAD_DIM, KV_WIDTH + (hk + 1) * HEAD_DIM)
        sink = jnp.zeros((m_rows, 1), F32)
        for g in range(GQA_GROUP):
            sink = jnp.where(row_head == g, sink_ref[hk * GQA_GROUP + g], sink)
        values = [(cached(cv_ref, hk), True), (kvn_b[:, vcols], False)]
        outs.append(_softmax_sink_pv(scores[hk], [mask_cache, mask_new], values, sink))
    yield
    for hk in range(N_KV_HEADS):
        for g in range(GQA_GROUP):
            h = hk * GQA_GROUP + g
            a_ref[rs, h * HEAD_DIM:(h + 1) * HEAD_DIM] = outs[hk][g * rows:(g + 1) * rows, :]
    a_ref[rs, :] = _rms(a_ref[rs, :], gn_ref[...])
    keep = wb - dec_len
    is_new = lax.broadcasted_iota(jnp.int32, (KV_WIDTH, wb), 1) >= keep
    for n in range(ns):
        s = grp * ns + n
        ts = slice(grp * rows + n * dec_len, grp * rows + (n + 1) * dec_len)
        for src, dst, cols in ((ck_ref, ko_ref, slice(0, KV_WIDTH)), (cv_ref, vo_ref, slice(KV_WIDTH, 2 * KV_WIDTH))):
            new_t = jnp.concatenate([jnp.zeros((keep, KV_WIDTH), F32), kvn_ref[ts, cols]], axis=0).T
            dst[s] = jnp.where(is_new, new_t, pltpu.roll(src[s], keep, axis=1))


def _hgrn_bound_consts(lbl_ref):
    lbl = lbl_ref[...]
    e = jnp.exp(lbl - jnp.max(lbl, axis=0, keepdims=True))
    lb = e[0:1, :] / jnp.sum(e, axis=0, keepdims=True)
    return 0.5 + 0.5 * lb, 0.5 - 0.5 * lb


def _hgrn_gates(hp, c0, c1):
    qa = _silu_t(hp[:, 0:HG_WIDTH])
    t = c1 * jnp.tanh(0.5 * hp[:, HG_WIDTH:2 * HG_WIDTH])
    f, kx = c0 + t, c1 - t
    v = hp[:, 2 * HG_WIDTH:3 * HG_WIDTH].astype(BF16)
    return qa, kx, _split2(jnp.log2(f)), v


def _hgrn_factors(qa, kx, g, g_mid, g_last):
    qt = qa * jnp.exp2(g - g_mid)
    kt = kx * jnp.exp2(g_mid - g)
    qg = (qt * jnp.exp2(g_mid)).astype(BF16)
    kd = (kt * jnp.exp2(g_last - g_mid)).astype(BF16)
    return qt.astype(BF16), kt.astype(BF16), qg, kd


def _segment_masks(nseq):
    ls = HG_TILE // nseq
    r = lax.broadcasted_iota(jnp.int32, (HG_TILE, HG_TILE), 0)
    c = lax.broadcasted_iota(jnp.int32, (HG_TILE, HG_TILE), 1)
    same = (r // ls) == (c // ls)
    return same, same & (c <= r), same & (c % ls < ls // 2)


def _as_bf16(mask):
    return mask.astype(F32).astype(BF16)


def _hgrn_short_tile(z_ref, c0, c1, gn, s_in, s_out, o_ref, ti, nseq):
    ch = HG_TILE
    ls = ch // nseq
    same, causal, first_half = _segment_masks(nseq)
    cum_lhs = _as_bf16(jnp.concatenate([causal, first_half, same], axis=0))
    sr = lax.broadcasted_iota(jnp.int32, (ch, nseq * HG_DV), 0)
    sc = lax.broadcasted_iota(jnp.int32, (ch, nseq * HG_DV), 1)
    seg_sel = _as_bf16((sr // ls) == (sc // HG_DV))
    rows = slice(ti * ch, (ti + 1) * ch)

    qa, kx, parts, v = _hgrn_gates(z_ref[rows, Z_HP], c0, c1)
    gs = sum(_dot(cum_lhs, p) for p in parts)
    dcol = sum(_dot_tn(p, seg_sel) for p in parts)
    yield
    qt, kt, qg, kd = _hgrn_factors(qa, kx, gs[0:ch], gs[ch:2 * ch], gs[2 * ch:3 * ch])
    cols = [slice(h * HG_DK, (h + 1) * HG_DK) for h in range(HG_HEADS)]
    scores = [_dot_nt(qt[:, cs], kt[:, cs]) for cs in cols]
    yield
    outs = []
    for h, cs in enumerate(cols):
        a = jnp.where(causal, scores[h], 0.0)
        o = _dot(a.astype(BF16), v[:, cs])
        inter = []
        for n in range(nseq):
            rs = slice(n * ls, (n + 1) * ls)
            s = s_in[ti * nseq + n, h]
            inter.append(_dot(qg[rs, cs], s.astype(BF16)))
            decay = jnp.exp2(dcol[cs, n * HG_DV:(n + 1) * HG_DV])
            s_out[ti * nseq + n, h] = s * decay + _dot_tn(kd[rs, cs], v[rs, cs])
        outs.append(o + jnp.concatenate(inter, axis=0))
    yield
    gate = _silu_t(z_ref[rows, Z_HG])
    for h, cs in enumerate(cols):
        o_ref[rows, cs] = _rms(outs[h], gn) * gate[:, cs]


def _sample_mixer_ffn_kernel(sink_ref, x1_ref, z_ref, kvn_ref, ck_ref, cv_ref, s_in_ref,
                             lbl_ref, gn_ref, ghg_ref, g_mixpost, g_pre, g_post, wout_ref, wgu_ref, wd_ref,
                             y_ref, ko_ref, vo_ref, s_out_ref, a_scr, o_scr, *, dec_len):
    tile = x1_ref.shape[0]
    c0, c1 = _hgrn_bound_consts(lbl_ref)
    nseq = HG_TILE // dec_len

    def mixers():
        gens = [_attn_sample_group(sink_ref, z_ref, kvn_ref, ck_ref, cv_ref, gn_ref, a_scr, ko_ref, vo_ref, grp,
                                   dec_len) for grp in range(tile // (ATTN_SEQ_GROUP * dec_len))]
        return gens + [_hgrn_short_tile(z_ref, c0, c1, ghg_ref[...], s_in_ref, s_out_ref, o_scr, ti, nseq)
                       for ti in range(tile // HG_TILE)]

    def ffn_part(side):
        mix = _dot(a_scr[...].astype(BF16), wout_ref[0:ATTN_WIDTH, :])
        side()
        mix = mix + _dot(o_scr[...].astype(BF16), wout_ref[ATTN_WIDTH:ATTN_WIDTH + HG_WIDTH, :])
        side()
        y_ref[...] = _mix_ffn(x1_ref[...], mix, g_mixpost[...], g_pre[...], g_post[...], wgu_ref, wd_ref, side)

    _skewed_step(pl.program_id(0), pl.num_programs(0) - 1, mixers, ffn_part)


def _sample_mixer_ffn(x1, kv, z, cache_k, cache_v, state, sinks, g_attn, lb_logits, g_hg,
                      g_mixpost, g_pre, g_post, wout, wgu, wd, dec_len, row0):
    tile = SAMPLE_TILE
    t = cache_k.shape[0] * dec_len
    n_tiles = t // tile
    spt = tile // dec_len
    assert t % tile == 0 and row0 % tile == 0 and tile % HG_TILE == 0 and tile % (ATTN_SEQ_GROUP * dec_len) == 0
    cur = lambda j: jnp.minimum(j, n_tiles - 1)
    prev = lambda j: jnp.maximum(j - 1, 0)
    cur_spec = lambda width: pl.BlockSpec((tile, width), lambda j: (row0 // tile + cur(j), 0))
    cache_spec = pl.BlockSpec((spt,) + cache_k.shape[1:], lambda j: (cur(j), 0, 0))
    state_spec = pl.BlockSpec((spt,) + state.shape[1:], lambda j: (cur(j), 0, 0, 0))
    return pl.pallas_call(
        functools.partial(_sample_mixer_ffn_kernel, dec_len=dec_len),
        grid=(n_tiles + 1,),
        in_specs=[pl.BlockSpec(memory_space=pltpu.SMEM),
                  pl.BlockSpec((tile, D_MODEL), lambda j: (row0 // tile + prev(j), 0)),
                  cur_spec(Z_WIDTH), cur_spec(2 * KV_WIDTH), cache_spec, cache_spec, state_spec,
                  _const_spec(lb_logits.shape), _const_spec(g_attn.shape), _const_spec(g_hg.shape),
                  _const_spec(g_mixpost.shape), _const_spec(g_pre.shape), _const_spec(g_post.shape),
                  _const_spec(wout.shape), _const_spec(wgu.shape), _const_spec(wd.shape)],
        out_specs=[pl.BlockSpec((tile, D_MODEL), lambda j: (prev(j), 0)), cache_spec, cache_spec, state_spec],
        out_shape=[jax.ShapeDtypeStruct((t, D_MODEL), F32), jax.ShapeDtypeStruct(cache_k.shape, F32),
                   jax.ShapeDtypeStruct(cache_v.shape, F32), jax.ShapeDtypeStruct(state.shape, F32)],
        scratch_shapes=[pltpu.VMEM((tile, ATTN_WIDTH), F32), pltpu.VMEM((tile, HG_WIDTH), F32)],
        compiler_params=pltpu.CompilerParams(dimension_semantics=("arbitrary",),
                                             vmem_limit_bytes=VMEM_LIMIT_BIG),
        name="sample_mixer_ffn",
    )(sinks, x1, z, kv, cache_k, cache_v, state, lb_logits, g_attn, g_hg,
      g_mixpost, g_pre, g_post, wout, wgu, wd)


def _mixer_ffn_kernel(sink_ref, x1_ref, z_ref, kv_ref, lbl_ref, gnt_ref, ghg_ref,
                      g_mixpost, g_pre, g_post, wout_ref, wgu_ref, wd_ref,
                      y_ref, s_out_ref, at_scr, o_scr, st_scr, *, n_tiles, tiles_per_seq):
    j = pl.program_id(0)
    tile = x1_ref.shape[0]
    pos = jnp.minimum(j, n_tiles - 1) % tiles_per_seq

    @pl.when(j == 0)
    def _():
        st_scr[...] = jnp.zeros(st_scr.shape, F32)

    def attn_block(blk):
        rs = slice(blk * WINDOW, (blk + 1) * WINDOW)

        def store(at):
            at_scr[:, rs] = at

        return _attn_block_t(sink_ref, z_ref[rs, Z_Q], kv_ref, pos * tile + blk * WINDOW, gnt_ref[...], store)

    ch = HG_TILE
    _, causal, _ = _segment_masks(1)
    cum_lhs = _as_bf16(causal)
    c0, c1 = _hgrn_bound_consts(lbl_ref)
    ghg = ghg_ref[...]

    def hgrn_chunk(ci):
        rs = slice(ci * ch, (ci + 1) * ch)
        qa, kx, parts, v = _hgrn_gates(z_ref[rs, Z_HP], c0, c1)
        yield
        g = sum(_dot(cum_lhs, p) for p in parts)
        yield
        g_last = g[ch - 1:ch, :]
        qt, kt, qg, kd = _hgrn_factors(qa, kx, g, g[ch // 2 - 1:ch // 2, :], g_last)
        decay = jnp.exp2(g_last)
        cols = [slice(h * HG_DK, (h + 1) * HG_DK) for h in range(HG_HEADS)]
        yield
        scores = [_dot_nt(qt[:, cs], kt[:, cs]) for cs in cols]
        yield
        probs = [jnp.where(causal, sc, 0.0).astype(BF16) for sc in scores]
        yield
        outs = []
        for h, cs in enumerate(cols):
            st = st_scr[h]
            if ci == 0:
                st = jnp.where(pos == 0, 0.0, st)
            outs.append(_dot(probs[h], v[:, cs]) + _dot_nt(qg[:, cs], st.astype(BF16)))
            st_scr[h] = st * decay[:, cs] + _dot_tn(v[:, cs], kd[:, cs])
        yield
        gate = _silu_t(z_ref[rs, Z_HG])
        for h, cs in enumerate(cols):
            o_scr[rs, cs] = _rms(outs[h], ghg) * gate[:, cs]

    def mixers():
        n_blk, n_chunk = tile // WINDOW, tile // ch
        gens = []
        for blk in range(n_blk):
            gens += [hgrn_chunk(ci) for ci in range(blk * n_chunk // n_blk, (blk + 1) * n_chunk // n_blk)]
            gens.insert(len(gens) - 1, attn_block(blk))
        return gens

    def ffn_part(side):
        mix = _dot_tn(at_scr[...].astype(BF16), wout_ref[0:ATTN_WIDTH, :])
        side()
        mix = mix + _dot(o_scr[...].astype(BF16), wout_ref[ATTN_WIDTH:ATTN_WIDTH + HG_WIDTH, :])
        side()
        y_ref[...] = _mix_ffn(x1_ref[...], mix, g_mixpost[...], g_pre[...], g_post[...], wgu_ref, wd_ref, side)

    _skewed_step(j, n_tiles, mixers, ffn_part)

    @pl.when((pos == tiles_per_seq - 1) & (j < n_tiles))
    def _():
        b = j // tiles_per_seq
        for h in range(HG_HEADS):
            s_out_ref[pl.ds(b, 1), h] = st_scr[h].T[None]


def _mixer_ffn(x1, kv, z, sinks, g_attn, lb_logits, g_hg, g_mixpost, g_pre, g_post, wout, wgu, wd,
               batch, seq):
    t = batch * seq
    tile = TOKEN_TILE
    assert seq % tile == 0
    n_tiles, tps = t // tile, seq // tile
    gnt = jnp.broadcast_to(g_attn.reshape(ATTN_WIDTH, 1), (ATTN_WIDTH, WINDOW))
    cur = lambda j: jnp.minimum(j, n_tiles - 1)
    prev = lambda j: jnp.maximum(j - 1, 0)
    cur_spec = lambda width: pl.BlockSpec((tile, width), lambda j: (cur(j), 0))
    prev_spec = pl.BlockSpec((tile, D_MODEL), lambda j: (prev(j), 0))
    s_shape = (batch, HG_HEADS, HG_DK, HG_DV)
    return pl.pallas_call(
        functools.partial(_mixer_ffn_kernel, n_tiles=n_tiles, tiles_per_seq=tps),
        grid=(n_tiles + 1,),
        in_specs=[pl.BlockSpec(memory_space=pltpu.SMEM),
                  prev_spec, cur_spec(Z_WIDTH),
                  pl.BlockSpec((seq, 2 * KV_WIDTH), lambda j: (cur(j) // tps, 0)),
                  _const_spec(lb_logits.shape), _const_spec(gnt.shape), _const_spec(g_hg.shape),
                  _const_spec(g_mixpost.shape), _const_spec(g_pre.shape), _const_spec(g_post.shape),
                  _const_spec(wout.shape), _const_spec(wgu.shape), _const_spec(wd.shape)],
        out_specs=[prev_spec, pl.BlockSpec(s_shape, lambda j: (0, 0, 0, 0))],
        out_shape=[jax.ShapeDtypeStruct((t, D_MODEL), F32), jax.ShapeDtypeStruct(s_shape, F32)],
        scratch_shapes=[pltpu.VMEM((ATTN_WIDTH, tile), F32), pltpu.VMEM((tile, HG_WIDTH), F32),
                        pltpu.VMEM((HG_HEADS, HG_DV, HG_DK), F32)],
        compiler_params=pltpu.CompilerParams(dimension_semantics=("arbitrary",),
                                             vmem_limit_bytes=VMEM_LIMIT_BIG),
        name="mixer_ffn",
    )(sinks, x1, z, kv, lb_logits, gnt, g_hg, g_mixpost, g_pre, g_post, wout, wgu, wd)


def kernel(x_prompt, x_sample, cache_k_win, cache_v_win, state_hgrn, w_in, b_in, attn_sinks, attn_out_norm,
           hg_lb_logits, hg_out_norm, w_out, ffn1_w_gu, ffn1_w_down, ffn2_w_gu, ffn2_w_down,
           norm_ffn1_pre, norm_ffn1_post, norm_mix_pre, norm_mix_post, norm_ffn2_pre, norm_ffn2_post):
    depth = w_in.shape[0]
    assert depth == 1, "single-layer trunk"
    batch, seq, _ = x_prompt.shape
    dec_batch, dec_len, _ = x_sample.shape
    wb = cache_k_win.shape[2]
    assert seq % WINDOW == 0 and wb == WINDOW and HG_TILE % dec_len == 0
    layer = 0
    row = lambda p: p[layer].reshape(1, -1).astype(F32)
    wgu1, wd1 = ffn1_w_gu[layer].astype(BF16), ffn1_w_down[layer].astype(BF16)
    wgu2, wd2 = ffn2_w_gu[layer].astype(BF16), ffn2_w_down[layer].astype(BF16)
    win, wout = w_in[layer].astype(BF16), w_out[layer].astype(BF16)
    sinks = attn_sinks[layer].astype(F32)
    lb_logits = hg_lb_logits.astype(F32)
    g_attn, g_hg = row(attn_out_norm), row(hg_out_norm)
    back_params = (row(norm_mix_post), row(norm_ffn2_pre), row(norm_ffn2_post), wout, wgu2, wd2)

    t_p, t_s = batch * seq, dec_batch * dec_len
    x1, kv, z = _ffn_in(x_prompt.reshape(t_p, D_MODEL), x_sample.reshape(t_s, D_MODEL),
                                row(norm_ffn1_pre), row(norm_ffn1_post), row(norm_mix_pre), wgu1, wd1, win, row(b_in))

    y_p, s_prompt = _mixer_ffn(x1, kv, z, sinks, g_attn, lb_logits, g_hg, *back_params, batch, seq)
    y_prompt = y_p.reshape(batch, seq, D_MODEL)
    kv_last = jnp.stack([kv[(b + 1) * seq - WINDOW:(b + 1) * seq] for b in range(batch)])
    k_prompt = kv_last[..., :KV_WIDTH].reshape(1, batch, WINDOW, N_KV_HEADS, HEAD_DIM)
    v_prompt = kv_last[..., KV_WIDTH:].reshape(1, batch, WINDOW, N_KV_HEADS, HEAD_DIM)

    def feature_major(buf):
        return jnp.transpose(buf, (0, 2, 3, 1)).reshape(dec_batch, KV_WIDTH, wb)

    def window_major(buf_t):
        return jnp.transpose(buf_t.reshape(dec_batch, N_KV_HEADS, HEAD_DIM, wb), (0, 3, 1, 2))[None]

    y_s, k_s, v_s, s_sample = _sample_mixer_ffn(
        x1, kv, z, feature_major(cache_k_win[layer]), feature_major(cache_v_win[layer]),
        state_hgrn[layer].astype(F32), sinks, g_attn, lb_logits, g_hg, *back_params, dec_len, t_p)
    y_sample = y_s.reshape(dec_batch, dec_len, D_MODEL)
    k_sample, v_sample = window_major(k_s), window_major(v_s)

    return (y_prompt, y_sample, k_prompt, v_prompt, s_prompt[None], k_sample, v_sample, s_sample[None])
```

```python
import functools

import jax
import jax.numpy as jnp
from jax import lax
from jax.experimental import pallas as pl
from jax.experimental.pallas import tpu as pltpu

F32 = jnp.float32
BF16 = jnp.bfloat16

D_MODEL = 1024
N_HEADS = 8
N_KV_HEADS = 2
HEAD_DIM = 64
GQA_GROUP = N_HEADS // N_KV_HEADS
WINDOW = 128
ATTN_WIDTH = N_HEADS * HEAD_DIM
KV_WIDTH = N_KV_HEADS * HEAD_DIM
SCALE = HEAD_DIM ** -0.5
HG_HEADS = 4
HG_DK = 128
HG_DV = 128
HG_WIDTH = HG_HEADS * HG_DV
D_FF = 2816
EPS = 1e-6

Q_COLS = (0, ATTN_WIDTH)
KV_COLS = (ATTN_WIDTH, ATTN_WIDTH + 2 * KV_WIDTH)
HP_COLS = (KV_COLS[1], KV_COLS[1] + 3 * HG_WIDTH)
HG_COLS = (HP_COLS[1], HP_COLS[1] + HG_WIDTH)
Z_Q = slice(0, ATTN_WIDTH)
Z_HQ = slice(ATTN_WIDTH, ATTN_WIDTH + HG_WIDTH)
Z_HI = slice(ATTN_WIDTH + HG_WIDTH, ATTN_WIDTH + 2 * HG_WIDTH)
Z_HG = slice(ATTN_WIDTH + 2 * HG_WIDTH, ATTN_WIDTH + 3 * HG_WIDTH)
Z_WIDTH = ATTN_WIDTH + 3 * HG_WIDTH

VMEM_LIMIT_BIG = 58 * 1024 * 1024
FF_CHUNKS = ((0, 512), (512, 1024), (1024, 1536), (1536, 2048), (2048, 2560), (2560, 2816))
TOKEN_TILE = 512
SAMPLE_TILE = 128
HG_TILE = 64
ATTN_SEQ_GROUP = 4
NEG = -1e30
LOG2E = 1.4426950408889634


def _rms(x, g):
    return x * lax.rsqrt(jnp.mean(x * x, axis=-1, keepdims=True) + EPS) * g


def _silu(x):
    return x * jax.nn.sigmoid(x)


def _silu_t(x):
    h = 0.5 * x
    return h * jnp.tanh(h) + h


def _dot(a, b):
    return jnp.dot(a, b, preferred_element_type=F32)


def _dot_nt(a, b):
    return lax.dot_general(a, b, (((1,), (1,)), ((), ())), preferred_element_type=F32)


def _dot_tn(a, b):
    return lax.dot_general(a, b, (((0,), (0,)), ((), ())), preferred_element_type=F32)


def _split2(x):
    a = x.astype(BF16)
    b = (x - a.astype(F32)).astype(BF16)
    return a, b


def _no_side_work():
    pass


def _swiglu_ffn(x, g_pre, wgu_ref, wd_ref, g_post, side=_no_side_work):
    h = _rms(x, g_pre).astype(BF16)
    acc = None
    for lo, hi in FF_CHUNKS:
        g = _dot(h, wgu_ref[:, lo:hi])
        side()
        u = _dot(h, wgu_ref[:, D_FF + lo:D_FF + hi])
        side()
        part = _dot((_silu(g) * u).astype(BF16), wd_ref[lo:hi, :])
        side()
        acc = part if acc is None else acc + part
    return x + 0.5 * _rms(acc, g_post)


def _const_spec(shape):
    zeros = (0,) * len(shape)
    return pl.BlockSpec(shape, lambda *_: zeros, pipeline_mode=pl.Buffered(1))


def _row_spec(tile, width):
    return pl.BlockSpec((tile, width), lambda i: (i, 0))


def _ffn_in_kernel(xa_ref, xb_ref, g_pre, g_post, g_mix, wgu_ref, wd_ref, win_ref, bin_ref,
                   x1_ref, kv_ref, z_ref, hf_ref, *, tiles_a):
    x = jnp.where(pl.program_id(0) < tiles_a, xa_ref[...], xb_ref[...])
    x1 = _swiglu_ffn(x, g_pre[...], wgu_ref, wd_ref, g_post[...])
    x1_ref[...] = x1
    h = _rms(x1, g_mix[...]).astype(BF16)

    def project(lo, hi):
        return _dot(h, win_ref[:, lo:hi]) + bin_ref[:, lo:hi]

    kv_ref[...] = project(*KV_COLS)
    z_ref[:, Z_Q] = project(*Q_COLS).astype(BF16)
    hp = project(HP_COLS[0], HG_COLS[1])
    hf_ref[...] = hp[:, HG_WIDTH:2 * HG_WIDTH]
    for dst, k in ((Z_HQ, 0), (Z_HI, 2), (Z_HG, 3)):
        z_ref[:, dst] = hp[:, k * HG_WIDTH:(k + 1) * HG_WIDTH].astype(BF16)


def _ffn_in(xa, xb, g_pre, g_post, g_mix, wgu, wd, win, b_in):
    tile = TOKEN_TILE
    assert xa.shape[0] % tile == 0 and xb.shape[0] % tile == 0
    tiles_a, tiles_b = xa.shape[0] // tile, xb.shape[0] // tile
    n_tiles = tiles_a + tiles_b
    outs = [(D_MODEL, F32), (2 * KV_WIDTH, F32), (Z_WIDTH, BF16), (HG_WIDTH, F32)]
    return pl.pallas_call(
        functools.partial(_ffn_in_kernel, tiles_a=tiles_a),
        grid=(n_tiles,),
        in_specs=[pl.BlockSpec((tile, D_MODEL), lambda j: (jnp.minimum(j, tiles_a - 1), 0)),
                  pl.BlockSpec((tile, D_MODEL), lambda j: (jnp.maximum(j - tiles_a, 0), 0)),
                  _const_spec(g_pre.shape), _const_spec(g_post.shape),
                  _const_spec(g_mix.shape), _const_spec(wgu.shape), _const_spec(wd.shape),
                  _const_spec(win.shape), _const_spec(b_in.shape)],
        out_specs=[_row_spec(tile, w) for w, _ in outs],
        out_shape=[jax.ShapeDtypeStruct((n_tiles * tile, w), dt) for w, dt in outs],
        compiler_params=pltpu.CompilerParams(dimension_semantics=("arbitrary",),
                                             vmem_limit_bytes=VMEM_LIMIT_BIG),
        name="ffn_in",
    )(xa, xb, g_pre, g_post, g_mix, wgu, wd, win, b_in)


def _mix_ffn(x1, mix, g_mixpost, g_pre, g_post, wgu_ref, wd_ref, side=_no_side_work):
    x2 = x1 + _rms(mix, g_mixpost)
    return _swiglu_ffn(x2, g_pre, wgu_ref, wd_ref, g_post, side)


def _staged(mixers):
    waiting, running = list(mixers), []
    n_slots = 3 * len(FF_CHUNKS) + 2
    starts_per_slot = -(-len(waiting) // (2 * n_slots // 3))

    def side():
        for _ in range(min(starts_per_slot, len(waiting))):
            running.append(waiting.pop(0))
        for gen in list(running):
            if next(gen, "done") == "done":
                running.remove(gen)

    def drain():
        while waiting or running:
            side()

    return side, drain


def _attn_block_t(sink_ref, q, kv_ref, q0, gnt, store):
    w = WINDOW
    nkeys = 2 * w
    r = lax.broadcasted_iota(jnp.int32, (nkeys, w), 0)
    c = lax.broadcasted_iota(jnp.int32, (nkeys, w), 1)
    first_half = lax.broadcasted_iota(jnp.int32, (w, 2 * HEAD_DIM), 1) < HEAD_DIM
    k0 = jnp.maximum(q0 - w, 0)
    dist = c + (q0 - k0) - r
    bias = jnp.where((dist >= 0) & (dist <= WINDOW), 0.0, NEG)
    kvb = kv_ref[pl.ds(pl.multiple_of(k0, w), nkeys), :]
    kk = kvb[:, 0:KV_WIDTH]
    k_nat = kk.astype(BF16)
    k_swp = pltpu.roll(kk, HEAD_DIM, axis=1).astype(BF16)
    vt = kvb[:, KV_WIDTH:2 * KV_WIDTH].T.astype(BF16)

    def masked_q(h):
        qp = q[:, (h // 2) * 2 * HEAD_DIM:(h // 2 + 1) * 2 * HEAD_DIM]
        return jnp.where(first_half != bool(h % 2), qp, 0.0).astype(BF16)

    groups = []
    for keys, use_swapped in ((k_nat, False), (k_swp, True)):
        heads = [h for h in range(N_HEADS) if (h // GQA_GROUP != h % 2) == use_swapped]
        groups.append((keys, heads, jnp.concatenate([masked_q(h) for h in heads], axis=0)))
    yield
    scores = [None] * N_HEADS
    for keys, heads, qs in groups:
        st = _dot_nt(keys, qs)
        for i, h in enumerate(heads):
            scores[h] = st[:, i * w:(i + 1) * w]
    yield
    probs, dens = [], []
    for h in range(N_HEADS):
        st = scores[h] * (SCALE * LOG2E) + bias
        sink2 = sink_ref[h] * LOG2E
        m = jnp.maximum(jnp.max(st, axis=0, keepdims=True), sink2)
        p = jnp.exp2(st - m)
        dens.append(jnp.sum(p, axis=0, keepdims=True) + jnp.exp2(sink2 - m))
        probs.append(p.astype(BF16))
    yield
    ot = _dot(vt, jnp.concatenate(probs, axis=1))
    yield
    rows = []
    for h in range(N_HEADS):
        hk = h // GQA_GROUP
        rows.append(ot[hk * HEAD_DIM:(hk + 1) * HEAD_DIM, h * w:(h + 1) * w] / dens[h])
    at = jnp.concatenate(rows, axis=0)
    inv = lax.rsqrt(jnp.sum(at * at, axis=0, keepdims=True) * (1.0 / ATTN_WIDTH) + EPS)
    store(at * inv * gnt)


def _softmax_sink_pv(scores, masks, values, sink):
    scores = [jnp.where(mk, s * SCALE, NEG) for s, mk in zip(scores, masks)]
    m = sink
    for s in scores:
        m = jnp.maximum(m, jnp.max(s, axis=-1, keepdims=True))
    den = jnp.exp(sink - m)
    out = None
    for s, (v, transposed) in zip(scores, values):
        p = jnp.exp(s - m)
        den = den + jnp.sum(p, axis=-1, keepdims=True)
        pv = (_dot_nt if transposed else _dot)(p.astype(BF16), v)
        out = pv if out is None else out + pv
    return out / den


def _attn_sample_group(sink_ref, z_ref, kvn_ref, ck_ref, cv_ref, gn_ref, a_ref, ko_ref, vo_ref, grp, dec_len):
    ns = ATTN_SEQ_GROUP
    rows = ns * dec_len
    wb = ck_ref.shape[2]
    m_rows = GQA_GROUP * rows
    r = lax.broadcasted_iota(jnp.int32, (m_rows, ns * wb), 0)
    c = lax.broadcasted_iota(jnp.int32, (m_rows, ns * wb), 1)
    r_seq, r_pos = (r % rows) // dec_len, r % dec_len
    mask_cache = (r_seq == c // wb) & (c % wb >= r_pos)
    r = lax.broadcasted_iota(jnp.int32, (m_rows, rows), 0)
    c = lax.broadcasted_iota(jnp.int32, (m_rows, rows), 1)
    mask_new = ((r % rows) // dec_len == c // dec_len) & (c % dec_len <= r % dec_len)
    row_head = lax.broadcasted_iota(jnp.int32, (m_rows, 1), 0) // rows

    rs = slice(grp * rows, (grp + 1) * rows)
    q = z_ref[rs, Z_Q].astype(F32)
    kvn_b = kvn_ref[rs, :].astype(BF16)

    def cached(ref, hk):
        frows = slice(hk * HEAD_DIM, (hk + 1) * HEAD_DIM)
        return jnp.concatenate([ref[grp * ns + n, frows, :] for n in range(ns)], axis=1).astype(BF16)

    scores = []
    for hk in range(N_KV_HEADS):
        kcols = slice(hk * HEAD_DIM, (hk + 1) * HEAD_DIM)
        heads = [hk * GQA_GROUP + g for g in range(GQA_GROUP)]
        qs = jnp.concatenate([q[:, h * HEAD_DIM:(h + 1) * HEAD_DIM] for h in heads], axis=0).astype(BF16)
        scores.append([_dot(qs, cached(ck_ref, hk)), _dot_nt(qs, kvn_b[:, kcols])])
    yield
    outs = []
    for hk in range(N_KV_HEADS):
        vcols = slice(KV_WIDTH + hk * HEAD_DIM, KV_WIDTH + (hk + 1) * HEAD_DIM)
        sink = jnp.zeros((m_rows, 1), F32)
        for g in range(GQA_GROUP):
            sink = jnp.where(row_head == g, sink_ref[hk * GQA_GROUP + g], sink)
        values = [(cached(cv_ref, hk), True), (kvn_b[:, vcols], False)]
        outs.append(_softmax_sink_pv(scores[hk], [mask_cache, mask_new], values, sink))
    yield
    for hk in range(N_KV_HEADS):
        for g in range(GQA_GROUP):
            h = hk * GQA_GROUP + g
            a_ref[rs, h * HEAD_DIM:(h + 1) * HEAD_DIM] = outs[hk][g * rows:(g + 1) * rows, :]
    a_ref[rs, :] = _rms(a_ref[rs, :], gn_ref[...])
    keep = wb - dec_len
    is_new = lax.broadcasted_iota(jnp.int32, (KV_WIDTH, wb), 1) >= keep
    for n in range(ns):
        s = grp * ns + n
        ts = slice(grp * rows + n * dec_len, grp * rows + (n + 1) * dec_len)
        for src, dst, cols in ((ck_ref, ko_ref, slice(0, KV_WIDTH)), (cv_ref, vo_ref, slice(KV_WIDTH, 2 * KV_WIDTH))):
            new_t = jnp.concatenate([jnp.zeros((keep, KV_WIDTH), F32), kvn_ref[ts, cols]], axis=0).T
            dst[s] = jnp.where(is_new, new_t, pltpu.roll(src[s], keep, axis=1))


def _hgrn_bound_consts(lbl_ref):
    lbl = lbl_ref[...]
    e = jnp.exp(lbl - jnp.max(lbl, axis=0, keepdims=True))
    lb = e[0:1, :] / jnp.sum(e, axis=0, keepdims=True)
    return 0.5 + 0.5 * lb, 0.5 - 0.5 * lb


def _hgrn_gates(z, hf, c0, c1):
    qa = _silu_t(z[:, Z_HQ].astype(F32))
    t = c1 * jnp.tanh(0.5 * hf)
    f, kx = c0 + t, c1 - t
    return qa, kx, _split2(jnp.log2(f)), z[:, Z_HI]


def _hgrn_factors(qa, kx, g, g_mid, g_last):
    qt = qa * jnp.exp2(g - g_mid)
    kt = kx * jnp.exp2(g_mid - g)
    qg = (qt * jnp.exp2(g_mid)).astype(BF16)
    kd = (kt * jnp.exp2(g_last - g_mid)).astype(BF16)
    return qt.astype(BF16), kt.astype(BF16), qg, kd


def _segment_masks(nseq):
    ls = HG_TILE // nseq
    r = lax.broadcasted_iota(jnp.int32, (HG_TILE, HG_TILE), 0)
    c = lax.broadcasted_iota(jnp.int32, (HG_TILE, HG_TILE), 1)
    same = (r // ls) == (c // ls)
    return same, same & (c <= r), same & (c % ls < ls // 2)


def _as_bf16(mask):
    return mask.astype(F32).astype(BF16)


def _hgrn_short_tile(z_ref, hf_ref, c0, c1, gn, s_in, s_out, o_ref, ti, nseq):
    ch = HG_TILE
    ls = ch // nseq
    same, causal, first_half = _segment_masks(nseq)
    cum_lhs = _as_bf16(jnp.concatenate([causal, first_half, same], axis=0))
    sr = lax.broadcasted_iota(jnp.int32, (ch, nseq * HG_DV), 0)
    sc = lax.broadcasted_iota(jnp.int32, (ch, nseq * HG_DV), 1)
    seg_sel = _as_bf16((sr // ls) == (sc // HG_DV))
    rows = slice(ti * ch, (ti + 1) * ch)

    qa, kx, parts, v = _hgrn_gates(z_ref[rows, :], hf_ref[rows, :], c0, c1)
    gs = sum(_dot(cum_lhs, p) for p in parts)
    dcol = sum(_dot_tn(p, seg_sel) for p in parts)
    yield
    qt, kt, qg, kd = _hgrn_factors(qa, kx, gs[0:ch], gs[ch:2 * ch], gs[2 * ch:3 * ch])
    cols = [slice(h * HG_DK, (h + 1) * HG_DK) for h in range(HG_HEADS)]
    scores = [_dot_nt(qt[:, cs], kt[:, cs]) for cs in cols]
    yield
    outs = []
    for h, cs in enumerate(cols):
        a = jnp.where(causal, scores[h], 0.0)
        o = _dot(a.astype(BF16), v[:, cs])
        inter = []
        for n in range(nseq):
            rs = slice(n * ls, (n + 1) * ls)
            s = s_in[ti * nseq + n, h]
            inter.append(_dot(qg[rs, cs], s.astype(BF16)))
            decay = jnp.exp2(dcol[cs, n * HG_DV:(n + 1) * HG_DV])
            s_out[ti * nseq + n, h] = s * decay + _dot_tn(kd[rs, cs], v[rs, cs])
        outs.append(o + jnp.concatenate(inter, axis=0))
    yield
    gate = _silu_t(z_ref[rows, Z_HG].astype(F32))
    for h, cs in enumerate(cols):
        o_ref[rows, cs] = _rms(outs[h], gn) * gate[:, cs]


def _sample_mixer_ffn_kernel(sink_ref, x1_ref, z_ref, hf_ref, kvn_ref, ck_ref, cv_ref, s_in_ref,
                             lbl_ref, gn_ref, ghg_ref, g_mixpost, g_pre, g_post, wout_ref, wgu_ref, wd_ref,
                             y_ref, ko_ref, vo_ref, s_out_ref, a_scr, o_scr, *, dec_len):
    j = pl.program_id(0)
    tile = x1_ref.shape[0]

    @pl.when(j == 0)
    def _():
        a_scr[...] = jnp.zeros(a_scr.shape, F32)
        o_scr[...] = jnp.zeros(o_scr.shape, F32)

    c0, c1 = _hgrn_bound_consts(lbl_ref)
    nseq = HG_TILE // dec_len
    mixers = [_attn_sample_group(sink_ref, z_ref, kvn_ref, ck_ref, cv_ref, gn_ref, a_scr, ko_ref, vo_ref, grp, dec_len)
              for grp in range(tile // (ATTN_SEQ_GROUP * dec_len))]
    mixers += [_hgrn_short_tile(z_ref, hf_ref, c0, c1, ghg_ref[...], s_in_ref, s_out_ref, o_scr, ti, nseq)
               for ti in range(tile // HG_TILE)]
    side, drain = _staged(mixers)

    mix = _dot(a_scr[...].astype(BF16), wout_ref[0:ATTN_WIDTH, :])
    side()
    mix = mix + _dot(o_scr[...].astype(BF16), wout_ref[ATTN_WIDTH:ATTN_WIDTH + HG_WIDTH, :])
    side()
    y_ref[...] = _mix_ffn(x1_ref[...], mix, g_mixpost[...], g_pre[...], g_post[...], wgu_ref, wd_ref, side)
    drain()


def _sample_mixer_ffn(x1, kv, z, hf, cache_k, cache_v, state, sinks, g_attn, lb_logits, g_hg,
                      g_mixpost, g_pre, g_post, wout, wgu, wd, dec_len, row0):
    tile = SAMPLE_TILE
    t = cache_k.shape[0] * dec_len
    n_tiles = t // tile
    spt = tile // dec_len
    assert t % tile == 0 and row0 % tile == 0 and tile % HG_TILE == 0 and tile % (ATTN_SEQ_GROUP * dec_len) == 0
    cur = lambda j: jnp.minimum(j, n_tiles - 1)
    prev = lambda j: jnp.maximum(j - 1, 0)
    cur_spec = lambda width: pl.BlockSpec((tile, width), lambda j: (row0 // tile + cur(j), 0))
    cache_spec = pl.BlockSpec((spt,) + cache_k.shape[1:], lambda j: (cur(j), 0, 0))
    state_spec = pl.BlockSpec((spt,) + state.shape[1:], lambda j: (cur(j), 0, 0, 0))
    return pl.pallas_call(
        functools.partial(_sample_mixer_ffn_kernel, dec_len=dec_len),
        grid=(n_tiles + 1,),
        in_specs=[pl.BlockSpec(memory_space=pltpu.SMEM),
                  pl.BlockSpec((tile, D_MODEL), lambda j: (row0 // tile + prev(j), 0)),
                  cur_spec(Z_WIDTH), cur_spec(HG_WIDTH), cur_spec(2 * KV_WIDTH), cache_spec, cache_spec, state_spec,
                  _const_spec(lb_logits.shape), _const_spec(g_attn.shape), _const_spec(g_hg.shape),
                  _const_spec(g_mixpost.shape), _const_spec(g_pre.shape), _const_spec(g_post.shape),
                  _const_spec(wout.shape), _const_spec(wgu.shape), _const_spec(wd.shape)],
        out_specs=[pl.BlockSpec((tile, D_MODEL), lambda j: (prev(j), 0)), cache_spec, cache_spec, state_spec],
        out_shape=[jax.ShapeDtypeStruct((t, D_MODEL), F32), jax.ShapeDtypeStruct(cache_k.shape, F32),
                   jax.ShapeDtypeStruct(cache_v.shape, F32), jax.ShapeDtypeStruct(state.shape, F32)],
        scratch_shapes=[pltpu.VMEM((tile, ATTN_WIDTH), F32), pltpu.VMEM((tile, HG_WIDTH), F32)],
        compiler_params=pltpu.CompilerParams(dimension_semantics=("arbitrary",),
                                             vmem_limit_bytes=VMEM_LIMIT_BIG),
        name="sample_mixer_ffn",
    )(sinks, x1, z, hf, kv, cache_k, cache_v, state, lb_logits, g_attn, g_hg,
      g_mixpost, g_pre, g_post, wout, wgu, wd)


def _mixer_ffn_kernel(sink_ref, x1_ref, z_ref, hf_ref, kv_ref, lbl_ref, gnt_ref, ghg_ref,
                      g_mixpost, g_pre, g_post, wout_ref, wgu_ref, wd_ref,
                      y_ref, s_out_ref, at_scr, o_scr, st_scr, *, n_tiles, tiles_per_seq):
    j = pl.program_id(0)
    tile = x1_ref.shape[0]
    pos = jnp.minimum(j, n_tiles - 1) % tiles_per_seq

    @pl.when(j == 0)
    def _():
        at_scr[...] = jnp.zeros(at_scr.shape, F32)
        o_scr[...] = jnp.zeros(o_scr.shape, F32)
        st_scr[...] = jnp.zeros(st_scr.shape, F32)

    def attn_block(blk):
        rs = slice(blk * WINDOW, (blk + 1) * WINDOW)

        def store(at):
            at_scr[:, rs] = at

        return _attn_block_t(sink_ref, z_ref[rs, Z_Q].astype(F32), kv_ref, pos * tile + blk * WINDOW, gnt_ref[...], store)

    ch = HG_TILE
    _, causal, _ = _segment_masks(1)
    cum_lhs = _as_bf16(causal)
    c0, c1 = _hgrn_bound_consts(lbl_ref)
    ghg = ghg_ref[...]

    def hgrn_chunk(ci):
        rs = slice(ci * ch, (ci + 1) * ch)
        qa, kx, parts, v = _hgrn_gates(z_ref[rs, :], hf_ref[rs, :], c0, c1)
        yield
        g = sum(_dot(cum_lhs, p) for p in parts)
        yield
        g_last = g[ch - 1:ch, :]
        qt, kt, qg, kd = _hgrn_factors(qa, kx, g, g[ch // 2 - 1:ch // 2, :], g_last)
        decay = jnp.exp2(g_last)
        cols = [slice(h * HG_DK, (h + 1) * HG_DK) for h in range(HG_HEADS)]
        yield
        scores = [_dot_nt(qt[:, cs], kt[:, cs]) for cs in cols]
        yield
        probs = [jnp.where(causal, sc, 0.0).astype(BF16) for sc in scores]
        yield
        outs = []
        for h, cs in enumerate(cols):
            st = st_scr[h]
            if ci == 0:
                st = jnp.where(pos == 0, 0.0, st)
            outs.append(_dot(probs[h], v[:, cs]) + _dot_nt(qg[:, cs], st.astype(BF16)))
            st_scr[h] = st * decay[:, cs] + _dot_tn(v[:, cs], kd[:, cs])
        yield
        gate = _silu_t(z_ref[rs, Z_HG].astype(F32))
        for h, cs in enumerate(cols):
            o_scr[rs, cs] = _rms(outs[h], ghg) * gate[:, cs]

    n_blk, n_chunk = tile // WINDOW, tile // ch
    mixers = []
    for blk in range(n_blk):
        mixers += [hgrn_chunk(ci) for ci in range(blk * n_chunk // n_blk, (blk + 1) * n_chunk // n_blk)]
        mixers.insert(len(mixers) - 1, attn_block(blk))
    side, drain = _staged(mixers)

    mix = _dot_tn(at_scr[...].astype(BF16), wout_ref[0:ATTN_WIDTH, :])
    side()
    mix = mix + _dot(o_scr[...].astype(BF16), wout_ref[ATTN_WIDTH:ATTN_WIDTH + HG_WIDTH, :])
    side()
    y_ref[...] = _mix_ffn(x1_ref[...], mix, g_mixpost[...], g_pre[...], g_post[...], wgu_ref, wd_ref, side)
    drain()

    @pl.when((pos == tiles_per_seq - 1) & (j < n_tiles))
    def _():
        b = j // tiles_per_seq
        for h in range(HG_HEADS):
            s_out_ref[pl.ds(b, 1), h] = st_scr[h].T[None]


def _mixer_ffn(x1, kv, z, hf, sinks, g_attn, lb_logits, g_hg, g_mixpost, g_pre, g_post, wout, wgu, wd,
               batch, seq):
    t = batch * seq
    tile = TOKEN_TILE
    assert seq % tile == 0
    n_tiles, tps = t // tile, seq // tile
    gnt = jnp.broadcast_to(g_attn.reshape(ATTN_WIDTH, 1), (ATTN_WIDTH, WINDOW))
    cur = lambda j: jnp.minimum(j, n_tiles - 1)
    prev = lambda j: jnp.maximum(j - 1, 0)
    cur_spec = lambda width: pl.BlockSpec((tile, width), lambda j: (cur(j), 0))
    prev_spec = pl.BlockSpec((tile, D_MODEL), lambda j: (prev(j), 0))
    s_shape = (batch, HG_HEADS, HG_DK, HG_DV)
    return pl.pallas_call(
        functools.partial(_mixer_ffn_kernel, n_tiles=n_tiles, tiles_per_seq=tps),
        grid=(n_tiles + 1,),
        in_specs=[pl.BlockSpec(memory_space=pltpu.SMEM),
                  prev_spec, cur_spec(Z_WIDTH), cur_spec(HG_WIDTH),
                  pl.BlockSpec((seq, 2 * KV_WIDTH), lambda j: (cur(j) // tps, 0)),
                  _const_spec(lb_logits.shape), _const_spec(gnt.shape), _const_spec(g_hg.shape),
                  _const_spec(g_mixpost.shape), _const_spec(g_pre.shape), _const_spec(g_post.shape),
                  _const_spec(wout.shape), _const_spec(wgu.shape), _const_spec(wd.shape)],
        out_specs=[prev_spec, pl.BlockSpec(s_shape, lambda j: (0, 0, 0, 0))],
        out_shape=[jax.ShapeDtypeStruct((t, D_MODEL), F32), jax.ShapeDtypeStruct(s_shape, F32)],
        scratch_shapes=[pltpu.VMEM((ATTN_WIDTH, tile), F32), pltpu.VMEM((tile, HG_WIDTH), F32),
                        pltpu.VMEM((HG_HEADS, HG_DV, HG_DK), F32)],
        compiler_params=pltpu.CompilerParams(dimension_semantics=("arbitrary",),
                                             vmem_limit_bytes=VMEM_LIMIT_BIG),
        name="mixer_ffn",
    )(sinks, x1, z, hf, kv, lb_logits, gnt, g_hg, g_mixpost, g_pre, g_post, wout, wgu, wd)


def kernel(x_prompt, x_sample, cache_k_win, cache_v_win, state_hgrn, w_in, b_in, attn_sinks, attn_out_norm,
           hg_lb_logits, hg_out_norm, w_out, ffn1_w_gu, ffn1_w_down, ffn2_w_gu, ffn2_w_down,
           norm_ffn1_pre, norm_ffn1_post, norm_mix_pre, norm_mix_post, norm_ffn2_pre, norm_ffn2_post):
    depth = w_in.shape[0]
    assert depth == 1, "single-layer trunk"
    batch, seq, _ = x_prompt.shape
    dec_batch, dec_len, _ = x_sample.shape
    wb = cache_k_win.shape[2]
    assert seq % WINDOW == 0 and wb == WINDOW and HG_TILE % dec_len == 0
    layer = 0
    row = lambda p: p[layer].reshape(1, -1).astype(F32)
    wgu1, wd1 = ffn1_w_gu[layer].astype(BF16), ffn1_w_down[layer].astype(BF16)
    wgu2, wd2 = ffn2_w_gu[layer].astype(BF16), ffn2_w_down[layer].astype(BF16)
    win, wout = w_in[layer].astype(BF16), w_out[layer].astype(BF16)
    sinks = attn_sinks[layer].astype(F32)
    lb_logits = hg_lb_logits.astype(F32)
    g_attn, g_hg = row(attn_out_norm), row(hg_out_norm)
    back_params = (row(norm_mix_post), row(norm_ffn2_pre), row(norm_ffn2_post), wout, wgu2, wd2)

    t_p, t_s = batch * seq, dec_batch * dec_len
    x1, kv, z, hf = _ffn_in(x_prompt.reshape(t_p, D_MODEL), x_sample.reshape(t_s, D_MODEL),
                                row(norm_ffn1_pre), row(norm_ffn1_post), row(norm_mix_pre), wgu1, wd1, win, row(b_in))

    y_p, s_prompt = _mixer_ffn(x1, kv, z, hf, sinks, g_attn, lb_logits, g_hg, *back_params, batch, seq)
    y_prompt = y_p.reshape(batch, seq, D_MODEL)
    kv_last = jnp.stack([kv[(b + 1) * seq - WINDOW:(b + 1) * seq] for b in range(batch)])
    k_prompt = kv_last[..., :KV_WIDTH].reshape(1, batch, WINDOW, N_KV_HEADS, HEAD_DIM)
    v_prompt = kv_last[..., KV_WIDTH:].reshape(1, batch, WINDOW, N_KV_HEADS, HEAD_DIM)

    def feature_major(buf):
        return jnp.transpose(buf, (0, 2, 3, 1)).reshape(dec_batch, KV_WIDTH, wb)

    def window_major(buf_t):
        return jnp.transpose(buf_t.reshape(dec_batch, N_KV_HEADS, HEAD_DIM, wb), (0, 3, 1, 2))[None]

    y_s, k_s, v_s, s_sample = _sample_mixer_ffn(
        x1, kv, z, hf, feature_major(cache_k_win[layer]), feature_major(cache_v_win[layer]),
        state_hgrn[layer].astype(F32), sinks, g_attn, lb_logits, g_hg, *back_params, dec_len, t_p)
    y_sample = y_s.reshape(dec_batch, dec_len, D_MODEL)
    k_sample, v_sample = window_major(k_s), window_major(v_s)

    return (y_prompt, y_sample, k_prompt, v_prompt, s_prompt[None], k_sample, v_sample, s_sample[None])
```

```python
import functools

import jax
import jax.numpy as jnp
from jax import lax
from jax.experimental import pallas as pl
from jax.experimental.pallas import tpu as pltpu

F32 = jnp.float32
BF16 = jnp.bfloat16

D_MODEL = 1024
N_HEADS = 8
N_KV_HEADS = 2
HEAD_DIM = 64
GQA_GROUP = N_HEADS // N_KV_HEADS
WINDOW = 128
ATTN_WIDTH = N_HEADS * HEAD_DIM
KV_WIDTH = N_KV_HEADS * HEAD_DIM
SCALE = HEAD_DIM ** -0.5
HG_HEADS = 4
HG_DK = 128
HG_DV = 128
HG_WIDTH = HG_HEADS * HG_DV
D_FF = 2816
EPS = 1e-6

Q_COLS = (0, ATTN_WIDTH)
KV_COLS = (ATTN_WIDTH, ATTN_WIDTH + 2 * KV_WIDTH)
HP_COLS = (KV_COLS[1], KV_COLS[1] + 3 * HG_WIDTH)
HG_COLS = (HP_COLS[1], HP_COLS[1] + HG_WIDTH)
Z_Q = slice(0, ATTN_WIDTH)
Z_HP = slice(ATTN_WIDTH, ATTN_WIDTH + 3 * HG_WIDTH)
Z_HG = slice(ATTN_WIDTH + 3 * HG_WIDTH, ATTN_WIDTH + 4 * HG_WIDTH)
Z_WIDTH = ATTN_WIDTH + 4 * HG_WIDTH

VMEM_LIMIT_BIG = 58 * 1024 * 1024
FF_CHUNKS = ((0, 512), (512, 1024), (1024, 1536), (1536, 2048), (2048, 2560), (2560, 2816))
TOKEN_TILE = 512
SAMPLE_TILE = 128
HG_TILE = 64
ATTN_SEQ_GROUP = 4
NEG = -1e30
LOG2E = 1.4426950408889634


def _rms(x, g):
    return x * lax.rsqrt(jnp.mean(x * x, axis=-1, keepdims=True) + EPS) * g


def _silu(x):
    return x * jax.nn.sigmoid(x)


def _silu_t(x):
    h = 0.5 * x
    return h * jnp.tanh(h) + h


def _dot(a, b):
    return jnp.dot(a, b, preferred_element_type=F32)


def _dot_nt(a, b):
    return lax.dot_general(a, b, (((1,), (1,)), ((), ())), preferred_element_type=F32)


def _dot_tn(a, b):
    return lax.dot_general(a, b, (((0,), (0,)), ((), ())), preferred_element_type=F32)


def _split2(x):
    a = x.astype(BF16)
    b = (x - a.astype(F32)).astype(BF16)
    return a, b


def _no_side_work():
    pass


def _swiglu_ffn(x, g_pre, wgu_ref, wd_ref, g_post, side=_no_side_work):
    h = _rms(x, g_pre).astype(BF16)
    acts = []
    for lo, hi in FF_CHUNKS:
        g = _dot(h, wgu_ref[:, lo:hi])
        side()
        u = _dot(h, wgu_ref[:, D_FF + lo:D_FF + hi])
        side()
        acts.append((_silu(g) * u).astype(BF16))
    down = _dot(jnp.concatenate(acts, axis=1), wd_ref[...])
    side()
    return x + 0.5 * _rms(down, g_post)


def _const_spec(shape):
    zeros = (0,) * len(shape)
    return pl.BlockSpec(shape, lambda *_: zeros, pipeline_mode=pl.Buffered(1))


def _row_spec(tile, width):
    return pl.BlockSpec((tile, width), lambda i: (i, 0))


def _ffn_in_kernel(xa_ref, xb_ref, g_pre, g_post, g_mix, wgu_ref, wd_ref, win_ref, bin_ref,
                   x1_ref, kv_ref, z_ref, *, tiles_a):
    x = jnp.where(pl.program_id(0) < tiles_a, xa_ref[...], xb_ref[...])
    x1 = _swiglu_ffn(x, g_pre[...], wgu_ref, wd_ref, g_post[...])
    x1_ref[...] = x1
    h = _rms(x1, g_mix[...]).astype(BF16)

    def project(lo, hi):
        return _dot(h, win_ref[:, lo:hi]) + bin_ref[:, lo:hi]

    kv_ref[...] = project(*KV_COLS)
    z_ref[:, Z_Q] = project(*Q_COLS)
    z_ref[:, Z_HP.start:Z_HG.stop] = project(HP_COLS[0], HG_COLS[1])


def _ffn_in(xa, xb, g_pre, g_post, g_mix, wgu, wd, win, b_in):
    tile = TOKEN_TILE
    assert xa.shape[0] % tile == 0 and xb.shape[0] % tile == 0
    tiles_a, tiles_b = xa.shape[0] // tile, xb.shape[0] // tile
    n_tiles = tiles_a + tiles_b
    widths = [D_MODEL, 2 * KV_WIDTH, Z_WIDTH]
    return pl.pallas_call(
        functools.partial(_ffn_in_kernel, tiles_a=tiles_a),
        grid=(n_tiles,),
        in_specs=[pl.BlockSpec((tile, D_MODEL), lambda j: (jnp.minimum(j, tiles_a - 1), 0)),
                  pl.BlockSpec((tile, D_MODEL), lambda j: (jnp.maximum(j - tiles_a, 0), 0)),
                  _const_spec(g_pre.shape), _const_spec(g_post.shape),
                  _const_spec(g_mix.shape), _const_spec(wgu.shape), _const_spec(wd.shape),
                  _const_spec(win.shape), _const_spec(b_in.shape)],
        out_specs=[_row_spec(tile, w) for w in widths],
        out_shape=[jax.ShapeDtypeStruct((n_tiles * tile, w), F32) for w in widths],
        compiler_params=pltpu.CompilerParams(dimension_semantics=("arbitrary",),
                                             vmem_limit_bytes=VMEM_LIMIT_BIG),
        name="ffn_in",
    )(xa, xb, g_pre, g_post, g_mix, wgu, wd, win, b_in)


def _mix_ffn(x1, mix, g_mixpost, g_pre, g_post, wgu_ref, wd_ref, side=_no_side_work):
    x2 = x1 + _rms(mix, g_mixpost)
    return _swiglu_ffn(x2, g_pre, wgu_ref, wd_ref, g_post, side)


def _staged(mixers):
    waiting, running = list(mixers), []
    n_slots = 2 * len(FF_CHUNKS) + 2
    starts_per_slot = -(-len(waiting) // n_slots)

    def side():
        for _ in range(min(starts_per_slot, len(waiting))):
            running.append(waiting.pop(0))
        for gen in list(running):
            if next(gen, "done") == "done":
                running.remove(gen)

    def drain():
        while waiting or running:
            side()

    return side, drain


def _attn_block(sink_ref, q, kv_ref, q0, gn, store):
    w = WINDOW
    nkeys = 2 * w
    r = lax.broadcasted_iota(jnp.int32, (nkeys, w), 0)
    c = lax.broadcasted_iota(jnp.int32, (nkeys, w), 1)
    first_half = lax.broadcasted_iota(jnp.int32, (w, 2 * HEAD_DIM), 1) < HEAD_DIM
    k0 = jnp.maximum(q0 - w, 0)
    dist = c + (q0 - k0) - r
    bias = jnp.where((dist >= 0) & (dist <= WINDOW), 0.0, NEG)
    kvb = kv_ref[pl.ds(pl.multiple_of(k0, w), nkeys), :]
    kk = kvb[:, 0:KV_WIDTH]
    k_nat = kk.astype(BF16)
    k_swp = pltpu.roll(kk, HEAD_DIM, axis=1).astype(BF16)
    vt = kvb[:, KV_WIDTH:2 * KV_WIDTH].T.astype(BF16)

    def masked_q(h):
        qp = q[:, (h // 2) * 2 * HEAD_DIM:(h // 2 + 1) * 2 * HEAD_DIM]
        return jnp.where(first_half != bool(h % 2), qp, 0.0).astype(BF16)

    groups = []
    for keys, use_swapped in ((k_nat, False), (k_swp, True)):
        heads = [h for h in range(N_HEADS) if (h // GQA_GROUP != h % 2) == use_swapped]
        groups.append((keys, heads, jnp.concatenate([masked_q(h) for h in heads], axis=0)))
    yield
    scores = [None] * N_HEADS
    for keys, heads, qs in groups:
        st = _dot_nt(keys, qs)
        for i, h in enumerate(heads):
            scores[h] = st[:, i * w:(i + 1) * w]
    yield
    probs, dens = [], []
    for h in range(N_HEADS):
        st = scores[h] * (SCALE * LOG2E) + bias
        sink2 = sink_ref[h] * LOG2E
        m = jnp.maximum(jnp.max(st, axis=0, keepdims=True), sink2)
        p = jnp.exp2(st - m)
        dens.append(jnp.sum(p, axis=0, keepdims=True) + jnp.exp2(sink2 - m))
        probs.append(p.astype(BF16))
    yield
    ot = _dot(vt, jnp.concatenate(probs, axis=1))
    yield
    rows = []
    for h in range(N_HEADS):
        hk = h // GQA_GROUP
        rows.append(ot[hk * HEAD_DIM:(hk + 1) * HEAD_DIM, h * w:(h + 1) * w] / dens[h])
    at = jnp.concatenate(rows, axis=0)
    inv = lax.rsqrt(jnp.sum(at * at, axis=0, keepdims=True) * (1.0 / ATTN_WIDTH) + EPS)
    store((at * inv).T * gn)


def _softmax_sink_pv(scores, masks, values, sink):
    scores = [jnp.where(mk, s * SCALE, NEG) for s, mk in zip(scores, masks)]
    m = sink
    for s in scores:
        m = jnp.maximum(m, jnp.max(s, axis=-1, keepdims=True))
    den = jnp.exp(sink - m)
    out = None
    for s, (v, transposed) in zip(scores, values):
        p = jnp.exp(s - m)
        den = den + jnp.sum(p, axis=-1, keepdims=True)
        pv = (_dot_nt if transposed else _dot)(p.astype(BF16), v)
        out = pv if out is None else out + pv
    return out / den


def _attn_sample_group(sink_ref, z_ref, kvn_ref, ck_ref, cv_ref, gn_ref, a_ref, ko_ref, vo_ref, grp, dec_len):
    ns = ATTN_SEQ_GROUP
    rows = ns * dec_len
    wb = ck_ref.shape[2]
    m_rows = GQA_GROUP * rows
    r = lax.broadcasted_iota(jnp.int32, (m_rows, ns * wb), 0)
    c = lax.broadcasted_iota(jnp.int32, (m_rows, ns * wb), 1)
    r_seq, r_pos = (r % rows) // dec_len, r % dec_len
    mask_cache = (r_seq == c // wb) & (c % wb >= r_pos)
    r = lax.broadcasted_iota(jnp.int32, (m_rows, rows), 0)
    c = lax.broadcasted_iota(jnp.int32, (m_rows, rows), 1)
    mask_new = ((r % rows) // dec_len == c // dec_len) & (c % dec_len <= r % dec_len)
    row_head = lax.broadcasted_iota(jnp.int32, (m_rows, 1), 0) // rows

    rs = slice(grp * rows, (grp + 1) * rows)
    q = z_ref[rs, Z_Q]
    kvn_b = kvn_ref[rs, :].astype(BF16)

    def cached(ref, hk):
        frows = slice(hk * HEAD_DIM, (hk + 1) * HEAD_DIM)
        return jnp.concatenate([ref[grp * ns + n, frows, :] for n in range(ns)], axis=1).astype(BF16)

    scores = []
    for hk in range(N_KV_HEADS):
        kcols = slice(hk * HEAD_DIM, (hk + 1) * HEAD_DIM)
        heads = [hk * GQA_GROUP + g for g in range(GQA_GROUP)]
        qs = jnp.concatenate([q[:, h * HEAD_DIM:(h + 1) * HEAD_DIM] for h in heads], axis=0).astype(BF16)
        scores.append([_dot(qs, cached(ck_ref, hk)), _dot_nt(qs, kvn_b[:, kcols])])
    yield
    outs = []
    for hk in range(N_KV_HEADS):
        vcols = slice(KV_WIDTH + hk * HEAD_DIM, KV_WIDTH + (hk + 1) * HEAD_DIM)
        sink = jnp.zeros((m_rows, 1), F32)
        for g in range(GQA_GROUP):
            sink = jnp.where(row_head == g, sink_ref[hk * GQA_GROUP + g], sink)
        values = [(cached(cv_ref, hk), True), (kvn_b[:, vcols], False)]
        outs.append(_softmax_sink_pv(scores[hk], [mask_cache, mask_new], values, sink))
    yield
    for hk in range(N_KV_HEADS):
        for g in range(GQA_GROUP):
            h = hk * GQA_GROUP + g
            a_ref[rs, h * HEAD_DIM:(h + 1) * HEAD_DIM] = outs[hk][g * rows:(g + 1) * rows, :]
    a_ref[rs, :] = _rms(a_ref[rs, :], gn_ref[...])
    keep = wb - dec_len
    is_new = lax.broadcasted_iota(jnp.int32, (KV_WIDTH, wb), 1) >= keep
    for n in range(ns):
        s = grp * ns + n
        ts = slice(grp * rows + n * dec_len, grp * rows + (n + 1) * dec_len)
        for src, dst, cols in ((ck_ref, ko_ref, slice(0, KV_WIDTH)), (cv_ref, vo_ref, slice(KV_WIDTH, 2 * KV_WIDTH))):
            new_t = jnp.concatenate([jnp.zeros((keep, KV_WIDTH), F32), kvn_ref[ts, cols]], axis=0).T
            dst[s] = jnp.where(is_new, new_t, pltpu.roll(src[s], keep, axis=1))


def _hgrn_bound_consts(lbl_ref):
    lbl = lbl_ref[...]
    e = jnp.exp(lbl - jnp.max(lbl, axis=0, keepdims=True))
    lb = e[0:1, :] / jnp.sum(e, axis=0, keepdims=True)
    return 0.5 + 0.5 * lb, 0.5 - 0.5 * lb


def _hgrn_gates(hp, c0, c1):
    qa = _silu_t(hp[:, 0:HG_WIDTH])
    t = c1 * jnp.tanh(0.5 * hp[:, HG_WIDTH:2 * HG_WIDTH])
    f, kx = c0 + t, c1 - t
    v = hp[:, 2 * HG_WIDTH:3 * HG_WIDTH].astype(BF16)
    return qa, kx, _split2(jnp.log2(f)), v


def _hgrn_factors(qa, kx, g, g_mid, g_last):
    qt = qa * jnp.exp2(g - g_mid)
    kt = kx * jnp.exp2(g_mid - g)
    qg = (qt * jnp.exp2(g_mid)).astype(BF16)
    kd = (kt * jnp.exp2(g_last - g_mid)).astype(BF16)
    return qt.astype(BF16), kt.astype(BF16), qg, kd


def _segment_masks(nseq):
    ls = HG_TILE // nseq
    r = lax.broadcasted_iota(jnp.int32, (HG_TILE, HG_TILE), 0)
    c = lax.broadcasted_iota(jnp.int32, (HG_TILE, HG_TILE), 1)
    same = (r // ls) == (c // ls)
    return same, same & (c <= r), same & (c % ls < ls // 2)


def _as_bf16(mask):
    return mask.astype(F32).astype(BF16)


def _hgrn_short_tile(z_ref, c0, c1, gn, s_in, s_out, o_ref, ti, nseq):
    ch = HG_TILE
    ls = ch // nseq
    same, causal, first_half = _segment_masks(nseq)
    cum_lhs = _as_bf16(jnp.concatenate([causal, first_half, same], axis=0))
    sr = lax.broadcasted_iota(jnp.int32, (ch, nseq * HG_DV), 0)
    sc = lax.broadcasted_iota(jnp.int32, (ch, nseq * HG_DV), 1)
    seg_sel = _as_bf16((sr // ls) == (sc // HG_DV))
    rows = slice(ti * ch, (ti + 1) * ch)

    qa, kx, parts, v = _hgrn_gates(z_ref[rows, Z_HP], c0, c1)
    gs = sum(_dot(cum_lhs, p) for p in parts)
    dcol = sum(_dot_tn(p, seg_sel) for p in parts)
    yield
    qt, kt, qg, kd = _hgrn_factors(qa, kx, gs[0:ch], gs[ch:2 * ch], gs[2 * ch:3 * ch])
    cols = [slice(h * HG_DK, (h + 1) * HG_DK) for h in range(HG_HEADS)]
    scores = [_dot_nt(qt[:, cs], kt[:, cs]) for cs in cols]
    yield
    outs = []
    for h, cs in enumerate(cols):
        a = jnp.where(causal, scores[h], 0.0)
        o = _dot(a.astype(BF16), v[:, cs])
        inter = []
        for n in range(nseq):
            rs = slice(n * ls, (n + 1) * ls)
            s = s_in[ti * nseq + n, h]
            inter.append(_dot(qg[rs, cs], s.astype(BF16)))
            decay = jnp.exp2(dcol[cs, n * HG_DV:(n + 1) * HG_DV])
            s_out[ti * nseq + n, h] = s * decay + _dot_tn(kd[rs, cs], v[rs, cs])
        outs.append(o + jnp.concatenate(inter, axis=0))
    yield
    gate = _silu_t(z_ref[rows, Z_HG])
    for h, cs in enumerate(cols):
        o_ref[rows, cs] = _rms(outs[h], gn) * gate[:, cs]


def _sample_mixer_ffn_kernel(sink_ref, x1_ref, z_ref, kvn_ref, ck_ref, cv_ref, s_in_ref,
                             lbl_ref, gn_ref, ghg_ref, g_mixpost, g_pre, g_post, wout_ref, wgu_ref, wd_ref,
                             y_ref, ko_ref, vo_ref, s_out_ref, ao_scr, *, dec_len):
    j = pl.program_id(0)
    tile = x1_ref.shape[0]

    @pl.when(j == 0)
    def _():
        ao_scr[...] = jnp.zeros(ao_scr.shape, F32)

    a_scr, o_scr = ao_scr.at[:, 0:ATTN_WIDTH], ao_scr.at[:, ATTN_WIDTH:ATTN_WIDTH + HG_WIDTH]
    c0, c1 = _hgrn_bound_consts(lbl_ref)
    nseq = HG_TILE // dec_len
    mixers = [_attn_sample_group(sink_ref, z_ref, kvn_ref, ck_ref, cv_ref, gn_ref, a_scr, ko_ref, vo_ref, grp, dec_len)
              for grp in range(tile // (ATTN_SEQ_GROUP * dec_len))]
    mixers += [_hgrn_short_tile(z_ref, c0, c1, ghg_ref[...], s_in_ref, s_out_ref, o_scr, ti, nseq)
               for ti in range(tile // HG_TILE)]
    side, drain = _staged(mixers)

    mix = _dot(ao_scr[...].astype(BF16), wout_ref[...])
    side()
    y_ref[...] = _mix_ffn(x1_ref[...], mix, g_mixpost[...], g_pre[...], g_post[...], wgu_ref, wd_ref, side)
    drain()


def _sample_mixer_ffn(x1, kv, z, cache_k, cache_v, state, sinks, g_attn, lb_logits, g_hg,
                      g_mixpost, g_pre, g_post, wout, wgu, wd, dec_len, row0):
    tile = SAMPLE_TILE
    t = cache_k.shape[0] * dec_len
    n_tiles = t // tile
    spt = tile // dec_len
    assert t % tile == 0 and row0 % tile == 0 and tile % HG_TILE == 0 and tile % (ATTN_SEQ_GROUP * dec_len) == 0
    cur = lambda j: jnp.minimum(j, n_tiles - 1)
    prev = lambda j: jnp.maximum(j - 1, 0)
    cur_spec = lambda width: pl.BlockSpec((tile, width), lambda j: (row0 // tile + cur(j), 0))
    cache_spec = pl.BlockSpec((spt,) + cache_k.shape[1:], lambda j: (cur(j), 0, 0))
    state_spec = pl.BlockSpec((spt,) + state.shape[1:], lambda j: (cur(j), 0, 0, 0))
    return pl.pallas_call(
        functools.partial(_sample_mixer_ffn_kernel, dec_len=dec_len),
        grid=(n_tiles + 1,),
        in_specs=[pl.BlockSpec(memory_space=pltpu.SMEM),
                  pl.BlockSpec((tile, D_MODEL), lambda j: (row0 // tile + prev(j), 0)),
                  cur_spec(Z_WIDTH), cur_spec(2 * KV_WIDTH), cache_spec, cache_spec, state_spec,
                  _const_spec(lb_logits.shape), _const_spec(g_attn.shape), _const_spec(g_hg.shape),
                  _const_spec(g_mixpost.shape), _const_spec(g_pre.shape), _const_spec(g_post.shape),
                  _const_spec(wout.shape), _const_spec(wgu.shape), _const_spec(wd.shape)],
        out_specs=[pl.BlockSpec((tile, D_MODEL), lambda j: (prev(j), 0)), cache_spec, cache_spec, state_spec],
        out_shape=[jax.ShapeDtypeStruct((t, D_MODEL), F32), jax.ShapeDtypeStruct(cache_k.shape, F32),
                   jax.ShapeDtypeStruct(cache_v.shape, F32), jax.ShapeDtypeStruct(state.shape, F32)],
        scratch_shapes=[pltpu.VMEM((tile, ATTN_WIDTH + HG_WIDTH), F32)],
        compiler_params=pltpu.CompilerParams(dimension_semantics=("arbitrary",),
                                             vmem_limit_bytes=VMEM_LIMIT_BIG),
        name="sample_mixer_ffn",
    )(sinks, x1, z, kv, cache_k, cache_v, state, lb_logits, g_attn, g_hg,
      g_mixpost, g_pre, g_post, wout, wgu, wd)


def _mixer_ffn_kernel(sink_ref, x1_ref, z_ref, kv_ref, lbl_ref, gn_ref, ghg_ref,
                      g_mixpost, g_pre, g_post, wout_ref, wgu_ref, wd_ref,
                      y_ref, s_out_ref, ao_scr, st_scr, *, n_tiles, tiles_per_seq):
    j = pl.program_id(0)
    tile = x1_ref.shape[0]
    pos = jnp.minimum(j, n_tiles - 1) % tiles_per_seq

    @pl.when(j == 0)
    def _():
        ao_scr[...] = jnp.zeros(ao_scr.shape, F32)
        st_scr[...] = jnp.zeros(st_scr.shape, F32)

    def attn_block(blk):
        rs = slice(blk * WINDOW, (blk + 1) * WINDOW)

        def store(a):
            ao_scr[rs, 0:ATTN_WIDTH] = a

        return _attn_block(sink_ref, z_ref[rs, Z_Q], kv_ref, pos * tile + blk * WINDOW, gn_ref[...], store)

    ch = HG_TILE
    _, causal, _ = _segment_masks(1)
    cum_lhs = _as_bf16(causal)
    c0, c1 = _hgrn_bound_consts(lbl_ref)
    ghg = ghg_ref[...]

    def hgrn_chunk(ci):
        rs = slice(ci * ch, (ci + 1) * ch)
        qa, kx, parts, v = _hgrn_gates(z_ref[rs, Z_HP], c0, c1)
        yield
        g = sum(_dot(cum_lhs, p) for p in parts)
        yield
        g_last = g[ch - 1:ch, :]
        qt, kt, qg, kd = _hgrn_factors(qa, kx, g, g[ch // 2 - 1:ch // 2, :], g_last)
        decay = jnp.exp2(g_last)
        cols = [slice(h * HG_DK, (h + 1) * HG_DK) for h in range(HG_HEADS)]
        yield
        scores = [_dot_nt(qt[:, cs], kt[:, cs]) for cs in cols]
        yield
        probs = [jnp.where(causal, sc, 0.0).astype(BF16) for sc in scores]
        yield
        outs = []
        for h, cs in enumerate(cols):
            st = st_scr[h]
            if ci == 0:
                st = jnp.where(pos == 0, 0.0, st)
            outs.append(_dot(probs[h], v[:, cs]) + _dot_nt(qg[:, cs], st.astype(BF16)))
            st_scr[h] = st * decay[:, cs] + _dot_tn(v[:, cs], kd[:, cs])
        yield
        gate = _silu_t(z_ref[rs, Z_HG])
        for h, cs in enumerate(cols):
            ao_scr[rs, ATTN_WIDTH + h * HG_DV:ATTN_WIDTH + (h + 1) * HG_DV] = _rms(outs[h], ghg) * gate[:, cs]

    n_blk, n_chunk = tile // WINDOW, tile // ch
    mixers = []
    for blk in range(n_blk):
        mixers += [hgrn_chunk(ci) for ci in range(blk * n_chunk // n_blk, (blk + 1) * n_chunk // n_blk)]
        mixers.insert(len(mixers) - 1, attn_block(blk))
    side, drain = _staged(mixers)

    mix = _dot(ao_scr[...].astype(BF16), wout_ref[...])
    side()
    y_ref[...] = _mix_ffn(x1_ref[...], mix, g_mixpost[...], g_pre[...], g_post[...], wgu_ref, wd_ref, side)
    drain()

    @pl.when((pos == tiles_per_seq - 1) & (j < n_tiles))
    def _():
        b = j // tiles_per_seq
        for h in range(HG_HEADS):
            s_out_ref[pl.ds(b, 1), h] = st_scr[h].T[None]


def _mixer_ffn(x1, kv, z, sinks, g_attn, lb_logits, g_hg, g_mixpost, g_pre, g_post, wout, wgu, wd,
               batch, seq):
    t = batch * seq
    tile = TOKEN_TILE
    assert seq % tile == 0
    n_tiles, tps = t // tile, seq // tile
    cur = lambda j: jnp.minimum(j, n_tiles - 1)
    prev = lambda j: jnp.maximum(j - 1, 0)
    cur_spec = lambda width: pl.BlockSpec((tile, width), lambda j: (cur(j), 0))
    prev_spec = pl.BlockSpec((tile, D_MODEL), lambda j: (prev(j), 0))
    s_shape = (batch, HG_HEADS, HG_DK, HG_DV)
    return pl.pallas_call(
        functools.partial(_mixer_ffn_kernel, n_tiles=n_tiles, tiles_per_seq=tps),
        grid=(n_tiles + 1,),
        in_specs=[pl.BlockSpec(memory_space=pltpu.SMEM),
                  prev_spec, cur_spec(Z_WIDTH),
                  pl.BlockSpec((seq, 2 * KV_WIDTH), lambda j: (cur(j) // tps, 0)),
                  _const_spec(lb_logits.shape), _const_spec(g_attn.shape), _const_spec(g_hg.shape),
                  _const_spec(g_mixpost.shape), _const_spec(g_pre.shape), _const_spec(g_post.shape),
                  _const_spec(wout.shape), _const_spec(wgu.shape), _const_spec(wd.shape)],
        out_specs=[prev_spec, pl.BlockSpec(s_shape, lambda j: (0, 0, 0, 0))],
        out_shape=[jax.ShapeDtypeStruct((t, D_MODEL), F32), jax.ShapeDtypeStruct(s_shape, F32)],
        scratch_shapes=[pltpu.VMEM((tile, ATTN_WIDTH + HG_WIDTH), F32), pltpu.VMEM((HG_HEADS, HG_DV, HG_DK), F32)],
        compiler_params=pltpu.CompilerParams(dimension_semantics=("arbitrary",),
                                             vmem_limit_bytes=VMEM_LIMIT_BIG),
        name="mixer_ffn",
    )(sinks, x1, z, kv, lb_logits, g_attn, g_hg, g_mixpost, g_pre, g_post, wout, wgu, wd)


def kernel(x_prompt, x_sample, cache_k_win, cache_v_win, state_hgrn, w_in, b_in, attn_sinks, attn_out_norm,
           hg_lb_logits, hg_out_norm, w_out, ffn1_w_gu, ffn1_w_down, ffn2_w_gu, ffn2_w_down,
           norm_ffn1_pre, norm_ffn1_post, norm_mix_pre, norm_mix_post, norm_ffn2_pre, norm_ffn2_post):
    depth = w_in.shape[0]
    assert depth == 1, "single-layer trunk"
    batch, seq, _ = x_prompt.shape
    dec_batch, dec_len, _ = x_sample.shape
    wb = cache_k_win.shape[2]
    assert seq % WINDOW == 0 and wb == WINDOW and HG_TILE % dec_len == 0
    layer = 0
    row = lambda p: p[layer].reshape(1, -1).astype(F32)
    wgu1, wd1 = ffn1_w_gu[layer].astype(BF16), ffn1_w_down[layer].astype(BF16)
    wgu2, wd2 = ffn2_w_gu[layer].astype(BF16), ffn2_w_down[layer].astype(BF16)
    win, wout = w_in[layer].astype(BF16), w_out[layer].astype(BF16)
    sinks = attn_sinks[layer].astype(F32)
    lb_logits = hg_lb_logits.astype(F32)
    g_attn, g_hg = row(attn_out_norm), row(hg_out_norm)
    back_params = (row(norm_mix_post), row(norm_ffn2_pre), row(norm_ffn2_post), wout, wgu2, wd2)

    t_p, t_s = batch * seq, dec_batch * dec_len
    x1, kv, z = _ffn_in(x_prompt.reshape(t_p, D_MODEL), x_sample.reshape(t_s, D_MODEL),
                                row(norm_ffn1_pre), row(norm_ffn1_post), row(norm_mix_pre), wgu1, wd1, win, row(b_in))

    y_p, s_prompt = _mixer_ffn(x1, kv, z, sinks, g_attn, lb_logits, g_hg, *back_params, batch, seq)
    y_prompt = y_p.reshape(batch, seq, D_MODEL)
    kv_last = jnp.stack([kv[(b + 1) * seq - WINDOW:(b + 1) * seq] for b in range(batch)])
    k_prompt = kv_last[..., :KV_WIDTH].reshape(1, batch, WINDOW, N_KV_HEADS, HEAD_DIM)
    v_prompt = kv_last[..., KV_WIDTH:].reshape(1, batch, WINDOW, N_KV_HEADS, HEAD_DIM)

    def feature_major(buf):
        return jnp.transpose(buf, (0, 2, 3, 1)).reshape(dec_batch, KV_WIDTH, wb)

    def window_major(buf_t):
        return jnp.transpose(buf_t.reshape(dec_batch, N_KV_HEADS, HEAD_DIM, wb), (0, 3, 1, 2))[None]

    y_s, k_s, v_s, s_sample = _sample_mixer_ffn(
        x1, kv, z, feature_major(cache_k_win[layer]), feature_major(cache_v_win[layer]),
        state_hgrn[layer].astype(F32), sinks, g_attn, lb_logits, g_hg, *back_params, dec_len, t_p)
    y_sample = y_s.reshape(dec_batch, dec_len, D_MODEL)
    k_sample, v_sample = window_major(k_s), window_major(v_s)

    return (y_prompt, y_sample, k_prompt, v_prompt, s_prompt[None], k_sample, v_sample, s_sample[None])
```

```python
import functools

import jax
import jax.numpy as jnp
from jax import lax
from jax.experimental import pallas as pl
from jax.experimental.pallas import tpu as pltpu

F32 = jnp.float32
BF16 = jnp.bfloat16

D_MODEL = 1024
N_HEADS = 8
N_KV_HEADS = 2
HEAD_DIM = 64
GQA_GROUP = N_HEADS // N_KV_HEADS
WINDOW = 128
ATTN_WIDTH = N_HEADS * HEAD_DIM
KV_WIDTH = N_KV_HEADS * HEAD_DIM
SCALE = HEAD_DIM ** -0.5
HG_HEADS = 4
HG_DK = 128
HG_DV = 128
HG_WIDTH = HG_HEADS * HG_DV
D_FF = 2816
EPS = 1e-6

Q_COLS = (0, ATTN_WIDTH)
KV_COLS = (ATTN_WIDTH, ATTN_WIDTH + 2 * KV_WIDTH)
HP_COLS = (KV_COLS[1], KV_COLS[1] + 3 * HG_WIDTH)
HG_COLS = (HP_COLS[1], HP_COLS[1] + HG_WIDTH)
Z_Q = slice(0, ATTN_WIDTH)
Z_HP = slice(ATTN_WIDTH, ATTN_WIDTH + 3 * HG_WIDTH)
Z_HG = slice(ATTN_WIDTH + 3 * HG_WIDTH, ATTN_WIDTH + 4 * HG_WIDTH)
Z_WIDTH = ATTN_WIDTH + 4 * HG_WIDTH

VMEM_LIMIT_BIG = 58 * 1024 * 1024
FF_CHUNKS = ((0, 512), (512, 1024), (1024, 1536), (1536, 2048), (2048, 2560), (2560, 2816))
TOKEN_TILE = 512
SAMPLE_TILE = 128
HG_TILE = 64
ATTN_SEQ_GROUP = 4
NEG = -1e30
LOG2E = 1.4426950408889634


def _rms(x, g):
    return x * lax.rsqrt(jnp.mean(x * x, axis=-1, keepdims=True) + EPS) * g


def _silu(x):
    return x * jax.nn.sigmoid(x)


def _silu_t(x):
    h = 0.5 * x
    return h * jnp.tanh(h) + h


def _dot(a, b):
    return jnp.dot(a, b, preferred_element_type=F32)


def _dot_nt(a, b):
    return lax.dot_general(a, b, (((1,), (1,)), ((), ())), preferred_element_type=F32)


def _dot_tn(a, b):
    return lax.dot_general(a, b, (((0,), (0,)), ((), ())), preferred_element_type=F32)


def _split2(x):
    a = x.astype(BF16)
    b = (x - a.astype(F32)).astype(BF16)
    return jnp.concatenate([a, b], axis=0)


def _twice(sel):
    return jnp.concatenate([sel, sel], axis=1)


def _no_side_work():
    pass


def _swiglu_ffn(x, g_pre, wgu_ref, wd_ref, g_post, side=_no_side_work):
    h = _rms(x, g_pre).astype(BF16)
    acts = []
    for lo, hi in FF_CHUNKS:
        g = _dot(h, wgu_ref[:, lo:hi])
        side()
        u = _dot(h, wgu_ref[:, D_FF + lo:D_FF + hi])
        side()
        acts.append((_silu(g) * u).astype(BF16))
    down = _dot(jnp.concatenate(acts, axis=1), wd_ref[...])
    side()
    return x + 0.5 * _rms(down, g_post)


def _const_spec(shape):
    zeros = (0,) * len(shape)
    return pl.BlockSpec(shape, lambda *_: zeros, pipeline_mode=pl.Buffered(1))


def _row_spec(tile, width):
    return pl.BlockSpec((tile, width), lambda i: (i, 0))


def _ffn_in_kernel(xa_ref, xb_ref, g_pre, g_post, g_mix, wgu_ref, wd_ref, win_ref, bin_ref,
                   x1_ref, kv_ref, z_ref, *, tiles_a):
    x = jnp.where(pl.program_id(0) < tiles_a, xa_ref[...], xb_ref[...])
    x1 = _swiglu_ffn(x, g_pre[...], wgu_ref, wd_ref, g_post[...])
    x1_ref[...] = x1
    h = _rms(x1, g_mix[...]).astype(BF16)

    def project(lo, hi):
        return _dot(h, win_ref[:, lo:hi]) + bin_ref[:, lo:hi]

    qkv = project(Q_COLS[0], KV_COLS[1])
    z_ref[:, Z_Q] = qkv[:, 0:ATTN_WIDTH]
    kv_ref[...] = qkv[:, ATTN_WIDTH:ATTN_WIDTH + 2 * KV_WIDTH]
    z_ref[:, Z_HP.start:Z_HG.stop] = project(HP_COLS[0], HG_COLS[1])


def _ffn_in(xa, xb, g_pre, g_post, g_mix, wgu, wd, win, b_in):
    tile = TOKEN_TILE
    assert xa.shape[0] % tile == 0 and xb.shape[0] % tile == 0
    tiles_a, tiles_b = xa.shape[0] // tile, xb.shape[0] // tile
    n_tiles = tiles_a + tiles_b
    widths = [D_MODEL, 2 * KV_WIDTH, Z_WIDTH]
    return pl.pallas_call(
        functools.partial(_ffn_in_kernel, tiles_a=tiles_a),
        grid=(n_tiles,),
        in_specs=[pl.BlockSpec((tile, D_MODEL), lambda j: (jnp.minimum(j, tiles_a - 1), 0)),
                  pl.BlockSpec((tile, D_MODEL), lambda j: (jnp.maximum(j - tiles_a, 0), 0)),
                  _const_spec(g_pre.shape), _const_spec(g_post.shape),
                  _const_spec(g_mix.shape), _const_spec(wgu.shape), _const_spec(wd.shape),
                  _const_spec(win.shape), _const_spec(b_in.shape)],
        out_specs=[_row_spec(tile, w) for w in widths],
        out_shape=[jax.ShapeDtypeStruct((n_tiles * tile, w), F32) for w in widths],
        compiler_params=pltpu.CompilerParams(dimension_semantics=("arbitrary",),
                                             vmem_limit_bytes=VMEM_LIMIT_BIG),
        name="ffn_in",
    )(xa, xb, g_pre, g_post, g_mix, wgu, wd, win, b_in)


def _mix_ffn(x1, mix, g_mixpost, g_pre, g_post, wgu_ref, wd_ref, side=_no_side_work):
    x2 = x1 + _rms(mix, g_mixpost)
    return _swiglu_ffn(x2, g_pre, wgu_ref, wd_ref, g_post, side)


def _staged(mixers):
    waiting, running = list(mixers), []
    n_slots = 2 * len(FF_CHUNKS) + 2
    starts_per_slot = -(-len(waiting) // n_slots)

    def side():
        for _ in range(min(starts_per_slot, len(waiting))):
            running.append(waiting.pop(0))
        for gen in list(running):
            if next(gen, "done") == "done":
                running.remove(gen)

    def drain():
        while waiting or running:
            side()

    return side, drain


def _attn_block(sink_ref, q, kv_ref, q0, gn, store):
    w = WINDOW
    nkeys = 2 * w
    r = lax.broadcasted_iota(jnp.int32, (nkeys, w), 0)
    c = lax.broadcasted_iota(jnp.int32, (nkeys, w), 1)
    first_half = lax.broadcasted_iota(jnp.int32, (w, 2 * HEAD_DIM), 1) < HEAD_DIM
    k0 = jnp.maximum(q0 - w, 0)
    dist = c + (q0 - k0) - r
    bias = jnp.where((dist >= 0) & (dist <= WINDOW), 0.0, NEG)
    kvb = kv_ref[pl.ds(pl.multiple_of(k0, w), nkeys), :]
    kk = kvb[:, 0:KV_WIDTH]
    k_nat = kk.astype(BF16)
    k_swp = pltpu.roll(kk, HEAD_DIM, axis=1).astype(BF16)
    vt = kvb[:, KV_WIDTH:2 * KV_WIDTH].T.astype(BF16)

    def masked_q(h):
        qp = q[:, (h // 2) * 2 * HEAD_DIM:(h // 2 + 1) * 2 * HEAD_DIM]
        return jnp.where(first_half != bool(h % 2), qp, 0.0).astype(BF16)

    groups = []
    for keys, use_swapped in ((k_nat, False), (k_swp, True)):
        heads = [h for h in range(N_HEADS) if (h // GQA_GROUP != h % 2) == use_swapped]
        groups.append((keys, heads, jnp.concatenate([masked_q(h) for h in heads], axis=0)))
    yield
    scores = [None] * N_HEADS
    for keys, heads, qs in groups:
        st = _dot_nt(keys, qs)
        for i, h in enumerate(heads):
            scores[h] = st[:, i * w:(i + 1) * w]
    yield
    probs, dens = [], []
    for h in range(N_HEADS):
        st = scores[h] * (SCALE * LOG2E) + bias
        sink2 = sink_ref[h] * LOG2E
        m = jnp.maximum(jnp.max(st, axis=0, keepdims=True), sink2)
        p = jnp.exp2(st - m)
        dens.append(jnp.sum(p, axis=0, keepdims=True) + jnp.exp2(sink2 - m))
        probs.append(p.astype(BF16))
    yield
    ot = _dot(vt, jnp.concatenate(probs, axis=1))
    yield
    rows = []
    for h in range(N_HEADS):
        hk = h // GQA_GROUP
        rows.append(ot[hk * HEAD_DIM:(hk + 1) * HEAD_DIM, h * w:(h + 1) * w] / dens[h])
    at = jnp.concatenate(rows, axis=0)
    inv = lax.rsqrt(jnp.sum(at * at, axis=0, keepdims=True) * (1.0 / ATTN_WIDTH) + EPS)
    store((at * inv).T * gn)


def _softmax_sink_pv(scores, masks, values, sink):
    scores = [jnp.where(mk, s * SCALE, NEG) for s, mk in zip(scores, masks)]
    m = sink
    for s in scores:
        m = jnp.maximum(m, jnp.max(s, axis=-1, keepdims=True))
    den = jnp.exp(sink - m)
    out = None
    for s, (v, transposed) in zip(scores, values):
        p = jnp.exp(s - m)
        den = den + jnp.sum(p, axis=-1, keepdims=True)
        pv = (_dot_nt if transposed else _dot)(p.astype(BF16), v)
        out = pv if out is None else out + pv
    return out / den


def _attn_sample_group(sink_ref, z_ref, kvn_ref, ck_ref, cv_ref, gn_ref, a_ref, ko_ref, vo_ref, grp, dec_len):
    ns = ATTN_SEQ_GROUP
    rows = ns * dec_len
    wb = ck_ref.shape[2]
    m_rows = GQA_GROUP * rows
    r = lax.broadcasted_iota(jnp.int32, (m_rows, ns * wb), 0)
    c = lax.broadcasted_iota(jnp.int32, (m_rows, ns * wb), 1)
    r_seq, r_pos = (r % rows) // dec_len, r % dec_len
    mask_cache = (r_seq == c // wb) & (c % wb >= r_pos)
    r = lax.broadcasted_iota(jnp.int32, (m_rows, rows), 0)
    c = lax.broadcasted_iota(jnp.int32, (m_rows, rows), 1)
    mask_new = ((r % rows) // dec_len == c // dec_len) & (c % dec_len <= r % dec_len)
    row_head = lax.broadcasted_iota(jnp.int32, (m_rows, 1), 0) // rows

    rs = slice(grp * rows, (grp + 1) * rows)
    q = z_ref[rs, Z_Q]
    kvn_b = kvn_ref[rs, :].astype(BF16)

    def cached(ref, hk):
        frows = slice(hk * HEAD_DIM, (hk + 1) * HEAD_DIM)
        return jnp.concatenate([ref[grp * ns + n, frows, :] for n in range(ns)], axis=1).astype(BF16)

    scores = []
    for hk in range(N_KV_HEADS):
        kcols = slice(hk * HEAD_DIM, (hk + 1) * HEAD_DIM)
        heads = [hk * GQA_GROUP + g for g in range(GQA_GROUP)]
        qs = jnp.concatenate([q[:, h * HEAD_DIM:(h + 1) * HEAD_DIM] for h in heads], axis=0).astype(BF16)
        scores.append([_dot(qs, cached(ck_ref, hk)), _dot_nt(qs, kvn_b[:, kcols])])
    yield
    outs = []
    for hk in range(N_KV_HEADS):
        vcols = slice(KV_WIDTH + hk * HEAD_DIM, KV_WIDTH + (hk + 1) * HEAD_DIM)
        sink = jnp.zeros((m_rows, 1), F32)
        for g in range(GQA_GROUP):
            sink = jnp.where(row_head == g, sink_ref[hk * GQA_GROUP + g], sink)
        values = [(cached(cv_ref, hk), True), (kvn_b[:, vcols], False)]
        outs.append(_softmax_sink_pv(scores[hk], [mask_cache, mask_new], values, sink))
    yield
    for hk in range(N_KV_HEADS):
        for g in range(GQA_GROUP):
            h = hk * GQA_GROUP + g
            a_ref[rs, h * HEAD_DIM:(h + 1) * HEAD_DIM] = outs[hk][g * rows:(g + 1) * rows, :]
    a_ref[rs, :] = _rms(a_ref[rs, :], gn_ref[...])
    keep = wb - dec_len
    is_new = lax.broadcasted_iota(jnp.int32, (KV_WIDTH, wb), 1) >= keep
    for n in range(ns):
        s = grp * ns + n
        ts = slice(grp * rows + n * dec_len, grp * rows + (n + 1) * dec_len)
        for src, dst, cols in ((ck_ref, ko_ref, slice(0, KV_WIDTH)), (cv_ref, vo_ref, slice(KV_WIDTH, 2 * KV_WIDTH))):
            new_t = jnp.concatenate([jnp.zeros((keep, KV_WIDTH), F32), kvn_ref[ts, cols]], axis=0).T
            dst[s] = jnp.where(is_new, new_t, pltpu.roll(src[s], keep, axis=1))


def _hgrn_bound_consts(lbl_ref):
    lbl = lbl_ref[...]
    e = jnp.exp(lbl - jnp.max(lbl, axis=0, keepdims=True))
    lb = e[0:1, :] / jnp.sum(e, axis=0, keepdims=True)
    return 0.5 + 0.5 * lb, 0.5 - 0.5 * lb


def _hgrn_gates(hp, c0, c1):
    qa = _silu_t(hp[:, 0:HG_WIDTH])
    t = c1 * jnp.tanh(0.5 * hp[:, HG_WIDTH:2 * HG_WIDTH])
    f, kx = c0 + t, c1 - t
    v = hp[:, 2 * HG_WIDTH:3 * HG_WIDTH].astype(BF16)
    return qa, kx, _split2(jnp.log2(f)), v


def _hgrn_factors(qa, kx, g, g_mid, g_last):
    qt = qa * jnp.exp2(g - g_mid)
    kt = kx * jnp.exp2(g_mid - g)
    qg = (qt * jnp.exp2(g_mid)).astype(BF16)
    kd = (kt * jnp.exp2(g_last - g_mid)).astype(BF16)
    return qt.astype(BF16), kt.astype(BF16), qg, kd


def _segment_masks(nseq):
    ls = HG_TILE // nseq
    r = lax.broadcasted_iota(jnp.int32, (HG_TILE, HG_TILE), 0)
    c = lax.broadcasted_iota(jnp.int32, (HG_TILE, HG_TILE), 1)
    same = (r // ls) == (c // ls)
    return same, same & (c <= r), same & (c % ls < ls // 2)


def _as_bf16(mask):
    return mask.astype(F32).astype(BF16)


def _hgrn_short_tile(z_ref, c0, c1, gn, s_in, s_out, o_ref, ti, nseq):
    ch = HG_TILE
    ls = ch // nseq
    same, causal, first_half = _segment_masks(nseq)
    cum_lhs = _twice(_as_bf16(jnp.concatenate([causal, first_half, same], axis=0)))
    sr = lax.broadcasted_iota(jnp.int32, (ch, nseq * HG_DV), 0)
    sc = lax.broadcasted_iota(jnp.int32, (ch, nseq * HG_DV), 1)
    seg_sel = _as_bf16((sr // ls) == (sc // HG_DV))
    seg_sel2 = jnp.concatenate([seg_sel, seg_sel], axis=0)
    rows = slice(ti * ch, (ti + 1) * ch)

    qa, kx, parts, v = _hgrn_gates(z_ref[rows, Z_HP], c0, c1)
    gs = _dot(cum_lhs, parts)
    dcol = _dot_tn(parts, seg_sel2)
    yield
    qt, kt, qg, kd = _hgrn_factors(qa, kx, gs[0:ch], gs[ch:2 * ch], gs[2 * ch:3 * ch])
    cols = [slice(h * HG_DK, (h + 1) * HG_DK) for h in range(HG_HEADS)]
    scores = [_dot_nt(qt[:, cs], kt[:, cs]) for cs in cols]
    yield
    outs = []
    for h, cs in enumerate(cols):
        a = jnp.where(causal, scores[h], 0.0)
        o = _dot(a.astype(BF16), v[:, cs])
        inter = []
        for n in range(nseq):
            rs = slice(n * ls, (n + 1) * ls)
            s = s_in[ti * nseq + n, h]
            inter.append(_dot(qg[rs, cs], s.astype(BF16)))
            decay = jnp.exp2(dcol[cs, n * HG_DV:(n + 1) * HG_DV])
            s_out[ti * nseq + n, h] = s * decay + _dot_tn(kd[rs, cs], v[rs, cs])
        outs.append(o + jnp.concatenate(inter, axis=0))
    yield
    gate = _silu_t(z_ref[rows, Z_HG])
    for h, cs in enumerate(cols):
        o_ref[rows, cs] = _rms(outs[h], gn) * gate[:, cs]


def _sample_mixer_ffn_kernel(sink_ref, x1_ref, z_ref, kvn_ref, ck_ref, cv_ref, s_in_ref,
                             lbl_ref, gn_ref, ghg_ref, g_mixpost, g_pre, g_post, wout_ref, wgu_ref, wd_ref,
                             y_ref, ko_ref, vo_ref, s_out_ref, ao_scr, *, dec_len):
    j = pl.program_id(0)
    tile = x1_ref.shape[0]

    @pl.when(j == 0)
    def _():
        ao_scr[...] = jnp.zeros(ao_scr.shape, F32)

    a_scr, o_scr = ao_scr.at[:, 0:ATTN_WIDTH], ao_scr.at[:, ATTN_WIDTH:ATTN_WIDTH + HG_WIDTH]
    c0, c1 = _hgrn_bound_consts(lbl_ref)
    nseq = HG_TILE // dec_len
    mixers = [_attn_sample_group(sink_ref, z_ref, kvn_ref, ck_ref, cv_ref, gn_ref, a_scr, ko_ref, vo_ref, grp, dec_len)
              for grp in range(tile // (ATTN_SEQ_GROUP * dec_len))]
    mixers += [_hgrn_short_tile(z_ref, c0, c1, ghg_ref[...], s_in_ref, s_out_ref, o_scr, ti, nseq)
               for ti in range(tile // HG_TILE)]
    side, drain = _staged(mixers)

    mix = _dot(ao_scr[...].astype(BF16), wout_ref[...])
    side()
    y_ref[...] = _mix_ffn(x1_ref[...], mix, g_mixpost[...], g_pre[...], g_post[...], wgu_ref, wd_ref, side)
    drain()


def _sample_mixer_ffn(x1, kv, z, cache_k, cache_v, state, sinks, g_attn, lb_logits, g_hg,
                      g_mixpost, g_pre, g_post, wout, wgu, wd, dec_len, row0):
    tile = SAMPLE_TILE
    t = cache_k.shape[0] * dec_len
    n_tiles = t // tile
    spt = tile // dec_len
    assert t % tile == 0 and row0 % tile == 0 and tile % HG_TILE == 0 and tile % (ATTN_SEQ_GROUP * dec_len) == 0
    cur = lambda j: jnp.minimum(j, n_tiles - 1)
    prev = lambda j: jnp.maximum(j - 1, 0)
    cur_spec = lambda width: pl.BlockSpec((tile, width), lambda j: (row0 // tile + cur(j), 0))
    cache_spec = pl.BlockSpec((spt,) + cache_k.shape[1:], lambda j: (cur(j), 0, 0))
    state_spec = pl.BlockSpec((spt,) + state.shape[1:], lambda j: (cur(j), 0, 0, 0))
    return pl.pallas_call(
        functools.partial(_sample_mixer_ffn_kernel, dec_len=dec_len),
        grid=(n_tiles + 1,),
        in_specs=[pl.BlockSpec(memory_space=pltpu.SMEM),
                  pl.BlockSpec((tile, D_MODEL), lambda j: (row0 // tile + prev(j), 0)),
                  cur_spec(Z_WIDTH), cur_spec(2 * KV_WIDTH), cache_spec, cache_spec, state_spec,
                  _const_spec(lb_logits.shape), _const_spec(g_attn.shape), _const_spec(g_hg.shape),
                  _const_spec(g_mixpost.shape), _const_spec(g_pre.shape), _const_spec(g_post.shape),
                  _const_spec(wout.shape), _const_spec(wgu.shape), _const_spec(wd.shape)],
        out_specs=[pl.BlockSpec((tile, D_MODEL), lambda j: (prev(j), 0)), cache_spec, cache_spec, state_spec],
        out_shape=[jax.ShapeDtypeStruct((t, D_MODEL), F32), jax.ShapeDtypeStruct(cache_k.shape, F32),
                   jax.ShapeDtypeStruct(cache_v.shape, F32), jax.ShapeDtypeStruct(state.shape, F32)],
        scratch_shapes=[pltpu.VMEM((tile, ATTN_WIDTH + HG_WIDTH), F32)],
        compiler_params=pltpu.CompilerParams(dimension_semantics=("arbitrary",),
                                             vmem_limit_bytes=VMEM_LIMIT_BIG),
        name="sample_mixer_ffn",
    )(sinks, x1, z, kv, cache_k, cache_v, state, lb_logits, g_attn, g_hg,
      g_mixpost, g_pre, g_post, wout, wgu, wd)


def _mixer_ffn_kernel(sink_ref, x1_ref, z_ref, kv_ref, lbl_ref, gn_ref, ghg_ref,
                      g_mixpost, g_pre, g_post, wout_ref, wgu_ref, wd_ref,
                      y_ref, s_out_ref, ao_scr, st_scr, *, n_tiles, tiles_per_seq):
    j = pl.program_id(0)
    tile = x1_ref.shape[0]
    pos = jnp.minimum(j, n_tiles - 1) % tiles_per_seq

    @pl.when(j == 0)
    def _():
        ao_scr[...] = jnp.zeros(ao_scr.shape, ao_scr.dtype)
        st_scr[...] = jnp.zeros(st_scr.shape, F32)

    def attn_block(blk):
        rs = slice(blk * WINDOW, (blk + 1) * WINDOW)

        def store(a):
            ao_scr[rs, 0:ATTN_WIDTH] = a.astype(ao_scr.dtype)

        return _attn_block(sink_ref, z_ref[rs, Z_Q], kv_ref, pos * tile + blk * WINDOW, gn_ref[...], store)

    ch = HG_TILE
    _, causal, _ = _segment_masks(1)
    cum_lhs = _twice(_as_bf16(causal))
    c0, c1 = _hgrn_bound_consts(lbl_ref)
    ghg = ghg_ref[...]

    def hgrn_chunk(ci):
        rs = slice(ci * ch, (ci + 1) * ch)
        qa, kx, parts, v = _hgrn_gates(z_ref[rs, Z_HP], c0, c1)
        yield
        g = _dot(cum_lhs, parts)
        yield
        g_last = g[ch - 1:ch, :]
        qt, kt, qg, kd = _hgrn_factors(qa, kx, g, g[ch // 2 - 1:ch // 2, :], g_last)
        decay = jnp.exp2(g_last)
        cols = [slice(h * HG_DK, (h + 1) * HG_DK) for h in range(HG_HEADS)]
        yield
        scores = [_dot_nt(qt[:, cs], kt[:, cs]) for cs in cols]
        yield
        probs = [jnp.where(causal, sc, 0.0).astype(BF16) for sc in scores]
        yield
        outs = []
        for h, cs in enumerate(cols):
            st = st_scr[h]
            if ci == 0:
                st = jnp.where(pos == 0, 0.0, st)
            outs.append(_dot(probs[h], v[:, cs]) + _dot_nt(qg[:, cs], st.astype(BF16)))
            st_scr[h] = st * decay[:, cs] + _dot_tn(v[:, cs], kd[:, cs])
        yield
        gate = _silu_t(z_ref[rs, Z_HG])
        for h, cs in enumerate(cols):
            o = _rms(outs[h], ghg) * gate[:, cs]
            ao_scr[rs, ATTN_WIDTH + h * HG_DV:ATTN_WIDTH + (h + 1) * HG_DV] = o.astype(ao_scr.dtype)

    n_blk, n_chunk = tile // WINDOW, tile // ch
    mixers = []
    for blk in range(n_blk):
        mixers += [hgrn_chunk(ci) for ci in range(blk * n_chunk // n_blk, (blk + 1) * n_chunk // n_blk)]
        mixers.insert(len(mixers) - 1, attn_block(blk))
    side, drain = _staged(mixers)

    mix = _dot(ao_scr[...].astype(BF16), wout_ref[...])
    side()
    y_ref[...] = _mix_ffn(x1_ref[...], mix, g_mixpost[...], g_pre[...], g_post[...], wgu_ref, wd_ref, side)
    drain()

    @pl.when((pos == tiles_per_seq - 1) & (j < n_tiles))
    def _():
        b = j // tiles_per_seq
        for h in range(HG_HEADS):
            s_out_ref[pl.ds(b, 1), h] = st_scr[h].T[None]


def _mixer_ffn(x1, kv, z, sinks, g_attn, lb_logits, g_hg, g_mixpost, g_pre, g_post, wout, wgu, wd,
               batch, seq):
    t = batch * seq
    tile = TOKEN_TILE
    assert seq % tile == 0
    n_tiles, tps = t // tile, seq // tile
    cur = lambda j: jnp.minimum(j, n_tiles - 1)
    prev = lambda j: jnp.maximum(j - 1, 0)
    cur_spec = lambda width: pl.BlockSpec((tile, width), lambda j: (cur(j), 0))
    prev_spec = pl.BlockSpec((tile, D_MODEL), lambda j: (prev(j), 0))
    s_shape = (batch, HG_HEADS, HG_DK, HG_DV)
    return pl.pallas_call(
        functools.partial(_mixer_ffn_kernel, n_tiles=n_tiles, tiles_per_seq=tps),
        grid=(n_tiles + 1,),
        in_specs=[pl.BlockSpec(memory_space=pltpu.SMEM),
                  prev_spec, cur_spec(Z_WIDTH),
                  pl.BlockSpec((seq, 2 * KV_WIDTH), lambda j: (cur(j) // tps, 0)),
                  _const_spec(lb_logits.shape), _const_spec(g_attn.shape), _const_spec(g_hg.shape),
                  _const_spec(g_mixpost.shape), _const_spec(g_pre.shape), _const_spec(g_post.shape),
                  _const_spec(wout.shape), _const_spec(wgu.shape), _const_spec(wd.shape)],
        out_specs=[prev_spec, pl.BlockSpec(s_shape, lambda j: (0, 0, 0, 0))],
        out_shape=[jax.ShapeDtypeStruct((t, D_MODEL), F32), jax.ShapeDtypeStruct(s_shape, F32)],
        scratch_shapes=[pltpu.VMEM((tile, ATTN_WIDTH + HG_WIDTH), BF16), pltpu.VMEM((HG_HEADS, HG_DV, HG_DK), F32)],
        compiler_params=pltpu.CompilerParams(dimension_semantics=("arbitrary",),
                                             vmem_limit_bytes=VMEM_LIMIT_BIG),
        name="mixer_ffn",
    )(sinks, x1, z, kv, lb_logits, g_attn, g_hg, g_mixpost, g_pre, g_post, wout, wgu, wd)


def kernel(x_prompt, x_sample, cache_k_win, cache_v_win, state_hgrn, w_in, b_in, attn_sinks, attn_out_norm,
           hg_lb_logits, hg_out_norm, w_out, ffn1_w_gu, ffn1_w_down, ffn2_w_gu, ffn2_w_down,
           norm_ffn1_pre, norm_ffn1_post, norm_mix_pre, norm_mix_post, norm_ffn2_pre, norm_ffn2_post):
    depth = w_in.shape[0]
    assert depth == 1, "single-layer trunk"
    batch, seq, _ = x_prompt.shape
    dec_batch, dec_len, _ = x_sample.shape
    wb = cache_k_win.shape[2]
    assert seq % WINDOW == 0 and wb == WINDOW and HG_TILE % dec_len == 0
    layer = 0
    row = lambda p: p[layer].reshape(1, -1).astype(F32)
    wgu1, wd1 = ffn1_w_gu[layer].astype(BF16), ffn1_w_down[layer].astype(BF16)
    wgu2, wd2 = ffn2_w_gu[layer].astype(BF16), ffn2_w_down[layer].astype(BF16)
    win, wout = w_in[layer].astype(BF16), w_out[layer].astype(BF16)
    sinks = attn_sinks[layer].astype(F32)
    lb_logits = hg_lb_logits.astype(F32)
    g_attn, g_hg = row(attn_out_norm), row(hg_out_norm)
    back_params = (row(norm_mix_post), row(norm_ffn2_pre), row(norm_ffn2_post), wout, wgu2, wd2)

    t_p, t_s = batch * seq, dec_batch * dec_len
    x1, kv, z = _ffn_in(x_prompt.reshape(t_p, D_MODEL), x_sample.reshape(t_s, D_MODEL),
                                row(norm_ffn1_pre), row(norm_ffn1_post), row(norm_mix_pre), wgu1, wd1, win, row(b_in))

    y_p, s_prompt = _mixer_ffn(x1, kv, z, sinks, g_attn, lb_logits, g_hg, *back_params, batch, seq)
    y_prompt = y_p.reshape(batch, seq, D_MODEL)
    kv_last = jnp.stack([kv[(b + 1) * seq - WINDOW:(b + 1) * seq] for b in range(batch)])
    k_prompt = kv_last[..., :KV_WIDTH].reshape(1, batch, WINDOW, N_KV_HEADS, HEAD_DIM)
    v_prompt = kv_last[..., KV_WIDTH:].reshape(1, batch, WINDOW, N_KV_HEADS, HEAD_DIM)

    def feature_major(buf):
        return jnp.transpose(buf, (0, 2, 3, 1)).reshape(dec_batch, KV_WIDTH, wb)

    def window_major(buf_t):
        return jnp.transpose(buf_t.reshape(dec_batch, N_KV_HEADS, HEAD_DIM, wb), (0, 3, 1, 2))[None]

    y_s, k_s, v_s, s_sample = _sample_mixer_ffn(
        x1, kv, z, feature_major(cache_k_win[layer]), feature_major(cache_v_win[layer]),
        state_hgrn[layer].astype(F32), sinks, g_attn, lb_logits, g_hg, *back_params, dec_len, t_p)
    y_sample = y_s.reshape(dec_batch, dec_len, D_MODEL)
    k_sample, v_sample = window_major(k_s), window_major(v_s)

    return (y_prompt, y_sample, k_prompt, v_prompt, s_prompt[None], k_sample, v_sample, s_sample[None])
```

```python
import functools

import jax
import jax.numpy as jnp
from jax import lax
from jax.experimental import pallas as pl
from jax.experimental.pallas import tpu as pltpu

F32 = jnp.float32
BF16 = jnp.bfloat16

D_MODEL = 1024
N_HEADS = 8
N_KV_HEADS = 2
HEAD_DIM = 64
GQA_GROUP = N_HEADS // N_KV_HEADS
WINDOW = 128
ATTN_WIDTH = N_HEADS * HEAD_DIM
KV_WIDTH = N_KV_HEADS * HEAD_DIM
SCALE = HEAD_DIM ** -0.5
HG_HEADS = 4
HG_DK = 128
HG_DV = 128
HG_WIDTH = HG_HEADS * HG_DV
D_FF = 2816
EPS = 1e-6

Q_COLS = (0, ATTN_WIDTH)
KV_COLS = (ATTN_WIDTH, ATTN_WIDTH + 2 * KV_WIDTH)
HP_COLS = (KV_COLS[1], KV_COLS[1] + 3 * HG_WIDTH)
HG_COLS = (HP_COLS[1], HP_COLS[1] + HG_WIDTH)
Z_Q = slice(0, ATTN_WIDTH)
Z_HP = slice(ATTN_WIDTH, ATTN_WIDTH + 3 * HG_WIDTH)
Z_HG = slice(ATTN_WIDTH + 3 * HG_WIDTH, ATTN_WIDTH + 4 * HG_WIDTH)
Z_WIDTH = ATTN_WIDTH + 4 * HG_WIDTH

VMEM_LIMIT_BIG = 58 * 1024 * 1024
FF_CHUNKS = ((0, 512), (512, 1024), (1024, 1536), (1536, 2048), (2048, 2560), (2560, 2816))
FF_CHUNKS_FINE = tuple((lo, lo + 256) for lo in range(0, D_FF, 256))
TOKEN_TILE = 512
SAMPLE_TILE = 128
HG_TILE = 64
ATTN_SEQ_GROUP = 4
NEG = -1e30
LOG2E = 1.4426950408889634


def _rms(x, g):
    return x * lax.rsqrt(jnp.mean(x * x, axis=-1, keepdims=True) + EPS) * g


def _silu(x):
    return x * jax.nn.sigmoid(x)


def _silu_t(x):
    h = 0.5 * x
    return h * jnp.tanh(h) + h


def _dot(a, b):
    return jnp.dot(a, b, preferred_element_type=F32)


def _dot_nt(a, b):
    return lax.dot_general(a, b, (((1,), (1,)), ((), ())), preferred_element_type=F32)


def _dot_tn(a, b):
    return lax.dot_general(a, b, (((0,), (0,)), ((), ())), preferred_element_type=F32)


def _split2(x):
    a = x.astype(BF16)
    b = (x - a.astype(F32)).astype(BF16)
    return jnp.concatenate([a, b], axis=0)


def _twice(sel):
    return jnp.concatenate([sel, sel], axis=1)


def _no_side_work():
    pass


def _swiglu_ffn(x, g_pre, wgu_ref, wd_ref, g_post, side=_no_side_work, chunks=FF_CHUNKS):
    h = _rms(x, g_pre).astype(BF16)
    acts = []
    for lo, hi in chunks:
        g = _dot(h, wgu_ref[:, lo:hi])
        side()
        u = _dot(h, wgu_ref[:, D_FF + lo:D_FF + hi])
        side()
        acts.append((_silu(g) * u).astype(BF16))
    down = _dot(jnp.concatenate(acts, axis=1), wd_ref[...])
    side()
    return x + 0.5 * _rms(down, g_post)


def _const_spec(shape):
    zeros = (0,) * len(shape)
    return pl.BlockSpec(shape, lambda *_: zeros, pipeline_mode=pl.Buffered(1))


def _row_spec(tile, width):
    return pl.BlockSpec((tile, width), lambda i: (i, 0))


def _ffn_in_kernel(xa_ref, xb_ref, g_pre, g_post, g_mix, wgu_ref, wd_ref, win_ref, bin_ref,
                   x1_ref, kv_ref, z_ref, *, tiles_a):
    x = jnp.where(pl.program_id(0) < tiles_a, xa_ref[...], xb_ref[...])
    x1 = _swiglu_ffn(x, g_pre[...], wgu_ref, wd_ref, g_post[...])
    x1_ref[...] = x1
    h = _rms(x1, g_mix[...]).astype(BF16)

    def project(lo, hi):
        return _dot(h, win_ref[:, lo:hi]) + bin_ref[:, lo:hi]

    qkv = project(Q_COLS[0], KV_COLS[1])
    z_ref[:, Z_Q] = qkv[:, 0:ATTN_WIDTH]
    kv_ref[...] = qkv[:, ATTN_WIDTH:ATTN_WIDTH + 2 * KV_WIDTH]
    z_ref[:, Z_HP.start:Z_HG.stop] = project(HP_COLS[0], HG_COLS[1])


def _ffn_in(xa, xb, g_pre, g_post, g_mix, wgu, wd, win, b_in):
    tile = TOKEN_TILE
    assert xa.shape[0] % tile == 0 and xb.shape[0] % tile == 0
    tiles_a, tiles_b = xa.shape[0] // tile, xb.shape[0] // tile
    n_tiles = tiles_a + tiles_b
    widths = [D_MODEL, 2 * KV_WIDTH, Z_WIDTH]
    return pl.pallas_call(
        functools.partial(_ffn_in_kernel, tiles_a=tiles_a),
        grid=(n_tiles,),
        in_specs=[pl.BlockSpec((tile, D_MODEL), lambda j: (jnp.minimum(j, tiles_a - 1), 0)),
                  pl.BlockSpec((tile, D_MODEL), lambda j: (jnp.maximum(j - tiles_a, 0), 0)),
                  _const_spec(g_pre.shape), _const_spec(g_post.shape),
                  _const_spec(g_mix.shape), _const_spec(wgu.shape), _const_spec(wd.shape),
                  _const_spec(win.shape), _const_spec(b_in.shape)],
        out_specs=[_row_spec(tile, w) for w in widths],
        out_shape=[jax.ShapeDtypeStruct((n_tiles * tile, w), F32) for w in widths],
        compiler_params=pltpu.CompilerParams(dimension_semantics=("arbitrary",),
                                             vmem_limit_bytes=VMEM_LIMIT_BIG),
        name="ffn_in",
    )(xa, xb, g_pre, g_post, g_mix, wgu, wd, win, b_in)


def _mix_ffn(x1, mix, g_mixpost, g_pre, g_post, wgu_ref, wd_ref, side=_no_side_work, chunks=FF_CHUNKS):
    x2 = x1 + _rms(mix, g_mixpost)
    return _swiglu_ffn(x2, g_pre, wgu_ref, wd_ref, g_post, side, chunks)


def _staged(mixers, chunks=FF_CHUNKS):
    waiting, running = list(mixers), []
    n_slots = 2 * len(chunks) + 2
    starts_per_slot = -(-len(waiting) // n_slots)

    def side():
        for _ in range(min(starts_per_slot, len(waiting))):
            running.append(waiting.pop(0))
        for gen in list(running):
            if next(gen, "done") == "done":
                running.remove(gen)

    def drain():
        while waiting or running:
            side()

    return side, drain


def _attn_block(sink_ref, q, kv_ref, q0, gn, store):
    w = WINDOW
    nkeys = 2 * w
    r = lax.broadcasted_iota(jnp.int32, (nkeys, w), 0)
    c = lax.broadcasted_iota(jnp.int32, (nkeys, w), 1)
    first_half = lax.broadcasted_iota(jnp.int32, (w, 2 * HEAD_DIM), 1) < HEAD_DIM
    k0 = jnp.maximum(q0 - w, 0)
    dist = c + (q0 - k0) - r
    bias = jnp.where((dist >= 0) & (dist <= WINDOW), 0.0, NEG)
    kvb = kv_ref[pl.ds(pl.multiple_of(k0, w), nkeys), :]
    kk = kvb[:, 0:KV_WIDTH]
    k_nat = kk.astype(BF16)
    k_swp = pltpu.roll(kk, HEAD_DIM, axis=1).astype(BF16)
    vt = kvb[:, KV_WIDTH:2 * KV_WIDTH].T.astype(BF16)

    def masked_q(h):
        qp = q[:, (h // 2) * 2 * HEAD_DIM:(h // 2 + 1) * 2 * HEAD_DIM]
        return jnp.where(first_half != bool(h % 2), qp, 0.0).astype(BF16)

    groups = []
    for keys, use_swapped in ((k_nat, False), (k_swp, True)):
        heads = [h for h in range(N_HEADS) if (h // GQA_GROUP != h % 2) == use_swapped]
        groups.append((keys, heads, jnp.concatenate([masked_q(h) for h in heads], axis=0)))
    yield
    scores = [None] * N_HEADS
    for keys, heads, qs in groups:
        st = _dot_nt(keys, qs)
        for i, h in enumerate(heads):
            scores[h] = st[:, i * w:(i + 1) * w]
    yield
    probs, dens = [], []
    for h in range(N_HEADS):
        st = scores[h] * (SCALE * LOG2E) + bias
        sink2 = sink_ref[h] * LOG2E
        m = jnp.maximum(jnp.max(st, axis=0, keepdims=True), sink2)
        p = jnp.exp2(st - m)
        dens.append(jnp.sum(p, axis=0, keepdims=True) + jnp.exp2(sink2 - m))
        probs.append(p.astype(BF16))
    yield
    ot = _dot(vt, jnp.concatenate(probs, axis=1))
    yield
    rows = []
    for h in range(N_HEADS):
        hk = h // GQA_GROUP
        rows.append(ot[hk * HEAD_DIM:(hk + 1) * HEAD_DIM, h * w:(h + 1) * w] / dens[h])
    at = jnp.concatenate(rows, axis=0)
    inv = lax.rsqrt(jnp.sum(at * at, axis=0, keepdims=True) * (1.0 / ATTN_WIDTH) + EPS)
    store((at * inv).T * gn)


def _softmax_sink_pv(scores, masks, values, sink):
    scores = [jnp.where(mk, s * SCALE, NEG) for s, mk in zip(scores, masks)]
    m = sink
    for s in scores:
        m = jnp.maximum(m, jnp.max(s, axis=-1, keepdims=True))
    den = jnp.exp(sink - m)
    out = None
    for s, (v, transposed) in zip(scores, values):
        p = jnp.exp(s - m)
        den = den + jnp.sum(p, axis=-1, keepdims=True)
        pv = (_dot_nt if transposed else _dot)(p.astype(BF16), v)
        out = pv if out is None else out + pv
    return out / den


def _attn_sample_group(sink_ref, z_ref, kvn_ref, ck_ref, cv_ref, gn_ref, a_ref, ko_ref, vo_ref, grp, dec_len):
    ns = ATTN_SEQ_GROUP
    rows = ns * dec_len
    wb = ck_ref.shape[2]
    m_rows = GQA_GROUP * rows
    r = lax.broadcasted_iota(jnp.int32, (m_rows, ns * wb), 0)
    c = lax.broadcasted_iota(jnp.int32, (m_rows, ns * wb), 1)
    r_seq, r_pos = (r % rows) // dec_len, r % dec_len
    mask_cache = (r_seq == c // wb) & (c % wb >= r_pos)
    r = lax.broadcasted_iota(jnp.int32, (m_rows, rows), 0)
    c = lax.broadcasted_iota(jnp.int32, (m_rows, rows), 1)
    mask_new = ((r % rows) // dec_len == c // dec_len) & (c % dec_len <= r % dec_len)
    row_head = lax.broadcasted_iota(jnp.int32, (m_rows, 1), 0) // rows

    rs = slice(grp * rows, (grp + 1) * rows)
    q = z_ref[rs, Z_Q]
    kvn_b = kvn_ref[rs, :].astype(BF16)

    def cached(ref, hk):
        frows = slice(hk * HEAD_DIM, (hk + 1) * HEAD_DIM)
        return jnp.concatenate([ref[grp * ns + n, frows, :] for n in range(ns)], axis=1).astype(BF16)

    scores = []
    for hk in range(N_KV_HEADS):
        kcols = slice(hk * HEAD_DIM, (hk + 1) * HEAD_DIM)
        heads = [hk * GQA_GROUP + g for g in range(GQA_GROUP)]
        qs = jnp.concatenate([q[:, h * HEAD_DIM:(h + 1) * HEAD_DIM] for h in heads], axis=0).astype(BF16)
        scores.append([_dot(qs, cached(ck_ref, hk)), _dot_nt(qs, kvn_b[:, kcols])])
    yield
    outs = []
    for hk in range(N_KV_HEADS):
        vcols = slice(KV_WIDTH + hk * HEAD_DIM, KV_WIDTH + (hk + 1) * HEAD_DIM)
        sink = jnp.zeros((m_rows, 1), F32)
        for g in range(GQA_GROUP):
            sink = jnp.where(row_head == g, sink_ref[hk * GQA_GROUP + g], sink)
        values = [(cached(cv_ref, hk), True), (kvn_b[:, vcols], False)]
        outs.append(_softmax_sink_pv(scores[hk], [mask_cache, mask_new], values, sink))
    yield
    for hk in range(N_KV_HEADS):
        for g in range(GQA_GROUP):
            h = hk * GQA_GROUP + g
            a_ref[rs, h * HEAD_DIM:(h + 1) * HEAD_DIM] = outs[hk][g * rows:(g + 1) * rows, :]
    a_ref[rs, :] = _rms(a_ref[rs, :], gn_ref[...])
    keep = wb - dec_len
    is_new = lax.broadcasted_iota(jnp.int32, (KV_WIDTH, wb), 1) >= keep
    for n in range(ns):
        s = grp * ns + n
        ts = slice(grp * rows + n * dec_len, grp * rows + (n + 1) * dec_len)
        for src, dst, cols in ((ck_ref, ko_ref, slice(0, KV_WIDTH)), (cv_ref, vo_ref, slice(KV_WIDTH, 2 * KV_WIDTH))):
            new_t = jnp.concatenate([jnp.zeros((keep, KV_WIDTH), F32), kvn_ref[ts, cols]], axis=0).T
            dst[s] = jnp.where(is_new, new_t, pltpu.roll(src[s], keep, axis=1))


def _hgrn_bound_consts(lbl_ref):
    lbl = lbl_ref[...]
    e = jnp.exp(lbl - jnp.max(lbl, axis=0, keepdims=True))
    lb = e[0:1, :] / jnp.sum(e, axis=0, keepdims=True)
    return 0.5 + 0.5 * lb, 0.5 - 0.5 * lb


def _hgrn_gates(hp, c0, c1):
    qa = _silu_t(hp[:, 0:HG_WIDTH])
    t = c1 * jnp.tanh(0.5 * hp[:, HG_WIDTH:2 * HG_WIDTH])
    f, kx = c0 + t, c1 - t
    v = hp[:, 2 * HG_WIDTH:3 * HG_WIDTH].astype(BF16)
    return qa, kx, _split2(jnp.log2(f)), v


def _hgrn_factors(qa, kx, g, g_mid, g_last):
    qt = qa * jnp.exp2(g - g_mid)
    kt = kx * jnp.exp2(g_mid - g)
    qg = (qt * jnp.exp2(g_mid)).astype(BF16)
    kd = (kt * jnp.exp2(g_last - g_mid)).astype(BF16)
    return qt.astype(BF16), kt.astype(BF16), qg, kd


def _segment_masks(nseq):
    ls = HG_TILE // nseq
    r = lax.broadcasted_iota(jnp.int32, (HG_TILE, HG_TILE), 0)
    c = lax.broadcasted_iota(jnp.int32, (HG_TILE, HG_TILE), 1)
    same = (r // ls) == (c // ls)
    return same, same & (c <= r), same & (c % ls < ls // 2)


def _as_bf16(mask):
    return mask.astype(F32).astype(BF16)


def _hgrn_short_tile(z_ref, c0, c1, gn, s_in, s_out, o_ref, ti, nseq):
    ch = HG_TILE
    ls = ch // nseq
    same, causal, first_half = _segment_masks(nseq)
    cum_lhs = _twice(_as_bf16(jnp.concatenate([causal, first_half, same], axis=0)))
    sr = lax.broadcasted_iota(jnp.int32, (ch, nseq * HG_DV), 0)
    sc = lax.broadcasted_iota(jnp.int32, (ch, nseq * HG_DV), 1)
    seg_sel = _as_bf16((sr // ls) == (sc // HG_DV))
    seg_sel2 = jnp.concatenate([seg_sel, seg_sel], axis=0)
    rows = slice(ti * ch, (ti + 1) * ch)

    qa, kx, parts, v = _hgrn_gates(z_ref[rows, Z_HP], c0, c1)
    gs = _dot(cum_lhs, parts)
    dcol = _dot_tn(parts, seg_sel2)
    yield
    qt, kt, qg, kd = _hgrn_factors(qa, kx, gs[0:ch], gs[ch:2 * ch], gs[2 * ch:3 * ch])
    cols = [slice(h * HG_DK, (h + 1) * HG_DK) for h in range(HG_HEADS)]
    scores = [_dot_nt(qt[:, cs], kt[:, cs]) for cs in cols]
    yield
    outs = []
    for h, cs in enumerate(cols):
        a = jnp.where(causal, scores[h], 0.0)
        o = _dot(a.astype(BF16), v[:, cs])
        inter = []
        for n in range(nseq):
            rs = slice(n * ls, (n + 1) * ls)
            s = s_in[ti * nseq + n, h]
            inter.append(_dot(qg[rs, cs], s.astype(BF16)))
            decay = jnp.exp2(dcol[cs, n * HG_DV:(n + 1) * HG_DV])
            s_out[ti * nseq + n, h] = s * decay + _dot_tn(kd[rs, cs], v[rs, cs])
        outs.append(o + jnp.concatenate(inter, axis=0))
    yield
    gate = _silu_t(z_ref[rows, Z_HG])
    for h, cs in enumerate(cols):
        o_ref[rows, cs] = _rms(outs[h], gn) * gate[:, cs]


def _sample_mixer_ffn_kernel(sink_ref, x1_ref, z_ref, kvn_ref, ck_ref, cv_ref, s_in_ref,
                             lbl_ref, gn_ref, ghg_ref, g_mixpost, g_pre, g_post, wout_ref, wgu_ref, wd_ref,
                             y_ref, ko_ref, vo_ref, s_out_ref, ao_scr, *, dec_len):
    j = pl.program_id(0)
    tile = x1_ref.shape[0]

    @pl.when(j == 0)
    def _():
        ao_scr[...] = jnp.zeros(ao_scr.shape, F32)

    a_scr, o_scr = ao_scr.at[:, 0:ATTN_WIDTH], ao_scr.at[:, ATTN_WIDTH:ATTN_WIDTH + HG_WIDTH]
    c0, c1 = _hgrn_bound_consts(lbl_ref)
    nseq = HG_TILE // dec_len
    mixers = [_attn_sample_group(sink_ref, z_ref, kvn_ref, ck_ref, cv_ref, gn_ref, a_scr, ko_ref, vo_ref, grp, dec_len)
              for grp in range(tile // (ATTN_SEQ_GROUP * dec_len))]
    mixers += [_hgrn_short_tile(z_ref, c0, c1, ghg_ref[...], s_in_ref, s_out_ref, o_scr, ti, nseq)
               for ti in range(tile // HG_TILE)]
    side, drain = _staged(mixers)

    mix = _dot(ao_scr[...].astype(BF16), wout_ref[...])
    side()
    y_ref[...] = _mix_ffn(x1_ref[...], mix, g_mixpost[...], g_pre[...], g_post[...], wgu_ref, wd_ref, side)
    drain()


def _sample_mixer_ffn(x1, kv, z, cache_k, cache_v, state, sinks, g_attn, lb_logits, g_hg,
                      g_mixpost, g_pre, g_post, wout, wgu, wd, dec_len, row0):
    tile = SAMPLE_TILE
    t = cache_k.shape[0] * dec_len
    n_tiles = t // tile
    spt = tile // dec_len
    assert t % tile == 0 and row0 % tile == 0 and tile % HG_TILE == 0 and tile % (ATTN_SEQ_GROUP * dec_len) == 0
    cur = lambda j: jnp.minimum(j, n_tiles - 1)
    prev = lambda j: jnp.maximum(j - 1, 0)
    cur_spec = lambda width: pl.BlockSpec((tile, width), lambda j: (row0 // tile + cur(j), 0))
    cache_spec = pl.BlockSpec((spt,) + cache_k.shape[1:], lambda j: (cur(j), 0, 0))
    state_spec = pl.BlockSpec((spt,) + state.shape[1:], lambda j: (cur(j), 0, 0, 0))
    return pl.pallas_call(
        functools.partial(_sample_mixer_ffn_kernel, dec_len=dec_len),
        grid=(n_tiles + 1,),
        in_specs=[pl.BlockSpec(memory_space=pltpu.SMEM),
                  pl.BlockSpec((tile, D_MODEL), lambda j: (row0 // tile + prev(j), 0)),
                  cur_spec(Z_WIDTH), cur_spec(2 * KV_WIDTH), cache_spec, cache_spec, state_spec,
                  _const_spec(lb_logits.shape), _const_spec(g_attn.shape), _const_spec(g_hg.shape),
                  _const_spec(g_mixpost.shape), _const_spec(g_pre.shape), _const_spec(g_post.shape),
                  _const_spec(wout.shape), _const_spec(wgu.shape), _const_spec(wd.shape)],
        out_specs=[pl.BlockSpec((tile, D_MODEL), lambda j: (prev(j), 0)), cache_spec, cache_spec, state_spec],
        out_shape=[jax.ShapeDtypeStruct((t, D_MODEL), F32), jax.ShapeDtypeStruct(cache_k.shape, F32),
                   jax.ShapeDtypeStruct(cache_v.shape, F32), jax.ShapeDtypeStruct(state.shape, F32)],
        scratch_shapes=[pltpu.VMEM((tile, ATTN_WIDTH + HG_WIDTH), F32)],
        compiler_params=pltpu.CompilerParams(dimension_semantics=("arbitrary",),
                                             vmem_limit_bytes=VMEM_LIMIT_BIG),
        name="sample_mixer_ffn",
    )(sinks, x1, z, kv, cache_k, cache_v, state, lb_logits, g_attn, g_hg,
      g_mixpost, g_pre, g_post, wout, wgu, wd)


def _mixer_ffn_kernel(sink_ref, x1_ref, z_ref, kv_ref, lbl_ref, gn_ref, ghg_ref,
                      g_mixpost, g_pre, g_post, wout_ref, wgu_ref, wd_ref,
                      y_ref, s_out_ref, ao_scr, st_scr, *, n_tiles, tiles_per_seq):
    j = pl.program_id(0)
    tile = x1_ref.shape[0]
    pos = jnp.minimum(j, n_tiles - 1) % tiles_per_seq

    @pl.when(j == 0)
    def _():
        ao_scr[...] = jnp.zeros(ao_scr.shape, ao_scr.dtype)
        st_scr[...] = jnp.zeros(st_scr.shape, F32)

    def attn_block(blk):
        rs = slice(blk * WINDOW, (blk + 1) * WINDOW)

        def store(a):
            ao_scr[rs, 0:ATTN_WIDTH] = a.astype(ao_scr.dtype)

        return _attn_block(sink_ref, z_ref[rs, Z_Q], kv_ref, pos * tile + blk * WINDOW, gn_ref[...], store)

    ch = HG_TILE
    _, causal, _ = _segment_masks(1)
    cum_lhs = _twice(_as_bf16(causal))
    c0, c1 = _hgrn_bound_consts(lbl_ref)
    ghg = ghg_ref[...]

    def hgrn_chunk(ci):
        rs = slice(ci * ch, (ci + 1) * ch)
        qa, kx, parts, v = _hgrn_gates(z_ref[rs, Z_HP], c0, c1)
        yield
        g = _dot(cum_lhs, parts)
        yield
        g_last = g[ch - 1:ch, :]
        qt, kt, qg, kd = _hgrn_factors(qa, kx, g, g[ch // 2 - 1:ch // 2, :], g_last)
        decay = jnp.exp2(g_last)
        cols = [slice(h * HG_DK, (h + 1) * HG_DK) for h in range(HG_HEADS)]
        yield
        scores = [_dot_nt(qt[:, cs], kt[:, cs]) for cs in cols]
        yield
        probs = [jnp.where(causal, sc, 0.0).astype(BF16) for sc in scores]
        yield
        outs = []
        for h, cs in enumerate(cols):
            st = st_scr[h]
            if ci == 0:
                st = jnp.where(pos == 0, 0.0, st)
            outs.append(_dot(probs[h], v[:, cs]) + _dot_nt(qg[:, cs], st.astype(BF16)))
            st_scr[h] = st * decay[:, cs] + _dot_tn(v[:, cs], kd[:, cs])
        yield
        gate = _silu_t(z_ref[rs, Z_HG])
        for h, cs in enumerate(cols):
            o = _rms(outs[h], ghg) * gate[:, cs]
            ao_scr[rs, ATTN_WIDTH + h * HG_DV:ATTN_WIDTH + (h + 1) * HG_DV] = o.astype(ao_scr.dtype)

    n_blk, n_chunk = tile // WINDOW, tile // ch
    mixers = []
    for blk in range(n_blk):
        mixers += [hgrn_chunk(ci) for ci in range(blk * n_chunk // n_blk, (blk + 1) * n_chunk // n_blk)]
        mixers.insert(len(mixers) - 1, attn_block(blk))
    side, drain = _staged(mixers, FF_CHUNKS_FINE)

    mix = _dot(ao_scr[...].astype(BF16), wout_ref[...])
    side()
    y_ref[...] = _mix_ffn(x1_ref[...], mix, g_mixpost[...], g_pre[...], g_post[...], wgu_ref, wd_ref, side,
                          FF_CHUNKS_FINE)
    drain()

    @pl.when((pos == tiles_per_seq - 1) & (j < n_tiles))
    def _():
        b = j // tiles_per_seq
        for h in range(HG_HEADS):
            s_out_ref[pl.ds(b, 1), h] = st_scr[h].T[None]


def _mixer_ffn(x1, kv, z, sinks, g_attn, lb_logits, g_hg, g_mixpost, g_pre, g_post, wout, wgu, wd,
               batch, seq):
    t = batch * seq
    tile = TOKEN_TILE
    assert seq % tile == 0
    n_tiles, tps = t // tile, seq // tile
    cur = lambda j: jnp.minimum(j, n_tiles - 1)
    prev = lambda j: jnp.maximum(j - 1, 0)
    cur_spec = lambda width: pl.BlockSpec((tile, width), lambda j: (cur(j), 0))
    prev_spec = pl.BlockSpec((tile, D_MODEL), lambda j: (prev(j), 0))
    s_shape = (batch, HG_HEADS, HG_DK, HG_DV)
    return pl.pallas_call(
        functools.partial(_mixer_ffn_kernel, n_tiles=n_tiles, tiles_per_seq=tps),
        grid=(n_tiles + 1,),
        in_specs=[pl.BlockSpec(memory_space=pltpu.SMEM),
                  prev_spec, cur_spec(Z_WIDTH),
                  pl.BlockSpec((seq, 2 * KV_WIDTH), lambda j: (cur(j) // tps, 0)),
                  _const_spec(lb_logits.shape), _const_spec(g_attn.shape), _const_spec(g_hg.shape),
                  _const_spec(g_mixpost.shape), _const_spec(g_pre.shape), _const_spec(g_post.shape),
                  _const_spec(wout.shape), _const_spec(wgu.shape), _const_spec(wd.shape)],
        out_specs=[prev_spec, pl.BlockSpec(s_shape, lambda j: (0, 0, 0, 0))],
        out_shape=[jax.ShapeDtypeStruct((t, D_MODEL), F32), jax.ShapeDtypeStruct(s_shape, F32)],
        scratch_shapes=[pltpu.VMEM((tile, ATTN_WIDTH + HG_WIDTH), BF16), pltpu.VMEM((HG_HEADS, HG_DV, HG_DK), F32)],
        compiler_params=pltpu.CompilerParams(dimension_semantics=("arbitrary",),
                                             vmem_limit_bytes=VMEM_LIMIT_BIG),
        name="mixer_ffn",
    )(sinks, x1, z, kv, lb_logits, g_attn, g_hg, g_mixpost, g_pre, g_post, wout, wgu, wd)


def kernel(x_prompt, x_sample, cache_k_win, cache_v_win, state_hgrn, w_in, b_in, attn_sinks, attn_out_norm,
           hg_lb_logits, hg_out_norm, w_out, ffn1_w_gu, ffn1_w_down, ffn2_w_gu, ffn2_w_down,
           norm_ffn1_pre, norm_ffn1_post, norm_mix_pre, norm_mix_post, norm_ffn2_pre, norm_ffn2_post):
    depth = w_in.shape[0]
    assert depth == 1, "single-layer trunk"
    batch, seq, _ = x_prompt.shape
    dec_batch, dec_len, _ = x_sample.shape
    wb = cache_k_win.shape[2]
    assert seq % WINDOW == 0 and wb == WINDOW and HG_TILE % dec_len == 0
    layer = 0
    row = lambda p: p[layer].reshape(1, -1).astype(F32)
    wgu1, wd1 = ffn1_w_gu[layer].astype(BF16), ffn1_w_down[layer].astype(BF16)
    wgu2, wd2 = ffn2_w_gu[layer].astype(BF16), ffn2_w_down[layer].astype(BF16)
    win, wout = w_in[layer].astype(BF16), w_out[layer].astype(BF16)
    sinks = attn_sinks[layer].astype(F32)
    lb_logits = hg_lb_logits.astype(F32)
    g_attn, g_hg = row(attn_out_norm), row(hg_out_norm)
    back_params = (row(norm_mix_post), row(norm_ffn2_pre), row(norm_ffn2_post), wout, wgu2, wd2)

    t_p, t_s = batch * seq, dec_batch * dec_len
    x1, kv, z = _ffn_in(x_prompt.reshape(t_p, D_MODEL), x_sample.reshape(t_s, D_MODEL),
                                row(norm_ffn1_pre), row(norm_ffn1_post), row(norm_mix_pre), wgu1, wd1, win, row(b_in))

    y_p, s_prompt = _mixer_ffn(x1, kv, z, sinks, g_attn, lb_logits, g_hg, *back_params, batch, seq)
    y_prompt = y_p.reshape(batch, seq, D_MODEL)
    kv_last = jnp.stack([kv[(b + 1) * seq - WINDOW:(b + 1) * seq] for b in range(batch)])
    k_prompt = kv_last[..., :KV_WIDTH].reshape(1, batch, WINDOW, N_KV_HEADS, HEAD_DIM)
    v_prompt = kv_last[..., KV_WIDTH:].reshape(1, batch, WINDOW, N_KV_HEADS, HEAD_DIM)

    def feature_major(buf):
        return jnp.transpose(buf, (0, 2, 3, 1)).reshape(dec_batch, KV_WIDTH, wb)

    def window_major(buf_t):
        return jnp.transpose(buf_t.reshape(dec_batch, N_KV_HEADS, HEAD_DIM, wb), (0, 3, 1, 2))[None]

    y_s, k_s, v_s, s_sample = _sample_mixer_ffn(
        x1, kv, z, feature_major(cache_k_win[layer]), feature_major(cache_v_win[layer]),
        state_hgrn[layer].astype(F32), sinks, g_attn, lb_logits, g_hg, *back_params, dec_len, t_p)
    y_sample = y_s.reshape(dec_batch, dec_len, D_MODEL)
    k_sample, v_sample = window_major(k_s), window_major(v_s)

    return (y_prompt, y_sample, k_prompt, v_prompt, s_prompt[None], k_sample, v_sample, s_sample[None])
```

```python
import functools

import jax
import jax.numpy as jnp
from jax import lax
from jax.experimental import pallas as pl
from jax.experimental.pallas import tpu as pltpu

F32 = jnp.float32
BF16 = jnp.bfloat16

D_MODEL = 1024
N_HEADS = 8
N_KV_HEADS = 2
HEAD_DIM = 64
GQA_GROUP = N_HEADS // N_KV_HEADS
WINDOW = 128
ATTN_WIDTH = N_HEADS * HEAD_DIM
KV_WIDTH = N_KV_HEADS * HEAD_DIM
SCALE = HEAD_DIM ** -0.5
HG_HEADS = 4
HG_DK = 128
HG_DV = 128
HG_WIDTH = HG_HEADS * HG_DV
D_FF = 2816
EPS = 1e-6

Q_COLS = (0, ATTN_WIDTH)
KV_COLS = (ATTN_WIDTH, ATTN_WIDTH + 2 * KV_WIDTH)
HP_COLS = (KV_COLS[1], KV_COLS[1] + 3 * HG_WIDTH)
HG_COLS = (HP_COLS[1], HP_COLS[1] + HG_WIDTH)
Z_Q = slice(0, ATTN_WIDTH)
Z_HP = slice(ATTN_WIDTH, ATTN_WIDTH + 3 * HG_WIDTH)
Z_HG = slice(ATTN_WIDTH + 3 * HG_WIDTH, ATTN_WIDTH + 4 * HG_WIDTH)
Z_WIDTH = ATTN_WIDTH + 4 * HG_WIDTH

VMEM_LIMIT_BIG = 58 * 1024 * 1024
FF_CHUNKS = tuple((lo, lo + 256) for lo in range(0, D_FF, 256))
FF_CHUNKS_FINE = FF_CHUNKS
TOKEN_TILE = 512
SAMPLE_TILE = 128
HG_TILE = 64
ATTN_SEQ_GROUP = 4
NEG = -1e30
LOG2E = 1.4426950408889634


def _rms(x, g):
    return x * lax.rsqrt(jnp.mean(x * x, axis=-1, keepdims=True) + EPS) * g


def _silu(x):
    return x * jax.nn.sigmoid(x)


def _silu_t(x):
    h = 0.5 * x
    return h * jnp.tanh(h) + h


def _dot(a, b):
    return jnp.dot(a, b, preferred_element_type=F32)


def _dot_nt(a, b):
    return lax.dot_general(a, b, (((1,), (1,)), ((), ())), preferred_element_type=F32)


def _dot_tn(a, b):
    return lax.dot_general(a, b, (((0,), (0,)), ((), ())), preferred_element_type=F32)


def _split2(x):
    a = x.astype(BF16)
    b = (x - a.astype(F32)).astype(BF16)
    return jnp.concatenate([a, b], axis=0)


def _twice(sel):
    return jnp.concatenate([sel, sel], axis=1)


def _no_side_work():
    pass


def _swiglu_ffn(x, g_pre, wgu_ref, wd_ref, g_post, side=_no_side_work, chunks=FF_CHUNKS):
    h = _rms(x, g_pre).astype(BF16)
    acts = []
    for lo, hi in chunks:
        g = _dot(h, wgu_ref[:, lo:hi])
        side()
        u = _dot(h, wgu_ref[:, D_FF + lo:D_FF + hi])
        side()
        acts.append((_silu(g) * u).astype(BF16))
    down = _dot(jnp.concatenate(acts, axis=1), wd_ref[...])
    side()
    return x + 0.5 * _rms(down, g_post)


def _const_spec(shape):
    zeros = (0,) * len(shape)
    return pl.BlockSpec(shape, lambda *_: zeros, pipeline_mode=pl.Buffered(1))


def _row_spec(tile, width):
    return pl.BlockSpec((tile, width), lambda i: (i, 0))


def _ffn_in_kernel(xa_ref, xb_ref, g_pre, g_post, g_mix, wgu_ref, wd_ref, win_ref, bin_ref,
                   x1_ref, kv_ref, z_ref, *, tiles_a):
    x = jnp.where(pl.program_id(0) < tiles_a, xa_ref[...], xb_ref[...])
    x1 = _swiglu_ffn(x, g_pre[...], wgu_ref, wd_ref, g_post[...])
    x1_ref[...] = x1
    h = _rms(x1, g_mix[...]).astype(BF16)

    def project(lo, hi):
        return _dot(h, win_ref[:, lo:hi]) + bin_ref[:, lo:hi]

    qkv = project(Q_COLS[0], KV_COLS[1])
    z_ref[:, Z_Q] = qkv[:, 0:ATTN_WIDTH]
    kv_ref[...] = qkv[:, ATTN_WIDTH:ATTN_WIDTH + 2 * KV_WIDTH]
    z_ref[:, Z_HP.start:Z_HG.stop] = project(HP_COLS[0], HG_COLS[1])


def _ffn_in(xa, xb, g_pre, g_post, g_mix, wgu, wd, win, b_in):
    tile = TOKEN_TILE
    assert xa.shape[0] % tile == 0 and xb.shape[0] % tile == 0
    tiles_a, tiles_b = xa.shape[0] // tile, xb.shape[0] // tile
    n_tiles = tiles_a + tiles_b
    widths = [D_MODEL, 2 * KV_WIDTH, Z_WIDTH]
    return pl.pallas_call(
        functools.partial(_ffn_in_kernel, tiles_a=tiles_a),
        grid=(n_tiles,),
        in_specs=[pl.BlockSpec((tile, D_MODEL), lambda j: (jnp.minimum(j, tiles_a - 1), 0)),
                  pl.BlockSpec((tile, D_MODEL), lambda j: (jnp.maximum(j - tiles_a, 0), 0)),
                  _const_spec(g_pre.shape), _const_spec(g_post.shape),
                  _const_spec(g_mix.shape), _const_spec(wgu.shape), _const_spec(wd.shape),
                  _const_spec(win.shape), _const_spec(b_in.shape)],
        out_specs=[_row_spec(tile, w) for w in widths],
        out_shape=[jax.ShapeDtypeStruct((n_tiles * tile, w), F32) for w in widths],
        compiler_params=pltpu.CompilerParams(dimension_semantics=("arbitrary",),
                                             vmem_limit_bytes=VMEM_LIMIT_BIG),
        name="ffn_in",
    )(xa, xb, g_pre, g_post, g_mix, wgu, wd, win, b_in)


def _mix_ffn(x1, mix, g_mixpost, g_pre, g_post, wgu_ref, wd_ref, side=_no_side_work, chunks=FF_CHUNKS):
    x2 = x1 + _rms(mix, g_mixpost)
    return _swiglu_ffn(x2, g_pre, wgu_ref, wd_ref, g_post, side, chunks)


def _staged(mixers, chunks=FF_CHUNKS):
    waiting, running = list(mixers), []
    n_slots = 2 * len(chunks) + 2
    starts_per_slot = -(-len(waiting) // n_slots)

    def side():
        for _ in range(min(starts_per_slot, len(waiting))):
            running.append(waiting.pop(0))
        for gen in list(running):
            if next(gen, "done") == "done":
                running.remove(gen)

    def drain():
        while waiting or running:
            side()

    return side, drain


def _attn_block(sink_ref, q, kv_ref, q0, gn, store):
    w = WINDOW
    nkeys = 2 * w
    r = lax.broadcasted_iota(jnp.int32, (nkeys, w), 0)
    c = lax.broadcasted_iota(jnp.int32, (nkeys, w), 1)
    first_half = lax.broadcasted_iota(jnp.int32, (w, 2 * HEAD_DIM), 1) < HEAD_DIM
    k0 = jnp.maximum(q0 - w, 0)
    dist = c + (q0 - k0) - r
    bias = jnp.where((dist >= 0) & (dist <= WINDOW), 0.0, NEG)
    kvb = kv_ref[pl.ds(pl.multiple_of(k0, w), nkeys), :]
    kk = kvb[:, 0:KV_WIDTH]
    k_nat = kk.astype(BF16)
    k_swp = pltpu.roll(kk, HEAD_DIM, axis=1).astype(BF16)
    vt = kvb[:, KV_WIDTH:2 * KV_WIDTH].T.astype(BF16)

    def masked_q(h):
        qp = q[:, (h // 2) * 2 * HEAD_DIM:(h // 2 + 1) * 2 * HEAD_DIM]
        return jnp.where(first_half != bool(h % 2), qp, 0.0).astype(BF16)

    groups = []
    for keys, use_swapped in ((k_nat, False), (k_swp, True)):
        heads = [h for h in range(N_HEADS) if (h // GQA_GROUP != h % 2) == use_swapped]
        groups.append((keys, heads, jnp.concatenate([masked_q(h) for h in heads], axis=0)))
    yield
    scores = [None] * N_HEADS
    for keys, heads, qs in groups:
        st = _dot_nt(keys, qs)
        for i, h in enumerate(heads):
            scores[h] = st[:, i * w:(i + 1) * w]
    yield
    probs, dens = [], []
    for h in range(N_HEADS):
        st = scores[h] * (SCALE * LOG2E) + bias
        sink2 = sink_ref[h] * LOG2E
        m = jnp.maximum(jnp.max(st, axis=0, keepdims=True), sink2)
        p = jnp.exp2(st - m)
        dens.append(jnp.sum(p, axis=0, keepdims=True) + jnp.exp2(sink2 - m))
        probs.append(p.astype(BF16))
    yield
    ot = _dot(vt, jnp.concatenate(probs, axis=1))
    yield
    rows = []
    for h in range(N_HEADS):
        hk = h // GQA_GROUP
        rows.append(ot[hk * HEAD_DIM:(hk + 1) * HEAD_DIM, h * w:(h + 1) * w] / dens[h])
    at = jnp.concatenate(rows, axis=0)
    inv = lax.rsqrt(jnp.sum(at * at, axis=0, keepdims=True) * (1.0 / ATTN_WIDTH) + EPS)
    store((at * inv).T * gn)


def _softmax_sink_pv(scores, masks, values, sink):
    scores = [jnp.where(mk, s * SCALE, NEG) for s, mk in zip(scores, masks)]
    m = sink
    for s in scores:
        m = jnp.maximum(m, jnp.max(s, axis=-1, keepdims=True))
    den = jnp.exp(sink - m)
    out = None
    for s, (v, transposed) in zip(scores, values):
        p = jnp.exp(s - m)
        den = den + jnp.sum(p, axis=-1, keepdims=True)
        pv = (_dot_nt if transposed else _dot)(p.astype(BF16), v)
        out = pv if out is None else out + pv
    return out / den


def _attn_sample_group(sink_ref, z_ref, kvn_ref, ck_ref, cv_ref, gn_ref, a_ref, ko_ref, vo_ref, grp, dec_len):
    ns = ATTN_SEQ_GROUP
    rows = ns * dec_len
    wb = ck_ref.shape[2]
    m_rows = GQA_GROUP * rows
    r = lax.broadcasted_iota(jnp.int32, (m_rows, ns * wb), 0)
    c = lax.broadcasted_iota(jnp.int32, (m_rows, ns * wb), 1)
    r_seq, r_pos = (r % rows) // dec_len, r % dec_len
    mask_cache = (r_seq == c // wb) & (c % wb >= r_pos)
    r = lax.broadcasted_iota(jnp.int32, (m_rows, rows), 0)
    c = lax.broadcasted_iota(jnp.int32, (m_rows, rows), 1)
    mask_new = ((r % rows) // dec_len == c // dec_len) & (c % dec_len <= r % dec_len)
    row_head = lax.broadcasted_iota(jnp.int32, (m_rows, 1), 0) // rows

    rs = slice(grp * rows, (grp + 1) * rows)
    q = z_ref[rs, Z_Q]
    kvn_b = kvn_ref[rs, :].astype(BF16)

    def cached(ref, hk):
        frows = slice(hk * HEAD_DIM, (hk + 1) * HEAD_DIM)
        return jnp.concatenate([ref[grp * ns + n, frows, :] for n in range(ns)], axis=1).astype(BF16)

    scores = []
    for hk in range(N_KV_HEADS):
        kcols = slice(hk * HEAD_DIM, (hk + 1) * HEAD_DIM)
        heads = [hk * GQA_GROUP + g for g in range(GQA_GROUP)]
        qs = jnp.concatenate([q[:, h * HEAD_DIM:(h + 1) * HEAD_DIM] for h in heads], axis=0).astype(BF16)
        scores.append([_dot(qs, cached(ck_ref, hk)), _dot_nt(qs, kvn_b[:, kcols])])
    yield
    outs = []
    for hk in range(N_KV_HEADS):
        vcols = slice(KV_WIDTH + hk * HEAD_DIM, KV_WIDTH + (hk + 1) * HEAD_DIM)
        sink = jnp.zeros((m_rows, 1), F32)
        for g in range(GQA_GROUP):
            sink = jnp.where(row_head == g, sink_ref[hk * GQA_GROUP + g], sink)
        values = [(cached(cv_ref, hk), True), (kvn_b[:, vcols], False)]
        outs.append(_softmax_sink_pv(scores[hk], [mask_cache, mask_new], values, sink))
    yield
    for hk in range(N_KV_HEADS):
        for g in range(GQA_GROUP):
            h = hk * GQA_GROUP + g
            a_ref[rs, h * HEAD_DIM:(h + 1) * HEAD_DIM] = outs[hk][g * rows:(g + 1) * rows, :]
    a_ref[rs, :] = _rms(a_ref[rs, :], gn_ref[...])
    keep = wb - dec_len
    is_new = lax.broadcasted_iota(jnp.int32, (KV_WIDTH, wb), 1) >= keep
    for n in range(ns):
        s = grp * ns + n
        ts = slice(grp * rows + n * dec_len, grp * rows + (n + 1) * dec_len)
        for src, dst, cols in ((ck_ref, ko_ref, slice(0, KV_WIDTH)), (cv_ref, vo_ref, slice(KV_WIDTH, 2 * KV_WIDTH))):
            new_t = jnp.concatenate([jnp.zeros((keep, KV_WIDTH), F32), kvn_ref[ts, cols]], axis=0).T
            dst[s] = jnp.where(is_new, new_t, pltpu.roll(src[s], keep, axis=1))


def _hgrn_bound_consts(lbl_ref):
    lbl = lbl_ref[...]
    e = jnp.exp(lbl - jnp.max(lbl, axis=0, keepdims=True))
    lb = e[0:1, :] / jnp.sum(e, axis=0, keepdims=True)
    return 0.5 + 0.5 * lb, 0.5 - 0.5 * lb


def _hgrn_gates(hp, c0, c1):
    qa = _silu_t(hp[:, 0:HG_WIDTH])
    t = c1 * jnp.tanh(0.5 * hp[:, HG_WIDTH:2 * HG_WIDTH])
    f, kx = c0 + t, c1 - t
    v = hp[:, 2 * HG_WIDTH:3 * HG_WIDTH].astype(BF16)
    return qa, kx, _split2(jnp.log2(f)), v


def _hgrn_factors(qa, kx, g, g_mid, g_last):
    qt = qa * jnp.exp2(g - g_mid)
    kt = kx * jnp.exp2(g_mid - g)
    qg = (qt * jnp.exp2(g_mid)).astype(BF16)
    kd = (kt * jnp.exp2(g_last - g_mid)).astype(BF16)
    return qt.astype(BF16), kt.astype(BF16), qg, kd


def _segment_masks(nseq):
    ls = HG_TILE // nseq
    r = lax.broadcasted_iota(jnp.int32, (HG_TILE, HG_TILE), 0)
    c = lax.broadcasted_iota(jnp.int32, (HG_TILE, HG_TILE), 1)
    same = (r // ls) == (c // ls)
    return same, same & (c <= r), same & (c % ls < ls // 2)


def _as_bf16(mask):
    return mask.astype(F32).astype(BF16)


def _hgrn_short_tile(z_ref, c0, c1, gn, s_in, s_out, o_ref, ti, nseq):
    ch = HG_TILE
    ls = ch // nseq
    same, causal, first_half = _segment_masks(nseq)
    cum_lhs = _twice(_as_bf16(jnp.concatenate([causal, first_half, same], axis=0)))
    sr = lax.broadcasted_iota(jnp.int32, (ch, nseq * HG_DV), 0)
    sc = lax.broadcasted_iota(jnp.int32, (ch, nseq * HG_DV), 1)
    seg_sel = _as_bf16((sr // ls) == (sc // HG_DV))
    seg_sel2 = jnp.concatenate([seg_sel, seg_sel], axis=0)
    rows = slice(ti * ch, (ti + 1) * ch)

    qa, kx, parts, v = _hgrn_gates(z_ref[rows, Z_HP], c0, c1)
    gs = _dot(cum_lhs, parts)
    dcol = _dot_tn(parts, seg_sel2)
    yield
    qt, kt, qg, kd = _hgrn_factors(qa, kx, gs[0:ch], gs[ch:2 * ch], gs[2 * ch:3 * ch])
    cols = [slice(h * HG_DK, (h + 1) * HG_DK) for h in range(HG_HEADS)]
    scores = [_dot_nt(qt[:, cs], kt[:, cs]) for cs in cols]
    yield
    outs = []
    for h, cs in enumerate(cols):
        a = jnp.where(causal, scores[h], 0.0)
        o = _dot(a.astype(BF16), v[:, cs])
        inter = []
        for n in range(nseq):
            rs = slice(n * ls, (n + 1) * ls)
            s = s_in[ti * nseq + n, h]
            inter.append(_dot(qg[rs, cs], s.astype(BF16)))
            decay = jnp.exp2(dcol[cs, n * HG_DV:(n + 1) * HG_DV])
            s_out[ti * nseq + n, h] = s * decay + _dot_tn(kd[rs, cs], v[rs, cs])
        outs.append(o + jnp.concatenate(inter, axis=0))
    yield
    gate = _silu_t(z_ref[rows, Z_HG])
    for h, cs in enumerate(cols):
        o_ref[rows, cs] = _rms(outs[h], gn) * gate[:, cs]


def _sample_mixer_ffn_kernel(sink_ref, x1_ref, z_ref, kvn_ref, ck_ref, cv_ref, s_in_ref,
                             lbl_ref, gn_ref, ghg_ref, g_mixpost, g_pre, g_post, wout_ref, wgu_ref, wd_ref,
                             y_ref, ko_ref, vo_ref, s_out_ref, ao_scr, *, dec_len):
    j = pl.program_id(0)
    tile = x1_ref.shape[0]

    @pl.when(j == 0)
    def _():
        ao_scr[...] = jnp.zeros(ao_scr.shape, F32)

    a_scr, o_scr = ao_scr.at[:, 0:ATTN_WIDTH], ao_scr.at[:, ATTN_WIDTH:ATTN_WIDTH + HG_WIDTH]
    c0, c1 = _hgrn_bound_consts(lbl_ref)
    nseq = HG_TILE // dec_len
    mixers = [_attn_sample_group(sink_ref, z_ref, kvn_ref, ck_ref, cv_ref, gn_ref, a_scr, ko_ref, vo_ref, grp, dec_len)
              for grp in range(tile // (ATTN_SEQ_GROUP * dec_len))]
    mixers += [_hgrn_short_tile(z_ref, c0, c1, ghg_ref[...], s_in_ref, s_out_ref, o_scr, ti, nseq)
               for ti in range(tile // HG_TILE)]
    side, drain = _staged(mixers)

    mix = _dot(ao_scr[...].astype(BF16), wout_ref[...])
    side()
    y_ref[...] = _mix_ffn(x1_ref[...], mix, g_mixpost[...], g_pre[...], g_post[...], wgu_ref, wd_ref, side)
    drain()


def _sample_mixer_ffn(x1, kv, z, cache_k, cache_v, state, sinks, g_attn, lb_logits, g_hg,
                      g_mixpost, g_pre, g_post, wout, wgu, wd, dec_len, row0):
    tile = SAMPLE_TILE
    t = cache_k.shape[0] * dec_len
    n_tiles = t // tile
    spt = tile // dec_len
    assert t % tile == 0 and row0 % tile == 0 and tile % HG_TILE == 0 and tile % (ATTN_SEQ_GROUP * dec_len) == 0
    cur = lambda j: jnp.minimum(j, n_tiles - 1)
    prev = lambda j: jnp.maximum(j - 1, 0)
    cur_spec = lambda width: pl.BlockSpec((tile, width), lambda j: (row0 // tile + cur(j), 0))
    cache_spec = pl.BlockSpec((spt,) + cache_k.shape[1:], lambda j: (cur(j), 0, 0))
    state_spec = pl.BlockSpec((spt,) + state.shape[1:], lambda j: (cur(j), 0, 0, 0))
    return pl.pallas_call(
        functools.partial(_sample_mixer_ffn_kernel, dec_len=dec_len),
        grid=(n_tiles + 1,),
        in_specs=[pl.BlockSpec(memory_space=pltpu.SMEM),
                  pl.BlockSpec((tile, D_MODEL), lambda j: (row0 // tile + prev(j), 0)),
                  cur_spec(Z_WIDTH), cur_spec(2 * KV_WIDTH), cache_spec, cache_spec, state_spec,
                  _const_spec(lb_logits.shape), _const_spec(g_attn.shape), _const_spec(g_hg.shape),
                  _const_spec(g_mixpost.shape), _const_spec(g_pre.shape), _const_spec(g_post.shape),
                  _const_spec(wout.shape), _const_spec(wgu.shape), _const_spec(wd.shape)],
        out_specs=[pl.BlockSpec((tile, D_MODEL), lambda j: (prev(j), 0)), cache_spec, cache_spec, state_spec],
        out_shape=[jax.ShapeDtypeStruct((t, D_MODEL), F32), jax.ShapeDtypeStruct(cache_k.shape, F32),
                   jax.ShapeDtypeStruct(cache_v.shape, F32), jax.ShapeDtypeStruct(state.shape, F32)],
        scratch_shapes=[pltpu.VMEM((tile, ATTN_WIDTH + HG_WIDTH), F32)],
        compiler_params=pltpu.CompilerParams(dimension_semantics=("arbitrary",),
                                             vmem_limit_bytes=VMEM_LIMIT_BIG),
        name="sample_mixer_ffn",
    )(sinks, x1, z, kv, cache_k, cache_v, state, lb_logits, g_attn, g_hg,
      g_mixpost, g_pre, g_post, wout, wgu, wd)


def _mixer_ffn_kernel(sink_ref, x1_ref, z_ref, kv_ref, lbl_ref, gn_ref, ghg_ref,
                      g_mixpost, g_pre, g_post, wout_ref, wgu_ref, wd_ref,
                      y_ref, s_out_ref, ao_scr, st_scr, *, n_tiles, tiles_per_seq):
    j = pl.program_id(0)
    tile = x1_ref.shape[0]
    pos = jnp.minimum(j, n_tiles - 1) % tiles_per_seq

    @pl.when(j == 0)
    def _():
        ao_scr[...] = jnp.zeros(ao_scr.shape, ao_scr.dtype)
        st_scr[...] = jnp.zeros(st_scr.shape, F32)

    def attn_block(blk):
        rs = slice(blk * WINDOW, (blk + 1) * WINDOW)

        def store(a):
            ao_scr[rs, 0:ATTN_WIDTH] = a.astype(ao_scr.dtype)

        return _attn_block(sink_ref, z_ref[rs, Z_Q], kv_ref, pos * tile + blk * WINDOW, gn_ref[...], store)

    ch = HG_TILE
    _, causal, _ = _segment_masks(1)
    cum_lhs = _twice(_as_bf16(causal))
    c0, c1 = _hgrn_bound_consts(lbl_ref)
    ghg = ghg_ref[...]

    def hgrn_chunk(ci):
        rs = slice(ci * ch, (ci + 1) * ch)
        qa, kx, parts, v = _hgrn_gates(z_ref[rs, Z_HP], c0, c1)
        yield
        g = _dot(cum_lhs, parts)
        yield
        g_last = g[ch - 1:ch, :]
        qt, kt, qg, kd = _hgrn_factors(qa, kx, g, g[ch // 2 - 1:ch // 2, :], g_last)
        decay = jnp.exp2(g_last)
        cols = [slice(h * HG_DK, (h + 1) * HG_DK) for h in range(HG_HEADS)]
        yield
        scores = [_dot_nt(qt[:, cs], kt[:, cs]) for cs in cols]
        yield
        probs = [jnp.where(causal, sc, 0.0).astype(BF16) for sc in scores]
        yield
        outs = []
        for h, cs in enumerate(cols):
            st = st_scr[h]
            if ci == 0:
                st = jnp.where(pos == 0, 0.0, st)
            outs.append(_dot(probs[h], v[:, cs]) + _dot_nt(qg[:, cs], st.astype(BF16)))
            st_scr[h] = st * decay[:, cs] + _dot_tn(v[:, cs], kd[:, cs])
        yield
        gate = _silu_t(z_ref[rs, Z_HG])
        for h, cs in enumerate(cols):
            o = _rms(outs[h], ghg) * gate[:, cs]
            ao_scr[rs, ATTN_WIDTH + h * HG_DV:ATTN_WIDTH + (h + 1) * HG_DV] = o.astype(ao_scr.dtype)

    n_blk, n_chunk = tile // WINDOW, tile // ch
    mixers = []
    for blk in range(n_blk):
        mixers += [hgrn_chunk(ci) for ci in range(blk * n_chunk // n_blk, (blk + 1) * n_chunk // n_blk)]
        mixers.insert(len(mixers) - 1, attn_block(blk))
    side, drain = _staged(mixers, FF_CHUNKS_FINE)

    mix = _dot(ao_scr[...].astype(BF16), wout_ref[...])
    side()
    y_ref[...] = _mix_ffn(x1_ref[...], mix, g_mixpost[...], g_pre[...], g_post[...], wgu_ref, wd_ref, side,
                          FF_CHUNKS_FINE)
    drain()

    @pl.when((pos == tiles_per_seq - 1) & (j < n_tiles))
    def _():
        b = j // tiles_per_seq
        for h in range(HG_HEADS):
            s_out_ref[pl.ds(b, 1), h] = st_scr[h].T[None]


def _mixer_ffn(x1, kv, z, sinks, g_attn, lb_logits, g_hg, g_mixpost, g_pre, g_post, wout, wgu, wd,
               batch, seq):
    t = batch * seq
    tile = TOKEN_TILE
    assert seq % tile == 0
    n_tiles, tps = t // tile, seq // tile
    cur = lambda j: jnp.minimum(j, n_tiles - 1)
    prev = lambda j: jnp.maximum(j - 1, 0)
    cur_spec = lambda width: pl.BlockSpec((tile, width), lambda j: (cur(j), 0))
    prev_spec = pl.BlockSpec((tile, D_MODEL), lambda j: (prev(j), 0))
    s_shape = (batch, HG_HEADS, HG_DK, HG_DV)
    return pl.pallas_call(
        functools.partial(_mixer_ffn_kernel, n_tiles=n_tiles, tiles_per_seq=tps),
        grid=(n_tiles + 1,),
        in_specs=[pl.BlockSpec(memory_space=pltpu.SMEM),
                  prev_spec, cur_spec(Z_WIDTH),
                  pl.BlockSpec((seq, 2 * KV_WIDTH), lambda j: (cur(j) // tps, 0)),
                  _const_spec(lb_logits.shape), _const_spec(g_attn.shape), _const_spec(g_hg.shape),
                  _const_spec(g_mixpost.shape), _const_spec(g_pre.shape), _const_spec(g_post.shape),
                  _const_spec(wout.shape), _const_spec(wgu.shape), _const_spec(wd.shape)],
        out_specs=[prev_spec, pl.BlockSpec(s_shape, lambda j: (0, 0, 0, 0))],
        out_shape=[jax.ShapeDtypeStruct((t, D_MODEL), F32), jax.ShapeDtypeStruct(s_shape, F32)],
        scratch_shapes=[pltpu.VMEM((tile, ATTN_WIDTH + HG_WIDTH), BF16), pltpu.VMEM((HG_HEADS, HG_DV, HG_DK), F32)],
        compiler_params=pltpu.CompilerParams(dimension_semantics=("arbitrary",),
                                             vmem_limit_bytes=VMEM_LIMIT_BIG),
        name="mixer_ffn",
    )(sinks, x1, z, kv, lb_logits, g_attn, g_hg, g_mixpost, g_pre, g_post, wout, wgu, wd)


def kernel(x_prompt, x_sample, cache_k_win, cache_v_win, state_hgrn, w_in, b_in, attn_sinks, attn_out_norm,
           hg_lb_logits, hg_out_norm, w_out, ffn1_w_gu, ffn1_w_down, ffn2_w_gu, ffn2_w_down,
           norm_ffn1_pre, norm_ffn1_post, norm_mix_pre, norm_mix_post, norm_ffn2_pre, norm_ffn2_post):
    depth = w_in.shape[0]
    assert depth == 1, "single-layer trunk"
    batch, seq, _ = x_prompt.shape
    dec_batch, dec_len, _ = x_sample.shape
    wb = cache_k_win.shape[2]
    assert seq % WINDOW == 0 and wb == WINDOW and HG_TILE % dec_len == 0
    layer = 0
    row = lambda p: p[layer].reshape(1, -1).astype(F32)
    wgu1, wd1 = ffn1_w_gu[layer].astype(BF16), ffn1_w_down[layer].astype(BF16)
    wgu2, wd2 = ffn2_w_gu[layer].astype(BF16), ffn2_w_down[layer].astype(BF16)
    win, wout = w_in[layer].astype(BF16), w_out[layer].astype(BF16)
    sinks = attn_sinks[layer].astype(F32)
    lb_logits = hg_lb_logits.astype(F32)
    g_attn, g_hg = row(attn_out_norm), row(hg_out_norm)
    back_params = (row(norm_mix_post), row(norm_ffn2_pre), row(norm_ffn2_post), wout, wgu2, wd2)

    t_p, t_s = batch * seq, dec_batch * dec_len
    x1, kv, z = _ffn_in(x_prompt.reshape(t_p, D_MODEL), x_sample.reshape(t_s, D_MODEL),
                                row(norm_ffn1_pre), row(norm_ffn1_post), row(norm_mix_pre), wgu1, wd1, win, row(b_in))

    y_p, s_prompt = _mixer_ffn(x1, kv, z, sinks, g_attn, lb_logits, g_hg, *back_params, batch, seq)
    y_prompt = y_p.reshape(batch, seq, D_MODEL)
    kv_last = jnp.stack([kv[(b + 1) * seq - WINDOW:(b + 1) * seq] for b in range(batch)])
    k_prompt = kv_last[..., :KV_WIDTH].reshape(1, batch, WINDOW, N_KV_HEADS, HEAD_DIM)
    v_prompt = kv_last[..., KV_WIDTH:].reshape(1, batch, WINDOW, N_KV_HEADS, HEAD_DIM)

    def feature_major(buf):
        return jnp.transpose(buf, (0, 2, 3, 1)).reshape(dec_batch, KV_WIDTH, wb)

    def window_major(buf_t):
        return jnp.transpose(buf_t.reshape(dec_batch, N_KV_HEADS, HEAD_DIM, wb), (0, 3, 1, 2))[None]

    y_s, k_s, v_s, s_sample = _sample_mixer_ffn(
        x1, kv, z, feature_major(cache_k_win[layer]), feature_major(cache_v_win[layer]),
        state_hgrn[layer].astype(F32), sinks, g_attn, lb_logits, g_hg, *back_params, dec_len, t_p)
    y_sample = y_s.reshape(dec_batch, dec_len, D_MODEL)
    k_sample, v_sample = window_major(k_s), window_major(v_s)

    return (y_prompt, y_sample, k_prompt, v_prompt, s_prompt[None], k_sample, v_sample, s_sample[None])
```

```python
import functools

import jax
import jax.numpy as jnp
from jax import lax
from jax.experimental import pallas as pl
from jax.experimental.pallas import tpu as pltpu

F32 = jnp.float32
BF16 = jnp.bfloat16

D_MODEL = 1024
N_HEADS = 8
N_KV_HEADS = 2
HEAD_DIM = 64
GQA_GROUP = N_HEADS // N_KV_HEADS
WINDOW = 128
ATTN_WIDTH = N_HEADS * HEAD_DIM
KV_WIDTH = N_KV_HEADS * HEAD_DIM
SCALE = HEAD_DIM ** -0.5
HG_HEADS = 4
HG_DK = 128
HG_DV = 128
HG_WIDTH = HG_HEADS * HG_DV
D_FF = 2816
EPS = 1e-6

Q_COLS = (0, ATTN_WIDTH)
KV_COLS = (ATTN_WIDTH, ATTN_WIDTH + 2 * KV_WIDTH)
HP_COLS = (KV_COLS[1], KV_COLS[1] + 3 * HG_WIDTH)
HG_COLS = (HP_COLS[1], HP_COLS[1] + HG_WIDTH)
Z_Q = slice(0, ATTN_WIDTH)
Z_HP = slice(ATTN_WIDTH, ATTN_WIDTH + 3 * HG_WIDTH)
Z_HG = slice(ATTN_WIDTH + 3 * HG_WIDTH, ATTN_WIDTH + 4 * HG_WIDTH)
Z_WIDTH = ATTN_WIDTH + 4 * HG_WIDTH

VMEM_LIMIT_BIG = 58 * 1024 * 1024
FF_CHUNKS = tuple((lo, lo + 256) for lo in range(0, D_FF, 256))
TOKEN_TILE = 512
SAMPLE_TILE = 128
HG_TILE = 64
ATTN_SEQ_GROUP = 4
MIXER_STAGES = 7
NEG = -1e30
LOG2E = 1.4426950408889634


def _rms(x, g):
    return x * lax.rsqrt(jnp.mean(x * x, axis=-1, keepdims=True) + EPS) * g


def _silu(x):
    return x * jax.nn.sigmoid(x)


def _silu_t(x):
    h = 0.5 * x
    return h * jnp.tanh(h) + h


def _dot(a, b):
    return jnp.dot(a, b, preferred_element_type=F32)


def _dot_nt(a, b):
    return lax.dot_general(a, b, (((1,), (1,)), ((), ())), preferred_element_type=F32)


def _dot_tn(a, b):
    return lax.dot_general(a, b, (((0,), (0,)), ((), ())), preferred_element_type=F32)


def _split2(x):
    a = x.astype(BF16)
    b = (x - a.astype(F32)).astype(BF16)
    return jnp.concatenate([a, b], axis=0)


def _twice(sel):
    return jnp.concatenate([sel, sel], axis=1)


def _no_side_work():
    pass


def _swiglu_ffn(x, g_pre, wgu_ref, wd_ref, g_post, side=_no_side_work):
    h = _rms(x, g_pre).astype(BF16)
    acts = []
    for lo, hi in FF_CHUNKS:
        g = _dot(h, wgu_ref[:, lo:hi])
        side()
        u = _dot(h, wgu_ref[:, D_FF + lo:D_FF + hi])
        side()
        acts.append((_silu(g) * u).astype(BF16))
    down = _dot(jnp.concatenate(acts, axis=1), wd_ref[...])
    side()
    return x + 0.5 * _rms(down, g_post)


def _const_spec(shape):
    zeros = (0,) * len(shape)
    return pl.BlockSpec(shape, lambda *_: zeros, pipeline_mode=pl.Buffered(1))


def _row_spec(tile, width):
    return pl.BlockSpec((tile, width), lambda i: (i, 0))


def _ffn_in_kernel(xa_ref, xb_ref, g_pre, g_post, g_mix, wgu_ref, wd_ref, win_ref, bin_ref,
                   x1_ref, kv_ref, z_ref, *, tiles_a):
    x = jnp.where(pl.program_id(0) < tiles_a, xa_ref[...], xb_ref[...])
    x1 = _swiglu_ffn(x, g_pre[...], wgu_ref, wd_ref, g_post[...])
    x1_ref[...] = x1
    h = _rms(x1, g_mix[...]).astype(BF16)

    def project(lo, hi):
        return _dot(h, win_ref[:, lo:hi]) + bin_ref[:, lo:hi]

    qkv = project(Q_COLS[0], KV_COLS[1])
    z_ref[:, Z_Q] = qkv[:, 0:ATTN_WIDTH]
    kv_ref[...] = qkv[:, ATTN_WIDTH:ATTN_WIDTH + 2 * KV_WIDTH]
    z_ref[:, Z_HP.start:Z_HG.stop] = project(HP_COLS[0], HG_COLS[1])


def _ffn_in(xa, xb, g_pre, g_post, g_mix, wgu, wd, win, b_in):
    tile = TOKEN_TILE
    assert xa.shape[0] % tile == 0 and xb.shape[0] % tile == 0
    tiles_a, tiles_b = xa.shape[0] // tile, xb.shape[0] // tile
    n_tiles = tiles_a + tiles_b
    widths = [D_MODEL, 2 * KV_WIDTH, Z_WIDTH]
    return pl.pallas_call(
        functools.partial(_ffn_in_kernel, tiles_a=tiles_a),
        grid=(n_tiles,),
        in_specs=[pl.BlockSpec((tile, D_MODEL), lambda j: (jnp.minimum(j, tiles_a - 1), 0)),
                  pl.BlockSpec((tile, D_MODEL), lambda j: (jnp.maximum(j - tiles_a, 0), 0)),
                  _const_spec(g_pre.shape), _const_spec(g_post.shape),
                  _const_spec(g_mix.shape), _const_spec(wgu.shape), _const_spec(wd.shape),
                  _const_spec(win.shape), _const_spec(b_in.shape)],
        out_specs=[_row_spec(tile, w) for w in widths],
        out_shape=[jax.ShapeDtypeStruct((n_tiles * tile, w), F32) for w in widths],
        compiler_params=pltpu.CompilerParams(dimension_semantics=("arbitrary",),
                                             vmem_limit_bytes=VMEM_LIMIT_BIG),
        name="ffn_in",
    )(xa, xb, g_pre, g_post, g_mix, wgu, wd, win, b_in)


def _mix_ffn(x1, mix, g_mixpost, g_pre, g_post, wgu_ref, wd_ref, side=_no_side_work):
    x2 = x1 + _rms(mix, g_mixpost)
    return _swiglu_ffn(x2, g_pre, wgu_ref, wd_ref, g_post, side)


def _staged(mixers):
    waiting, running = list(mixers), []
    n_slots = 2 * len(FF_CHUNKS) + 2
    start_slot = [k * (n_slots - MIXER_STAGES) // len(waiting) for k in range(len(waiting))]
    slot = [0]

    def side():
        while waiting and start_slot[len(start_slot) - len(waiting)] <= slot[0]:
            running.append(waiting.pop(0))
        slot[0] += 1
        for gen in list(running):
            if next(gen, "done") == "done":
                running.remove(gen)

    def drain():
        while waiting or running:
            side()

    return side, drain


def _attn_block(sink_ref, q, kv_ref, q0, gn, store):
    w = WINDOW
    nkeys = 2 * w
    r = lax.broadcasted_iota(jnp.int32, (nkeys, w), 0)
    c = lax.broadcasted_iota(jnp.int32, (nkeys, w), 1)
    first_half = lax.broadcasted_iota(jnp.int32, (w, 2 * HEAD_DIM), 1) < HEAD_DIM
    k0 = jnp.maximum(q0 - w, 0)
    dist = c + (q0 - k0) - r
    bias = jnp.where((dist >= 0) & (dist <= WINDOW), 0.0, NEG)
    kvb = kv_ref[pl.ds(pl.multiple_of(k0, w), nkeys), :]
    kk = kvb[:, 0:KV_WIDTH]
    k_nat = kk.astype(BF16)
    k_swp = pltpu.roll(kk, HEAD_DIM, axis=1).astype(BF16)
    vt = kvb[:, KV_WIDTH:2 * KV_WIDTH].T.astype(BF16)

    def masked_q(h):
        qp = q[:, (h // 2) * 2 * HEAD_DIM:(h // 2 + 1) * 2 * HEAD_DIM]
        return jnp.where(first_half != bool(h % 2), qp, 0.0).astype(BF16)

    groups = []
    for keys, use_swapped in ((k_nat, False), (k_swp, True)):
        heads = [h for h in range(N_HEADS) if (h // GQA_GROUP != h % 2) == use_swapped]
        groups.append((keys, heads, jnp.concatenate([masked_q(h) for h in heads], axis=0)))
    yield
    scores = [None] * N_HEADS
    for keys, heads, qs in groups:
        st = _dot_nt(keys, qs)
        for i, h in enumerate(heads):
            scores[h] = st[:, i * w:(i + 1) * w]
    yield
    probs, dens = [], []
    for h in range(N_HEADS):
        st = scores[h] * (SCALE * LOG2E) + bias
        sink2 = sink_ref[h] * LOG2E
        m = jnp.maximum(jnp.max(st, axis=0, keepdims=True), sink2)
        p = jnp.exp2(st - m)
        dens.append(jnp.sum(p, axis=0, keepdims=True) + jnp.exp2(sink2 - m))
        probs.append(p.astype(BF16))
    yield
    ot = _dot(vt, jnp.concatenate(probs, axis=1))
    yield
    rows = []
    for h in range(N_HEADS):
        hk = h // GQA_GROUP
        rows.append(ot[hk * HEAD_DIM:(hk + 1) * HEAD_DIM, h * w:(h + 1) * w] / dens[h])
    at = jnp.concatenate(rows, axis=0)
    inv = lax.rsqrt(jnp.sum(at * at, axis=0, keepdims=True) * (1.0 / ATTN_WIDTH) + EPS)
    store((at * inv).T * gn)


def _softmax_sink_pv(scores, masks, values, sink):
    scores = [jnp.where(mk, s * SCALE, NEG) for s, mk in zip(scores, masks)]
    m = sink
    for s in scores:
        m = jnp.maximum(m, jnp.max(s, axis=-1, keepdims=True))
    den = jnp.exp(sink - m)
    out = None
    for s, (v, transposed) in zip(scores, values):
        p = jnp.exp(s - m)
        den = den + jnp.sum(p, axis=-1, keepdims=True)
        pv = (_dot_nt if transposed else _dot)(p.astype(BF16), v)
        out = pv if out is None else out + pv
    return out / den


def _attn_sample_group(sink_ref, z_ref, kvn_ref, ck_ref, cv_ref, gn_ref, a_ref, ko_ref, vo_ref, grp, dec_len):
    ns = ATTN_SEQ_GROUP
    rows = ns * dec_len
    wb = ck_ref.shape[2]
    m_rows = GQA_GROUP * rows
    r = lax.broadcasted_iota(jnp.int32, (m_rows, ns * wb), 0)
    c = lax.broadcasted_iota(jnp.int32, (m_rows, ns * wb), 1)
    r_seq, r_pos = (r % rows) // dec_len, r % dec_len
    mask_cache = (r_seq == c // wb) & (c % wb >= r_pos)
    r = lax.broadcasted_iota(jnp.int32, (m_rows, rows), 0)
    c = lax.broadcasted_iota(jnp.int32, (m_rows, rows), 1)
    mask_new = ((r % rows) // dec_len == c // dec_len) & (c % dec_len <= r % dec_len)
    row_head = lax.broadcasted_iota(jnp.int32, (m_rows, 1), 0) // rows

    rs = slice(grp * rows, (grp + 1) * rows)
    q = z_ref[rs, Z_Q]
    kvn_b = kvn_ref[rs, :].astype(BF16)

    def cached(ref, hk):
        frows = slice(hk * HEAD_DIM, (hk + 1) * HEAD_DIM)
        return jnp.concatenate([ref[grp * ns + n, frows, :] for n in range(ns)], axis=1).astype(BF16)

    scores = []
    for hk in range(N_KV_HEADS):
        kcols = slice(hk * HEAD_DIM, (hk + 1) * HEAD_DIM)
        heads = [hk * GQA_GROUP + g for g in range(GQA_GROUP)]
        qs = jnp.concatenate([q[:, h * HEAD_DIM:(h + 1) * HEAD_DIM] for h in heads], axis=0).astype(BF16)
        scores.append([_dot(qs, cached(ck_ref, hk)), _dot_nt(qs, kvn_b[:, kcols])])
    yield
    outs = []
    for hk in range(N_KV_HEADS):
        vcols = slice(KV_WIDTH + hk * HEAD_DIM, KV_WIDTH + (hk + 1) * HEAD_DIM)
        sink = jnp.zeros((m_rows, 1), F32)
        for g in range(GQA_GROUP):
            sink = jnp.where(row_head == g, sink_ref[hk * GQA_GROUP + g], sink)
        values = [(cached(cv_ref, hk), True), (kvn_b[:, vcols], False)]
        outs.append(_softmax_sink_pv(scores[hk], [mask_cache, mask_new], values, sink))
    yield
    for hk in range(N_KV_HEADS):
        for g in range(GQA_GROUP):
            h = hk * GQA_GROUP + g
            a_ref[rs, h * HEAD_DIM:(h + 1) * HEAD_DIM] = outs[hk][g * rows:(g + 1) * rows, :]
    a_ref[rs, :] = _rms(a_ref[rs, :], gn_ref[...])
    keep = wb - dec_len
    is_new = lax.broadcasted_iota(jnp.int32, (KV_WIDTH, wb), 1) >= keep
    for n in range(ns):
        s = grp * ns + n
        ts = slice(grp * rows + n * dec_len, grp * rows + (n + 1) * dec_len)
        for src, dst, cols in ((ck_ref, ko_ref, slice(0, KV_WIDTH)), (cv_ref, vo_ref, slice(KV_WIDTH, 2 * KV_WIDTH))):
            new_t = jnp.concatenate([jnp.zeros((keep, KV_WIDTH), F32), kvn_ref[ts, cols]], axis=0).T
            dst[s] = jnp.where(is_new, new_t, pltpu.roll(src[s], keep, axis=1))


def _hgrn_bound_consts(lbl_ref):
    lbl = lbl_ref[...]
    e = jnp.exp(lbl - jnp.max(lbl, axis=0, keepdims=True))
    lb = e[0:1, :] / jnp.sum(e, axis=0, keepdims=True)
    return 0.5 + 0.5 * lb, 0.5 - 0.5 * lb


def _hgrn_gates(hp, c0, c1):
    qa = _silu_t(hp[:, 0:HG_WIDTH])
    t = c1 * jnp.tanh(0.5 * hp[:, HG_WIDTH:2 * HG_WIDTH])
    f, kx = c0 + t, c1 - t
    v = hp[:, 2 * HG_WIDTH:3 * HG_WIDTH].astype(BF16)
    return qa, kx, _split2(jnp.log2(f)), v


def _hgrn_factors(qa, kx, g, g_mid, g_last):
    qt = qa * jnp.exp2(g - g_mid)
    kt = kx * jnp.exp2(g_mid - g)
    qg = (qt * jnp.exp2(g_mid)).astype(BF16)
    kd = (kt * jnp.exp2(g_last - g_mid)).astype(BF16)
    return qt.astype(BF16), kt.astype(BF16), qg, kd


def _segment_masks(nseq):
    ls = HG_TILE // nseq
    r = lax.broadcasted_iota(jnp.int32, (HG_TILE, HG_TILE), 0)
    c = lax.broadcasted_iota(jnp.int32, (HG_TILE, HG_TILE), 1)
    same = (r // ls) == (c // ls)
    return same, same & (c <= r), same & (c % ls < ls // 2)


def _as_bf16(mask):
    return mask.astype(F32).astype(BF16)


def _hgrn_short_tile(z_ref, c0, c1, gn, s_in, s_out, o_ref, ti, nseq):
    ch = HG_TILE
    ls = ch // nseq
    same, causal, first_half = _segment_masks(nseq)
    cum_lhs = _twice(_as_bf16(jnp.concatenate([causal, first_half, same], axis=0)))
    sr = lax.broadcasted_iota(jnp.int32, (ch, nseq * HG_DV), 0)
    sc = lax.broadcasted_iota(jnp.int32, (ch, nseq * HG_DV), 1)
    seg_sel = _as_bf16((sr // ls) == (sc // HG_DV))
    seg_sel2 = jnp.concatenate([seg_sel, seg_sel], axis=0)
    rows = slice(ti * ch, (ti + 1) * ch)

    qa, kx, parts, v = _hgrn_gates(z_ref[rows, Z_HP], c0, c1)
    gs = _dot(cum_lhs, parts)
    dcol = _dot_tn(parts, seg_sel2)
    yield
    qt, kt, qg, kd = _hgrn_factors(qa, kx, gs[0:ch], gs[ch:2 * ch], gs[2 * ch:3 * ch])
    cols = [slice(h * HG_DK, (h + 1) * HG_DK) for h in range(HG_HEADS)]
    scores = [_dot_nt(qt[:, cs], kt[:, cs]) for cs in cols]
    yield
    outs = []
    for h, cs in enumerate(cols):
        a = jnp.where(causal, scores[h], 0.0)
        o = _dot(a.astype(BF16), v[:, cs])
        inter = []
        for n in range(nseq):
            rs = slice(n * ls, (n + 1) * ls)
            s = s_in[ti * nseq + n, h]
            inter.append(_dot(qg[rs, cs], s.astype(BF16)))
            decay = jnp.exp2(dcol[cs, n * HG_DV:(n + 1) * HG_DV])
            s_out[ti * nseq + n, h] = s * decay + _dot_tn(kd[rs, cs], v[rs, cs])
        outs.append(o + jnp.concatenate(inter, axis=0))
    yield
    gate = _silu_t(z_ref[rows, Z_HG])
    for h, cs in enumerate(cols):
        o_ref[rows, cs] = _rms(outs[h], gn) * gate[:, cs]


def _sample_mixer_ffn_kernel(sink_ref, x1_ref, z_ref, kvn_ref, ck_ref, cv_ref, s_in_ref,
                             lbl_ref, gn_ref, ghg_ref, g_mixpost, g_pre, g_post, wout_ref, wgu_ref, wd_ref,
                             y_ref, ko_ref, vo_ref, s_out_ref, ao_scr, *, dec_len):
    j = pl.program_id(0)
    tile = x1_ref.shape[0]

    @pl.when(j == 0)
    def _():
        ao_scr[...] = jnp.zeros(ao_scr.shape, F32)

    a_scr, o_scr = ao_scr.at[:, 0:ATTN_WIDTH], ao_scr.at[:, ATTN_WIDTH:ATTN_WIDTH + HG_WIDTH]
    c0, c1 = _hgrn_bound_consts(lbl_ref)
    nseq = HG_TILE // dec_len
    mixers = [_attn_sample_group(sink_ref, z_ref, kvn_ref, ck_ref, cv_ref, gn_ref, a_scr, ko_ref, vo_ref, grp, dec_len)
              for grp in range(tile // (ATTN_SEQ_GROUP * dec_len))]
    mixers += [_hgrn_short_tile(z_ref, c0, c1, ghg_ref[...], s_in_ref, s_out_ref, o_scr, ti, nseq)
               for ti in range(tile // HG_TILE)]
    side, drain = _staged(mixers)

    mix = _dot(ao_scr[...].astype(BF16), wout_ref[...])
    side()
    y_ref[...] = _mix_ffn(x1_ref[...], mix, g_mixpost[...], g_pre[...], g_post[...], wgu_ref, wd_ref, side)
    drain()


def _sample_mixer_ffn(x1, kv, z, cache_k, cache_v, state, sinks, g_attn, lb_logits, g_hg,
                      g_mixpost, g_pre, g_post, wout, wgu, wd, dec_len, row0):
    tile = SAMPLE_TILE
    t = cache_k.shape[0] * dec_len
    n_tiles = t // tile
    spt = tile // dec_len
    assert t % tile == 0 and row0 % tile == 0 and tile % HG_TILE == 0 and tile % (ATTN_SEQ_GROUP * dec_len) == 0
    cur = lambda j: jnp.minimum(j, n_tiles - 1)
    prev = lambda j: jnp.maximum(j - 1, 0)
    cur_spec = lambda width: pl.BlockSpec((tile, width), lambda j: (row0 // tile + cur(j), 0))
    cache_spec = pl.BlockSpec((spt,) + cache_k.shape[1:], lambda j: (cur(j), 0, 0))
    state_spec = pl.BlockSpec((spt,) + state.shape[1:], lambda j: (cur(j), 0, 0, 0))
    return pl.pallas_call(
        functools.partial(_sample_mixer_ffn_kernel, dec_len=dec_len),
        grid=(n_tiles + 1,),
        in_specs=[pl.BlockSpec(memory_space=pltpu.SMEM),
                  pl.BlockSpec((tile, D_MODEL), lambda j: (row0 // tile + prev(j), 0)),
                  cur_spec(Z_WIDTH), cur_spec(2 * KV_WIDTH), cache_spec, cache_spec, state_spec,
                  _const_spec(lb_logits.shape), _const_spec(g_attn.shape), _const_spec(g_hg.shape),
                  _const_spec(g_mixpost.shape), _const_spec(g_pre.shape), _const_spec(g_post.shape),
                  _const_spec(wout.shape), _const_spec(wgu.shape), _const_spec(wd.shape)],
        out_specs=[pl.BlockSpec((tile, D_MODEL), lambda j: (prev(j), 0)), cache_spec, cache_spec, state_spec],
        out_shape=[jax.ShapeDtypeStruct((t, D_MODEL), F32), jax.ShapeDtypeStruct(cache_k.shape, F32),
                   jax.ShapeDtypeStruct(cache_v.shape, F32), jax.ShapeDtypeStruct(state.shape, F32)],
        scratch_shapes=[pltpu.VMEM((tile, ATTN_WIDTH + HG_WIDTH), F32)],
        compiler_params=pltpu.CompilerParams(dimension_semantics=("arbitrary",),
                                             vmem_limit_bytes=VMEM_LIMIT_BIG),
        name="sample_mixer_ffn",
    )(sinks, x1, z, kv, cache_k, cache_v, state, lb_logits, g_attn, g_hg,
      g_mixpost, g_pre, g_post, wout, wgu, wd)


def _mixer_ffn_kernel(sink_ref, x1_ref, z_ref, kv_ref, lbl_ref, gn_ref, ghg_ref,
                      g_mixpost, g_pre, g_post, wout_ref, wgu_ref, wd_ref,
                      y_ref, s_out_ref, ao_scr, st_scr, *, n_tiles, tiles_per_seq):
    j = pl.program_id(0)
    tile = x1_ref.shape[0]
    pos = jnp.minimum(j, n_tiles - 1) % tiles_per_seq

    @pl.when(j == 0)
    def _():
        ao_scr[...] = jnp.zeros(ao_scr.shape, ao_scr.dtype)
        st_scr[...] = jnp.zeros(st_scr.shape, F32)

    def attn_block(blk):
        rs = slice(blk * WINDOW, (blk + 1) * WINDOW)

        def store(a):
            ao_scr[rs, 0:ATTN_WIDTH] = a.astype(ao_scr.dtype)

        return _attn_block(sink_ref, z_ref[rs, Z_Q], kv_ref, pos * tile + blk * WINDOW, gn_ref[...], store)

    ch = HG_TILE
    _, causal, _ = _segment_masks(1)
    cum_lhs = _twice(_as_bf16(causal))
    c0, c1 = _hgrn_bound_consts(lbl_ref)
    ghg = ghg_ref[...]

    def hgrn_chunk(ci):
        rs = slice(ci * ch, (ci + 1) * ch)
        qa, kx, parts, v = _hgrn_gates(z_ref[rs, Z_HP], c0, c1)
        yield
        g = _dot(cum_lhs, parts)
        yield
        g_last = g[ch - 1:ch, :]
        qt, kt, qg, kd = _hgrn_factors(qa, kx, g, g[ch // 2 - 1:ch // 2, :], g_last)
        decay = jnp.exp2(g_last)
        cols = [slice(h * HG_DK, (h + 1) * HG_DK) for h in range(HG_HEADS)]
        yield
        scores = [_dot_nt(qt[:, cs], kt[:, cs]) for cs in cols]
        yield
        probs = [jnp.where(causal, sc, 0.0).astype(BF16) for sc in scores]
        yield
        outs = []
        for h, cs in enumerate(cols):
            st = st_scr[h]
            if ci == 0:
                st = jnp.where(pos == 0, 0.0, st)
            outs.append(_dot(probs[h], v[:, cs]) + _dot_nt(qg[:, cs], st.astype(BF16)))
            st_scr[h] = st * decay[:, cs] + _dot_tn(v[:, cs], kd[:, cs])
        yield
        gate = _silu_t(z_ref[rs, Z_HG])
        for h, cs in enumerate(cols):
            o = _rms(outs[h], ghg) * gate[:, cs]
            ao_scr[rs, ATTN_WIDTH + h * HG_DV:ATTN_WIDTH + (h + 1) * HG_DV] = o.astype(ao_scr.dtype)

    n_blk, n_chunk = tile // WINDOW, tile // ch
    mixers = []
    for blk in range(n_blk):
        mixers += [hgrn_chunk(ci) for ci in range(blk * n_chunk // n_blk, (blk + 1) * n_chunk // n_blk)]
        mixers.insert(len(mixers) - 1, attn_block(blk))
    side, drain = _staged(mixers)

    mix = _dot(ao_scr[...].astype(BF16), wout_ref[...])
    side()
    y_ref[...] = _mix_ffn(x1_ref[...], mix, g_mixpost[...], g_pre[...], g_post[...], wgu_ref, wd_ref, side)
    drain()

    @pl.when((pos == tiles_per_seq - 1) & (j < n_tiles))
    def _():
        b = j // tiles_per_seq
        for h in range(HG_HEADS):
            s_out_ref[pl.ds(b, 1), h] = st_scr[h].T[None]


def _mixer_ffn(x1, kv, z, sinks, g_attn, lb_logits, g_hg, g_mixpost, g_pre, g_post, wout, wgu, wd,
               batch, seq):
    t = batch * seq
    tile = TOKEN_TILE
    assert seq % tile == 0
    n_tiles, tps = t // tile, seq // tile
    cur = lambda j: jnp.minimum(j, n_tiles - 1)
    prev = lambda j: jnp.maximum(j - 1, 0)
    cur_spec = lambda width: pl.BlockSpec((tile, width), lambda j: (cur(j), 0))
    prev_spec = pl.BlockSpec((tile, D_MODEL), lambda j: (prev(j), 0))
    s_shape = (batch, HG_HEADS, HG_DK, HG_DV)
    return pl.pallas_call(
        functools.partial(_mixer_ffn_kernel, n_tiles=n_tiles, tiles_per_seq=tps),
        grid=(n_tiles + 1,),
        in_specs=[pl.BlockSpec(memory_space=pltpu.SMEM),
                  prev_spec, cur_spec(Z_WIDTH),
                  pl.BlockSpec((seq, 2 * KV_WIDTH), lambda j: (cur(j) // tps, 0)),
                  _const_spec(lb_logits.shape), _const_spec(g_attn.shape), _const_spec(g_hg.shape),
                  _const_spec(g_mixpost.shape), _const_spec(g_pre.shape), _const_spec(g_post.shape),
                  _const_spec(wout.shape), _const_spec(wgu.shape), _const_spec(wd.shape)],
        out_specs=[prev_spec, pl.BlockSpec(s_shape, lambda j: (0, 0, 0, 0))],
        out_shape=[jax.ShapeDtypeStruct((t, D_MODEL), F32), jax.ShapeDtypeStruct(s_shape, F32)],
        scratch_shapes=[pltpu.VMEM((tile, ATTN_WIDTH + HG_WIDTH), BF16), pltpu.VMEM((HG_HEADS, HG_DV, HG_DK), F32)],
        compiler_params=pltpu.CompilerParams(dimension_semantics=("arbitrary",),
                                             vmem_limit_bytes=VMEM_LIMIT_BIG),
        name="mixer_ffn",
    )(sinks, x1, z, kv, lb_logits, g_attn, g_hg, g_mixpost, g_pre, g_post, wout, wgu, wd)


def kernel(x_prompt, x_sample, cache_k_win, cache_v_win, state_hgrn, w_in, b_in, attn_sinks, attn_out_norm,
           hg_lb_logits, hg_out_norm, w_out, ffn1_w_gu, ffn1_w_down, ffn2_w_gu, ffn2_w_down,
           norm_ffn1_pre, norm_ffn1_post, norm_mix_pre, norm_mix_post, norm_ffn2_pre, norm_ffn2_post):
    depth = w_in.shape[0]
    assert depth == 1, "single-layer trunk"
    batch, seq, _ = x_prompt.shape
    dec_batch, dec_len, _ = x_sample.shape
    wb = cache_k_win.shape[2]
    assert seq % WINDOW == 0 and wb == WINDOW and HG_TILE % dec_len == 0
    layer = 0
    row = lambda p: p[layer].reshape(1, -1).astype(F32)
    wgu1, wd1 = ffn1_w_gu[layer].astype(BF16), ffn1_w_down[layer].astype(BF16)
    wgu2, wd2 = ffn2_w_gu[layer].astype(BF16), ffn2_w_down[layer].astype(BF16)
    win, wout = w_in[layer].astype(BF16), w_out[layer].astype(BF16)
    sinks = attn_sinks[layer].astype(F32)
    lb_logits = hg_lb_logits.astype(F32)
    g_attn, g_hg = row(attn_out_norm), row(hg_out_norm)
    back_params = (row(norm_mix_post), row(norm_ffn2_pre), row(norm_ffn2_post), wout, wgu2, wd2)

    t_p, t_s = batch * seq, dec_batch * dec_len
    x1, kv, z = _ffn_in(x_prompt.reshape(t_p, D_MODEL), x_sample.reshape(t_s, D_MODEL),
                                row(norm_ffn1_pre), row(norm_ffn1_post), row(norm_mix_pre), wgu1, wd1, win, row(b_in))

    y_p, s_prompt = _mixer_ffn(x1, kv, z, sinks, g_attn, lb_logits, g_hg, *back_params, batch, seq)
    y_prompt = y_p.reshape(batch, seq, D_MODEL)
    kv_last = jnp.stack([kv[(b + 1) * seq - WINDOW:(b + 1) * seq] for b in range(batch)])
    k_prompt = kv_last[..., :KV_WIDTH].reshape(1, batch, WINDOW, N_KV_HEADS, HEAD_DIM)
    v_prompt = kv_last[..., KV_WIDTH:].reshape(1, batch, WINDOW, N_KV_HEADS, HEAD_DIM)

    def feature_major(buf):
        return jnp.transpose(buf, (0, 2, 3, 1)).reshape(dec_batch, KV_WIDTH, wb)

    def window_major(buf_t):
        return jnp.transpose(buf_t.reshape(dec_batch, N_KV_HEADS, HEAD_DIM, wb), (0, 3, 1, 2))[None]

    y_s, k_s, v_s, s_sample = _sample_mixer_ffn(
        x1, kv, z, feature_major(cache_k_win[layer]), feature_major(cache_v_win[layer]),
        state_hgrn[layer].astype(F32), sinks, g_attn, lb_logits, g_hg, *back_params, dec_len, t_p)
    y_sample = y_s.reshape(dec_batch, dec_len, D_MODEL)
    k_sample, v_sample = window_major(k_s), window_major(v_s)

    return (y_prompt, y_sample, k_prompt, v_prompt, s_prompt[None], k_sample, v_sample, s_sample[None])
```

```python
import functools

import jax
import jax.numpy as jnp
from jax import lax
from jax.experimental import pallas as pl
from jax.experimental.pallas import tpu as pltpu

F32 = jnp.float32
BF16 = jnp.bfloat16

D_MODEL = 1024
N_HEADS = 8
N_KV_HEADS = 2
HEAD_DIM = 64
GQA_GROUP = N_HEADS // N_KV_HEADS
WINDOW = 128
ATTN_WIDTH = N_HEADS * HEAD_DIM
KV_WIDTH = N_KV_HEADS * HEAD_DIM
SCALE = HEAD_DIM ** -0.5
HG_HEADS = 4
HG_DK = 128
HG_DV = 128
HG_WIDTH = HG_HEADS * HG_DV
D_FF = 2816
EPS = 1e-6

Q_COLS = (0, ATTN_WIDTH)
KV_COLS = (ATTN_WIDTH, ATTN_WIDTH + 2 * KV_WIDTH)
HP_COLS = (KV_COLS[1], KV_COLS[1] + 3 * HG_WIDTH)
HG_COLS = (HP_COLS[1], HP_COLS[1] + HG_WIDTH)
Z_Q = slice(0, ATTN_WIDTH)
Z_HP = slice(ATTN_WIDTH, ATTN_WIDTH + 3 * HG_WIDTH)
Z_HG = slice(ATTN_WIDTH + 3 * HG_WIDTH, ATTN_WIDTH + 4 * HG_WIDTH)
Z_WIDTH = ATTN_WIDTH + 4 * HG_WIDTH

VMEM_LIMIT_BIG = 58 * 1024 * 1024
FF_CHUNKS = tuple((lo, lo + 256) for lo in range(0, D_FF, 256))
TOKEN_TILE = 512
SAMPLE_TILE = 128
HG_TILE = 64
ATTN_SEQ_GROUP = 4
MIXER_STAGES = 7
NEG = -1e30
LOG2E = 1.4426950408889634


def _rms(x, g):
    return x * lax.rsqrt(jnp.mean(x * x, axis=-1, keepdims=True) + EPS) * g


def _silu(x):
    return x * jax.nn.sigmoid(x)


def _silu_t(x):
    h = 0.5 * x
    return h * jnp.tanh(h) + h


def _dot(a, b):
    return jnp.dot(a, b, preferred_element_type=F32)


def _dot_nt(a, b):
    return lax.dot_general(a, b, (((1,), (1,)), ((), ())), preferred_element_type=F32)


def _dot_tn(a, b):
    return lax.dot_general(a, b, (((0,), (0,)), ((), ())), preferred_element_type=F32)


def _split2(x):
    a = x.astype(BF16)
    b = (x - a.astype(F32)).astype(BF16)
    return jnp.concatenate([a, b], axis=0)


def _twice(sel):
    return jnp.concatenate([sel, sel], axis=1)


def _no_side_work():
    pass


def _swiglu_ffn(x, g_pre, wgu_ref, wd_ref, g_post, side=_no_side_work):
    h = _rms(x, g_pre).astype(BF16)
    acts = []
    for lo, hi in FF_CHUNKS:
        g = _dot(h, wgu_ref[:, lo:hi])
        side()
        u = _dot(h, wgu_ref[:, D_FF + lo:D_FF + hi])
        side()
        acts.append((_silu(g) * u).astype(BF16))
    down = _dot(jnp.concatenate(acts, axis=1), wd_ref[...])
    side()
    return x + 0.5 * _rms(down, g_post)


def _const_spec(shape):
    zeros = (0,) * len(shape)
    return pl.BlockSpec(shape, lambda *_: zeros, pipeline_mode=pl.Buffered(1))


def _row_spec(tile, width):
    return pl.BlockSpec((tile, width), lambda i: (i, 0))


def _ffn_in_kernel(xa_ref, xb_ref, g_pre, g_post, g_mix, wgu_ref, wd_ref, win_ref, bin_ref,
                   x1_ref, kv_ref, z_ref, *, tiles_a):
    from_a = pl.program_id(0) < tiles_a

    def half(rs):
        x = jnp.where(from_a, xa_ref[rs, :], xb_ref[rs, :])
        h = _rms(x, g_pre[...]).astype(BF16)
        yield
        acts = []
        for lo, hi in FF_CHUNKS:
            g = _dot(h, wgu_ref[:, lo:hi])
            yield
            u = _dot(h, wgu_ref[:, D_FF + lo:D_FF + hi])
            yield
            acts.append((_silu(g) * u).astype(BF16))
        down = _dot(jnp.concatenate(acts, axis=1), wd_ref[...])
        yield
        x1 = x + 0.5 * _rms(down, g_post[...])
        x1_ref[rs, :] = x1
        h = _rms(x1, g_mix[...]).astype(BF16)
        yield
        qkv = _dot(h, win_ref[:, Q_COLS[0]:KV_COLS[1]]) + bin_ref[:, Q_COLS[0]:KV_COLS[1]]
        z_ref[rs, Z_Q] = qkv[:, 0:ATTN_WIDTH]
        kv_ref[rs, :] = qkv[:, ATTN_WIDTH:ATTN_WIDTH + 2 * KV_WIDTH]
        yield
        z_ref[rs, Z_HP.start:Z_HG.stop] = (_dot(h, win_ref[:, HP_COLS[0]:HG_COLS[1]])
                                           + bin_ref[:, HP_COLS[0]:HG_COLS[1]])

    rows = x1_ref.shape[0] // 2
    first, second = half(slice(0, rows)), half(slice(rows, 2 * rows))
    next(first)
    alive = [first, second]
    while alive:
        alive = [gen for gen in alive if next(gen, "done") != "done"]


def _ffn_in(xa, xb, g_pre, g_post, g_mix, wgu, wd, win, b_in):
    tile = TOKEN_TILE
    assert xa.shape[0] % tile == 0 and xb.shape[0] % tile == 0
    tiles_a, tiles_b = xa.shape[0] // tile, xb.shape[0] // tile
    n_tiles = tiles_a + tiles_b
    widths = [D_MODEL, 2 * KV_WIDTH, Z_WIDTH]
    return pl.pallas_call(
        functools.partial(_ffn_in_kernel, tiles_a=tiles_a),
        grid=(n_tiles,),
        in_specs=[pl.BlockSpec((tile, D_MODEL), lambda j: (jnp.minimum(j, tiles_a - 1), 0)),
                  pl.BlockSpec((tile, D_MODEL), lambda j: (jnp.maximum(j - tiles_a, 0), 0)),
                  _const_spec(g_pre.shape), _const_spec(g_post.shape),
                  _const_spec(g_mix.shape), _const_spec(wgu.shape), _const_spec(wd.shape),
                  _const_spec(win.shape), _const_spec(b_in.shape)],
        out_specs=[_row_spec(tile, w) for w in widths],
        out_shape=[jax.ShapeDtypeStruct((n_tiles * tile, w), F32) for w in widths],
        compiler_params=pltpu.CompilerParams(dimension_semantics=("arbitrary",),
                                             vmem_limit_bytes=VMEM_LIMIT_BIG),
        name="ffn_in",
    )(xa, xb, g_pre, g_post, g_mix, wgu, wd, win, b_in)


def _mix_ffn(x1, mix, g_mixpost, g_pre, g_post, wgu_ref, wd_ref, side=_no_side_work):
    x2 = x1 + _rms(mix, g_mixpost)
    return _swiglu_ffn(x2, g_pre, wgu_ref, wd_ref, g_post, side)


def _staged(mixers):
    waiting, running = list(mixers), []
    n_slots = 2 * len(FF_CHUNKS) + 2
    start_slot = [k * (n_slots - MIXER_STAGES) // len(waiting) for k in range(len(waiting))]
    slot = [0]

    def side():
        while waiting and start_slot[len(start_slot) - len(waiting)] <= slot[0]:
            running.append(waiting.pop(0))
        slot[0] += 1
        for gen in list(running):
            if next(gen, "done") == "done":
                running.remove(gen)

    def drain():
        while waiting or running:
            side()

    return side, drain


def _attn_block(sink_ref, q, kv_ref, q0, gn, store):
    w = WINDOW
    nkeys = 2 * w
    r = lax.broadcasted_iota(jnp.int32, (nkeys, w), 0)
    c = lax.broadcasted_iota(jnp.int32, (nkeys, w), 1)
    first_half = lax.broadcasted_iota(jnp.int32, (w, 2 * HEAD_DIM), 1) < HEAD_DIM
    k0 = jnp.maximum(q0 - w, 0)
    dist = c + (q0 - k0) - r
    bias = jnp.where((dist >= 0) & (dist <= WINDOW), 0.0, NEG)
    kvb = kv_ref[pl.ds(pl.multiple_of(k0, w), nkeys), :]
    kk = kvb[:, 0:KV_WIDTH]
    k_nat = kk.astype(BF16)
    k_swp = pltpu.roll(kk, HEAD_DIM, axis=1).astype(BF16)
    vt = kvb[:, KV_WIDTH:2 * KV_WIDTH].T.astype(BF16)

    def masked_q(h):
        qp = q[:, (h // 2) * 2 * HEAD_DIM:(h // 2 + 1) * 2 * HEAD_DIM]
        return jnp.where(first_half != bool(h % 2), qp, 0.0).astype(BF16)

    groups = []
    for keys, use_swapped in ((k_nat, False), (k_swp, True)):
        heads = [h for h in range(N_HEADS) if (h // GQA_GROUP != h % 2) == use_swapped]
        groups.append((keys, heads, jnp.concatenate([masked_q(h) for h in heads], axis=0)))
    yield
    scores = [None] * N_HEADS
    for keys, heads, qs in groups:
        st = _dot_nt(keys, qs)
        for i, h in enumerate(heads):
            scores[h] = st[:, i * w:(i + 1) * w]
    yield
    probs, dens = [], []
    for h in range(N_HEADS):
        st = scores[h] * (SCALE * LOG2E) + bias
        sink2 = sink_ref[h] * LOG2E
        m = jnp.maximum(jnp.max(st, axis=0, keepdims=True), sink2)
        p = jnp.exp2(st - m)
        dens.append(jnp.sum(p, axis=0, keepdims=True) + jnp.exp2(sink2 - m))
        probs.append(p.astype(BF16))
        if h % GQA_GROUP == GQA_GROUP - 1:
            yield
    ot = _dot(vt, jnp.concatenate(probs, axis=1))
    yield
    rows = []
    for h in range(N_HEADS):
        hk = h // GQA_GROUP
        rows.append(ot[hk * HEAD_DIM:(hk + 1) * HEAD_DIM, h * w:(h + 1) * w] / dens[h])
    at = jnp.concatenate(rows, axis=0)
    inv = lax.rsqrt(jnp.sum(at * at, axis=0, keepdims=True) * (1.0 / ATTN_WIDTH) + EPS)
    store((at * inv).T * gn)


def _softmax_sink_pv(scores, masks, values, sink):
    scores = [jnp.where(mk, s * SCALE, NEG) for s, mk in zip(scores, masks)]
    m = sink
    for s in scores:
        m = jnp.maximum(m, jnp.max(s, axis=-1, keepdims=True))
    den = jnp.exp(sink - m)
    out = None
    for s, (v, transposed) in zip(scores, values):
        p = jnp.exp(s - m)
        den = den + jnp.sum(p, axis=-1, keepdims=True)
        pv = (_dot_nt if transposed else _dot)(p.astype(BF16), v)
        out = pv if out is None else out + pv
    return out / den


def _attn_sample_group(sink_ref, z_ref, kvn_ref, ck_ref, cv_ref, gn_ref, a_ref, ko_ref, vo_ref, grp, dec_len):
    ns = ATTN_SEQ_GROUP
    rows = ns * dec_len
    wb = ck_ref.shape[2]
    m_rows = GQA_GROUP * rows
    r = lax.broadcasted_iota(jnp.int32, (m_rows, ns * wb), 0)
    c = lax.broadcasted_iota(jnp.int32, (m_rows, ns * wb), 1)
    r_seq, r_pos = (r % rows) // dec_len, r % dec_len
    mask_cache = (r_seq == c // wb) & (c % wb >= r_pos)
    r = lax.broadcasted_iota(jnp.int32, (m_rows, rows), 0)
    c = lax.broadcasted_iota(jnp.int32, (m_rows, rows), 1)
    mask_new = ((r % rows) // dec_len == c // dec_len) & (c % dec_len <= r % dec_len)
    row_head = lax.broadcasted_iota(jnp.int32, (m_rows, 1), 0) // rows

    rs = slice(grp * rows, (grp + 1) * rows)
    q = z_ref[rs, Z_Q]
    kvn_b = kvn_ref[rs, :].astype(BF16)

    def cached(ref, hk):
        frows = slice(hk * HEAD_DIM, (hk + 1) * HEAD_DIM)
        return jnp.concatenate([ref[grp * ns + n, frows, :] for n in range(ns)], axis=1).astype(BF16)

    scores = []
    for hk in range(N_KV_HEADS):
        kcols = slice(hk * HEAD_DIM, (hk + 1) * HEAD_DIM)
        heads = [hk * GQA_GROUP + g for g in range(GQA_GROUP)]
        qs = jnp.concatenate([q[:, h * HEAD_DIM:(h + 1) * HEAD_DIM] for h in heads], axis=0).astype(BF16)
        scores.append([_dot(qs, cached(ck_ref, hk)), _dot_nt(qs, kvn_b[:, kcols])])
    yield
    outs = []
    for hk in range(N_KV_HEADS):
        vcols = slice(KV_WIDTH + hk * HEAD_DIM, KV_WIDTH + (hk + 1) * HEAD_DIM)
        sink = jnp.zeros((m_rows, 1), F32)
        for g in range(GQA_GROUP):
            sink = jnp.where(row_head == g, sink_ref[hk * GQA_GROUP + g], sink)
        values = [(cached(cv_ref, hk), True), (kvn_b[:, vcols], False)]
        outs.append(_softmax_sink_pv(scores[hk], [mask_cache, mask_new], values, sink))
    yield
    for hk in range(N_KV_HEADS):
        for g in range(GQA_GROUP):
            h = hk * GQA_GROUP + g
            a_ref[rs, h * HEAD_DIM:(h + 1) * HEAD_DIM] = outs[hk][g * rows:(g + 1) * rows, :]
    a_ref[rs, :] = _rms(a_ref[rs, :], gn_ref[...])
    keep = wb - dec_len
    is_new = lax.broadcasted_iota(jnp.int32, (KV_WIDTH, wb), 1) >= keep
    for n in range(ns):
        s = grp * ns + n
        ts = slice(grp * rows + n * dec_len, grp * rows + (n + 1) * dec_len)
        for src, dst, cols in ((ck_ref, ko_ref, slice(0, KV_WIDTH)), (cv_ref, vo_ref, slice(KV_WIDTH, 2 * KV_WIDTH))):
            new_t = jnp.concatenate([jnp.zeros((keep, KV_WIDTH), F32), kvn_ref[ts, cols]], axis=0).T
            dst[s] = jnp.where(is_new, new_t, pltpu.roll(src[s], keep, axis=1))


def _hgrn_bound_consts(lbl_ref):
    lbl = lbl_ref[...]
    e = jnp.exp(lbl - jnp.max(lbl, axis=0, keepdims=True))
    lb = e[0:1, :] / jnp.sum(e, axis=0, keepdims=True)
    return 0.5 + 0.5 * lb, 0.5 - 0.5 * lb


def _hgrn_gates(hp, c0, c1):
    qa = _silu_t(hp[:, 0:HG_WIDTH])
    t = c1 * jnp.tanh(0.5 * hp[:, HG_WIDTH:2 * HG_WIDTH])
    f, kx = c0 + t, c1 - t
    v = hp[:, 2 * HG_WIDTH:3 * HG_WIDTH].astype(BF16)
    return qa, kx, _split2(jnp.log2(f)), v


def _hgrn_factors(qa, kx, g, g_mid, g_last):
    qt = qa * jnp.exp2(g - g_mid)
    kt = kx * jnp.exp2(g_mid - g)
    qg = (qt * jnp.exp2(g_mid)).astype(BF16)
    kd = (kt * jnp.exp2(g_last - g_mid)).astype(BF16)
    return qt.astype(BF16), kt.astype(BF16), qg, kd


def _segment_masks(nseq):
    ls = HG_TILE // nseq
    r = lax.broadcasted_iota(jnp.int32, (HG_TILE, HG_TILE), 0)
    c = lax.broadcasted_iota(jnp.int32, (HG_TILE, HG_TILE), 1)
    same = (r // ls) == (c // ls)
    return same, same & (c <= r), same & (c % ls < ls // 2)


def _as_bf16(mask):
    return mask.astype(F32).astype(BF16)


def _hgrn_short_tile(z_ref, c0, c1, gn, s_in, s_out, o_ref, ti, nseq):
    ch = HG_TILE
    ls = ch // nseq
    same, causal, first_half = _segment_masks(nseq)
    cum_lhs = _twice(_as_bf16(jnp.concatenate([causal, first_half, same], axis=0)))
    sr = lax.broadcasted_iota(jnp.int32, (ch, nseq * HG_DV), 0)
    sc = lax.broadcasted_iota(jnp.int32, (ch, nseq * HG_DV), 1)
    seg_sel = _as_bf16((sr // ls) == (sc // HG_DV))
    seg_sel2 = jnp.concatenate([seg_sel, seg_sel], axis=0)
    rows = slice(ti * ch, (ti + 1) * ch)

    qa, kx, parts, v = _hgrn_gates(z_ref[rows, Z_HP], c0, c1)
    gs = _dot(cum_lhs, parts)
    dcol = _dot_tn(parts, seg_sel2)
    yield
    qt, kt, qg, kd = _hgrn_factors(qa, kx, gs[0:ch], gs[ch:2 * ch], gs[2 * ch:3 * ch])
    cols = [slice(h * HG_DK, (h + 1) * HG_DK) for h in range(HG_HEADS)]
    scores = [_dot_nt(qt[:, cs], kt[:, cs]) for cs in cols]
    yield
    outs = []
    for h, cs in enumerate(cols):
        a = jnp.where(causal, scores[h], 0.0)
        o = _dot(a.astype(BF16), v[:, cs])
        inter = []
        for n in range(nseq):
            rs = slice(n * ls, (n + 1) * ls)
            s = s_in[ti * nseq + n, h]
            inter.append(_dot(qg[rs, cs], s.astype(BF16)))
            decay = jnp.exp2(dcol[cs, n * HG_DV:(n + 1) * HG_DV])
            s_out[ti * nseq + n, h] = s * decay + _dot_tn(kd[rs, cs], v[rs, cs])
        outs.append(o + jnp.concatenate(inter, axis=0))
    yield
    gate = _silu_t(z_ref[rows, Z_HG])
    for h, cs in enumerate(cols):
        o_ref[rows, cs] = _rms(outs[h], gn) * gate[:, cs]


def _sample_mixer_ffn_kernel(sink_ref, x1_ref, z_ref, kvn_ref, ck_ref, cv_ref, s_in_ref,
                             lbl_ref, gn_ref, ghg_ref, g_mixpost, g_pre, g_post, wout_ref, wgu_ref, wd_ref,
                             y_ref, ko_ref, vo_ref, s_out_ref, ao_scr, *, dec_len):
    j = pl.program_id(0)
    tile = x1_ref.shape[0]

    @pl.when(j == 0)
    def _():
        ao_scr[...] = jnp.zeros(ao_scr.shape, F32)

    a_scr, o_scr = ao_scr.at[:, 0:ATTN_WIDTH], ao_scr.at[:, ATTN_WIDTH:ATTN_WIDTH + HG_WIDTH]
    c0, c1 = _hgrn_bound_consts(lbl_ref)
    nseq = HG_TILE // dec_len
    mixers = [_attn_sample_group(sink_ref, z_ref, kvn_ref, ck_ref, cv_ref, gn_ref, a_scr, ko_ref, vo_ref, grp, dec_len)
              for grp in range(tile // (ATTN_SEQ_GROUP * dec_len))]
    mixers += [_hgrn_short_tile(z_ref, c0, c1, ghg_ref[...], s_in_ref, s_out_ref, o_scr, ti, nseq)
               for ti in range(tile // HG_TILE)]
    side, drain = _staged(mixers)

    mix = _dot(ao_scr[...].astype(BF16), wout_ref[...])
    side()
    y_ref[...] = _mix_ffn(x1_ref[...], mix, g_mixpost[...], g_pre[...], g_post[...], wgu_ref, wd_ref, side)
    drain()


def _sample_mixer_ffn(x1, kv, z, cache_k, cache_v, state, sinks, g_attn, lb_logits, g_hg,
                      g_mixpost, g_pre, g_post, wout, wgu, wd, dec_len, row0):
    tile = SAMPLE_TILE
    t = cache_k.shape[0] * dec_len
    n_tiles = t // tile
    spt = tile // dec_len
    assert t % tile == 0 and row0 % tile == 0 and tile % HG_TILE == 0 and tile % (ATTN_SEQ_GROUP * dec_len) == 0
    cur = lambda j: jnp.minimum(j, n_tiles - 1)
    prev = lambda j: jnp.maximum(j - 1, 0)
    cur_spec = lambda width: pl.BlockSpec((tile, width), lambda j: (row0 // tile + cur(j), 0))
    cache_spec = pl.BlockSpec((spt,) + cache_k.shape[1:], lambda j: (cur(j), 0, 0))
    state_spec = pl.BlockSpec((spt,) + state.shape[1:], lambda j: (cur(j), 0, 0, 0))
    return pl.pallas_call(
        functools.partial(_sample_mixer_ffn_kernel, dec_len=dec_len),
        grid=(n_tiles + 1,),
        in_specs=[pl.BlockSpec(memory_space=pltpu.SMEM),
                  pl.BlockSpec((tile, D_MODEL), lambda j: (row0 // tile + prev(j), 0)),
                  cur_spec(Z_WIDTH), cur_spec(2 * KV_WIDTH), cache_spec, cache_spec, state_spec,
                  _const_spec(lb_logits.shape), _const_spec(g_attn.shape), _const_spec(g_hg.shape),
                  _const_spec(g_mixpost.shape), _const_spec(g_pre.shape), _const_spec(g_post.shape),
                  _const_spec(wout.shape), _const_spec(wgu.shape), _const_spec(wd.shape)],
        out_specs=[pl.BlockSpec((tile, D_MODEL), lambda j: (prev(j), 0)), cache_spec, cache_spec, state_spec],
        out_shape=[jax.ShapeDtypeStruct((t, D_MODEL), F32), jax.ShapeDtypeStruct(cache_k.shape, F32),
                   jax.ShapeDtypeStruct(cache_v.shape, F32), jax.ShapeDtypeStruct(state.shape, F32)],
        scratch_shapes=[pltpu.VMEM((tile, ATTN_WIDTH + HG_WIDTH), F32)],
        compiler_params=pltpu.CompilerParams(dimension_semantics=("arbitrary",),
                                             vmem_limit_bytes=VMEM_LIMIT_BIG),
        name="sample_mixer_ffn",
    )(sinks, x1, z, kv, cache_k, cache_v, state, lb_logits, g_attn, g_hg,
      g_mixpost, g_pre, g_post, wout, wgu, wd)


def _mixer_ffn_kernel(sink_ref, x1_ref, z_ref, kv_ref, lbl_ref, gn_ref, ghg_ref,
                      g_mixpost, g_pre, g_post, wout_ref, wgu_ref, wd_ref,
                      y_ref, s_out_ref, ao_scr, st_scr, *, n_tiles, tiles_per_seq):
    j = pl.program_id(0)
    tile = x1_ref.shape[0]
    pos = jnp.minimum(j, n_tiles - 1) % tiles_per_seq

    @pl.when(j == 0)
    def _():
        ao_scr[...] = jnp.zeros(ao_scr.shape, ao_scr.dtype)
        st_scr[...] = jnp.zeros(st_scr.shape, F32)

    def attn_block(blk):
        rs = slice(blk * WINDOW, (blk + 1) * WINDOW)

        def store(a):
            ao_scr[rs, 0:ATTN_WIDTH] = a.astype(ao_scr.dtype)

        return _attn_block(sink_ref, z_ref[rs, Z_Q], kv_ref, pos * tile + blk * WINDOW, gn_ref[...], store)

    ch = HG_TILE
    _, causal, _ = _segment_masks(1)
    cum_lhs = _twice(_as_bf16(causal))
    c0, c1 = _hgrn_bound_consts(lbl_ref)
    ghg = ghg_ref[...]

    def hgrn_chunk(ci):
        rs = slice(ci * ch, (ci + 1) * ch)
        qa, kx, parts, v = _hgrn_gates(z_ref[rs, Z_HP], c0, c1)
        yield
        g = _dot(cum_lhs, parts)
        yield
        g_last = g[ch - 1:ch, :]
        qt, kt, qg, kd = _hgrn_factors(qa, kx, g, g[ch // 2 - 1:ch // 2, :], g_last)
        decay = jnp.exp2(g_last)
        cols = [slice(h * HG_DK, (h + 1) * HG_DK) for h in range(HG_HEADS)]
        yield
        scores = [_dot_nt(qt[:, cs], kt[:, cs]) for cs in cols]
        yield
        probs = [jnp.where(causal, sc, 0.0).astype(BF16) for sc in scores]
        yield
        outs = []
        for h, cs in enumerate(cols):
            st = st_scr[h]
            if ci == 0:
                st = jnp.where(pos == 0, 0.0, st)
            outs.append(_dot(probs[h], v[:, cs]) + _dot_nt(qg[:, cs], st.astype(BF16)))
            st_scr[h] = st * decay[:, cs] + _dot_tn(v[:, cs], kd[:, cs])
        yield
        gate = _silu_t(z_ref[rs, Z_HG])
        for h, cs in enumerate(cols):
            o = _rms(outs[h], ghg) * gate[:, cs]
            ao_scr[rs, ATTN_WIDTH + h * HG_DV:ATTN_WIDTH + (h + 1) * HG_DV] = o.astype(ao_scr.dtype)

    n_blk, n_chunk = tile // WINDOW, tile // ch
    mixers = []
    for blk in range(n_blk):
        mixers += [hgrn_chunk(ci) for ci in range(blk * n_chunk // n_blk, (blk + 1) * n_chunk // n_blk)]
        mixers.insert(len(mixers) - 1, attn_block(blk))
    side, drain = _staged(mixers)

    mix = _dot(ao_scr[...].astype(BF16), wout_ref[...])
    side()
    y_ref[...] = _mix_ffn(x1_ref[...], mix, g_mixpost[...], g_pre[...], g_post[...], wgu_ref, wd_ref, side)
    drain()

    @pl.when((pos == tiles_per_seq - 1) & (j < n_tiles))
    def _():
        b = j // tiles_per_seq
        for h in range(HG_HEADS):
            s_out_ref[pl.ds(b, 1), h] = st_scr[h].T[None]


def _mixer_ffn(x1, kv, z, sinks, g_attn, lb_logits, g_hg, g_mixpost, g_pre, g_post, wout, wgu, wd,
               batch, seq):
    t = batch * seq
    tile = TOKEN_TILE
    assert seq % tile == 0
    n_tiles, tps = t // tile, seq // tile
    cur = lambda j: jnp.minimum(j, n_tiles - 1)
    prev = lambda j: jnp.maximum(j - 1, 0)
    cur_spec = lambda width: pl.BlockSpec((tile, width), lambda j: (cur(j), 0))
    prev_spec = pl.BlockSpec((tile, D_MODEL), lambda j: (prev(j), 0))
    s_shape = (batch, HG_HEADS, HG_DK, HG_DV)
    return pl.pallas_call(
        functools.partial(_mixer_ffn_kernel, n_tiles=n_tiles, tiles_per_seq=tps),
        grid=(n_tiles + 1,),
        in_specs=[pl.BlockSpec(memory_space=pltpu.SMEM),
                  prev_spec, cur_spec(Z_WIDTH),
                  pl.BlockSpec((seq, 2 * KV_WIDTH), lambda j: (cur(j) // tps, 0)),
                  _const_spec(lb_logits.shape), _const_spec(g_attn.shape), _const_spec(g_hg.shape),
                  _const_spec(g_mixpost.shape), _const_spec(g_pre.shape), _const_spec(g_post.shape),
                  _const_spec(wout.shape), _const_spec(wgu.shape), _const_spec(wd.shape)],
        out_specs=[prev_spec, pl.BlockSpec(s_shape, lambda j: (0, 0, 0, 0))],
        out_shape=[jax.ShapeDtypeStruct((t, D_MODEL), F32), jax.ShapeDtypeStruct(s_shape, F32)],
        scratch_shapes=[pltpu.VMEM((tile, ATTN_WIDTH + HG_WIDTH), BF16), pltpu.VMEM((HG_HEADS, HG_DV, HG_DK), F32)],
        compiler_params=pltpu.CompilerParams(dimension_semantics=("arbitrary",),
                                             vmem_limit_bytes=VMEM_LIMIT_BIG),
        name="mixer_ffn",
    )(sinks, x1, z, kv, lb_logits, g_attn, g_hg, g_mixpost, g_pre, g_post, wout, wgu, wd)


def kernel(x_prompt, x_sample, cache_k_win, cache_v_win, state_hgrn, w_in, b_in, attn_sinks, attn_out_norm,
           hg_lb_logits, hg_out_norm, w_out, ffn1_w_gu, ffn1_w_down, ffn2_w_gu, ffn2_w_down,
           norm_ffn1_pre, norm_ffn1_post, norm_mix_pre, norm_mix_post, norm_ffn2_pre, norm_ffn2_post):
    depth = w_in.shape[0]
    assert depth == 1, "single-layer trunk"
    batch, seq, _ = x_prompt.shape
    dec_batch, dec_len, _ = x_sample.shape
    wb = cache_k_win.shape[2]
    assert seq % WINDOW == 0 and wb == WINDOW and HG_TILE % dec_len == 0
    layer = 0
    row = lambda p: p[layer].reshape(1, -1).astype(F32)
    wgu1, wd1 = ffn1_w_gu[layer].astype(BF16), ffn1_w_down[layer].astype(BF16)
    wgu2, wd2 = ffn2_w_gu[layer].astype(BF16), ffn2_w_down[layer].astype(BF16)
    win, wout = w_in[layer].astype(BF16), w_out[layer].astype(BF16)
    sinks = attn_sinks[layer].astype(F32)
    lb_logits = hg_lb_logits.astype(F32)
    g_attn, g_hg = row(attn_out_norm), row(hg_out_norm)
    back_params = (row(norm_mix_post), row(norm_ffn2_pre), row(norm_ffn2_post), wout, wgu2, wd2)

    t_p, t_s = batch * seq, dec_batch * dec_len
    x1, kv, z = _ffn_in(x_prompt.reshape(t_p, D_MODEL), x_sample.reshape(t_s, D_MODEL),
                                row(norm_ffn1_pre), row(norm_ffn1_post), row(norm_mix_pre), wgu1, wd1, win, row(b_in))

    y_p, s_prompt = _mixer_ffn(x1, kv, z, sinks, g_attn, lb_logits, g_hg, *back_params, batch, seq)
    y_prompt = y_p.reshape(batch, seq, D_MODEL)
    kv_last = jnp.stack([kv[(b + 1) * seq - WINDOW:(b + 1) * seq] for b in range(batch)])
    k_prompt = kv_last[..., :KV_WIDTH].reshape(1, batch, WINDOW, N_KV_HEADS, HEAD_DIM)
    v_prompt = kv_last[..., KV_WIDTH:].reshape(1, batch, WINDOW, N_KV_HEADS, HEAD_DIM)

    def feature_major(buf):
        return jnp.transpose(buf, (0, 2, 3, 1)).reshape(dec_batch, KV_WIDTH, wb)

    def window_major(buf_t):
        return jnp.transpose(buf_t.reshape(dec_batch, N_KV_HEADS, HEAD_DIM, wb), (0, 3, 1, 2))[None]

    y_s, k_s, v_s, s_sample = _sample_mixer_ffn(
        x1, kv, z, feature_major(cache_k_win[layer]), feature_major(cache_v_win[layer]),
        state_hgrn[layer].astype(F32), sinks, g_attn, lb_logits, g_hg, *back_params, dec_len, t_p)
    y_sample = y_s.reshape(dec_batch, dec_len, D_MODEL)
    k_sample, v_sample = window_major(k_s), window_major(v_s)

    return (y_prompt, y_sample, k_prompt, v_prompt, s_prompt[None], k_sample, v_sample, s_sample[None])
```

```python
import functools

import jax
import jax.numpy as jnp
from jax import lax
from jax.experimental import pallas as pl
from jax.experimental.pallas import tpu as pltpu

F32 = jnp.float32
BF16 = jnp.bfloat16

D_MODEL = 1024
N_HEADS = 8
N_KV_HEADS = 2
HEAD_DIM = 64
GQA_GROUP = N_HEADS // N_KV_HEADS
WINDOW = 128
ATTN_WIDTH = N_HEADS * HEAD_DIM
KV_WIDTH = N_KV_HEADS * HEAD_DIM
SCALE = HEAD_DIM ** -0.5
HG_HEADS = 4
HG_DK = 128
HG_DV = 128
HG_WIDTH = HG_HEADS * HG_DV
D_FF = 2816
EPS = 1e-6

Q_COLS = (0, ATTN_WIDTH)
KV_COLS = (ATTN_WIDTH, ATTN_WIDTH + 2 * KV_WIDTH)
HP_COLS = (KV_COLS[1], KV_COLS[1] + 3 * HG_WIDTH)
HG_COLS = (HP_COLS[1], HP_COLS[1] + HG_WIDTH)
Z_Q = slice(0, ATTN_WIDTH)
Z_HP = slice(ATTN_WIDTH, ATTN_WIDTH + 3 * HG_WIDTH)
Z_HG = slice(ATTN_WIDTH + 3 * HG_WIDTH, ATTN_WIDTH + 4 * HG_WIDTH)
Z_WIDTH = ATTN_WIDTH + 4 * HG_WIDTH

VMEM_LIMIT_BIG = 58 * 1024 * 1024
FF_CHUNKS = tuple((lo, lo + 256) for lo in range(0, D_FF, 256))
TOKEN_TILE = 512
SAMPLE_TILE = 128
HG_TILE = 64
ATTN_SEQ_GROUP = 4
MIXER_STAGES = 7
NEG = -1e30
LOG2E = 1.4426950408889634


def _rms(x, g):
    return x * lax.rsqrt(jnp.mean(x * x, axis=-1, keepdims=True) + EPS) * g


def _silu(x):
    return x * jax.nn.sigmoid(x)


def _silu_t(x):
    h = 0.5 * x
    return h * jnp.tanh(h) + h


def _dot(a, b):
    return jnp.dot(a, b, preferred_element_type=F32)


def _dot_nt(a, b):
    return lax.dot_general(a, b, (((1,), (1,)), ((), ())), preferred_element_type=F32)


def _dot_tn(a, b):
    return lax.dot_general(a, b, (((0,), (0,)), ((), ())), preferred_element_type=F32)


def _split2(x):
    a = x.astype(BF16)
    b = (x - a.astype(F32)).astype(BF16)
    return jnp.concatenate([a, b], axis=0)


def _twice(sel):
    return jnp.concatenate([sel, sel], axis=1)


def _no_side_work():
    pass


def _swiglu_ffn(x, g_pre, wgu_ref, wd_ref, g_post, side=_no_side_work):
    h = _rms(x, g_pre).astype(BF16)
    acts = []
    for lo, hi in FF_CHUNKS:
        g = _dot(h, wgu_ref[:, lo:hi])
        side()
        u = _dot(h, wgu_ref[:, D_FF + lo:D_FF + hi])
        side()
        acts.append((_silu(g) * u).astype(BF16))
    down = _dot(jnp.concatenate(acts, axis=1), wd_ref[...])
    side()
    return x + 0.5 * _rms(down, g_post)


def _const_spec(shape):
    zeros = (0,) * len(shape)
    return pl.BlockSpec(shape, lambda *_: zeros, pipeline_mode=pl.Buffered(1))


def _row_spec(tile, width):
    return pl.BlockSpec((tile, width), lambda i: (i, 0))


def _ffn_in_kernel(xa_ref, xb_ref, g_pre, g_post, g_mix, wgu_ref, wd_ref, win_ref, bin_ref,
                   x1_ref, kv_ref, z_ref, *, tiles_a):
    from_a = pl.program_id(0) < tiles_a

    def half(rs):
        x = jnp.where(from_a, xa_ref[rs, :], xb_ref[rs, :])
        h = _rms(x, g_pre[...]).astype(BF16)
        yield
        acts = []
        for lo, hi in FF_CHUNKS:
            g = _dot(h, wgu_ref[:, lo:hi])
            yield
            u = _dot(h, wgu_ref[:, D_FF + lo:D_FF + hi])
            yield
            acts.append((_silu(g) * u).astype(BF16))
        down = _dot(jnp.concatenate(acts, axis=1), wd_ref[...])
        yield
        x1 = x + 0.5 * _rms(down, g_post[...])
        x1_ref[rs, :] = x1
        h = _rms(x1, g_mix[...]).astype(BF16)
        yield
        qkv = _dot(h, win_ref[:, Q_COLS[0]:KV_COLS[1]]) + bin_ref[:, Q_COLS[0]:KV_COLS[1]]
        z_ref[rs, Z_Q] = qkv[:, 0:ATTN_WIDTH]
        kv_ref[rs, :] = qkv[:, ATTN_WIDTH:ATTN_WIDTH + 2 * KV_WIDTH]
        yield
        z_ref[rs, Z_HP.start:Z_HG.stop] = (_dot(h, win_ref[:, HP_COLS[0]:HG_COLS[1]])
                                           + bin_ref[:, HP_COLS[0]:HG_COLS[1]])

    rows = x1_ref.shape[0] // 2
    first, second = half(slice(0, rows)), half(slice(rows, 2 * rows))
    next(first)
    alive = [first, second]
    while alive:
        alive = [gen for gen in alive if next(gen, "done") != "done"]


def _ffn_in(xa, xb, g_pre, g_post, g_mix, wgu, wd, win, b_in):
    tile = TOKEN_TILE
    assert xa.shape[0] % tile == 0 and xb.shape[0] % tile == 0
    tiles_a, tiles_b = xa.shape[0] // tile, xb.shape[0] // tile
    n_tiles = tiles_a + tiles_b
    widths = [D_MODEL, 2 * KV_WIDTH, Z_WIDTH]
    return pl.pallas_call(
        functools.partial(_ffn_in_kernel, tiles_a=tiles_a),
        grid=(n_tiles,),
        in_specs=[pl.BlockSpec((tile, D_MODEL), lambda j: (jnp.minimum(j, tiles_a - 1), 0)),
                  pl.BlockSpec((tile, D_MODEL), lambda j: (jnp.maximum(j - tiles_a, 0), 0)),
                  _const_spec(g_pre.shape), _const_spec(g_post.shape),
                  _const_spec(g_mix.shape), _const_spec(wgu.shape), _const_spec(wd.shape),
                  _const_spec(win.shape), _const_spec(b_in.shape)],
        out_specs=[_row_spec(tile, w) for w in widths],
        out_shape=[jax.ShapeDtypeStruct((n_tiles * tile, w), F32) for w in widths],
        compiler_params=pltpu.CompilerParams(dimension_semantics=("arbitrary",),
                                             vmem_limit_bytes=VMEM_LIMIT_BIG),
        name="ffn_in",
    )(xa, xb, g_pre, g_post, g_mix, wgu, wd, win, b_in)


def _mix_ffn(x1, mix, g_mixpost, g_pre, g_post, wgu_ref, wd_ref, side=_no_side_work):
    x2 = x1 + _rms(mix, g_mixpost)
    return _swiglu_ffn(x2, g_pre, wgu_ref, wd_ref, g_post, side)


def _staged(mixers, n_slots=2 * len(FF_CHUNKS) + 2):
    waiting, running = list(mixers), []
    start_slot = [k * (n_slots - MIXER_STAGES) // len(waiting) for k in range(len(waiting))]
    slot = [0]

    def side():
        while waiting and start_slot[len(start_slot) - len(waiting)] <= slot[0]:
            running.append(waiting.pop(0))
        slot[0] += 1
        for gen in list(running):
            if next(gen, "done") == "done":
                running.remove(gen)

    def drain():
        while waiting or running:
            side()

    return side, drain


def _attn_block(sink_ref, q, kv_ref, q0, gn, store):
    w = WINDOW
    nkeys = 2 * w
    r = lax.broadcasted_iota(jnp.int32, (nkeys, w), 0)
    c = lax.broadcasted_iota(jnp.int32, (nkeys, w), 1)
    first_half = lax.broadcasted_iota(jnp.int32, (w, 2 * HEAD_DIM), 1) < HEAD_DIM
    k0 = jnp.maximum(q0 - w, 0)
    dist = c + (q0 - k0) - r
    bias = jnp.where((dist >= 0) & (dist <= WINDOW), 0.0, NEG)
    kvb = kv_ref[pl.ds(pl.multiple_of(k0, w), nkeys), :]
    kk = kvb[:, 0:KV_WIDTH]
    k_nat = kk.astype(BF16)
    k_swp = pltpu.roll(kk, HEAD_DIM, axis=1).astype(BF16)
    vt = kvb[:, KV_WIDTH:2 * KV_WIDTH].T.astype(BF16)

    def masked_q(h):
        qp = q[:, (h // 2) * 2 * HEAD_DIM:(h // 2 + 1) * 2 * HEAD_DIM]
        return jnp.where(first_half != bool(h % 2), qp, 0.0).astype(BF16)

    groups = []
    for keys, use_swapped in ((k_nat, False), (k_swp, True)):
        heads = [h for h in range(N_HEADS) if (h // GQA_GROUP != h % 2) == use_swapped]
        groups.append((keys, heads, jnp.concatenate([masked_q(h) for h in heads], axis=0)))
    yield
    scores = [None] * N_HEADS
    for keys, heads, qs in groups:
        st = _dot_nt(keys, qs)
        for i, h in enumerate(heads):
            scores[h] = st[:, i * w:(i + 1) * w]
    yield
    probs, dens = [], []
    for h in range(N_HEADS):
        st = scores[h] * (SCALE * LOG2E) + bias
        sink2 = sink_ref[h] * LOG2E
        m = jnp.maximum(jnp.max(st, axis=0, keepdims=True), sink2)
        p = jnp.exp2(st - m)
        dens.append(jnp.sum(p, axis=0, keepdims=True) + jnp.exp2(sink2 - m))
        probs.append(p.astype(BF16))
        if h % GQA_GROUP == GQA_GROUP - 1:
            yield
    ot = _dot(vt, jnp.concatenate(probs, axis=1))
    yield
    rows = []
    for h in range(N_HEADS):
        hk = h // GQA_GROUP
        rows.append(ot[hk * HEAD_DIM:(hk + 1) * HEAD_DIM, h * w:(h + 1) * w] / dens[h])
    at = jnp.concatenate(rows, axis=0)
    inv = lax.rsqrt(jnp.sum(at * at, axis=0, keepdims=True) * (1.0 / ATTN_WIDTH) + EPS)
    store((at * inv).T * gn)


def _softmax_sink_pv(scores, masks, values, sink):
    scores = [jnp.where(mk, s * SCALE, NEG) for s, mk in zip(scores, masks)]
    m = sink
    for s in scores:
        m = jnp.maximum(m, jnp.max(s, axis=-1, keepdims=True))
    den = jnp.exp(sink - m)
    out = None
    for s, (v, transposed) in zip(scores, values):
        p = jnp.exp(s - m)
        den = den + jnp.sum(p, axis=-1, keepdims=True)
        pv = (_dot_nt if transposed else _dot)(p.astype(BF16), v)
        out = pv if out is None else out + pv
    return out / den


def _attn_sample_group(sink_ref, z_ref, kvn_ref, ck_ref, cv_ref, gn_ref, a_ref, ko_ref, vo_ref, grp, dec_len):
    ns = ATTN_SEQ_GROUP
    rows = ns * dec_len
    wb = ck_ref.shape[2]
    m_rows = GQA_GROUP * rows
    r = lax.broadcasted_iota(jnp.int32, (m_rows, ns * wb), 0)
    c = lax.broadcasted_iota(jnp.int32, (m_rows, ns * wb), 1)
    r_seq, r_pos = (r % rows) // dec_len, r % dec_len
    mask_cache = (r_seq == c // wb) & (c % wb >= r_pos)
    r = lax.broadcasted_iota(jnp.int32, (m_rows, rows), 0)
    c = lax.broadcasted_iota(jnp.int32, (m_rows, rows), 1)
    mask_new = ((r % rows) // dec_len == c // dec_len) & (c % dec_len <= r % dec_len)
    row_head = lax.broadcasted_iota(jnp.int32, (m_rows, 1), 0) // rows

    rs = slice(grp * rows, (grp + 1) * rows)
    q = z_ref[rs, Z_Q]
    kvn_b = kvn_ref[rs, :].astype(BF16)

    def cached(ref, hk):
        frows = slice(hk * HEAD_DIM, (hk + 1) * HEAD_DIM)
        return jnp.concatenate([ref[grp * ns + n, frows, :] for n in range(ns)], axis=1).astype(BF16)

    scores = []
    for hk in range(N_KV_HEADS):
        kcols = slice(hk * HEAD_DIM, (hk + 1) * HEAD_DIM)
        heads = [hk * GQA_GROUP + g for g in range(GQA_GROUP)]
        qs = jnp.concatenate([q[:, h * HEAD_DIM:(h + 1) * HEAD_DIM] for h in heads], axis=0).astype(BF16)
        scores.append([_dot(qs, cached(ck_ref, hk)), _dot_nt(qs, kvn_b[:, kcols])])
    yield
    outs = []
    for hk in range(N_KV_HEADS):
        vcols = slice(KV_WIDTH + hk * HEAD_DIM, KV_WIDTH + (hk + 1) * HEAD_DIM)
        sink = jnp.zeros((m_rows, 1), F32)
        for g in range(GQA_GROUP):
            sink = jnp.where(row_head == g, sink_ref[hk * GQA_GROUP + g], sink)
        values = [(cached(cv_ref, hk), True), (kvn_b[:, vcols], False)]
        outs.append(_softmax_sink_pv(scores[hk], [mask_cache, mask_new], values, sink))
    yield
    for hk in range(N_KV_HEADS):
        for g in range(GQA_GROUP):
            h = hk * GQA_GROUP + g
            a_ref[rs, h * HEAD_DIM:(h + 1) * HEAD_DIM] = outs[hk][g * rows:(g + 1) * rows, :]
    a_ref[rs, :] = _rms(a_ref[rs, :], gn_ref[...])
    keep = wb - dec_len
    is_new = lax.broadcasted_iota(jnp.int32, (KV_WIDTH, wb), 1) >= keep
    for n in range(ns):
        s = grp * ns + n
        ts = slice(grp * rows + n * dec_len, grp * rows + (n + 1) * dec_len)
        for src, dst, cols in ((ck_ref, ko_ref, slice(0, KV_WIDTH)), (cv_ref, vo_ref, slice(KV_WIDTH, 2 * KV_WIDTH))):
            new_t = jnp.concatenate([jnp.zeros((keep, KV_WIDTH), F32), kvn_ref[ts, cols]], axis=0).T
            dst[s] = jnp.where(is_new, new_t, pltpu.roll(src[s], keep, axis=1))


def _hgrn_bound_consts(lbl_ref):
    lbl = lbl_ref[...]
    e = jnp.exp(lbl - jnp.max(lbl, axis=0, keepdims=True))
    lb = e[0:1, :] / jnp.sum(e, axis=0, keepdims=True)
    return 0.5 + 0.5 * lb, 0.5 - 0.5 * lb


def _hgrn_gates(hp, c0, c1):
    qa = _silu_t(hp[:, 0:HG_WIDTH])
    t = c1 * jnp.tanh(0.5 * hp[:, HG_WIDTH:2 * HG_WIDTH])
    f, kx = c0 + t, c1 - t
    v = hp[:, 2 * HG_WIDTH:3 * HG_WIDTH].astype(BF16)
    return qa, kx, _split2(jnp.log2(f)), v


def _hgrn_factors(qa, kx, g, g_mid, g_last):
    qt = qa * jnp.exp2(g - g_mid)
    kt = kx * jnp.exp2(g_mid - g)
    qg = (qt * jnp.exp2(g_mid)).astype(BF16)
    kd = (kt * jnp.exp2(g_last - g_mid)).astype(BF16)
    return qt.astype(BF16), kt.astype(BF16), qg, kd


def _segment_masks(nseq):
    ls = HG_TILE // nseq
    r = lax.broadcasted_iota(jnp.int32, (HG_TILE, HG_TILE), 0)
    c = lax.broadcasted_iota(jnp.int32, (HG_TILE, HG_TILE), 1)
    same = (r // ls) == (c // ls)
    return same, same & (c <= r), same & (c % ls < ls // 2)


def _as_bf16(mask):
    return mask.astype(F32).astype(BF16)


def _hgrn_short_tile(z_ref, c0, c1, gn, s_in, s_out, o_ref, ti, nseq):
    ch = HG_TILE
    ls = ch // nseq
    same, causal, first_half = _segment_masks(nseq)
    cum_lhs = _twice(_as_bf16(jnp.concatenate([causal, first_half, same], axis=0)))
    sr = lax.broadcasted_iota(jnp.int32, (ch, nseq * HG_DV), 0)
    sc = lax.broadcasted_iota(jnp.int32, (ch, nseq * HG_DV), 1)
    seg_sel = _as_bf16((sr // ls) == (sc // HG_DV))
    seg_sel2 = jnp.concatenate([seg_sel, seg_sel], axis=0)
    rows = slice(ti * ch, (ti + 1) * ch)

    qa, kx, parts, v = _hgrn_gates(z_ref[rows, Z_HP], c0, c1)
    gs = _dot(cum_lhs, parts)
    dcol = _dot_tn(parts, seg_sel2)
    yield
    qt, kt, qg, kd = _hgrn_factors(qa, kx, gs[0:ch], gs[ch:2 * ch], gs[2 * ch:3 * ch])
    cols = [slice(h * HG_DK, (h + 1) * HG_DK) for h in range(HG_HEADS)]
    scores = [_dot_nt(qt[:, cs], kt[:, cs]) for cs in cols]
    yield
    outs = []
    for h, cs in enumerate(cols):
        a = jnp.where(causal, scores[h], 0.0)
        o = _dot(a.astype(BF16), v[:, cs])
        inter = []
        for n in range(nseq):
            rs = slice(n * ls, (n + 1) * ls)
            s = s_in[ti * nseq + n, h]
            inter.append(_dot(qg[rs, cs], s.astype(BF16)))
            decay = jnp.exp2(dcol[cs, n * HG_DV:(n + 1) * HG_DV])
            s_out[ti * nseq + n, h] = s * decay + _dot_tn(kd[rs, cs], v[rs, cs])
        outs.append(o + jnp.concatenate(inter, axis=0))
    yield
    gate = _silu_t(z_ref[rows, Z_HG])
    for h, cs in enumerate(cols):
        o_ref[rows, cs] = _rms(outs[h], gn) * gate[:, cs]


def _sample_mixer_ffn_kernel(sink_ref, x1_ref, z_ref, kvn_ref, ck_ref, cv_ref, s_in_ref,
                             lbl_ref, gn_ref, ghg_ref, g_mixpost, g_pre, g_post, wout_ref, wgu_ref, wd_ref,
                             y_ref, ko_ref, vo_ref, s_out_ref, ao_scr, *, dec_len):
    j = pl.program_id(0)
    tile = x1_ref.shape[0]

    @pl.when(j == 0)
    def _():
        ao_scr[...] = jnp.zeros(ao_scr.shape, F32)

    a_scr, o_scr = ao_scr.at[:, 0:ATTN_WIDTH], ao_scr.at[:, ATTN_WIDTH:ATTN_WIDTH + HG_WIDTH]
    c0, c1 = _hgrn_bound_consts(lbl_ref)
    nseq = HG_TILE // dec_len
    mixers = [_attn_sample_group(sink_ref, z_ref, kvn_ref, ck_ref, cv_ref, gn_ref, a_scr, ko_ref, vo_ref, grp, dec_len)
              for grp in range(tile // (ATTN_SEQ_GROUP * dec_len))]
    mixers += [_hgrn_short_tile(z_ref, c0, c1, ghg_ref[...], s_in_ref, s_out_ref, o_scr, ti, nseq)
               for ti in range(tile // HG_TILE)]
    side, drain = _staged(mixers)

    mix = _dot(ao_scr[...].astype(BF16), wout_ref[...])
    side()
    y_ref[...] = _mix_ffn(x1_ref[...], mix, g_mixpost[...], g_pre[...], g_post[...], wgu_ref, wd_ref, side)
    drain()


def _sample_mixer_ffn(x1, kv, z, cache_k, cache_v, state, sinks, g_attn, lb_logits, g_hg,
                      g_mixpost, g_pre, g_post, wout, wgu, wd, dec_len, row0):
    tile = SAMPLE_TILE
    t = cache_k.shape[0] * dec_len
    n_tiles = t // tile
    spt = tile // dec_len
    assert t % tile == 0 and row0 % tile == 0 and tile % HG_TILE == 0 and tile % (ATTN_SEQ_GROUP * dec_len) == 0
    cur = lambda j: jnp.minimum(j, n_tiles - 1)
    prev = lambda j: jnp.maximum(j - 1, 0)
    cur_spec = lambda width: pl.BlockSpec((tile, width), lambda j: (row0 // tile + cur(j), 0))
    cache_spec = pl.BlockSpec((spt,) + cache_k.shape[1:], lambda j: (cur(j), 0, 0))
    state_spec = pl.BlockSpec((spt,) + state.shape[1:], lambda j: (cur(j), 0, 0, 0))
    return pl.pallas_call(
        functools.partial(_sample_mixer_ffn_kernel, dec_len=dec_len),
        grid=(n_tiles + 1,),
        in_specs=[pl.BlockSpec(memory_space=pltpu.SMEM),
                  pl.BlockSpec((tile, D_MODEL), lambda j: (row0 // tile + prev(j), 0)),
                  cur_spec(Z_WIDTH), cur_spec(2 * KV_WIDTH), cache_spec, cache_spec, state_spec,
                  _const_spec(lb_logits.shape), _const_spec(g_attn.shape), _const_spec(g_hg.shape),
                  _const_spec(g_mixpost.shape), _const_spec(g_pre.shape), _const_spec(g_post.shape),
                  _const_spec(wout.shape), _const_spec(wgu.shape), _const_spec(wd.shape)],
        out_specs=[pl.BlockSpec((tile, D_MODEL), lambda j: (prev(j), 0)), cache_spec, cache_spec, state_spec],
        out_shape=[jax.ShapeDtypeStruct((t, D_MODEL), F32), jax.ShapeDtypeStruct(cache_k.shape, F32),
                   jax.ShapeDtypeStruct(cache_v.shape, F32), jax.ShapeDtypeStruct(state.shape, F32)],
        scratch_shapes=[pltpu.VMEM((tile, ATTN_WIDTH + HG_WIDTH), F32)],
        compiler_params=pltpu.CompilerParams(dimension_semantics=("arbitrary",),
                                             vmem_limit_bytes=VMEM_LIMIT_BIG),
        name="sample_mixer_ffn",
    )(sinks, x1, z, kv, cache_k, cache_v, state, lb_logits, g_attn, g_hg,
      g_mixpost, g_pre, g_post, wout, wgu, wd)


def _mixer_ffn_kernel(sink_ref, x1_ref, z_ref, kv_ref, lbl_ref, gn_ref, ghg_ref,
                      g_mixpost, g_pre, g_post, wout_ref, wgu_ref, wd_ref,
                      y_ref, s_out_ref, ao_scr, st_scr, *, n_tiles, tiles_per_seq):
    j = pl.program_id(0)
    tile = x1_ref.shape[0]
    pos = jnp.minimum(j, n_tiles - 1) % tiles_per_seq

    @pl.when(j == 0)
    def _():
        ao_scr[...] = jnp.zeros(ao_scr.shape, ao_scr.dtype)
        st_scr[...] = jnp.zeros(st_scr.shape, F32)

    def attn_block(blk):
        rs = slice(blk * WINDOW, (blk + 1) * WINDOW)

        def store(a):
            ao_scr[rs, 0:ATTN_WIDTH] = a.astype(ao_scr.dtype)

        return _attn_block(sink_ref, z_ref[rs, Z_Q], kv_ref, pos * tile + blk * WINDOW, gn_ref[...], store)

    ch = HG_TILE
    _, causal, _ = _segment_masks(1)
    cum_lhs = _twice(_as_bf16(causal))
    c0, c1 = _hgrn_bound_consts(lbl_ref)
    ghg = ghg_ref[...]

    def hgrn_chunk(ci):
        rs = slice(ci * ch, (ci + 1) * ch)
        qa, kx, parts, v = _hgrn_gates(z_ref[rs, Z_HP], c0, c1)
        yield
        g = _dot(cum_lhs, parts)
        yield
        g_last = g[ch - 1:ch, :]
        qt, kt, qg, kd = _hgrn_factors(qa, kx, g, g[ch // 2 - 1:ch // 2, :], g_last)
        decay = jnp.exp2(g_last)
        cols = [slice(h * HG_DK, (h + 1) * HG_DK) for h in range(HG_HEADS)]
        yield
        scores = [_dot_nt(qt[:, cs], kt[:, cs]) for cs in cols]
        yield
        probs = [jnp.where(causal, sc, 0.0).astype(BF16) for sc in scores]
        yield
        outs = []
        for h, cs in enumerate(cols):
            st = st_scr[h]
            if ci == 0:
                st = jnp.where(pos == 0, 0.0, st)
            outs.append(_dot(probs[h], v[:, cs]) + _dot_nt(qg[:, cs], st.astype(BF16)))
            st_scr[h] = st * decay[:, cs] + _dot_tn(v[:, cs], kd[:, cs])
        yield
        gate = _silu_t(z_ref[rs, Z_HG])
        for h, cs in enumerate(cols):
            o = _rms(outs[h], ghg) * gate[:, cs]
            ao_scr[rs, ATTN_WIDTH + h * HG_DV:ATTN_WIDTH + (h + 1) * HG_DV] = o.astype(ao_scr.dtype)

    n_blk, n_chunk = tile // WINDOW, tile // ch
    mixers = []
    for blk in range(n_blk):
        mixers += [hgrn_chunk(ci) for ci in range(blk * n_chunk // n_blk, (blk + 1) * n_chunk // n_blk)]
        mixers.insert(len(mixers) - 1, attn_block(blk))
    side, drain = _staged(mixers, 2 * (2 * len(FF_CHUNKS) + 2))

    def ffn_half(rs):
        mix = _dot(ao_scr[rs, :].astype(BF16), wout_ref[...])
        yield
        x2 = x1_ref[rs, :] + _rms(mix, g_mixpost[...])
        h = _rms(x2, g_pre[...]).astype(BF16)
        acts = []
        for lo, hi in FF_CHUNKS:
            g = _dot(h, wgu_ref[:, lo:hi])
            yield
            u = _dot(h, wgu_ref[:, D_FF + lo:D_FF + hi])
            yield
            acts.append((_silu(g) * u).astype(BF16))
        down = _dot(jnp.concatenate(acts, axis=1), wd_ref[...])
        yield
        y_ref[rs, :] = x2 + 0.5 * _rms(down, g_post[...])

    first, second = ffn_half(slice(0, tile // 2)), ffn_half(slice(tile // 2, tile))
    next(first)
    next(second)
    halves = [first, second]
    while halves:
        for gen in list(halves):
            if next(gen, "done") == "done":
                halves.remove(gen)
            side()
    drain()

    @pl.when((pos == tiles_per_seq - 1) & (j < n_tiles))
    def _():
        b = j // tiles_per_seq
        for h in range(HG_HEADS):
            s_out_ref[pl.ds(b, 1), h] = st_scr[h].T[None]


def _mixer_ffn(x1, kv, z, sinks, g_attn, lb_logits, g_hg, g_mixpost, g_pre, g_post, wout, wgu, wd,
               batch, seq):
    t = batch * seq
    tile = TOKEN_TILE
    assert seq % tile == 0
    n_tiles, tps = t // tile, seq // tile
    cur = lambda j: jnp.minimum(j, n_tiles - 1)
    prev = lambda j: jnp.maximum(j - 1, 0)
    cur_spec = lambda width: pl.BlockSpec((tile, width), lambda j: (cur(j), 0))
    prev_spec = pl.BlockSpec((tile, D_MODEL), lambda j: (prev(j), 0))
    s_shape = (batch, HG_HEADS, HG_DK, HG_DV)
    return pl.pallas_call(
        functools.partial(_mixer_ffn_kernel, n_tiles=n_tiles, tiles_per_seq=tps),
        grid=(n_tiles + 1,),
        in_specs=[pl.BlockSpec(memory_space=pltpu.SMEM),
                  prev_spec, cur_spec(Z_WIDTH),
                  pl.BlockSpec((seq, 2 * KV_WIDTH), lambda j: (cur(j) // tps, 0)),
                  _const_spec(lb_logits.shape), _const_spec(g_attn.shape), _const_spec(g_hg.shape),
                  _const_spec(g_mixpost.shape), _const_spec(g_pre.shape), _const_spec(g_post.shape),
                  _const_spec(wout.shape), _const_spec(wgu.shape), _const_spec(wd.shape)],
        out_specs=[prev_spec, pl.BlockSpec(s_shape, lambda j: (0, 0, 0, 0))],
        out_shape=[jax.ShapeDtypeStruct((t, D_MODEL), F32), jax.ShapeDtypeStruct(s_shape, F32)],
        scratch_shapes=[pltpu.VMEM((tile, ATTN_WIDTH + HG_WIDTH), BF16), pltpu.VMEM((HG_HEADS, HG_DV, HG_DK), F32)],
        compiler_params=pltpu.CompilerParams(dimension_semantics=("arbitrary",),
                                             vmem_limit_bytes=VMEM_LIMIT_BIG),
        name="mixer_ffn",
    )(sinks, x1, z, kv, lb_logits, g_attn, g_hg, g_mixpost, g_pre, g_post, wout, wgu, wd)


def kernel(x_prompt, x_sample, cache_k_win, cache_v_win, state_hgrn, w_in, b_in, attn_sinks, attn_out_norm,
           hg_lb_logits, hg_out_norm, w_out, ffn1_w_gu, ffn1_w_down, ffn2_w_gu, ffn2_w_down,
           norm_ffn1_pre, norm_ffn1_post, norm_mix_pre, norm_mix_post, norm_ffn2_pre, norm_ffn2_post):
    depth = w_in.shape[0]
    assert depth == 1, "single-layer trunk"
    batch, seq, _ = x_prompt.shape
    dec_batch, dec_len, _ = x_sample.shape
    wb = cache_k_win.shape[2]
    assert seq % WINDOW == 0 and wb == WINDOW and HG_TILE % dec_len == 0
    layer = 0
    row = lambda p: p[layer].reshape(1, -1).astype(F32)
    wgu1, wd1 = ffn1_w_gu[layer].astype(BF16), ffn1_w_down[layer].astype(BF16)
    wgu2, wd2 = ffn2_w_gu[layer].astype(BF16), ffn2_w_down[layer].astype(BF16)
    win, wout = w_in[layer].astype(BF16), w_out[layer].astype(BF16)
    sinks = attn_sinks[layer].astype(F32)
    lb_logits = hg_lb_logits.astype(F32)
    g_attn, g_hg = row(attn_out_norm), row(hg_out_norm)
    back_params = (row(norm_mix_post), row(norm_ffn2_pre), row(norm_ffn2_post), wout, wgu2, wd2)

    t_p, t_s = batch * seq, dec_batch * dec_len
    x1, kv, z = _ffn_in(x_prompt.reshape(t_p, D_MODEL), x_sample.reshape(t_s, D_MODEL),
                                row(norm_ffn1_pre), row(norm_ffn1_post), row(norm_mix_pre), wgu1, wd1, win, row(b_in))

    y_p, s_prompt = _mixer_ffn(x1, kv, z, sinks, g_attn, lb_logits, g_hg, *back_params, batch, seq)
    y_prompt = y_p.reshape(batch, seq, D_MODEL)
    kv_last = jnp.stack([kv[(b + 1) * seq - WINDOW:(b + 1) * seq] for b in range(batch)])
    k_prompt = kv_last[..., :KV_WIDTH].reshape(1, batch, WINDOW, N_KV_HEADS, HEAD_DIM)
    v_prompt = kv_last[..., KV_WIDTH:].reshape(1, batch, WINDOW, N_KV_HEADS, HEAD_DIM)

    def feature_major(buf):
        return jnp.transpose(buf, (0, 2, 3, 1)).reshape(dec_batch, KV_WIDTH, wb)

    def window_major(buf_t):
        return jnp.transpose(buf_t.reshape(dec_batch, N_KV_HEADS, HEAD_DIM, wb), (0, 3, 1, 2))[None]

    y_s, k_s, v_s, s_sample = _sample_mixer_ffn(
        x1, kv, z, feature_major(cache_k_win[layer]), feature_major(cache_v_win[layer]),
        state_hgrn[layer].astype(F32), sinks, g_attn, lb_logits, g_hg, *back_params, dec_len, t_p)
    y_sample = y_s.reshape(dec_batch, dec_len, D_MODEL)
    k_sample, v_sample = window_major(k_s), window_major(v_s)

    return (y_prompt, y_sample, k_prompt, v_prompt, s_prompt[None], k_sample, v_sample, s_sample[None])
```

```python
import functools

import jax
import jax.numpy as jnp
from jax import lax
from jax.experimental import pallas as pl
from jax.experimental.pallas import tpu as pltpu

F32 = jnp.float32
BF16 = jnp.bfloat16

D_MODEL = 1024
N_HEADS = 8
N_KV_HEADS = 2
HEAD_DIM = 64
GQA_GROUP = N_HEADS // N_KV_HEADS
WINDOW = 128
ATTN_WIDTH = N_HEADS * HEAD_DIM
KV_WIDTH = N_KV_HEADS * HEAD_DIM
SCALE = HEAD_DIM ** -0.5
HG_HEADS = 4
HG_DK = 128
HG_DV = 128
HG_WIDTH = HG_HEADS * HG_DV
D_FF = 2816
EPS = 1e-6

Q_COLS = (0, ATTN_WIDTH)
KV_COLS = (ATTN_WIDTH, ATTN_WIDTH + 2 * KV_WIDTH)
HP_COLS = (KV_COLS[1], KV_COLS[1] + 3 * HG_WIDTH)
HG_COLS = (HP_COLS[1], HP_COLS[1] + HG_WIDTH)
Z_Q = slice(0, ATTN_WIDTH)
Z_HP = slice(ATTN_WIDTH, ATTN_WIDTH + 3 * HG_WIDTH)
Z_HG = slice(ATTN_WIDTH + 3 * HG_WIDTH, ATTN_WIDTH + 4 * HG_WIDTH)
Z_WIDTH = ATTN_WIDTH + 4 * HG_WIDTH

VMEM_LIMIT_BIG = 58 * 1024 * 1024
FF_CHUNKS = tuple((lo, lo + 256) for lo in range(0, D_FF, 256))
TOKEN_TILE = 512
SAMPLE_TILE = 128
HG_TILE = 64
ATTN_SEQ_GROUP = 4
MIXER_STAGES = 7
FFN_MATMULS = 2 * len(FF_CHUNKS) + 2
NEG = -1e30
LOG2E = 1.4426950408889634


def _rms(x, g):
    return x * lax.rsqrt(jnp.mean(x * x, axis=-1, keepdims=True) + EPS) * g


def _silu(x):
    return x * jax.nn.sigmoid(x)


def _silu_t(x):
    h = 0.5 * x
    return h * jnp.tanh(h) + h


def _dot(a, b):
    return jnp.dot(a, b, preferred_element_type=F32)


def _dot_nt(a, b):
    return lax.dot_general(a, b, (((1,), (1,)), ((), ())), preferred_element_type=F32)


def _dot_tn(a, b):
    return lax.dot_general(a, b, (((0,), (0,)), ((), ())), preferred_element_type=F32)


def _split2(x):
    a = x.astype(BF16)
    b = (x - a.astype(F32)).astype(BF16)
    return jnp.concatenate([a, b], axis=0)


def _twice(sel):
    return jnp.concatenate([sel, sel], axis=1)


def _no_side_work():
    pass


def _swiglu_stages(x, g_pre, wgu_ref, wd_ref, g_post):
    h = _rms(x, g_pre).astype(BF16)
    acts = []
    for lo, hi in FF_CHUNKS:
        g = _dot(h, wgu_ref[:, lo:hi])
        yield
        u = _dot(h, wgu_ref[:, D_FF + lo:D_FF + hi])
        yield
        acts.append((_silu(g) * u).astype(BF16))
    down = _dot(jnp.concatenate(acts, axis=1), wd_ref[...])
    yield
    return x + 0.5 * _rms(down, g_post)


def _round_robin(gens, side=_no_side_work):
    alive = list(gens)
    while alive:
        for gen in list(alive):
            if next(gen, "done") == "done":
                alive.remove(gen)
            side()


def _const_spec(shape):
    zeros = (0,) * len(shape)
    return pl.BlockSpec(shape, lambda *_: zeros, pipeline_mode=pl.Buffered(1))


def _row_spec(tile, width):
    return pl.BlockSpec((tile, width), lambda i: (i, 0))


def _ffn_in_kernel(xa_ref, xb_ref, g_pre, g_post, g_mix, wgu_ref, wd_ref, win_ref, bin_ref,
                   x1_ref, kv_ref, z_ref, *, tiles_a):
    from_a = pl.program_id(0) < tiles_a

    def half(rs):
        x = jnp.where(from_a, xa_ref[rs, :], xb_ref[rs, :])
        x1 = yield from _swiglu_stages(x, g_pre[...], wgu_ref, wd_ref, g_post[...])
        x1_ref[rs, :] = x1
        h = _rms(x1, g_mix[...]).astype(BF16)
        yield
        qkv = _dot(h, win_ref[:, Q_COLS[0]:KV_COLS[1]]) + bin_ref[:, Q_COLS[0]:KV_COLS[1]]
        z_ref[rs, Z_Q] = qkv[:, 0:ATTN_WIDTH]
        kv_ref[rs, :] = qkv[:, ATTN_WIDTH:ATTN_WIDTH + 2 * KV_WIDTH]
        yield
        z_ref[rs, Z_HP.start:Z_HG.stop] = (_dot(h, win_ref[:, HP_COLS[0]:HG_COLS[1]])
                                           + bin_ref[:, HP_COLS[0]:HG_COLS[1]])

    rows = x1_ref.shape[0] // 2
    first, second = half(slice(0, rows)), half(slice(rows, 2 * rows))
    next(first)
    _round_robin([first, second])


def _ffn_in(xa, xb, g_pre, g_post, g_mix, wgu, wd, win, b_in):
    tile = TOKEN_TILE
    assert xa.shape[0] % tile == 0 and xb.shape[0] % tile == 0
    tiles_a, tiles_b = xa.shape[0] // tile, xb.shape[0] // tile
    n_tiles = tiles_a + tiles_b
    widths = [D_MODEL, 2 * KV_WIDTH, Z_WIDTH]
    return pl.pallas_call(
        functools.partial(_ffn_in_kernel, tiles_a=tiles_a),
        grid=(n_tiles,),
        in_specs=[pl.BlockSpec((tile, D_MODEL), lambda j: (jnp.minimum(j, tiles_a - 1), 0)),
                  pl.BlockSpec((tile, D_MODEL), lambda j: (jnp.maximum(j - tiles_a, 0), 0)),
                  _const_spec(g_pre.shape), _const_spec(g_post.shape),
                  _const_spec(g_mix.shape), _const_spec(wgu.shape), _const_spec(wd.shape),
                  _const_spec(win.shape), _const_spec(b_in.shape)],
        out_specs=[_row_spec(tile, w) for w in widths],
        out_shape=[jax.ShapeDtypeStruct((n_tiles * tile, w), F32) for w in widths],
        compiler_params=pltpu.CompilerParams(dimension_semantics=("arbitrary",),
                                             vmem_limit_bytes=VMEM_LIMIT_BIG),
        name="ffn_in",
    )(xa, xb, g_pre, g_post, g_mix, wgu, wd, win, b_in)


def _out_ffn_stages(rs, ao_scr, x1_ref, y_ref, g_mixpost, g_pre, g_post, wout_ref, wgu_ref, wd_ref):
    mix = _dot(ao_scr[rs, :].astype(BF16), wout_ref[...])
    yield
    x2 = x1_ref[rs, :] + _rms(mix, g_mixpost[...])
    y_ref[rs, :] = yield from _swiglu_stages(x2, g_pre[...], wgu_ref, wd_ref, g_post[...])


def _staged(mixers, n_slots):
    waiting, running = list(mixers), []
    start_slot = [k * (n_slots - MIXER_STAGES) // len(waiting) for k in range(len(waiting))]
    slot = [0]

    def side():
        while waiting and start_slot[len(start_slot) - len(waiting)] <= slot[0]:
            running.append(waiting.pop(0))
        slot[0] += 1
        for gen in list(running):
            if next(gen, "done") == "done":
                running.remove(gen)

    def drain():
        while waiting or running:
            side()

    return side, drain


def _attn_block(sink_ref, q, kv_ref, q0, gn, store):
    w = WINDOW
    nkeys = 2 * w
    r = lax.broadcasted_iota(jnp.int32, (nkeys, w), 0)
    c = lax.broadcasted_iota(jnp.int32, (nkeys, w), 1)
    first_half = lax.broadcasted_iota(jnp.int32, (w, 2 * HEAD_DIM), 1) < HEAD_DIM
    k0 = jnp.maximum(q0 - w, 0)
    dist = c + (q0 - k0) - r
    bias = jnp.where((dist >= 0) & (dist <= WINDOW), 0.0, NEG)
    kvb = kv_ref[pl.ds(pl.multiple_of(k0, w), nkeys), :]
    kk = kvb[:, 0:KV_WIDTH]
    k_nat = kk.astype(BF16)
    k_swp = pltpu.roll(kk, HEAD_DIM, axis=1).astype(BF16)
    vt = kvb[:, KV_WIDTH:2 * KV_WIDTH].T.astype(BF16)

    def masked_q(h):
        qp = q[:, (h // 2) * 2 * HEAD_DIM:(h // 2 + 1) * 2 * HEAD_DIM]
        return jnp.where(first_half != bool(h % 2), qp, 0.0).astype(BF16)

    groups = []
    for keys, use_swapped in ((k_nat, False), (k_swp, True)):
        heads = [h for h in range(N_HEADS) if (h // GQA_GROUP != h % 2) == use_swapped]
        groups.append((keys, heads, jnp.concatenate([masked_q(h) for h in heads], axis=0)))
    yield
    scores = [None] * N_HEADS
    for keys, heads, qs in groups:
        st = _dot_nt(keys, qs)
        for i, h in enumerate(heads):
            scores[h] = st[:, i * w:(i + 1) * w]
    yield
    probs, dens = [], []
    for h in range(N_HEADS):
        st = scores[h] * (SCALE * LOG2E) + bias
        sink2 = sink_ref[h] * LOG2E
        m = jnp.maximum(jnp.max(st, axis=0, keepdims=True), sink2)
        p = jnp.exp2(st - m)
        dens.append(jnp.sum(p, axis=0, keepdims=True) + jnp.exp2(sink2 - m))
        probs.append(p.astype(BF16))
        if h % GQA_GROUP == GQA_GROUP - 1:
            yield
    ot = _dot(vt, jnp.concatenate(probs, axis=1))
    yield
    rows = []
    for h in range(N_HEADS):
        hk = h // GQA_GROUP
        rows.append(ot[hk * HEAD_DIM:(hk + 1) * HEAD_DIM, h * w:(h + 1) * w] / dens[h])
    at = jnp.concatenate(rows, axis=0)
    inv = lax.rsqrt(jnp.sum(at * at, axis=0, keepdims=True) * (1.0 / ATTN_WIDTH) + EPS)
    store((at * inv).T * gn)


def _softmax_sink_pv(scores, masks, values, sink):
    scores = [jnp.where(mk, s * SCALE, NEG) for s, mk in zip(scores, masks)]
    m = sink
    for s in scores:
        m = jnp.maximum(m, jnp.max(s, axis=-1, keepdims=True))
    den = jnp.exp(sink - m)
    out = None
    for s, (v, transposed) in zip(scores, values):
        p = jnp.exp(s - m)
        den = den + jnp.sum(p, axis=-1, keepdims=True)
        pv = (_dot_nt if transposed else _dot)(p.astype(BF16), v)
        out = pv if out is None else out + pv
    return out / den


def _attn_sample_group(sink_ref, z_ref, kvn_ref, ck_ref, cv_ref, gn_ref, a_ref, ko_ref, vo_ref, grp, dec_len):
    ns = ATTN_SEQ_GROUP
    rows = ns * dec_len
    wb = ck_ref.shape[2]
    m_rows = GQA_GROUP * rows
    r = lax.broadcasted_iota(jnp.int32, (m_rows, ns * wb), 0)
    c = lax.broadcasted_iota(jnp.int32, (m_rows, ns * wb), 1)
    r_seq, r_pos = (r % rows) // dec_len, r % dec_len
    mask_cache = (r_seq == c // wb) & (c % wb >= r_pos)
    r = lax.broadcasted_iota(jnp.int32, (m_rows, rows), 0)
    c = lax.broadcasted_iota(jnp.int32, (m_rows, rows), 1)
    mask_new = ((r % rows) // dec_len == c // dec_len) & (c % dec_len <= r % dec_len)
    row_head = lax.broadcasted_iota(jnp.int32, (m_rows, 1), 0) // rows

    rs = slice(grp * rows, (grp + 1) * rows)
    q = z_ref[rs, Z_Q]
    kvn_b = kvn_ref[rs, :].astype(BF16)

    def cached(ref, hk):
        frows = slice(hk * HEAD_DIM, (hk + 1) * HEAD_DIM)
        return jnp.concatenate([ref[grp * ns + n, frows, :] for n in range(ns)], axis=1).astype(BF16)

    scores = []
    for hk in range(N_KV_HEADS):
        kcols = slice(hk * HEAD_DIM, (hk + 1) * HEAD_DIM)
        heads = [hk * GQA_GROUP + g for g in range(GQA_GROUP)]
        qs = jnp.concatenate([q[:, h * HEAD_DIM:(h + 1) * HEAD_DIM] for h in heads], axis=0).astype(BF16)
        scores.append([_dot(qs, cached(ck_ref, hk)), _dot_nt(qs, kvn_b[:, kcols])])
    yield
    outs = []
    for hk in range(N_KV_HEADS):
        vcols = slice(KV_WIDTH + hk * HEAD_DIM, KV_WIDTH + (hk + 1) * HEAD_DIM)
        sink = jnp.zeros((m_rows, 1), F32)
        for g in range(GQA_GROUP):
            sink = jnp.where(row_head == g, sink_ref[hk * GQA_GROUP + g], sink)
        values = [(cached(cv_ref, hk), True), (kvn_b[:, vcols], False)]
        outs.append(_softmax_sink_pv(scores[hk], [mask_cache, mask_new], values, sink))
    yield
    for hk in range(N_KV_HEADS):
        for g in range(GQA_GROUP):
            h = hk * GQA_GROUP + g
            a_ref[rs, h * HEAD_DIM:(h + 1) * HEAD_DIM] = outs[hk][g * rows:(g + 1) * rows, :]
    a_ref[rs, :] = _rms(a_ref[rs, :], gn_ref[...])
    keep = wb - dec_len
    is_new = lax.broadcasted_iota(jnp.int32, (KV_WIDTH, wb), 1) >= keep
    for n in range(ns):
        s = grp * ns + n
        ts = slice(grp * rows + n * dec_len, grp * rows + (n + 1) * dec_len)
        for src, dst, cols in ((ck_ref, ko_ref, slice(0, KV_WIDTH)), (cv_ref, vo_ref, slice(KV_WIDTH, 2 * KV_WIDTH))):
            new_t = jnp.concatenate([jnp.zeros((keep, KV_WIDTH), F32), kvn_ref[ts, cols]], axis=0).T
            dst[s] = jnp.where(is_new, new_t, pltpu.roll(src[s], keep, axis=1))


def _hgrn_bound_consts(lbl_ref):
    lbl = lbl_ref[...]
    e = jnp.exp(lbl - jnp.max(lbl, axis=0, keepdims=True))
    lb = e[0:1, :] / jnp.sum(e, axis=0, keepdims=True)
    return 0.5 + 0.5 * lb, 0.5 - 0.5 * lb


def _hgrn_gates(hp, c0, c1):
    qa = _silu_t(hp[:, 0:HG_WIDTH])
    t = c1 * jnp.tanh(0.5 * hp[:, HG_WIDTH:2 * HG_WIDTH])
    f, kx = c0 + t, c1 - t
    v = hp[:, 2 * HG_WIDTH:3 * HG_WIDTH].astype(BF16)
    return qa, kx, _split2(jnp.log2(f)), v


def _hgrn_factors(qa, kx, g, g_mid, g_last):
    qt = qa * jnp.exp2(g - g_mid)
    kt = kx * jnp.exp2(g_mid - g)
    qg = (qt * jnp.exp2(g_mid)).astype(BF16)
    kd = (kt * jnp.exp2(g_last - g_mid)).astype(BF16)
    return qt.astype(BF16), kt.astype(BF16), qg, kd


def _segment_masks(nseq):
    ls = HG_TILE // nseq
    r = lax.broadcasted_iota(jnp.int32, (HG_TILE, HG_TILE), 0)
    c = lax.broadcasted_iota(jnp.int32, (HG_TILE, HG_TILE), 1)
    same = (r // ls) == (c // ls)
    return same, same & (c <= r), same & (c % ls < ls // 2)


def _as_bf16(mask):
    return mask.astype(F32).astype(BF16)


def _hgrn_short_tile(z_ref, c0, c1, gn, s_in, s_out, o_ref, ti, nseq):
    ch = HG_TILE
    ls = ch // nseq
    same, causal, first_half = _segment_masks(nseq)
    cum_lhs = _twice(_as_bf16(jnp.concatenate([causal, first_half, same], axis=0)))
    sr = lax.broadcasted_iota(jnp.int32, (ch, nseq * HG_DV), 0)
    sc = lax.broadcasted_iota(jnp.int32, (ch, nseq * HG_DV), 1)
    seg_sel = _as_bf16((sr // ls) == (sc // HG_DV))
    seg_sel2 = jnp.concatenate([seg_sel, seg_sel], axis=0)
    rows = slice(ti * ch, (ti + 1) * ch)

    qa, kx, parts, v = _hgrn_gates(z_ref[rows, Z_HP], c0, c1)
    gs = _dot(cum_lhs, parts)
    dcol = _dot_tn(parts, seg_sel2)
    yield
    qt, kt, qg, kd = _hgrn_factors(qa, kx, gs[0:ch], gs[ch:2 * ch], gs[2 * ch:3 * ch])
    cols = [slice(h * HG_DK, (h + 1) * HG_DK) for h in range(HG_HEADS)]
    scores = [_dot_nt(qt[:, cs], kt[:, cs]) for cs in cols]
    yield
    outs = []
    for h, cs in enumerate(cols):
        a = jnp.where(causal, scores[h], 0.0)
        o = _dot(a.astype(BF16), v[:, cs])
        inter = []
        for n in range(nseq):
            rs = slice(n * ls, (n + 1) * ls)
            s = s_in[ti * nseq + n, h]
            inter.append(_dot(qg[rs, cs], s.astype(BF16)))
            decay = jnp.exp2(dcol[cs, n * HG_DV:(n + 1) * HG_DV])
            s_out[ti * nseq + n, h] = s * decay + _dot_tn(kd[rs, cs], v[rs, cs])
        outs.append(o + jnp.concatenate(inter, axis=0))
    yield
    gate = _silu_t(z_ref[rows, Z_HG])
    for h, cs in enumerate(cols):
        o_ref[rows, cs] = _rms(outs[h], gn) * gate[:, cs]


def _sample_mixer_ffn_kernel(sink_ref, x1_ref, z_ref, kvn_ref, ck_ref, cv_ref, s_in_ref,
                             lbl_ref, gn_ref, ghg_ref, g_mixpost, g_pre, g_post, wout_ref, wgu_ref, wd_ref,
                             y_ref, ko_ref, vo_ref, s_out_ref, ao_scr, *, dec_len):
    j = pl.program_id(0)
    tile = x1_ref.shape[0]

    @pl.when(j == 0)
    def _():
        ao_scr[...] = jnp.zeros(ao_scr.shape, F32)

    a_scr, o_scr = ao_scr.at[:, 0:ATTN_WIDTH], ao_scr.at[:, ATTN_WIDTH:ATTN_WIDTH + HG_WIDTH]
    c0, c1 = _hgrn_bound_consts(lbl_ref)
    nseq = HG_TILE // dec_len
    mixers = [_attn_sample_group(sink_ref, z_ref, kvn_ref, ck_ref, cv_ref, gn_ref, a_scr, ko_ref, vo_ref, grp, dec_len)
              for grp in range(tile // (ATTN_SEQ_GROUP * dec_len))]
    mixers += [_hgrn_short_tile(z_ref, c0, c1, ghg_ref[...], s_in_ref, s_out_ref, o_scr, ti, nseq)
               for ti in range(tile // HG_TILE)]
    side, drain = _staged(mixers, FFN_MATMULS)

    ffn = _out_ffn_stages(slice(0, tile), ao_scr, x1_ref, y_ref, g_mixpost, g_pre, g_post, wout_ref, wgu_ref, wd_ref)
    _round_robin([ffn], side)
    drain()


def _sample_mixer_ffn(x1, kv, z, cache_k, cache_v, state, sinks, g_attn, lb_logits, g_hg,
                      g_mixpost, g_pre, g_post, wout, wgu, wd, dec_len, row0):
    tile = SAMPLE_TILE
    t = cache_k.shape[0] * dec_len
    n_tiles = t // tile
    spt = tile // dec_len
    assert t % tile == 0 and row0 % tile == 0 and tile % HG_TILE == 0 and tile % (ATTN_SEQ_GROUP * dec_len) == 0
    cur = lambda j: jnp.minimum(j, n_tiles - 1)
    prev = lambda j: jnp.maximum(j - 1, 0)
    cur_spec = lambda width: pl.BlockSpec((tile, width), lambda j: (row0 // tile + cur(j), 0))
    cache_spec = pl.BlockSpec((spt,) + cache_k.shape[1:], lambda j: (cur(j), 0, 0))
    state_spec = pl.BlockSpec((spt,) + state.shape[1:], lambda j: (cur(j), 0, 0, 0))
    return pl.pallas_call(
        functools.partial(_sample_mixer_ffn_kernel, dec_len=dec_len),
        grid=(n_tiles + 1,),
        in_specs=[pl.BlockSpec(memory_space=pltpu.SMEM),
                  pl.BlockSpec((tile, D_MODEL), lambda j: (row0 // tile + prev(j), 0)),
                  cur_spec(Z_WIDTH), cur_spec(2 * KV_WIDTH), cache_spec, cache_spec, state_spec,
                  _const_spec(lb_logits.shape), _const_spec(g_attn.shape), _const_spec(g_hg.shape),
                  _const_spec(g_mixpost.shape), _const_spec(g_pre.shape), _const_spec(g_post.shape),
                  _const_spec(wout.shape), _const_spec(wgu.shape), _const_spec(wd.shape)],
        out_specs=[pl.BlockSpec((tile, D_MODEL), lambda j: (prev(j), 0)), cache_spec, cache_spec, state_spec],
        out_shape=[jax.ShapeDtypeStruct((t, D_MODEL), F32), jax.ShapeDtypeStruct(cache_k.shape, F32),
                   jax.ShapeDtypeStruct(cache_v.shape, F32), jax.ShapeDtypeStruct(state.shape, F32)],
        scratch_shapes=[pltpu.VMEM((tile, ATTN_WIDTH + HG_WIDTH), F32)],
        compiler_params=pltpu.CompilerParams(dimension_semantics=("arbitrary",),
                                             vmem_limit_bytes=VMEM_LIMIT_BIG),
        name="sample_mixer_ffn",
    )(sinks, x1, z, kv, cache_k, cache_v, state, lb_logits, g_attn, g_hg,
      g_mixpost, g_pre, g_post, wout, wgu, wd)


def _mixer_ffn_kernel(sink_ref, x1_ref, z_ref, kv_ref, lbl_ref, gn_ref, ghg_ref,
                      g_mixpost, g_pre, g_post, wout_ref, wgu_ref, wd_ref,
                      y_ref, s_out_ref, ao_scr, st_scr, *, n_tiles, tiles_per_seq):
    j = pl.program_id(0)
    tile = x1_ref.shape[0]
    pos = jnp.minimum(j, n_tiles - 1) % tiles_per_seq

    @pl.when(j == 0)
    def _():
        ao_scr[...] = jnp.zeros(ao_scr.shape, ao_scr.dtype)
        st_scr[...] = jnp.zeros(st_scr.shape, F32)

    def attn_block(blk):
        rs = slice(blk * WINDOW, (blk + 1) * WINDOW)

        def store(a):
            ao_scr[rs, 0:ATTN_WIDTH] = a.astype(ao_scr.dtype)

        return _attn_block(sink_ref, z_ref[rs, Z_Q], kv_ref, pos * tile + blk * WINDOW, gn_ref[...], store)

    ch = HG_TILE
    _, causal, _ = _segment_masks(1)
    cum_lhs = _twice(_as_bf16(causal))
    c0, c1 = _hgrn_bound_consts(lbl_ref)
    ghg = ghg_ref[...]

    def hgrn_chunk(ci):
        rs = slice(ci * ch, (ci + 1) * ch)
        qa, kx, parts, v = _hgrn_gates(z_ref[rs, Z_HP], c0, c1)
        yield
        g = _dot(cum_lhs, parts)
        yield
        g_last = g[ch - 1:ch, :]
        qt, kt, qg, kd = _hgrn_factors(qa, kx, g, g[ch // 2 - 1:ch // 2, :], g_last)
        decay = jnp.exp2(g_last)
        cols = [slice(h * HG_DK, (h + 1) * HG_DK) for h in range(HG_HEADS)]
        yield
        scores = [_dot_nt(qt[:, cs], kt[:, cs]) for cs in cols]
        yield
        probs = [jnp.where(causal, sc, 0.0).astype(BF16) for sc in scores]
        yield
        outs = []
        for h, cs in enumerate(cols):
            st = st_scr[h]
            if ci == 0:
                st = jnp.where(pos == 0, 0.0, st)
            outs.append(_dot(probs[h], v[:, cs]) + _dot_nt(qg[:, cs], st.astype(BF16)))
            st_scr[h] = st * decay[:, cs] + _dot_tn(v[:, cs], kd[:, cs])
        yield
        gate = _silu_t(z_ref[rs, Z_HG])
        for h, cs in enumerate(cols):
            o = _rms(outs[h], ghg) * gate[:, cs]
            ao_scr[rs, ATTN_WIDTH + h * HG_DV:ATTN_WIDTH + (h + 1) * HG_DV] = o.astype(ao_scr.dtype)

    n_blk, n_chunk = tile // WINDOW, tile // ch
    mixers = []
    for blk in range(n_blk):
        mixers += [hgrn_chunk(ci) for ci in range(blk * n_chunk // n_blk, (blk + 1) * n_chunk // n_blk)]
        mixers.insert(len(mixers) - 1, attn_block(blk))
    side, drain = _staged(mixers, 2 * FFN_MATMULS)

    halves = [_out_ffn_stages(rs, ao_scr, x1_ref, y_ref, g_mixpost, g_pre, g_post, wout_ref, wgu_ref, wd_ref)
              for rs in (slice(0, tile // 2), slice(tile // 2, tile))]
    for gen in halves:
        next(gen)
    _round_robin(halves, side)
    drain()

    @pl.when((pos == tiles_per_seq - 1) & (j < n_tiles))
    def _():
        b = j // tiles_per_seq
        for h in range(HG_HEADS):
            s_out_ref[pl.ds(b, 1), h] = st_scr[h].T[None]


def _mixer_ffn(x1, kv, z, sinks, g_attn, lb_logits, g_hg, g_mixpost, g_pre, g_post, wout, wgu, wd,
               batch, seq):
    t = batch * seq
    tile = TOKEN_TILE
    assert seq % tile == 0
    n_tiles, tps = t // tile, seq // tile
    cur = lambda j: jnp.minimum(j, n_tiles - 1)
    prev = lambda j: jnp.maximum(j - 1, 0)
    cur_spec = lambda width: pl.BlockSpec((tile, width), lambda j: (cur(j), 0))
    prev_spec = pl.BlockSpec((tile, D_MODEL), lambda j: (prev(j), 0))
    s_shape = (batch, HG_HEADS, HG_DK, HG_DV)
    return pl.pallas_call(
        functools.partial(_mixer_ffn_kernel, n_tiles=n_tiles, tiles_per_seq=tps),
        grid=(n_tiles + 1,),
        in_specs=[pl.BlockSpec(memory_space=pltpu.SMEM),
                  prev_spec, cur_spec(Z_WIDTH),
                  pl.BlockSpec((seq, 2 * KV_WIDTH), lambda j: (cur(j) // tps, 0)),
                  _const_spec(lb_logits.shape), _const_spec(g_attn.shape), _const_spec(g_hg.shape),
                  _const_spec(g_mixpost.shape), _const_spec(g_pre.shape), _const_spec(g_post.shape),
                  _const_spec(wout.shape), _const_spec(wgu.shape), _const_spec(wd.shape)],
        out_specs=[prev_spec, pl.BlockSpec(s_shape, lambda j: (0, 0, 0, 0))],
        out_shape=[jax.ShapeDtypeStruct((t, D_MODEL), F32), jax.ShapeDtypeStruct(s_shape, F32)],
        scratch_shapes=[pltpu.VMEM((tile, ATTN_WIDTH + HG_WIDTH), BF16), pltpu.VMEM((HG_HEADS, HG_DV, HG_DK), F32)],
        compiler_params=pltpu.CompilerParams(dimension_semantics=("arbitrary",),
                                             vmem_limit_bytes=VMEM_LIMIT_BIG),
        name="mixer_ffn",
    )(sinks, x1, z, kv, lb_logits, g_attn, g_hg, g_mixpost, g_pre, g_post, wout, wgu, wd)


def kernel(x_prompt, x_sample, cache_k_win, cache_v_win, state_hgrn, w_in, b_in, attn_sinks, attn_out_norm,
           hg_lb_logits, hg_out_norm, w_out, ffn1_w_gu, ffn1_w_down, ffn2_w_gu, ffn2_w_down,
           norm_ffn1_pre, norm_ffn1_post, norm_mix_pre, norm_mix_post, norm_ffn2_pre, norm_ffn2_post):
    depth = w_in.shape[0]
    assert depth == 1, "single-layer trunk"
    batch, seq, _ = x_prompt.shape
    dec_batch, dec_len, _ = x_sample.shape
    wb = cache_k_win.shape[2]
    assert seq % WINDOW == 0 and wb == WINDOW and HG_TILE % dec_len == 0
    layer = 0
    row = lambda p: p[layer].reshape(1, -1).astype(F32)
    wgu1, wd1 = ffn1_w_gu[layer].astype(BF16), ffn1_w_down[layer].astype(BF16)
    wgu2, wd2 = ffn2_w_gu[layer].astype(BF16), ffn2_w_down[layer].astype(BF16)
    win, wout = w_in[layer].astype(BF16), w_out[layer].astype(BF16)
    sinks = attn_sinks[layer].astype(F32)
    lb_logits = hg_lb_logits.astype(F32)
    g_attn, g_hg = row(attn_out_norm), row(hg_out_norm)
    back_params = (row(norm_mix_post), row(norm_ffn2_pre), row(norm_ffn2_post), wout, wgu2, wd2)

    t_p, t_s = batch * seq, dec_batch * dec_len
    x1, kv, z = _ffn_in(x_prompt.reshape(t_p, D_MODEL), x_sample.reshape(t_s, D_MODEL),
                                row(norm_ffn1_pre), row(norm_ffn1_post), row(norm_mix_pre), wgu1, wd1, win, row(b_in))

    y_p, s_prompt = _mixer_ffn(x1, kv, z, sinks, g_attn, lb_logits, g_hg, *back_params, batch, seq)
    y_prompt = y_p.reshape(batch, seq, D_MODEL)
    kv_last = jnp.stack([kv[(b + 1) * seq - WINDOW:(b + 1) * seq] for b in range(batch)])
    k_prompt = kv_last[..., :KV_WIDTH].reshape(1, batch, WINDOW, N_KV_HEADS, HEAD_DIM)
    v_prompt = kv_last[..., KV_WIDTH:].reshape(1, batch, WINDOW, N_KV_HEADS, HEAD_DIM)

    def feature_major(buf):
        return jnp.transpose(buf, (0, 2, 3, 1)).reshape(dec_batch, KV_WIDTH, wb)

    def window_major(buf_t):
        return jnp.transpose(buf_t.reshape(dec_batch, N_KV_HEADS, HEAD_DIM, wb), (0, 3, 1, 2))[None]

    y_s, k_s, v_s, s_sample = _sample_mixer_ffn(
        x1, kv, z, feature_major(cache_k_win[layer]), feature_major(cache_v_win[layer]),
        state_hgrn[layer].astype(F32), sinks, g_attn, lb_logits, g_hg, *back_params, dec_len, t_p)
    y_sample = y_s.reshape(dec_batch, dec_len, D_MODEL)
    k_sample, v_sample = window_major(k_s), window_major(v_s)

    return (y_prompt, y_sample, k_prompt, v_prompt, s_prompt[None], k_sample, v_sample, s_sample[None])
```

```python
import functools

import jax
import jax.numpy as jnp
from jax import lax
from jax.experimental import pallas as pl
from jax.experimental.pallas import tpu as pltpu

F32 = jnp.float32
BF16 = jnp.bfloat16

D_MODEL = 1024
N_HEADS = 8
N_KV_HEADS = 2
HEAD_DIM = 64
GQA_GROUP = N_HEADS // N_KV_HEADS
WINDOW = 128
ATTN_WIDTH = N_HEADS * HEAD_DIM
KV_WIDTH = N_KV_HEADS * HEAD_DIM
SCALE = HEAD_DIM ** -0.5
HG_HEADS = 4
HG_DK = 128
HG_DV = 128
HG_WIDTH = HG_HEADS * HG_DV
D_FF = 2816
EPS = 1e-6

Q_COLS = (0, ATTN_WIDTH)
KV_COLS = (ATTN_WIDTH, ATTN_WIDTH + 2 * KV_WIDTH)
HP_COLS = (KV_COLS[1], KV_COLS[1] + 3 * HG_WIDTH)
HG_COLS = (HP_COLS[1], HP_COLS[1] + HG_WIDTH)
Z_Q = slice(0, ATTN_WIDTH)
Z_HP = slice(ATTN_WIDTH, ATTN_WIDTH + 3 * HG_WIDTH)
Z_HG = slice(ATTN_WIDTH + 3 * HG_WIDTH, ATTN_WIDTH + 4 * HG_WIDTH)
Z_WIDTH = ATTN_WIDTH + 4 * HG_WIDTH

VMEM_LIMIT_BIG = 58 * 1024 * 1024
FF_CHUNKS = tuple((lo, lo + 256) for lo in range(0, D_FF, 256))
TOKEN_TILE = 512
SAMPLE_TILE = 128
HG_TILE = 64
ATTN_SEQ_GROUP = 4
MIXER_STAGES = 7
FFN_MATMULS = 2 * len(FF_CHUNKS) + 2
BF16_SUBLANES = 16
NEG = -1e30
LOG2E = 1.4426950408889634


def _rms(x, g):
    return x * lax.rsqrt(jnp.mean(x * x, axis=-1, keepdims=True) + EPS) * g


def _silu(x):
    return x * jax.nn.sigmoid(x)


def _silu_t(x):
    h = 0.5 * x
    return h * jnp.tanh(h) + h


def _dot(a, b):
    return jnp.dot(a, b, preferred_element_type=F32)


def _dot_nt(a, b):
    return lax.dot_general(a, b, (((1,), (1,)), ((), ())), preferred_element_type=F32)


def _dot_tn(a, b):
    return lax.dot_general(a, b, (((0,), (0,)), ((), ())), preferred_element_type=F32)


def _split2(x):
    a = x.astype(BF16)
    b = (x - a.astype(F32)).astype(BF16)
    return jnp.concatenate([a, b], axis=0)


def _twice(sel):
    return jnp.concatenate([sel, sel], axis=1)


def _no_side_work():
    pass


def _swiglu_stages(x, g_pre, wgu_ref, wd_ref, g_post):
    h = _rms(x, g_pre).astype(BF16)
    acts = []
    for lo, hi in FF_CHUNKS:
        g = _dot(h, wgu_ref[:, lo:hi])
        yield
        u = _dot(h, wgu_ref[:, D_FF + lo:D_FF + hi])
        yield
        acts.append((_silu(g) * u).astype(BF16))
    down = _dot(jnp.concatenate(acts, axis=1), wd_ref[...])
    yield
    return x + 0.5 * _rms(down, g_post)


def _round_robin(gens, side=_no_side_work):
    alive = list(gens)
    while alive:
        for gen in list(alive):
            if next(gen, "done") == "done":
                alive.remove(gen)
            side()


def _const_spec(shape):
    zeros = (0,) * len(shape)
    return pl.BlockSpec(shape, lambda *_: zeros, pipeline_mode=pl.Buffered(1))


def _row_spec(tile, width):
    return pl.BlockSpec((tile, width), lambda i: (i, 0))


def _ffn_in_kernel(xa_ref, xb_ref, g_pre, g_post, g_mix, wgu_ref, wd_ref, win_ref, bin_ref, *rest, tiles_a):
    n_later = (len(rest) - 3) // 2
    later_f32, (x1_ref, kv_ref, z_ref), later_bf16 = rest[:n_later], rest[n_later:n_later + 3], rest[n_later + 3:]
    from_a = pl.program_id(0) < tiles_a

    def cast_later_weights():
        for src, dst in zip(later_f32, later_bf16):
            yield
            dst[...] = src[...].astype(BF16)

    def half(rs):
        x = jnp.where(from_a, xa_ref[rs, :], xb_ref[rs, :])
        x1 = yield from _swiglu_stages(x, g_pre[...], wgu_ref, wd_ref, g_post[...])
        x1_ref[rs, :] = x1
        h = _rms(x1, g_mix[...]).astype(BF16)
        yield
        qkv = _dot(h, win_ref[:, Q_COLS[0]:KV_COLS[1]]) + bin_ref[:, Q_COLS[0]:KV_COLS[1]]
        z_ref[rs, Z_Q] = qkv[:, 0:ATTN_WIDTH]
        kv_ref[rs, :] = qkv[:, ATTN_WIDTH:ATTN_WIDTH + 2 * KV_WIDTH]
        yield
        z_ref[rs, Z_HP.start:Z_HG.stop] = (_dot(h, win_ref[:, HP_COLS[0]:HG_COLS[1]])
                                           + bin_ref[:, HP_COLS[0]:HG_COLS[1]])

    rows = x1_ref.shape[0] // 2
    first, second = half(slice(0, rows)), half(slice(rows, 2 * rows))
    next(first)
    _round_robin([first, second, cast_later_weights()])


def _chunk_spec(shape, n_steps):
    rows, width = shape
    n_chunks = max(d for d in range(1, n_steps + 1) if rows % (d * BF16_SUBLANES) == 0)
    return pl.BlockSpec((rows // n_chunks, width), lambda j: (jnp.minimum(j, n_chunks - 1), 0))


def _ffn_in(xa, xb, g_pre, g_post, g_mix, wgu, wd, win, b_in, later_weights):
    tile = TOKEN_TILE
    assert xa.shape[0] % tile == 0 and xb.shape[0] % tile == 0
    tiles_a, tiles_b = xa.shape[0] // tile, xb.shape[0] // tile
    n_tiles = tiles_a + tiles_b
    widths = [D_MODEL, 2 * KV_WIDTH, Z_WIDTH]
    later_specs = [_chunk_spec(w.shape, n_tiles) for w in later_weights]
    x1, kv, z, *later_bf16 = pl.pallas_call(
        functools.partial(_ffn_in_kernel, tiles_a=tiles_a),
        grid=(n_tiles,),
        in_specs=[pl.BlockSpec((tile, D_MODEL), lambda j: (jnp.minimum(j, tiles_a - 1), 0)),
                  pl.BlockSpec((tile, D_MODEL), lambda j: (jnp.maximum(j - tiles_a, 0), 0)),
                  _const_spec(g_pre.shape), _const_spec(g_post.shape),
                  _const_spec(g_mix.shape), _const_spec(wgu.shape), _const_spec(wd.shape),
                  _const_spec(win.shape), _const_spec(b_in.shape)] + later_specs,
        out_specs=[_row_spec(tile, w) for w in widths] + later_specs,
        out_shape=([jax.ShapeDtypeStruct((n_tiles * tile, w), F32) for w in widths]
                   + [jax.ShapeDtypeStruct(w.shape, BF16) for w in later_weights]),
        compiler_params=pltpu.CompilerParams(dimension_semantics=("arbitrary",),
                                             vmem_limit_bytes=VMEM_LIMIT_BIG),
        name="ffn_in",
    )(xa, xb, g_pre, g_post, g_mix, wgu, wd, win, b_in, *later_weights)
    return x1, kv, z, later_bf16


def _out_ffn_stages(rs, ao_scr, x1_ref, y_ref, g_mixpost, g_pre, g_post, wout_ref, wgu_ref, wd_ref):
    mix = _dot(ao_scr[rs, :].astype(BF16), wout_ref[...])
    yield
    x2 = x1_ref[rs, :] + _rms(mix, g_mixpost[...])
    y_ref[rs, :] = yield from _swiglu_stages(x2, g_pre[...], wgu_ref, wd_ref, g_post[...])


def _staged(mixers, n_slots):
    waiting, running = list(mixers), []
    start_slot = [k * (n_slots - MIXER_STAGES) // len(waiting) for k in range(len(waiting))]
    slot = [0]

    def side():
        while waiting and start_slot[len(start_slot) - len(waiting)] <= slot[0]:
            running.append(waiting.pop(0))
        slot[0] += 1
        for gen in list(running):
            if next(gen, "done") == "done":
                running.remove(gen)

    def drain():
        while waiting or running:
            side()

    return side, drain


def _attn_block(sink_ref, q, kv_ref, q0, gn, store):
    w = WINDOW
    nkeys = 2 * w
    r = lax.broadcasted_iota(jnp.int32, (nkeys, w), 0)
    c = lax.broadcasted_iota(jnp.int32, (nkeys, w), 1)
    first_half = lax.broadcasted_iota(jnp.int32, (w, 2 * HEAD_DIM), 1) < HEAD_DIM
    k0 = jnp.maximum(q0 - w, 0)
    dist = c + (q0 - k0) - r
    bias = jnp.where((dist >= 0) & (dist <= WINDOW), 0.0, NEG)
    kvb = kv_ref[pl.ds(pl.multiple_of(k0, w), nkeys), :]
    kk = kvb[:, 0:KV_WIDTH]
    k_nat = kk.astype(BF16)
    k_swp = pltpu.roll(kk, HEAD_DIM, axis=1).astype(BF16)
    vt = kvb[:, KV_WIDTH:2 * KV_WIDTH].T.astype(BF16)

    def masked_q(h):
        qp = q[:, (h // 2) * 2 * HEAD_DIM:(h // 2 + 1) * 2 * HEAD_DIM]
        return jnp.where(first_half != bool(h % 2), qp, 0.0).astype(BF16)

    groups = []
    for keys, use_swapped in ((k_nat, False), (k_swp, True)):
        heads = [h for h in range(N_HEADS) if (h // GQA_GROUP != h % 2) == use_swapped]
        groups.append((keys, heads, jnp.concatenate([masked_q(h) for h in heads], axis=0)))
    yield
    scores = [None] * N_HEADS
    for keys, heads, qs in groups:
        st = _dot_nt(keys, qs)
        for i, h in enumerate(heads):
            scores[h] = st[:, i * w:(i + 1) * w]
    yield
    probs, dens = [], []
    for h in range(N_HEADS):
        st = scores[h] * (SCALE * LOG2E) + bias
        sink2 = sink_ref[h] * LOG2E
        m = jnp.maximum(jnp.max(st, axis=0, keepdims=True), sink2)
        p = jnp.exp2(st - m)
        dens.append(jnp.sum(p, axis=0, keepdims=True) + jnp.exp2(sink2 - m))
        probs.append(p.astype(BF16))
        if h % GQA_GROUP == GQA_GROUP - 1:
            yield
    ot = _dot(vt, jnp.concatenate(probs, axis=1))
    yield
    rows = []
    for h in range(N_HEADS):
        hk = h // GQA_GROUP
        rows.append(ot[hk * HEAD_DIM:(hk + 1) * HEAD_DIM, h * w:(h + 1) * w] / dens[h])
    at = jnp.concatenate(rows, axis=0)
    inv = lax.rsqrt(jnp.sum(at * at, axis=0, keepdims=True) * (1.0 / ATTN_WIDTH) + EPS)
    store((at * inv).T * gn)


def _softmax_sink_pv(scores, masks, values, sink):
    scores = [jnp.where(mk, s * SCALE, NEG) for s, mk in zip(scores, masks)]
    m = sink
    for s in scores:
        m = jnp.maximum(m, jnp.max(s, axis=-1, keepdims=True))
    den = jnp.exp(sink - m)
    out = None
    for s, (v, transposed) in zip(scores, values):
        p = jnp.exp(s - m)
        den = den + jnp.sum(p, axis=-1, keepdims=True)
        pv = (_dot_nt if transposed else _dot)(p.astype(BF16), v)
        out = pv if out is None else out + pv
    return out / den


def _attn_sample_group(sink_ref, z_ref, kvn_ref, ck_ref, cv_ref, gn_ref, a_ref, ko_ref, vo_ref, grp, dec_len):
    ns = ATTN_SEQ_GROUP
    rows = ns * dec_len
    wb = ck_ref.shape[2]
    m_rows = GQA_GROUP * rows
    r = lax.broadcasted_iota(jnp.int32, (m_rows, ns * wb), 0)
    c = lax.broadcasted_iota(jnp.int32, (m_rows, ns * wb), 1)
    r_seq, r_pos = (r % rows) // dec_len, r % dec_len
    mask_cache = (r_seq == c // wb) & (c % wb >= r_pos)
    r = lax.broadcasted_iota(jnp.int32, (m_rows, rows), 0)
    c = lax.broadcasted_iota(jnp.int32, (m_rows, rows), 1)
    mask_new = ((r % rows) // dec_len == c // dec_len) & (c % dec_len <= r % dec_len)
    row_head = lax.broadcasted_iota(jnp.int32, (m_rows, 1), 0) // rows

    rs = slice(grp * rows, (grp + 1) * rows)
    q = z_ref[rs, Z_Q]
    kvn_b = kvn_ref[rs, :].astype(BF16)

    def cached(ref, hk):
        frows = slice(hk * HEAD_DIM, (hk + 1) * HEAD_DIM)
        return jnp.concatenate([ref[grp * ns + n, frows, :] for n in range(ns)], axis=1).astype(BF16)

    scores = []
    for hk in range(N_KV_HEADS):
        kcols = slice(hk * HEAD_DIM, (hk + 1) * HEAD_DIM)
        heads = [hk * GQA_GROUP + g for g in range(GQA_GROUP)]
        qs = jnp.concatenate([q[:, h * HEAD_DIM:(h + 1) * HEAD_DIM] for h in heads], axis=0).astype(BF16)
        scores.append([_dot(qs, cached(ck_ref, hk)), _dot_nt(qs, kvn_b[:, kcols])])
    yield
    outs = []
    for hk in range(N_KV_HEADS):
        vcols = slice(KV_WIDTH + hk * HEAD_DIM, KV_WIDTH + (hk + 1) * HEAD_DIM)
        sink = jnp.zeros((m_rows, 1), F32)
        for g in range(GQA_GROUP):
            sink = jnp.where(row_head == g, sink_ref[hk * GQA_GROUP + g], sink)
        values = [(cached(cv_ref, hk), True), (kvn_b[:, vcols], False)]
        outs.append(_softmax_sink_pv(scores[hk], [mask_cache, mask_new], values, sink))
    yield
    for hk in range(N_KV_HEADS):
        for g in range(GQA_GROUP):
            h = hk * GQA_GROUP + g
            a_ref[rs, h * HEAD_DIM:(h + 1) * HEAD_DIM] = outs[hk][g * rows:(g + 1) * rows, :]
    a_ref[rs, :] = _rms(a_ref[rs, :], gn_ref[...])
    keep = wb - dec_len
    is_new = lax.broadcasted_iota(jnp.int32, (KV_WIDTH, wb), 1) >= keep
    for n in range(ns):
        s = grp * ns + n
        ts = slice(grp * rows + n * dec_len, grp * rows + (n + 1) * dec_len)
        for src, dst, cols in ((ck_ref, ko_ref, slice(0, KV_WIDTH)), (cv_ref, vo_ref, slice(KV_WIDTH, 2 * KV_WIDTH))):
            new_t = jnp.concatenate([jnp.zeros((keep, KV_WIDTH), F32), kvn_ref[ts, cols]], axis=0).T
            dst[s] = jnp.where(is_new, new_t, pltpu.roll(src[s], keep, axis=1))


def _hgrn_bound_consts(lbl_ref):
    lbl = lbl_ref[...]
    e = jnp.exp(lbl - jnp.max(lbl, axis=0, keepdims=True))
    lb = e[0:1, :] / jnp.sum(e, axis=0, keepdims=True)
    return 0.5 + 0.5 * lb, 0.5 - 0.5 * lb


def _hgrn_gates(hp, c0, c1):
    qa = _silu_t(hp[:, 0:HG_WIDTH])
    t = c1 * jnp.tanh(0.5 * hp[:, HG_WIDTH:2 * HG_WIDTH])
    f, kx = c0 + t, c1 - t
    v = hp[:, 2 * HG_WIDTH:3 * HG_WIDTH].astype(BF16)
    return qa, kx, _split2(jnp.log2(f)), v


def _hgrn_factors(qa, kx, g, g_mid, g_last):
    qt = qa * jnp.exp2(g - g_mid)
    kt = kx * jnp.exp2(g_mid - g)
    qg = (qt * jnp.exp2(g_mid)).astype(BF16)
    kd = (kt * jnp.exp2(g_last - g_mid)).astype(BF16)
    return qt.astype(BF16), kt.astype(BF16), qg, kd


def _segment_masks(nseq):
    ls = HG_TILE // nseq
    r = lax.broadcasted_iota(jnp.int32, (HG_TILE, HG_TILE), 0)
    c = lax.broadcasted_iota(jnp.int32, (HG_TILE, HG_TILE), 1)
    same = (r // ls) == (c // ls)
    return same, same & (c <= r), same & (c % ls < ls // 2)


def _as_bf16(mask):
    return mask.astype(F32).astype(BF16)


def _hgrn_short_tile(z_ref, c0, c1, gn, s_in, s_out, o_ref, ti, nseq):
    ch = HG_TILE
    ls = ch // nseq
    same, causal, first_half = _segment_masks(nseq)
    cum_lhs = _twice(_as_bf16(jnp.concatenate([causal, first_half, same], axis=0)))
    sr = lax.broadcasted_iota(jnp.int32, (ch, nseq * HG_DV), 0)
    sc = lax.broadcasted_iota(jnp.int32, (ch, nseq * HG_DV), 1)
    seg_sel = _as_bf16((sr // ls) == (sc // HG_DV))
    seg_sel2 = jnp.concatenate([seg_sel, seg_sel], axis=0)
    rows = slice(ti * ch, (ti + 1) * ch)

    qa, kx, parts, v = _hgrn_gates(z_ref[rows, Z_HP], c0, c1)
    gs = _dot(cum_lhs, parts)
    dcol = _dot_tn(parts, seg_sel2)
    yield
    qt, kt, qg, kd = _hgrn_factors(qa, kx, gs[0:ch], gs[ch:2 * ch], gs[2 * ch:3 * ch])
    cols = [slice(h * HG_DK, (h + 1) * HG_DK) for h in range(HG_HEADS)]
    scores = [_dot_nt(qt[:, cs], kt[:, cs]) for cs in cols]
    yield
    outs = []
    for h, cs in enumerate(cols):
        a = jnp.where(causal, scores[h], 0.0)
        o = _dot(a.astype(BF16), v[:, cs])
        inter = []
        for n in range(nseq):
            rs = slice(n * ls, (n + 1) * ls)
            s = s_in[ti * nseq + n, h]
            inter.append(_dot(qg[rs, cs], s.astype(BF16)))
            decay = jnp.exp2(dcol[cs, n * HG_DV:(n + 1) * HG_DV])
            s_out[ti * nseq + n, h] = s * decay + _dot_tn(kd[rs, cs], v[rs, cs])
        outs.append(o + jnp.concatenate(inter, axis=0))
    yield
    gate = _silu_t(z_ref[rows, Z_HG])
    for h, cs in enumerate(cols):
        o_ref[rows, cs] = _rms(outs[h], gn) * gate[:, cs]


def _sample_mixer_ffn_kernel(sink_ref, x1_ref, z_ref, kvn_ref, ck_ref, cv_ref, s_in_ref,
                             lbl_ref, gn_ref, ghg_ref, g_mixpost, g_pre, g_post, wout_ref, wgu_ref, wd_ref,
                             y_ref, ko_ref, vo_ref, s_out_ref, ao_scr, *, dec_len):
    j = pl.program_id(0)
    tile = x1_ref.shape[0]

    @pl.when(j == 0)
    def _():
        ao_scr[...] = jnp.zeros(ao_scr.shape, F32)

    a_scr, o_scr = ao_scr.at[:, 0:ATTN_WIDTH], ao_scr.at[:, ATTN_WIDTH:ATTN_WIDTH + HG_WIDTH]
    c0, c1 = _hgrn_bound_consts(lbl_ref)
    nseq = HG_TILE // dec_len
    mixers = [_attn_sample_group(sink_ref, z_ref, kvn_ref, ck_ref, cv_ref, gn_ref, a_scr, ko_ref, vo_ref, grp, dec_len)
              for grp in range(tile // (ATTN_SEQ_GROUP * dec_len))]
    mixers += [_hgrn_short_tile(z_ref, c0, c1, ghg_ref[...], s_in_ref, s_out_ref, o_scr, ti, nseq)
               for ti in range(tile // HG_TILE)]
    side, drain = _staged(mixers, FFN_MATMULS)

    ffn = _out_ffn_stages(slice(0, tile), ao_scr, x1_ref, y_ref, g_mixpost, g_pre, g_post, wout_ref, wgu_ref, wd_ref)
    _round_robin([ffn], side)
    drain()


def _sample_mixer_ffn(x1, kv, z, cache_k, cache_v, state, sinks, g_attn, lb_logits, g_hg,
                      g_mixpost, g_pre, g_post, wout, wgu, wd, dec_len, row0):
    tile = SAMPLE_TILE
    t = cache_k.shape[0] * dec_len
    n_tiles = t // tile
    spt = tile // dec_len
    assert t % tile == 0 and row0 % tile == 0 and tile % HG_TILE == 0 and tile % (ATTN_SEQ_GROUP * dec_len) == 0
    cur = lambda j: jnp.minimum(j, n_tiles - 1)
    prev = lambda j: jnp.maximum(j - 1, 0)
    cur_spec = lambda width: pl.BlockSpec((tile, width), lambda j: (row0 // tile + cur(j), 0))
    cache_spec = pl.BlockSpec((spt,) + cache_k.shape[1:], lambda j: (cur(j), 0, 0))
    state_spec = pl.BlockSpec((spt,) + state.shape[1:], lambda j: (cur(j), 0, 0, 0))
    return pl.pallas_call(
        functools.partial(_sample_mixer_ffn_kernel, dec_len=dec_len),
        grid=(n_tiles + 1,),
        in_specs=[pl.BlockSpec(memory_space=pltpu.SMEM),
                  pl.BlockSpec((tile, D_MODEL), lambda j: (row0 // tile + prev(j), 0)),
                  cur_spec(Z_WIDTH), cur_spec(2 * KV_WIDTH), cache_spec, cache_spec, state_spec,
                  _const_spec(lb_logits.shape), _const_spec(g_attn.shape), _const_spec(g_hg.shape),
                  _const_spec(g_mixpost.shape), _const_spec(g_pre.shape), _const_spec(g_post.shape),
                  _const_spec(wout.shape), _const_spec(wgu.shape), _const_spec(wd.shape)],
        out_specs=[pl.BlockSpec((tile, D_MODEL), lambda j: (prev(j), 0)), cache_spec, cache_spec, state_spec],
        out_shape=[jax.ShapeDtypeStruct((t, D_MODEL), F32), jax.ShapeDtypeStruct(cache_k.shape, F32),
                   jax.ShapeDtypeStruct(cache_v.shape, F32), jax.ShapeDtypeStruct(state.shape, F32)],
        scratch_shapes=[pltpu.VMEM((tile, ATTN_WIDTH + HG_WIDTH), F32)],
        compiler_params=pltpu.CompilerParams(dimension_semantics=("arbitrary",),
                                             vmem_limit_bytes=VMEM_LIMIT_BIG),
        name="sample_mixer_ffn",
    )(sinks, x1, z, kv, cache_k, cache_v, state, lb_logits, g_attn, g_hg,
      g_mixpost, g_pre, g_post, wout, wgu, wd)


def _mixer_ffn_kernel(sink_ref, x1_ref, z_ref, kv_ref, lbl_ref, gn_ref, ghg_ref,
                      g_mixpost, g_pre, g_post, wout_ref, wgu_ref, wd_ref,
                      y_ref, s_out_ref, ao_scr, st_scr, *, n_tiles, tiles_per_seq):
    j = pl.program_id(0)
    tile = x1_ref.shape[0]
    pos = jnp.minimum(j, n_tiles - 1) % tiles_per_seq

    @pl.when(j == 0)
    def _():
        ao_scr[...] = jnp.zeros(ao_scr.shape, ao_scr.dtype)
        st_scr[...] = jnp.zeros(st_scr.shape, F32)

    def attn_block(blk):
        rs = slice(blk * WINDOW, (blk + 1) * WINDOW)

        def store(a):
            ao_scr[rs, 0:ATTN_WIDTH] = a.astype(ao_scr.dtype)

        return _attn_block(sink_ref, z_ref[rs, Z_Q], kv_ref, pos * tile + blk * WINDOW, gn_ref[...], store)

    ch = HG_TILE
    _, causal, _ = _segment_masks(1)
    cum_lhs = _twice(_as_bf16(causal))
    c0, c1 = _hgrn_bound_consts(lbl_ref)
    ghg = ghg_ref[...]

    def hgrn_chunk(ci):
        rs = slice(ci * ch, (ci + 1) * ch)
        qa, kx, parts, v = _hgrn_gates(z_ref[rs, Z_HP], c0, c1)
        yield
        g = _dot(cum_lhs, parts)
        yield
        g_last = g[ch - 1:ch, :]
        qt, kt, qg, kd = _hgrn_factors(qa, kx, g, g[ch // 2 - 1:ch // 2, :], g_last)
        decay = jnp.exp2(g_last)
        cols = [slice(h * HG_DK, (h + 1) * HG_DK) for h in range(HG_HEADS)]
        yield
        scores = [_dot_nt(qt[:, cs], kt[:, cs]) for cs in cols]
        yield
        probs = [jnp.where(causal, sc, 0.0).astype(BF16) for sc in scores]
        yield
        outs = []
        for h, cs in enumerate(cols):
            st = st_scr[h]
            if ci == 0:
                st = jnp.where(pos == 0, 0.0, st)
            outs.append(_dot(probs[h], v[:, cs]) + _dot_nt(qg[:, cs], st.astype(BF16)))
            st_scr[h] = st * decay[:, cs] + _dot_tn(v[:, cs], kd[:, cs])
        yield
        gate = _silu_t(z_ref[rs, Z_HG])
        for h, cs in enumerate(cols):
            o = _rms(outs[h], ghg) * gate[:, cs]
            ao_scr[rs, ATTN_WIDTH + h * HG_DV:ATTN_WIDTH + (h + 1) * HG_DV] = o.astype(ao_scr.dtype)

    n_blk, n_chunk = tile // WINDOW, tile // ch
    mixers = []
    for blk in range(n_blk):
        mixers += [hgrn_chunk(ci) for ci in range(blk * n_chunk // n_blk, (blk + 1) * n_chunk // n_blk)]
        mixers.insert(len(mixers) - 1, attn_block(blk))
    side, drain = _staged(mixers, 2 * FFN_MATMULS)

    halves = [_out_ffn_stages(rs, ao_scr, x1_ref, y_ref, g_mixpost, g_pre, g_post, wout_ref, wgu_ref, wd_ref)
              for rs in (slice(0, tile // 2), slice(tile // 2, tile))]
    for gen in halves:
        next(gen)
    _round_robin(halves, side)
    drain()

    @pl.when((pos == tiles_per_seq - 1) & (j < n_tiles))
    def _():
        b = j // tiles_per_seq
        for h in range(HG_HEADS):
            s_out_ref[pl.ds(b, 1), h] = st_scr[h].T[None]


def _mixer_ffn(x1, kv, z, sinks, g_attn, lb_logits, g_hg, g_mixpost, g_pre, g_post, wout, wgu, wd,
               batch, seq):
    t = batch * seq
    tile = TOKEN_TILE
    assert seq % tile == 0
    n_tiles, tps = t // tile, seq // tile
    cur = lambda j: jnp.minimum(j, n_tiles - 1)
    prev = lambda j: jnp.maximum(j - 1, 0)
    cur_spec = lambda width: pl.BlockSpec((tile, width), lambda j: (cur(j), 0))
    prev_spec = pl.BlockSpec((tile, D_MODEL), lambda j: (prev(j), 0))
    s_shape = (batch, HG_HEADS, HG_DK, HG_DV)
    return pl.pallas_call(
        functools.partial(_mixer_ffn_kernel, n_tiles=n_tiles, tiles_per_seq=tps),
        grid=(n_tiles + 1,),
        in_specs=[pl.BlockSpec(memory_space=pltpu.SMEM),
                  prev_spec, cur_spec(Z_WIDTH),
                  pl.BlockSpec((seq, 2 * KV_WIDTH), lambda j: (cur(j) // tps, 0)),
                  _const_spec(lb_logits.shape), _const_spec(g_attn.shape), _const_spec(g_hg.shape),
                  _const_spec(g_mixpost.shape), _const_spec(g_pre.shape), _const_spec(g_post.shape),
                  _const_spec(wout.shape), _const_spec(wgu.shape), _const_spec(wd.shape)],
        out_specs=[prev_spec, pl.BlockSpec(s_shape, lambda j: (0, 0, 0, 0))],
        out_shape=[jax.ShapeDtypeStruct((t, D_MODEL), F32), jax.ShapeDtypeStruct(s_shape, F32)],
        scratch_shapes=[pltpu.VMEM((tile, ATTN_WIDTH + HG_WIDTH), BF16), pltpu.VMEM((HG_HEADS, HG_DV, HG_DK), F32)],
        compiler_params=pltpu.CompilerParams(dimension_semantics=("arbitrary",),
                                             vmem_limit_bytes=VMEM_LIMIT_BIG),
        name="mixer_ffn",
    )(sinks, x1, z, kv, lb_logits, g_attn, g_hg, g_mixpost, g_pre, g_post, wout, wgu, wd)


def kernel(x_prompt, x_sample, cache_k_win, cache_v_win, state_hgrn, w_in, b_in, attn_sinks, attn_out_norm,
           hg_lb_logits, hg_out_norm, w_out, ffn1_w_gu, ffn1_w_down, ffn2_w_gu, ffn2_w_down,
           norm_ffn1_pre, norm_ffn1_post, norm_mix_pre, norm_mix_post, norm_ffn2_pre, norm_ffn2_post):
    depth = w_in.shape[0]
    assert depth == 1, "single-layer trunk"
    batch, seq, _ = x_prompt.shape
    dec_batch, dec_len, _ = x_sample.shape
    wb = cache_k_win.shape[2]
    assert seq % WINDOW == 0 and wb == WINDOW and HG_TILE % dec_len == 0
    layer = 0
    row = lambda p: p[layer].reshape(1, -1).astype(F32)
    wgu1, wd1 = ffn1_w_gu[layer].astype(BF16), ffn1_w_down[layer].astype(BF16)
    win = w_in[layer].astype(BF16)
    sinks = attn_sinks[layer].astype(F32)
    lb_logits = hg_lb_logits.astype(F32)
    g_attn, g_hg = row(attn_out_norm), row(hg_out_norm)

    t_p, t_s = batch * seq, dec_batch * dec_len
    x1, kv, z, (wout, wgu2, wd2) = _ffn_in(
        x_prompt.reshape(t_p, D_MODEL), x_sample.reshape(t_s, D_MODEL),
        row(norm_ffn1_pre), row(norm_ffn1_post), row(norm_mix_pre), wgu1, wd1, win, row(b_in),
        [w[layer].astype(F32) for w in (w_out, ffn2_w_gu, ffn2_w_down)])
    back_params = (row(norm_mix_post), row(norm_ffn2_pre), row(norm_ffn2_post), wout, wgu2, wd2)

    y_p, s_prompt = _mixer_ffn(x1, kv, z, sinks, g_attn, lb_logits, g_hg, *back_params, batch, seq)
    y_prompt = y_p.reshape(batch, seq, D_MODEL)
    kv_last = jnp.stack([kv[(b + 1) * seq - WINDOW:(b + 1) * seq] for b in range(batch)])
    k_prompt = kv_last[..., :KV_WIDTH].reshape(1, batch, WINDOW, N_KV_HEADS, HEAD_DIM)
    v_prompt = kv_last[..., KV_WIDTH:].reshape(1, batch, WINDOW, N_KV_HEADS, HEAD_DIM)

    def feature_major(buf):
        return jnp.transpose(buf, (0, 2, 3, 1)).reshape(dec_batch, KV_WIDTH, wb)

    def window_major(buf_t):
        return jnp.transpose(buf_t.reshape(dec_batch, N_KV_HEADS, HEAD_DIM, wb), (0, 3, 1, 2))[None]

    y_s, k_s, v_s, s_sample = _sample_mixer_ffn(
        x1, kv, z, feature_major(cache_k_win[layer]), feature_major(cache_v_win[layer]),
        state_hgrn[layer].astype(F32), sinks, g_attn, lb_logits, g_hg, *back_params, dec_len, t_p)
    y_sample = y_s.reshape(dec_batch, dec_len, D_MODEL)
    k_sample, v_sample = window_major(k_s), window_major(v_s)

    return (y_prompt, y_sample, k_prompt, v_prompt, s_prompt[None], k_sample, v_sample, s_sample[None])
```

```python
import functools

import jax
import jax.numpy as jnp
from jax import lax
from jax.experimental import pallas as pl
from jax.experimental.pallas import tpu as pltpu

F32 = jnp.float32
BF16 = jnp.bfloat16

D_MODEL = 1024
N_HEADS = 8
N_KV_HEADS = 2
HEAD_DIM = 64
GQA_GROUP = N_HEADS // N_KV_HEADS
WINDOW = 128
ATTN_WIDTH = N_HEADS * HEAD_DIM
KV_WIDTH = N_KV_HEADS * HEAD_DIM
SCALE = HEAD_DIM ** -0.5
HG_HEADS = 4
HG_DK = 128
HG_DV = 128
HG_WIDTH = HG_HEADS * HG_DV
D_FF = 2816
EPS = 1e-6

Q_COLS = (0, ATTN_WIDTH)
KV_COLS = (ATTN_WIDTH, ATTN_WIDTH + 2 * KV_WIDTH)
HP_COLS = (KV_COLS[1], KV_COLS[1] + 3 * HG_WIDTH)
HG_COLS = (HP_COLS[1], HP_COLS[1] + HG_WIDTH)
Z_Q = slice(0, ATTN_WIDTH)
Z_HP = slice(ATTN_WIDTH, ATTN_WIDTH + 3 * HG_WIDTH)
Z_HG = slice(ATTN_WIDTH + 3 * HG_WIDTH, ATTN_WIDTH + 4 * HG_WIDTH)
Z_WIDTH = ATTN_WIDTH + 4 * HG_WIDTH

VMEM_LIMIT_BIG = 58 * 1024 * 1024
FF_CHUNKS = tuple((lo, lo + 256) for lo in range(0, D_FF, 256))
TOKEN_TILE = 512
SAMPLE_TILE = 128
HG_TILE = 64
ATTN_SEQ_GROUP = 4
MIXER_STAGES = 7
FFN_MATMULS = 2 * len(FF_CHUNKS) + 2
BF16_SUBLANES = 16
FETCH_ROWS = 32
FETCH_SLOTS = 4
NEG = -1e30
LOG2E = 1.4426950408889634


def _rms(x, g):
    return x * lax.rsqrt(jnp.mean(x * x, axis=-1, keepdims=True) + EPS) * g


def _silu(x):
    return x * jax.nn.sigmoid(x)


def _silu_t(x):
    h = 0.5 * x
    return h * jnp.tanh(h) + h


def _dot(a, b):
    return jnp.dot(a, b, preferred_element_type=F32)


def _dot_nt(a, b):
    return lax.dot_general(a, b, (((1,), (1,)), ((), ())), preferred_element_type=F32)


def _dot_tn(a, b):
    return lax.dot_general(a, b, (((0,), (0,)), ((), ())), preferred_element_type=F32)


def _split2(x):
    a = x.astype(BF16)
    b = (x - a.astype(F32)).astype(BF16)
    return jnp.concatenate([a, b], axis=0)


def _twice(sel):
    return jnp.concatenate([sel, sel], axis=1)


def _no_side_work():
    pass


def _swiglu_stages(x, g_pre, wgu_ref, wd_ref, g_post):
    h = _rms(x, g_pre).astype(BF16)
    acts = []
    for lo, hi in FF_CHUNKS:
        g = _dot(h, wgu_ref[:, lo:hi])
        yield
        u = _dot(h, wgu_ref[:, D_FF + lo:D_FF + hi])
        yield
        acts.append((_silu(g) * u).astype(BF16))
    down = _dot(jnp.concatenate(acts, axis=1), wd_ref[...])
    yield
    return x + 0.5 * _rms(down, g_post)


def _round_robin(gens, side=_no_side_work):
    alive = list(gens)
    while alive:
        for gen in list(alive):
            if next(gen, "done") == "done":
                alive.remove(gen)
            side()


def _const_spec(shape):
    zeros = (0,) * len(shape)
    return pl.BlockSpec(shape, lambda *_: zeros, pipeline_mode=pl.Buffered(1))


def _row_spec(tile, width):
    return pl.BlockSpec((tile, width), lambda i: (i, 0))


def _fetch_as_bf16(pairs, stage, sem):
    n_slots, rows_per_chunk, _ = stage.shape
    chunks = [(src, dst, r) for src, dst in pairs for r in range(0, src.shape[0], rows_per_chunk)]

    def copy(i):
        src, _, r = chunks[i]
        return pltpu.make_async_copy(src.at[r:r + rows_per_chunk, :], stage.at[i % n_slots, :, 0:src.shape[1]],
                                     sem.at[i % n_slots])

    for i in range(min(n_slots, len(chunks))):
        copy(i).start()
    for i, (src, dst, r) in enumerate(chunks):
        copy(i).wait()
        dst[r:r + rows_per_chunk, :] = stage[i % n_slots, :, 0:src.shape[1]].astype(BF16)
        if i + n_slots < len(chunks):
            copy(i + n_slots).start()


def _ffn_in_kernel(xa_ref, xb_ref, g_pre, g_post, g_mix, wgu_hbm, wd_hbm, win_hbm, bin_ref, *rest, tiles_a):
    *rest, wgu_ref, wd_ref, win_ref, stage, sem = rest
    n_later = (len(rest) - 3) // 2
    later_f32, (x1_ref, kv_ref, z_ref), later_bf16 = rest[:n_later], rest[n_later:n_later + 3], rest[n_later + 3:]
    from_a = pl.program_id(0) < tiles_a

    @pl.when(pl.program_id(0) == 0)
    def _():
        _fetch_as_bf16([(wgu_hbm, wgu_ref), (wd_hbm, wd_ref), (win_hbm, win_ref)], stage, sem)

    def cast_later_weights():
        for src, dst in zip(later_f32, later_bf16):
            yield
            dst[...] = src[...].astype(BF16)

    def half(rs):
        x = jnp.where(from_a, xa_ref[rs, :], xb_ref[rs, :])
        x1 = yield from _swiglu_stages(x, g_pre[...], wgu_ref, wd_ref, g_post[...])
        x1_ref[rs, :] = x1
        h = _rms(x1, g_mix[...]).astype(BF16)
        yield
        qkv = _dot(h, win_ref[:, Q_COLS[0]:KV_COLS[1]]) + bin_ref[:, Q_COLS[0]:KV_COLS[1]]
        z_ref[rs, Z_Q] = qkv[:, 0:ATTN_WIDTH]
        kv_ref[rs, :] = qkv[:, ATTN_WIDTH:ATTN_WIDTH + 2 * KV_WIDTH]
        yield
        z_ref[rs, Z_HP.start:Z_HG.stop] = (_dot(h, win_ref[:, HP_COLS[0]:HG_COLS[1]])
                                           + bin_ref[:, HP_COLS[0]:HG_COLS[1]])

    rows = x1_ref.shape[0] // 2
    first, second = half(slice(0, rows)), half(slice(rows, 2 * rows))
    next(first)
    _round_robin([first, second, cast_later_weights()])


def _chunk_spec(shape, n_steps):
    rows, width = shape
    n_chunks = max(d for d in range(1, n_steps + 1) if rows % (d * BF16_SUBLANES) == 0)
    return pl.BlockSpec((rows // n_chunks, width), lambda j: (jnp.minimum(j, n_chunks - 1), 0))


def _ffn_in(xa, xb, g_pre, g_post, g_mix, wgu, wd, win, b_in, later_weights):
    tile = TOKEN_TILE
    assert xa.shape[0] % tile == 0 and xb.shape[0] % tile == 0
    tiles_a, tiles_b = xa.shape[0] // tile, xb.shape[0] // tile
    n_tiles = tiles_a + tiles_b
    widths = [D_MODEL, 2 * KV_WIDTH, Z_WIDTH]
    later_specs = [_chunk_spec(w.shape, n_tiles) for w in later_weights]
    own_weights = (wgu, wd, win)
    assert all(w.dtype == F32 and w.shape[0] % FETCH_ROWS == 0 for w in own_weights)
    in_hbm = pl.BlockSpec(memory_space=pl.ANY)
    x1, kv, z, *later_bf16 = pl.pallas_call(
        functools.partial(_ffn_in_kernel, tiles_a=tiles_a),
        grid=(n_tiles,),
        in_specs=[pl.BlockSpec((tile, D_MODEL), lambda j: (jnp.minimum(j, tiles_a - 1), 0)),
                  pl.BlockSpec((tile, D_MODEL), lambda j: (jnp.maximum(j - tiles_a, 0), 0)),
                  _const_spec(g_pre.shape), _const_spec(g_post.shape),
                  _const_spec(g_mix.shape), in_hbm, in_hbm, in_hbm, _const_spec(b_in.shape)] + later_specs,
        out_specs=[_row_spec(tile, w) for w in widths] + later_specs,
        out_shape=([jax.ShapeDtypeStruct((n_tiles * tile, w), F32) for w in widths]
                   + [jax.ShapeDtypeStruct(w.shape, BF16) for w in later_weights]),
        scratch_shapes=[pltpu.VMEM(w.shape, BF16) for w in own_weights]
                       + [pltpu.VMEM((FETCH_SLOTS, FETCH_ROWS, max(w.shape[1] for w in own_weights)), F32),
                          pltpu.SemaphoreType.DMA((FETCH_SLOTS,))],
        compiler_params=pltpu.CompilerParams(dimension_semantics=("arbitrary",),
                                             vmem_limit_bytes=VMEM_LIMIT_BIG),
        name="ffn_in",
    )(xa, xb, g_pre, g_post, g_mix, wgu, wd, win, b_in, *later_weights)
    return x1, kv, z, later_bf16


def _out_ffn_stages(rs, ao_scr, x1_ref, y_ref, g_mixpost, g_pre, g_post, wout_ref, wgu_ref, wd_ref):
    mix = _dot(ao_scr[rs, :].astype(BF16), wout_ref[...])
    yield
    x2 = x1_ref[rs, :] + _rms(mix, g_mixpost[...])
    y_ref[rs, :] = yield from _swiglu_stages(x2, g_pre[...], wgu_ref, wd_ref, g_post[...])


def _staged(mixers, n_slots):
    waiting, running = list(mixers), []
    start_slot = [k * (n_slots - MIXER_STAGES) // len(waiting) for k in range(len(waiting))]
    slot = [0]

    def side():
        while waiting and start_slot[len(start_slot) - len(waiting)] <= slot[0]:
            running.append(waiting.pop(0))
        slot[0] += 1
        for gen in list(running):
            if next(gen, "done") == "done":
                running.remove(gen)

    def drain():
        while waiting or running:
            side()

    return side, drain


def _attn_block(sink_ref, q, kv_ref, q0, gn, store):
    w = WINDOW
    nkeys = 2 * w
    r = lax.broadcasted_iota(jnp.int32, (nkeys, w), 0)
    c = lax.broadcasted_iota(jnp.int32, (nkeys, w), 1)
    first_half = lax.broadcasted_iota(jnp.int32, (w, 2 * HEAD_DIM), 1) < HEAD_DIM
    k0 = jnp.maximum(q0 - w, 0)
    dist = c + (q0 - k0) - r
    bias = jnp.where((dist >= 0) & (dist <= WINDOW), 0.0, NEG)
    kvb = kv_ref[pl.ds(pl.multiple_of(k0, w), nkeys), :]
    kk = kvb[:, 0:KV_WIDTH]
    k_nat = kk.astype(BF16)
    k_swp = pltpu.roll(kk, HEAD_DIM, axis=1).astype(BF16)
    vt = kvb[:, KV_WIDTH:2 * KV_WIDTH].T.astype(BF16)

    def masked_q(h):
        qp = q[:, (h // 2) * 2 * HEAD_DIM:(h // 2 + 1) * 2 * HEAD_DIM]
        return jnp.where(first_half != bool(h % 2), qp, 0.0).astype(BF16)

    groups = []
    for keys, use_swapped in ((k_nat, False), (k_swp, True)):
        heads = [h for h in range(N_HEADS) if (h // GQA_GROUP != h % 2) == use_swapped]
        groups.append((keys, heads, jnp.concatenate([masked_q(h) for h in heads], axis=0)))
    yield
    scores = [None] * N_HEADS
    for keys, heads, qs in groups:
        st = _dot_nt(keys, qs)
        for i, h in enumerate(heads):
            scores[h] = st[:, i * w:(i + 1) * w]
    yield
    probs, dens = [], []
    for h in range(N_HEADS):
        st = scores[h] * (SCALE * LOG2E) + bias
        sink2 = sink_ref[h] * LOG2E
        m = jnp.maximum(jnp.max(st, axis=0, keepdims=True), sink2)
        p = jnp.exp2(st - m)
        dens.append(jnp.sum(p, axis=0, keepdims=True) + jnp.exp2(sink2 - m))
        probs.append(p.astype(BF16))
        if h % GQA_GROUP == GQA_GROUP - 1:
            yield
    ot = _dot(vt, jnp.concatenate(probs, axis=1))
    yield
    rows = []
    for h in range(N_HEADS):
        hk = h // GQA_GROUP
        rows.append(ot[hk * HEAD_DIM:(hk + 1) * HEAD_DIM, h * w:(h + 1) * w] / dens[h])
    at = jnp.concatenate(rows, axis=0)
    inv = lax.rsqrt(jnp.sum(at * at, axis=0, keepdims=True) * (1.0 / ATTN_WIDTH) + EPS)
    store((at * inv).T * gn)


def _softmax_sink_pv(scores, masks, values, sink):
    scores = [jnp.where(mk, s * SCALE, NEG) for s, mk in zip(scores, masks)]
    m = sink
    for s in scores:
        m = jnp.maximum(m, jnp.max(s, axis=-1, keepdims=True))
    den = jnp.exp(sink - m)
    out = None
    for s, (v, transposed) in zip(scores, values):
        p = jnp.exp(s - m)
        den = den + jnp.sum(p, axis=-1, keepdims=True)
        pv = (_dot_nt if transposed else _dot)(p.astype(BF16), v)
        out = pv if out is None else out + pv
    return out / den


def _attn_sample_group(sink_ref, z_ref, kvn_ref, ck_ref, cv_ref, gn_ref, a_ref, ko_ref, vo_ref, grp, dec_len):
    ns = ATTN_SEQ_GROUP
    rows = ns * dec_len
    wb = ck_ref.shape[2]
    m_rows = GQA_GROUP * rows
    r = lax.broadcasted_iota(jnp.int32, (m_rows, ns * wb), 0)
    c = lax.broadcasted_iota(jnp.int32, (m_rows, ns * wb), 1)
    r_seq, r_pos = (r % rows) // dec_len, r % dec_len
    mask_cache = (r_seq == c // wb) & (c % wb >= r_pos)
    r = lax.broadcasted_iota(jnp.int32, (m_rows, rows), 0)
    c = lax.broadcasted_iota(jnp.int32, (m_rows, rows), 1)
    mask_new = ((r % rows) // dec_len == c // dec_len) & (c % dec_len <= r % dec_len)
    row_head = lax.broadcasted_iota(jnp.int32, (m_rows, 1), 0) // rows

    rs = slice(grp * rows, (grp + 1) * rows)
    q = z_ref[rs, Z_Q]
    kvn_b = kvn_ref[rs, :].astype(BF16)

    def cached(ref, hk):
        frows = slice(hk * HEAD_DIM, (hk + 1) * HEAD_DIM)
        return jnp.concatenate([ref[grp * ns + n, frows, :] for n in range(ns)], axis=1).astype(BF16)

    scores = []
    for hk in range(N_KV_HEADS):
        kcols = slice(hk * HEAD_DIM, (hk + 1) * HEAD_DIM)
        heads = [hk * GQA_GROUP + g for g in range(GQA_GROUP)]
        qs = jnp.concatenate([q[:, h * HEAD_DIM:(h + 1) * HEAD_DIM] for h in heads], axis=0).astype(BF16)
        scores.append([_dot(qs, cached(ck_ref, hk)), _dot_nt(qs, kvn_b[:, kcols])])
    yield
    outs = []
    for hk in range(N_KV_HEADS):
        vcols = slice(KV_WIDTH + hk * HEAD_DIM, KV_WIDTH + (hk + 1) * HEAD_DIM)
        sink = jnp.zeros((m_rows, 1), F32)
        for g in range(GQA_GROUP):
            sink = jnp.where(row_head == g, sink_ref[hk * GQA_GROUP + g], sink)
        values = [(cached(cv_ref, hk), True), (kvn_b[:, vcols], False)]
        outs.append(_softmax_sink_pv(scores[hk], [mask_cache, mask_new], values, sink))
    yield
    for hk in range(N_KV_HEADS):
        for g in range(GQA_GROUP):
            h = hk * GQA_GROUP + g
            a_ref[rs, h * HEAD_DIM:(h + 1) * HEAD_DIM] = outs[hk][g * rows:(g + 1) * rows, :]
    a_ref[rs, :] = _rms(a_ref[rs, :], gn_ref[...])
    keep = wb - dec_len
    is_new = lax.broadcasted_iota(jnp.int32, (KV_WIDTH, wb), 1) >= keep
    for n in range(ns):
        s = grp * ns + n
        ts = slice(grp * rows + n * dec_len, grp * rows + (n + 1) * dec_len)
        for src, dst, cols in ((ck_ref, ko_ref, slice(0, KV_WIDTH)), (cv_ref, vo_ref, slice(KV_WIDTH, 2 * KV_WIDTH))):
            new_t = jnp.concatenate([jnp.zeros((keep, KV_WIDTH), F32), kvn_ref[ts, cols]], axis=0).T
            dst[s] = jnp.where(is_new, new_t, pltpu.roll(src[s], keep, axis=1))


def _hgrn_bound_consts(lbl_ref):
    lbl = lbl_ref[...]
    e = jnp.exp(lbl - jnp.max(lbl, axis=0, keepdims=True))
    lb = e[0:1, :] / jnp.sum(e, axis=0, keepdims=True)
    return 0.5 + 0.5 * lb, 0.5 - 0.5 * lb


def _hgrn_gates(hp, c0, c1):
    qa = _silu_t(hp[:, 0:HG_WIDTH])
    t = c1 * jnp.tanh(0.5 * hp[:, HG_WIDTH:2 * HG_WIDTH])
    f, kx = c0 + t, c1 - t
    v = hp[:, 2 * HG_WIDTH:3 * HG_WIDTH].astype(BF16)
    return qa, kx, _split2(jnp.log2(f)), v


def _hgrn_factors(qa, kx, g, g_mid, g_last):
    qt = qa * jnp.exp2(g - g_mid)
    kt = kx * jnp.exp2(g_mid - g)
    qg = (qt * jnp.exp2(g_mid)).astype(BF16)
    kd = (kt * jnp.exp2(g_last - g_mid)).astype(BF16)
    return qt.astype(BF16), kt.astype(BF16), qg, kd


def _segment_masks(nseq):
    ls = HG_TILE // nseq
    r = lax.broadcasted_iota(jnp.int32, (HG_TILE, HG_TILE), 0)
    c = lax.broadcasted_iota(jnp.int32, (HG_TILE, HG_TILE), 1)
    same = (r // ls) == (c // ls)
    return same, same & (c <= r), same & (c % ls < ls // 2)


def _as_bf16(mask):
    return mask.astype(F32).astype(BF16)


def _hgrn_short_tile(z_ref, c0, c1, gn, s_in, s_out, o_ref, ti, nseq):
    ch = HG_TILE
    ls = ch // nseq
    same, causal, first_half = _segment_masks(nseq)
    cum_lhs = _twice(_as_bf16(jnp.concatenate([causal, first_half, same], axis=0)))
    sr = lax.broadcasted_iota(jnp.int32, (ch, nseq * HG_DV), 0)
    sc = lax.broadcasted_iota(jnp.int32, (ch, nseq * HG_DV), 1)
    seg_sel = _as_bf16((sr // ls) == (sc // HG_DV))
    seg_sel2 = jnp.concatenate([seg_sel, seg_sel], axis=0)
    rows = slice(ti * ch, (ti + 1) * ch)

    qa, kx, parts, v = _hgrn_gates(z_ref[rows, Z_HP], c0, c1)
    gs = _dot(cum_lhs, parts)
    dcol = _dot_tn(parts, seg_sel2)
    yield
    qt, kt, qg, kd = _hgrn_factors(qa, kx, gs[0:ch], gs[ch:2 * ch], gs[2 * ch:3 * ch])
    cols = [slice(h * HG_DK, (h + 1) * HG_DK) for h in range(HG_HEADS)]
    scores = [_dot_nt(qt[:, cs], kt[:, cs]) for cs in cols]
    yield
    outs = []
    for h, cs in enumerate(cols):
        a = jnp.where(causal, scores[h], 0.0)
        o = _dot(a.astype(BF16), v[:, cs])
        inter = []
        for n in range(nseq):
            rs = slice(n * ls, (n + 1) * ls)
            s = s_in[ti * nseq + n, h]
            inter.append(_dot(qg[rs, cs], s.astype(BF16)))
            decay = jnp.exp2(dcol[cs, n * HG_DV:(n + 1) * HG_DV])
            s_out[ti * nseq + n, h] = s * decay + _dot_tn(kd[rs, cs], v[rs, cs])
        outs.append(o + jnp.concatenate(inter, axis=0))
    yield
    gate = _silu_t(z_ref[rows, Z_HG])
    for h, cs in enumerate(cols):
        o_ref[rows, cs] = _rms(outs[h], gn) * gate[:, cs]


def _sample_mixer_ffn_kernel(sink_ref, x1_ref, z_ref, kvn_ref, ck_ref, cv_ref, s_in_ref,
                             lbl_ref, gn_ref, ghg_ref, g_mixpost, g_pre, g_post, wout_ref, wgu_ref, wd_ref,
                             y_ref, ko_ref, vo_ref, s_out_ref, ao_scr, *, dec_len):
    j = pl.program_id(0)
    tile = x1_ref.shape[0]

    @pl.when(j == 0)
    def _():
        ao_scr[...] = jnp.zeros(ao_scr.shape, F32)

    a_scr, o_scr = ao_scr.at[:, 0:ATTN_WIDTH], ao_scr.at[:, ATTN_WIDTH:ATTN_WIDTH + HG_WIDTH]
    c0, c1 = _hgrn_bound_consts(lbl_ref)
    nseq = HG_TILE // dec_len
    mixers = [_attn_sample_group(sink_ref, z_ref, kvn_ref, ck_ref, cv_ref, gn_ref, a_scr, ko_ref, vo_ref, grp, dec_len)
              for grp in range(tile // (ATTN_SEQ_GROUP * dec_len))]
    mixers += [_hgrn_short_tile(z_ref, c0, c1, ghg_ref[...], s_in_ref, s_out_ref, o_scr, ti, nseq)
               for ti in range(tile // HG_TILE)]
    side, drain = _staged(mixers, FFN_MATMULS)

    ffn = _out_ffn_stages(slice(0, tile), ao_scr, x1_ref, y_ref, g_mixpost, g_pre, g_post, wout_ref, wgu_ref, wd_ref)
    _round_robin([ffn], side)
    drain()


def _sample_mixer_ffn(x1, kv, z, cache_k, cache_v, state, sinks, g_attn, lb_logits, g_hg,
                      g_mixpost, g_pre, g_post, wout, wgu, wd, dec_len, row0):
    tile = SAMPLE_TILE
    t = cache_k.shape[0] * dec_len
    n_tiles = t // tile
    spt = tile // dec_len
    assert t % tile == 0 and row0 % tile == 0 and tile % HG_TILE == 0 and tile % (ATTN_SEQ_GROUP * dec_len) == 0
    cur = lambda j: jnp.minimum(j, n_tiles - 1)
    prev = lambda j: jnp.maximum(j - 1, 0)
    cur_spec = lambda width: pl.BlockSpec((tile, width), lambda j: (row0 // tile + cur(j), 0))
    cache_spec = pl.BlockSpec((spt,) + cache_k.shape[1:], lambda j: (cur(j), 0, 0))
    state_spec = pl.BlockSpec((spt,) + state.shape[1:], lambda j: (cur(j), 0, 0, 0))
    return pl.pallas_call(
        functools.partial(_sample_mixer_ffn_kernel, dec_len=dec_len),
        grid=(n_tiles + 1,),
        in_specs=[pl.BlockSpec(memory_space=pltpu.SMEM),
                  pl.BlockSpec((tile, D_MODEL), lambda j: (row0 // tile + prev(j), 0)),
                  cur_spec(Z_WIDTH), cur_spec(2 * KV_WIDTH), cache_spec, cache_spec, state_spec,
                  _const_spec(lb_logits.shape), _const_spec(g_attn.shape), _const_spec(g_hg.shape),
                  _const_spec(g_mixpost.shape), _const_spec(g_pre.shape), _const_spec(g_post.shape),
                  _const_spec(wout.shape), _const_spec(wgu.shape), _const_spec(wd.shape)],
        out_specs=[pl.BlockSpec((tile, D_MODEL), lambda j: (prev(j), 0)), cache_spec, cache_spec, state_spec],
        out_shape=[jax.ShapeDtypeStruct((t, D_MODEL), F32), jax.ShapeDtypeStruct(cache_k.shape, F32),
                   jax.ShapeDtypeStruct(cache_v.shape, F32), jax.ShapeDtypeStruct(state.shape, F32)],
        scratch_shapes=[pltpu.VMEM((tile, ATTN_WIDTH + HG_WIDTH), F32)],
        compiler_params=pltpu.CompilerParams(dimension_semantics=("arbitrary",),
                                             vmem_limit_bytes=VMEM_LIMIT_BIG),
        name="sample_mixer_ffn",
    )(sinks, x1, z, kv, cache_k, cache_v, state, lb_logits, g_attn, g_hg,
      g_mixpost, g_pre, g_post, wout, wgu, wd)


def _mixer_ffn_kernel(sink_ref, x1_ref, z_ref, kv_ref, lbl_ref, gn_ref, ghg_ref,
                      g_mixpost, g_pre, g_post, wout_ref, wgu_ref, wd_ref,
                      y_ref, s_out_ref, ao_scr, st_scr, *, n_tiles, tiles_per_seq):
    j = pl.program_id(0)
    tile = x1_ref.shape[0]
    pos = jnp.minimum(j, n_tiles - 1) % tiles_per_seq

    @pl.when(j == 0)
    def _():
        ao_scr[...] = jnp.zeros(ao_scr.shape, ao_scr.dtype)
        st_scr[...] = jnp.zeros(st_scr.shape, F32)

    def attn_block(blk):
        rs = slice(blk * WINDOW, (blk + 1) * WINDOW)

        def store(a):
            ao_scr[rs, 0:ATTN_WIDTH] = a.astype(ao_scr.dtype)

        return _attn_block(sink_ref, z_ref[rs, Z_Q], kv_ref, pos * tile + blk * WINDOW, gn_ref[...], store)

    ch = HG_TILE
    _, causal, _ = _segment_masks(1)
    cum_lhs = _twice(_as_bf16(causal))
    c0, c1 = _hgrn_bound_consts(lbl_ref)
    ghg = ghg_ref[...]

    def hgrn_chunk(ci):
        rs = slice(ci * ch, (ci + 1) * ch)
        qa, kx, parts, v = _hgrn_gates(z_ref[rs, Z_HP], c0, c1)
        yield
        g = _dot(cum_lhs, parts)
        yield
        g_last = g[ch - 1:ch, :]
        qt, kt, qg, kd = _hgrn_factors(qa, kx, g, g[ch // 2 - 1:ch // 2, :], g_last)
        decay = jnp.exp2(g_last)
        cols = [slice(h * HG_DK, (h + 1) * HG_DK) for h in range(HG_HEADS)]
        yield
        scores = [_dot_nt(qt[:, cs], kt[:, cs]) for cs in cols]
        yield
        probs = [jnp.where(causal, sc, 0.0).astype(BF16) for sc in scores]
        yield
        outs = []
        for h, cs in enumerate(cols):
            st = st_scr[h]
            if ci == 0:
                st = jnp.where(pos == 0, 0.0, st)
            outs.append(_dot(probs[h], v[:, cs]) + _dot_nt(qg[:, cs], st.astype(BF16)))
            st_scr[h] = st * decay[:, cs] + _dot_tn(v[:, cs], kd[:, cs])
        yield
        gate = _silu_t(z_ref[rs, Z_HG])
        for h, cs in enumerate(cols):
            o = _rms(outs[h], ghg) * gate[:, cs]
            ao_scr[rs, ATTN_WIDTH + h * HG_DV:ATTN_WIDTH + (h + 1) * HG_DV] = o.astype(ao_scr.dtype)

    n_blk, n_chunk = tile // WINDOW, tile // ch
    mixers = []
    for blk in range(n_blk):
        mixers += [hgrn_chunk(ci) for ci in range(blk * n_chunk // n_blk, (blk + 1) * n_chunk // n_blk)]
        mixers.insert(len(mixers) - 1, attn_block(blk))
    side, drain = _staged(mixers, 2 * FFN_MATMULS)

    halves = [_out_ffn_stages(rs, ao_scr, x1_ref, y_ref, g_mixpost, g_pre, g_post, wout_ref, wgu_ref, wd_ref)
              for rs in (slice(0, tile // 2), slice(tile // 2, tile))]
    for gen in halves:
        next(gen)
    _round_robin(halves, side)
    drain()

    @pl.when((pos == tiles_per_seq - 1) & (j < n_tiles))
    def _():
        b = j // tiles_per_seq
        for h in range(HG_HEADS):
            s_out_ref[pl.ds(b, 1), h] = st_scr[h].T[None]


def _mixer_ffn(x1, kv, z, sinks, g_attn, lb_logits, g_hg, g_mixpost, g_pre, g_post, wout, wgu, wd,
               batch, seq):
    t = batch * seq
    tile = TOKEN_TILE
    assert seq % tile == 0
    n_tiles, tps = t // tile, seq // tile
    cur = lambda j: jnp.minimum(j, n_tiles - 1)
    prev = lambda j: jnp.maximum(j - 1, 0)
    cur_spec = lambda width: pl.BlockSpec((tile, width), lambda j: (cur(j), 0))
    prev_spec = pl.BlockSpec((tile, D_MODEL), lambda j: (prev(j), 0))
    s_shape = (batch, HG_HEADS, HG_DK, HG_DV)
    return pl.pallas_call(
        functools.partial(_mixer_ffn_kernel, n_tiles=n_tiles, tiles_per_seq=tps),
        grid=(n_tiles + 1,),
        in_specs=[pl.BlockSpec(memory_space=pltpu.SMEM),
                  prev_spec, cur_spec(Z_WIDTH),
                  pl.BlockSpec((seq, 2 * KV_WIDTH), lambda j: (cur(j) // tps, 0)),
                  _const_spec(lb_logits.shape), _const_spec(g_attn.shape), _const_spec(g_hg.shape),
                  _const_spec(g_mixpost.shape), _const_spec(g_pre.shape), _const_spec(g_post.shape),
                  _const_spec(wout.shape), _const_spec(wgu.shape), _const_spec(wd.shape)],
        out_specs=[prev_spec, pl.BlockSpec(s_shape, lambda j: (0, 0, 0, 0))],
        out_shape=[jax.ShapeDtypeStruct((t, D_MODEL), F32), jax.ShapeDtypeStruct(s_shape, F32)],
        scratch_shapes=[pltpu.VMEM((tile, ATTN_WIDTH + HG_WIDTH), BF16), pltpu.VMEM((HG_HEADS, HG_DV, HG_DK), F32)],
        compiler_params=pltpu.CompilerParams(dimension_semantics=("arbitrary",),
                                             vmem_limit_bytes=VMEM_LIMIT_BIG),
        name="mixer_ffn",
    )(sinks, x1, z, kv, lb_logits, g_attn, g_hg, g_mixpost, g_pre, g_post, wout, wgu, wd)


def kernel(x_prompt, x_sample, cache_k_win, cache_v_win, state_hgrn, w_in, b_in, attn_sinks, attn_out_norm,
           hg_lb_logits, hg_out_norm, w_out, ffn1_w_gu, ffn1_w_down, ffn2_w_gu, ffn2_w_down,
           norm_ffn1_pre, norm_ffn1_post, norm_mix_pre, norm_mix_post, norm_ffn2_pre, norm_ffn2_post):
    depth = w_in.shape[0]
    assert depth == 1, "single-layer trunk"
    batch, seq, _ = x_prompt.shape
    dec_batch, dec_len, _ = x_sample.shape
    wb = cache_k_win.shape[2]
    assert seq % WINDOW == 0 and wb == WINDOW and HG_TILE % dec_len == 0
    layer = 0
    row = lambda p: p[layer].reshape(1, -1).astype(F32)
    wgu1, wd1, win = ffn1_w_gu[layer].astype(F32), ffn1_w_down[layer].astype(F32), w_in[layer].astype(F32)
    sinks = attn_sinks[layer].astype(F32)
    lb_logits = hg_lb_logits.astype(F32)
    g_attn, g_hg = row(attn_out_norm), row(hg_out_norm)

    t_p, t_s = batch * seq, dec_batch * dec_len
    x1, kv, z, (wout, wgu2, wd2) = _ffn_in(
        x_prompt.reshape(t_p, D_MODEL), x_sample.reshape(t_s, D_MODEL),
        row(norm_ffn1_pre), row(norm_ffn1_post), row(norm_mix_pre), wgu1, wd1, win, row(b_in),
        [w[layer].astype(F32) for w in (w_out, ffn2_w_gu, ffn2_w_down)])
    back_params = (row(norm_mix_post), row(norm_ffn2_pre), row(norm_ffn2_post), wout, wgu2, wd2)

    y_p, s_prompt = _mixer_ffn(x1, kv, z, sinks, g_attn, lb_logits, g_hg, *back_params, batch, seq)
    y_prompt = y_p.reshape(batch, seq, D_MODEL)
    kv_last = jnp.stack([kv[(b + 1) * seq - WINDOW:(b + 1) * seq] for b in range(batch)])
    k_prompt = kv_last[..., :KV_WIDTH].reshape(1, batch, WINDOW, N_KV_HEADS, HEAD_DIM)
    v_prompt = kv_last[..., KV_WIDTH:].reshape(1, batch, WINDOW, N_KV_HEADS, HEAD_DIM)

    def feature_major(buf):
        return jnp.transpose(buf, (0, 2, 3, 1)).reshape(dec_batch, KV_WIDTH, wb)

    def window_major(buf_t):
        return jnp.transpose(buf_t.reshape(dec_batch, N_KV_HEADS, HEAD_DIM, wb), (0, 3, 1, 2))[None]

    y_s, k_s, v_s, s_sample = _sample_mixer_ffn(
        x1, kv, z, feature_major(cache_k_win[layer]), feature_major(cache_v_win[layer]),
        state_hgrn[layer].astype(F32), sinks, g_attn, lb_logits, g_hg, *back_params, dec_len, t_p)
    y_sample = y_s.reshape(dec_batch, dec_len, D_MODEL)
    k_sample, v_sample = window_major(k_s), window_major(v_s)

    return (y_prompt, y_sample, k_prompt, v_prompt, s_prompt[None], k_sample, v_sample, s_sample[None])
```
